```python
import jax, jax.numpy as jnp
from jax import lax
import numpy as np

D_MODEL = 2048
BATCH = 8
SEQ = 4096
DEPTH = 2

N_META = 16
HEAD_DIM = 128
N_HEADS_FOX = D_MODEL // (2 * HEAD_DIM)
N_HEADS_SB = D_MODEL // (2 * HEAD_DIM)
D_FOX = N_HEADS_FOX * HEAD_DIM
D_SB = N_HEADS_SB * HEAD_DIM
D_MIX = D_FOX + D_SB
D_IN = 3 * D_FOX + 3 * D_SB + N_HEADS_FOX
D_FF = 256 * ((8 * D_MODEL // 3 + 255) // 256)
QB = 128
EPS = 1e-6
SPLITS = [D_FOX, 2 * D_FOX, 3 * D_FOX, 3 * D_FOX + D_SB, 3 * D_FOX + 2 * D_SB, 3 * D_FOX + 3 * D_SB]

kernel_name = "hymba_fox_stickbreaking_macaron"


def rms_norm(x, g):
    xf = x.astype(jnp.float32)
    y = xf * lax.rsqrt(jnp.mean(xf * xf, axis=-1, keepdims=True) + EPS)
    return (y * g.astype(jnp.float32)).astype(x.dtype)


def swiglu(x, w_gate, w_up, w_down):
    return (jax.nn.silu(x @ w_gate) * (x @ w_up)) @ w_down


def to_heads(t, n_heads, pad):
    B, L, _ = t.shape
    t = t.reshape(B, L, n_heads, HEAD_DIM).transpose(0, 2, 1, 3)
    return jnp.pad(t, ((0, 0), (0, 0), (pad, 0), (0, 0)))


def to_blocks(t):
    B, H, Lp = t.shape[:3]
    t = t.reshape((B, H, Lp // QB, QB) + t.shape[3:])
    return jnp.moveaxis(t, 2, 0)


def from_blocks(o):
    nb, B, H, qb, dh = o.shape
    return o.transpose(1, 2, 0, 3, 4).reshape(B, H, nb * qb, dh)


def forgetting_attention(q, k, v, log_f, valid):
    Lp, dh = q.shape[2], q.shape[3]
    c = jnp.cumsum(log_f, axis=-1)
    key_pos = jnp.arange(Lp)
    scale = dh ** -0.5
    starts = jnp.arange(Lp // QB) * QB

    def one_block(args):
        qb, cb, start = args
        q_pos = start + jnp.arange(QB)
        s = jnp.einsum('bhqd,bhkd->bhqk', qb, k).astype(jnp.float32) * scale
        s = s + cb[..., None] - c[:, :, None, :]
        diag = key_pos[None, :] == q_pos[:, None]
        allowed = (key_pos[None, :] <= q_pos[:, None]) & (valid[None, :] | diag)
        p = jax.nn.softmax(jnp.where(allowed, s, -jnp.inf), axis=-1)
        return jnp.einsum('bhqk,bhkd->bhqd', p.astype(v.dtype), v)

    return from_blocks(lax.map(one_block, (to_blocks(q), to_blocks(c), starts)))


def stick_breaking_attention(q, k, v, valid):
    Lp, dh = q.shape[2], q.shape[3]
    key_pos = jnp.arange(Lp)
    scale = dh ** -0.5
    starts = jnp.arange(Lp // QB) * QB

    def one_block(args):
        qb, start = args
        q_pos = start + jnp.arange(QB)
        z = jnp.einsum('bhqd,bhkd->bhqk', qb, k).astype(jnp.float32) * scale
        allowed = (key_pos[None, :] < q_pos[:, None]) & valid[None, :]
        log_1m = jnp.where(allowed, jax.nn.log_sigmoid(-z), 0.0)
        later = lax.cumsum(log_1m, axis=3, reverse=True) - log_1m
        a = jnp.where(allowed, jnp.exp(jax.nn.log_sigmoid(z) + later), 0.0)
        return jnp.einsum('bhqk,bhkd->bhqd', a.astype(v.dtype), v)

    return from_blocks(lax.map(one_block, (to_blocks(q), starts)))


def head_group_norm(o, g, pad):
    o = o[:, :, pad:]
    of = o.astype(jnp.float32)
    of = of * lax.rsqrt(jnp.mean(of * of, axis=-1, keepdims=True) + EPS)
    B, H, L, dh = o.shape
    of = of.transpose(0, 2, 1, 3).reshape(B, L, H * dh)
    return (of * g.astype(jnp.float32)).astype(o.dtype)


def hybrid_mixer(xn, w_in, b_forget, g_fox, g_sb, w_out):
    B, L, _ = xn.shape
    proj = xn @ w_in
    q_f, k_f, v_f, q_s, k_s, v_s, f_logit = jnp.split(proj, SPLITS, axis=-1)
    pad = (-L) % QB
    valid = jnp.arange(L + pad) >= pad
    log_f = jax.nn.log_sigmoid((f_logit + b_forget).astype(jnp.float32)).transpose(0, 2, 1)
    log_f = jnp.pad(log_f, ((0, 0), (0, 0), (pad, 0)))
    o_f = forgetting_attention(to_heads(q_f, N_HEADS_FOX, pad), to_heads(k_f, N_HEADS_FOX, pad),
                               to_heads(v_f, N_HEADS_FOX, pad), log_f, valid)
    o_s = stick_breaking_attention(to_heads(q_s, N_HEADS_SB, pad), to_heads(k_s, N_HEADS_SB, pad),
                                   to_heads(v_s, N_HEADS_SB, pad), valid)
    o = jnp.concatenate([head_group_norm(o_f, g_fox, pad), head_group_norm(o_s, g_sb, pad)], axis=-1)
    return o @ w_out


def _fwd_setup_inputs(seed: int = 0) -> dict:
    key = jax.random.key(seed)
    ks = jax.random.split(key, 20)
    f32 = jnp.float32
    nrm = lambda k, shape, scale: jax.random.normal(k, shape, f32) * scale
    gain = lambda k, shape: 1.0 + 0.02 * jax.random.normal(k, shape, f32)
    return {
        "x": jax.random.normal(ks[0], (BATCH, SEQ, D_MODEL), f32),
        "meta_tokens": nrm(ks[1], (N_META, D_MODEL), 1.0),
        "ffn1_norm": gain(ks[2], (DEPTH, D_MODEL)),
        "ffn1_w_gate": nrm(ks[3], (DEPTH, D_MODEL, D_FF), D_MODEL ** -0.5),
        "ffn1_w_up": nrm(ks[4], (DEPTH, D_MODEL, D_FF), D_MODEL ** -0.5),
        "ffn1_w_down": nrm(ks[5], (DEPTH, D_FF, D_MODEL), D_FF ** -0.5),
        "mix_norm": gain(ks[6], (DEPTH, D_MODEL)),
        "w_in": nrm(ks[7], (DEPTH, D_MODEL, D_IN), D_MODEL ** -0.5),
        "b_forget": jax.random.uniform(ks[8], (DEPTH, N_HEADS_FOX), f32, 1.0, 5.0),
        "g_fox": gain(ks[9], (DEPTH, D_FOX)),
        "g_sb": gain(ks[10], (DEPTH, D_SB)),
        "w_out": nrm(ks[11], (DEPTH, D_MIX, D_MODEL), D_MIX ** -0.5),
        "ffn2_norm": gain(ks[12], (DEPTH, D_MODEL)),
        "ffn2_w_gate": nrm(ks[13], (DEPTH, D_MODEL, D_FF), D_MODEL ** -0.5),
        "ffn2_w_up": nrm(ks[14], (DEPTH, D_MODEL, D_FF), D_MODEL ** -0.5),
        "ffn2_w_down": nrm(ks[15], (DEPTH, D_FF, D_MODEL), D_FF ** -0.5),
        "final_norm": gain(ks[16], (D_MODEL,)),
    }


def _fwd_reference(x, meta_tokens, ffn1_norm, ffn1_w_gate, ffn1_w_up, ffn1_w_down, mix_norm, w_in,
              b_forget, g_fox, g_sb, w_out, ffn2_norm, ffn2_w_gate, ffn2_w_up, ffn2_w_down,
              final_norm):
    B = x.shape[0]
    meta = jnp.broadcast_to(meta_tokens[None].astype(x.dtype), (B, N_META, x.shape[-1]))
    h = jnp.concatenate([meta, x], axis=1)
    for l in range(DEPTH):
        h = h + 0.5 * swiglu(rms_norm(h, ffn1_norm[l]), ffn1_w_gate[l], ffn1_w_up[l], ffn1_w_down[l])
        h = h + hybrid_mixer(rms_norm(h, mix_norm[l]), w_in[l], b_forget[l], g_fox[l], g_sb[l], w_out[l])
        h = h + 0.5 * swiglu(rms_norm(h, ffn2_norm[l]), ffn2_w_gate[l], ffn2_w_up[l], ffn2_w_down[l])
    return rms_norm(h, final_norm)[:, N_META:]


import jax as _jax
import jax.numpy as _jnp

TWIN_FORMAT = 'train_step'
FWD_PARAMS = ['x', 'meta_tokens', 'ffn1_norm', 'ffn1_w_gate', 'ffn1_w_up', 'ffn1_w_down', 'mix_norm', 'w_in', 'b_forget', 'g_fox', 'g_sb', 'w_out', 'ffn2_norm', 'ffn2_w_gate', 'ffn2_w_up', 'ffn2_w_down', 'final_norm']
TWIN_WEIGHTS = ['meta_tokens', 'ffn1_norm', 'ffn1_w_gate', 'ffn1_w_up', 'ffn1_w_down', 'mix_norm', 'w_in', 'b_forget', 'g_fox', 'g_sb', 'w_out', 'ffn2_norm', 'ffn2_w_gate', 'ffn2_w_up', 'ffn2_w_down', 'final_norm']
TWIN_DIFF_INPUT = 'x'
TWIN_INPUTS = ['x', 'meta_tokens', 'ffn1_norm', 'ffn1_w_gate', 'ffn1_w_up', 'ffn1_w_down', 'mix_norm', 'w_in', 'b_forget', 'g_fox', 'g_sb', 'w_out', 'ffn2_norm', 'ffn2_w_gate', 'ffn2_w_up', 'ffn2_w_down', 'final_norm', 'loss_target', 'm_meta_tokens', 'm_ffn1_norm', 'm_ffn1_w_gate', 'm_ffn1_w_up', 'm_ffn1_w_down', 'm_mix_norm', 'm_w_in', 'm_b_forget', 'm_g_fox', 'm_g_sb', 'm_w_out', 'm_ffn2_norm', 'm_ffn2_w_gate', 'm_ffn2_w_up', 'm_ffn2_w_down', 'm_final_norm', 'v_meta_tokens', 'v_ffn1_norm', 'v_ffn1_w_gate', 'v_ffn1_w_up', 'v_ffn1_w_down', 'v_mix_norm', 'v_w_in', 'v_b_forget', 'v_g_fox', 'v_g_sb', 'v_w_out', 'v_ffn2_norm', 'v_ffn2_w_gate', 'v_ffn2_w_up', 'v_ffn2_w_down', 'v_final_norm']
TWIN_OUTPUTS = ['loss', 'grad_x', 'grad_meta_tokens', 'grad_ffn1_norm', 'grad_ffn1_w_gate', 'grad_ffn1_w_up', 'grad_ffn1_w_down', 'grad_mix_norm', 'grad_w_in', 'grad_b_forget', 'grad_g_fox', 'grad_g_sb', 'grad_w_out', 'grad_ffn2_norm', 'grad_ffn2_w_gate', 'grad_ffn2_w_up', 'grad_ffn2_w_down', 'grad_final_norm', 'delta_meta_tokens', 'delta_ffn1_norm', 'delta_ffn1_w_gate', 'delta_ffn1_w_up', 'delta_ffn1_w_down', 'delta_mix_norm', 'delta_w_in', 'delta_b_forget', 'delta_g_fox', 'delta_g_sb', 'delta_w_out', 'delta_ffn2_norm', 'delta_ffn2_w_gate', 'delta_ffn2_w_up', 'delta_ffn2_w_down', 'delta_final_norm', 'new_m_meta_tokens', 'new_m_ffn1_norm', 'new_m_ffn1_w_gate', 'new_m_ffn1_w_up', 'new_m_ffn1_w_down', 'new_m_mix_norm', 'new_m_w_in', 'new_m_b_forget', 'new_m_g_fox', 'new_m_g_sb', 'new_m_w_out', 'new_m_ffn2_norm', 'new_m_ffn2_w_gate', 'new_m_ffn2_w_up', 'new_m_ffn2_w_down', 'new_m_final_norm', 'new_v_meta_tokens', 'new_v_ffn1_norm', 'new_v_ffn1_w_gate', 'new_v_ffn1_w_up', 'new_v_ffn1_w_down', 'new_v_mix_norm', 'new_v_w_in', 'new_v_b_forget', 'new_v_g_fox', 'new_v_g_sb', 'new_v_w_out', 'new_v_ffn2_norm', 'new_v_ffn2_w_gate', 'new_v_ffn2_w_up', 'new_v_ffn2_w_down', 'new_v_final_norm']
TWIN_LEAF_KINDS = {'loss': 'loss', 'grad_x': 'grad_x', 'grad_meta_tokens': 'grad_w', 'grad_ffn1_norm': 'grad_w', 'grad_ffn1_w_gate': 'grad_w', 'grad_ffn1_w_up': 'grad_w', 'grad_ffn1_w_down': 'grad_w', 'grad_mix_norm': 'grad_w', 'grad_w_in': 'grad_w', 'grad_b_forget': 'grad_w', 'grad_g_fox': 'grad_w', 'grad_g_sb': 'grad_w', 'grad_w_out': 'grad_w', 'grad_ffn2_norm': 'grad_w', 'grad_ffn2_w_gate': 'grad_w', 'grad_ffn2_w_up': 'grad_w', 'grad_ffn2_w_down': 'grad_w', 'grad_final_norm': 'grad_w', 'delta_meta_tokens': 'delta_w', 'delta_ffn1_norm': 'delta_w', 'delta_ffn1_w_gate': 'delta_w', 'delta_ffn1_w_up': 'delta_w', 'delta_ffn1_w_down': 'delta_w', 'delta_mix_norm': 'delta_w', 'delta_w_in': 'delta_w', 'delta_b_forget': 'delta_w', 'delta_g_fox': 'delta_w', 'delta_g_sb': 'delta_w', 'delta_w_out': 'delta_w', 'delta_ffn2_norm': 'delta_w', 'delta_ffn2_w_gate': 'delta_w', 'delta_ffn2_w_up': 'delta_w', 'delta_ffn2_w_down': 'delta_w', 'delta_final_norm': 'delta_w', 'new_m_meta_tokens': 'new_m', 'new_m_ffn1_norm': 'new_m', 'new_m_ffn1_w_gate': 'new_m', 'new_m_ffn1_w_up': 'new_m', 'new_m_ffn1_w_down': 'new_m', 'new_m_mix_norm': 'new_m', 'new_m_w_in': 'new_m', 'new_m_b_forget': 'new_m', 'new_m_g_fox': 'new_m', 'new_m_g_sb': 'new_m', 'new_m_w_out': 'new_m', 'new_m_ffn2_norm': 'new_m', 'new_m_ffn2_w_gate': 'new_m', 'new_m_ffn2_w_up': 'new_m', 'new_m_ffn2_w_down': 'new_m', 'new_m_final_norm': 'new_m', 'new_v_meta_tokens': 'new_v', 'new_v_ffn1_norm': 'new_v', 'new_v_ffn1_w_gate': 'new_v', 'new_v_ffn1_w_up': 'new_v', 'new_v_ffn1_w_down': 'new_v', 'new_v_mix_norm': 'new_v', 'new_v_w_in': 'new_v', 'new_v_b_forget': 'new_v', 'new_v_g_fox': 'new_v', 'new_v_g_sb': 'new_v', 'new_v_w_out': 'new_v', 'new_v_ffn2_norm': 'new_v', 'new_v_ffn2_w_gate': 'new_v', 'new_v_ffn2_w_up': 'new_v', 'new_v_ffn2_w_down': 'new_v', 'new_v_final_norm': 'new_v'}


def _forward(args):
    return _fwd_reference(*[args[k] for k in FWD_PARAMS])


def _output_shape():
    def fwd():
        inp = _fwd_setup_inputs(0)
        return _fwd_reference(*[inp[k] for k in FWD_PARAMS])
    out = _jax.eval_shape(fwd)
    return out.shape, out.dtype

N_MICROBATCH = 1
ADAM_LR = 0.001
ADAM_B1 = 0.9
ADAM_B2 = 0.999
ADAM_EPS = 1e-08
ADAM_WD = 0.01
ADAM_STEP = 10
PER_EXAMPLE_BATCH_AXIS = {'x': 0, 'loss_target': 0}
SHARED_INPUTS = []
_WEIGHT_DTYPES = {'meta_tokens': _jnp.float32, 'ffn1_norm': _jnp.float32, 'ffn1_w_gate': _jnp.float32, 'ffn1_w_up': _jnp.float32, 'ffn1_w_down': _jnp.float32, 'mix_norm': _jnp.float32, 'w_in': _jnp.float32, 'b_forget': _jnp.float32, 'g_fox': _jnp.float32, 'g_sb': _jnp.float32, 'w_out': _jnp.float32, 'ffn2_norm': _jnp.float32, 'ffn2_w_gate': _jnp.float32, 'ffn2_w_up': _jnp.float32, 'ffn2_w_down': _jnp.float32, 'final_norm': _jnp.float32}
MOMENT_SCALE = {'meta_tokens': 3.964094e-03, 'ffn1_norm': 3.913204e-02, 'ffn1_w_gate': 1.713210e-02, 'ffn1_w_up': 1.659507e-02, 'ffn1_w_down': 2.755425e-02, 'mix_norm': 8.132731e-02, 'w_in': 4.686995e-02, 'b_forget': 3.435518e-01, 'g_fox': 6.400411e-02, 'g_sb': 5.990368e-02, 'w_out': 6.249425e-02, 'ffn2_norm': 2.490581e-02, 'ffn2_w_gate': 1.061890e-02, 'ffn2_w_up': 1.030832e-02, 'ffn2_w_down': 1.710136e-02, 'final_norm': 1.599942e+01}


def _to_microbatches(a, axis):
    t = _jnp.moveaxis(a, axis, 0)
    t = t.reshape((N_MICROBATCH, t.shape[0] // N_MICROBATCH) + t.shape[1:])
    return _jnp.moveaxis(t, 1, axis + 1)


def setup_inputs(seed: int = 0) -> dict:
    inp = _fwd_setup_inputs(seed)
    key = _jax.random.fold_in(_jax.random.key(seed), 7919)
    shape, _ = _output_shape()
    out = dict(inp)
    out["loss_target"] = _jax.random.normal(_jax.random.fold_in(key, 0), shape, _jnp.float32)
    for i, name in enumerate(TWIN_WEIGHTS):
        w = inp[name].astype(_jnp.float32)
        if MOMENT_SCALE is None:
            s = _jnp.sqrt(_jnp.mean(_jnp.square(w)) + 1e-30)
        else:
            s = MOMENT_SCALE[name]
        km, kv = _jax.random.split(_jax.random.fold_in(key, i + 1))
        out[name] = w
        out["m_" + name] = s * _jax.random.normal(km, w.shape, _jnp.float32)
        out["v_" + name] = (s * s) * _jax.random.uniform(kv, w.shape, _jnp.float32, 0.5, 1.5)
    if N_MICROBATCH > 1:
        for name, axis in PER_EXAMPLE_BATCH_AXIS.items():
            out[name] = _to_microbatches(out[name], axis)
    return {'x': out['x'], 'meta_tokens': out['meta_tokens'], 'ffn1_norm': out['ffn1_norm'], 'ffn1_w_gate': out['ffn1_w_gate'], 'ffn1_w_up': out['ffn1_w_up'], 'ffn1_w_down': out['ffn1_w_down'], 'mix_norm': out['mix_norm'], 'w_in': out['w_in'], 'b_forget': out['b_forget'], 'g_fox': out['g_fox'], 'g_sb': out['g_sb'], 'w_out': out['w_out'], 'ffn2_norm': out['ffn2_norm'], 'ffn2_w_gate': out['ffn2_w_gate'], 'ffn2_w_up': out['ffn2_w_up'], 'ffn2_w_down': out['ffn2_w_down'], 'final_norm': out['final_norm'], 'loss_target': out['loss_target'], 'm_meta_tokens': out['m_meta_tokens'], 'm_ffn1_norm': out['m_ffn1_norm'], 'm_ffn1_w_gate': out['m_ffn1_w_gate'], 'm_ffn1_w_up': out['m_ffn1_w_up'], 'm_ffn1_w_down': out['m_ffn1_w_down'], 'm_mix_norm': out['m_mix_norm'], 'm_w_in': out['m_w_in'], 'm_b_forget': out['m_b_forget'], 'm_g_fox': out['m_g_fox'], 'm_g_sb': out['m_g_sb'], 'm_w_out': out['m_w_out'], 'm_ffn2_norm': out['m_ffn2_norm'], 'm_ffn2_w_gate': out['m_ffn2_w_gate'], 'm_ffn2_w_up': out['m_ffn2_w_up'], 'm_ffn2_w_down': out['m_ffn2_w_down'], 'm_final_norm': out['m_final_norm'], 'v_meta_tokens': out['v_meta_tokens'], 'v_ffn1_norm': out['v_ffn1_norm'], 'v_ffn1_w_gate': out['v_ffn1_w_gate'], 'v_ffn1_w_up': out['v_ffn1_w_up'], 'v_ffn1_w_down': out['v_ffn1_w_down'], 'v_mix_norm': out['v_mix_norm'], 'v_w_in': out['v_w_in'], 'v_b_forget': out['v_b_forget'], 'v_g_fox': out['v_g_fox'], 'v_g_sb': out['v_g_sb'], 'v_w_out': out['v_w_out'], 'v_ffn2_norm': out['v_ffn2_norm'], 'v_ffn2_w_gate': out['v_ffn2_w_gate'], 'v_ffn2_w_up': out['v_ffn2_w_up'], 'v_ffn2_w_down': out['v_ffn2_w_down'], 'v_final_norm': out['v_final_norm']}


def _loss(weights, diff, rest, loss_target):
    with _jax.named_scope("forward"):
        args = {**rest, TWIN_DIFF_INPUT: diff, **{k: w.astype(_WEIGHT_DTYPES[k]) for k, w in weights.items()}}
        y = _forward(args)
    with _jax.named_scope("loss_head"):
        err = _jnp.square(y.astype(_jnp.float32) - loss_target)
        return 0.5 * _jnp.sum(_jnp.mean(err, axis=-1)) if err.ndim else 0.5 * err


def _adamw(w, g, m, v):
    m = ADAM_B1 * m + (1.0 - ADAM_B1) * g
    v = ADAM_B2 * v + (1.0 - ADAM_B2) * _jnp.square(g)
    m_hat = m / (1.0 - ADAM_B1 ** ADAM_STEP)
    v_hat = v / (1.0 - ADAM_B2 ** ADAM_STEP)
    delta = -ADAM_LR * (m_hat / (_jnp.sqrt(v_hat) + ADAM_EPS) + ADAM_WD * w)
    return delta, m, v


def reference(x, meta_tokens, ffn1_norm, ffn1_w_gate, ffn1_w_up, ffn1_w_down, mix_norm, w_in, b_forget, g_fox, g_sb, w_out, ffn2_norm, ffn2_w_gate, ffn2_w_up, ffn2_w_down, final_norm, loss_target, m_meta_tokens, m_ffn1_norm, m_ffn1_w_gate, m_ffn1_w_up, m_ffn1_w_down, m_mix_norm, m_w_in, m_b_forget, m_g_fox, m_g_sb, m_w_out, m_ffn2_norm, m_ffn2_w_gate, m_ffn2_w_up, m_ffn2_w_down, m_final_norm, v_meta_tokens, v_ffn1_norm, v_ffn1_w_gate, v_ffn1_w_up, v_ffn1_w_down, v_mix_norm, v_w_in, v_b_forget, v_g_fox, v_g_sb, v_w_out, v_ffn2_norm, v_ffn2_w_gate, v_ffn2_w_up, v_ffn2_w_down, v_final_norm):
    given = dict(x=x, meta_tokens=meta_tokens, ffn1_norm=ffn1_norm, ffn1_w_gate=ffn1_w_gate, ffn1_w_up=ffn1_w_up, ffn1_w_down=ffn1_w_down, mix_norm=mix_norm, w_in=w_in, b_forget=b_forget, g_fox=g_fox, g_sb=g_sb, w_out=w_out, ffn2_norm=ffn2_norm, ffn2_w_gate=ffn2_w_gate, ffn2_w_up=ffn2_w_up, ffn2_w_down=ffn2_w_down, final_norm=final_norm, loss_target=loss_target, m_meta_tokens=m_meta_tokens, m_ffn1_norm=m_ffn1_norm, m_ffn1_w_gate=m_ffn1_w_gate, m_ffn1_w_up=m_ffn1_w_up, m_ffn1_w_down=m_ffn1_w_down, m_mix_norm=m_mix_norm, m_w_in=m_w_in, m_b_forget=m_b_forget, m_g_fox=m_g_fox, m_g_sb=m_g_sb, m_w_out=m_w_out, m_ffn2_norm=m_ffn2_norm, m_ffn2_w_gate=m_ffn2_w_gate, m_ffn2_w_up=m_ffn2_w_up, m_ffn2_w_down=m_ffn2_w_down, m_final_norm=m_final_norm, v_meta_tokens=v_meta_tokens, v_ffn1_norm=v_ffn1_norm, v_ffn1_w_gate=v_ffn1_w_gate, v_ffn1_w_up=v_ffn1_w_up, v_ffn1_w_down=v_ffn1_w_down, v_mix_norm=v_mix_norm, v_w_in=v_w_in, v_b_forget=v_b_forget, v_g_fox=v_g_fox, v_g_sb=v_g_sb, v_w_out=v_w_out, v_ffn2_norm=v_ffn2_norm, v_ffn2_w_gate=v_ffn2_w_gate, v_ffn2_w_up=v_ffn2_w_up, v_ffn2_w_down=v_ffn2_w_down, v_final_norm=v_final_norm)
    weights = {n: given[n] for n in TWIN_WEIGHTS}
    shared = {n: given[n] for n in SHARED_INPUTS}
    per_example = {n: given[n] for n in ['x']}
    grad_fn = _jax.value_and_grad(_loss, argnums=(0, 1))

    def one_microbatch(ex, loss_target):
        ex = dict(ex)
        diff = ex.pop(TWIN_DIFF_INPUT)
        return grad_fn(weights, diff, {**shared, **ex}, loss_target)

    if N_MICROBATCH == 1:
        loss, (grad_w, grad_x) = one_microbatch(per_example, given["loss_target"])
    else:
        def body(carry, xs):
            loss_sum, grad_sum = carry
            l_k, (gw_k, gx_k) = one_microbatch(xs[0], xs[1])
            with _jax.named_scope("update"):
                return (loss_sum + l_k, _jax.tree.map(_jnp.add, grad_sum, gw_k)), gx_k

        init = (_jnp.zeros((), _jnp.float32), _jax.tree.map(_jnp.zeros_like, weights))
        (loss, grad_w), grad_x = _jax.lax.scan(body, init, (per_example, given["loss_target"]))
    with _jax.named_scope("update"):
        delta_w, new_m, new_v = {}, {}, {}
        for n in TWIN_WEIGHTS:
            delta_w[n], new_m[n], new_v[n] = _adamw(weights[n], grad_w[n], given["m_" + n], given["v_" + n])
    return (loss, grad_x, *[grad_w[n] for n in TWIN_WEIGHTS], *[delta_w[n] for n in TWIN_WEIGHTS],
            *[new_m[n] for n in TWIN_WEIGHTS], *[new_v[n] for n in TWIN_WEIGHTS])
```

```python
import functools

import jax
import jax.numpy as jnp
from jax import lax
from jax.experimental import pallas as pl
from jax.experimental.pallas import tpu as pltpu

F32 = jnp.float32
BF16 = jnp.bfloat16

N_DEV = 8
N_META = 16
HEAD_DIM = 128
ROW_BLOCK = 128
LANES = 128
EPS = 1e-6
NEG = -1e30
ADAM_LR = 0.001
ADAM_B1 = 0.9
ADAM_B2 = 0.999
ADAM_EPS = 1e-08
ADAM_WD = 0.01
ADAM_STEP = 10
VMEM_LIMIT_BYTES = 56 * 1024 * 1024
MESH = pl.DeviceIdType.MESH

NT_DIMS = (((1,), (1,)), ((), ()))
TN_DIMS = (((0,), (0,)), ((), ()))
NN_DIMS = (((1,), (0,)), ((), ()))


def _tile(n, cap, align):
    best = None
    for d in range(align, min(n, cap) + 1, align):
        if n % d == 0:
            best = d
    return best if best is not None else n


def _params(*sem):
    return pltpu.CompilerParams(dimension_semantics=sem, vmem_limit_bytes=VMEM_LIMIT_BYTES)


def _dot(a, b, dims=NN_DIMS):
    return lax.dot_general(a, b, dims, preferred_element_type=F32)


def _dot_split(x, u):
    hi = x.astype(BF16)
    lo = (x - hi.astype(F32)).astype(BF16)
    return _dot(hi, u) + _dot(lo, u)


def _my_position():
    return lax.axis_index("x"), lax.axis_index("y"), lax.axis_index("c")


def _all_gather(arrs, name):
    n = len(arrs)

    def body(*refs):
        ins, outs = refs[:n], refs[n:2 * n]
        send_sems, recv_sems, local_sems = refs[2 * n:]
        x, y, c = _my_position()
        me, sibling = (x, y, c), (x, y, 1 - c)
        chips = [(1 - x, y), (x, 1 - y), (1 - x, 1 - y)]

        def copy(a, k, block, to, src=None):
            slot = outs[a].at[4 * block[0] + 2 * block[1] + block[2]]
            return pltpu.make_async_remote_copy(
                src_ref=slot if src is None else src, dst_ref=slot,
                send_sem=send_sems.at[a, k], recv_sem=recv_sems.at[a, k],
                device_id=to, device_id_type=MESH)

        started = []
        for a in range(n):
            mine = pltpu.make_async_copy(ins[a], outs[a].at[4 * x + 2 * y + c], local_sems.at[a])
            mine.start()
            started.append(mine)
        sends = []
        for a in range(n):
            first = [copy(a, 0, me, sibling, src=ins[a])]
            first += [copy(a, 1 + j, me, (*chip, c), src=ins[a]) for j, chip in enumerate(chips)]
            for cp in first:
                cp.start()
            sends += first
        for a in range(n):
            for j, chip in enumerate(chips):
                copy(a, 1 + j, (*chip, c), me).wait_recv()
                passed = copy(a, 4 + j, (*chip, c), sibling)
                passed.start()
                sends.append(passed)
        for a in range(n):
            copy(a, 0, sibling, me).wait_recv()
            for j, chip in enumerate(chips):
                copy(a, 4 + j, (*chip, 1 - c), me).wait_recv()
        for cp in sends:
            cp.wait_send()
        for mine in started:
            mine.wait()

    any_spec = pl.BlockSpec(memory_space=pl.ANY)
    return pl.pallas_call(
        body, name=name,
        out_shape=[jax.ShapeDtypeStruct((N_DEV,) + a.shape, a.dtype) for a in arrs],
        in_specs=[any_spec] * n, out_specs=[any_spec] * n,
        scratch_shapes=[pltpu.SemaphoreType.DMA((n, 7)), pltpu.SemaphoreType.DMA((n, 7)),
                        pltpu.SemaphoreType.DMA((n,))],
    )(*arrs)


def _exchange_partials(parts, name):
    n = len(parts)

    def body(*refs):
        ins, outs = refs[:n], refs[n:2 * n]
        send_sems, recv_sems, local_sems = refs[2 * n:]
        x, y, c = _my_position()
        me = 4 * x + 2 * y + c

        def peer_of(r):
            return (x ^ ((r >> 2) & 1), y ^ ((r >> 1) & 1), c ^ (r & 1))

        def copy(a, r):
            px, py, pc = peer_of(r)
            return pltpu.make_async_remote_copy(
                src_ref=ins[a].at[4 * px + 2 * py + pc], dst_ref=outs[a].at[me],
                send_sem=send_sems.at[a, r - 1], recv_sem=recv_sems.at[a, r - 1],
                device_id=(px, py, pc), device_id_type=MESH)

        def arrival(a, r):
            px, py, pc = peer_of(r)
            slot = outs[a].at[4 * px + 2 * py + pc]
            return pltpu.make_async_remote_copy(
                src_ref=slot, dst_ref=slot, send_sem=send_sems.at[a, r - 1], recv_sem=recv_sems.at[a, r - 1],
                device_id=(px, py, pc), device_id_type=MESH)

        local = []
        for a in range(n):
            mine = pltpu.make_async_copy(ins[a].at[me], outs[a].at[me], local_sems.at[a])
            mine.start()
            local.append(mine)
        sends = [copy(a, r) for a in range(n) for r in range(1, N_DEV)]
        for cp in sends:
            cp.start()
        for a in range(n):
            for r in range(1, N_DEV):
                arrival(a, r).wait_recv()
        for cp in sends:
            cp.wait_send()
        for mine in local:
            mine.wait()

    any_spec = pl.BlockSpec(memory_space=pl.ANY)
    return pl.pallas_call(
        body, name=name,
        out_shape=[jax.ShapeDtypeStruct(p.shape, p.dtype) for p in parts],
        in_specs=[any_spec] * n, out_specs=[any_spec] * n,
        scratch_shapes=[pltpu.SemaphoreType.DMA((n, 7)), pltpu.SemaphoreType.DMA((n, 7)),
                        pltpu.SemaphoreType.DMA((n,))],
    )(*parts)


def _all_reduce_small(vec, name):
    n = vec.shape[1]

    def body(v_ref, o_ref, buf, send_sems, recv_sems):
        x, y, c = _my_position()
        me = 4 * x + 2 * y + c

        def peer_of(r):
            return (x ^ ((r >> 2) & 1), y ^ ((r >> 1) & 1), c ^ (r & 1))

        def copy(r):
            px, py, pc = peer_of(r)
            return pltpu.make_async_remote_copy(
                src_ref=v_ref, dst_ref=buf.at[me], send_sem=send_sems.at[r - 1], recv_sem=recv_sems.at[r - 1],
                device_id=(px, py, pc), device_id_type=MESH)

        def arrival(r):
            px, py, pc = peer_of(r)
            slot = buf.at[4 * px + 2 * py + pc]
            return pltpu.make_async_remote_copy(
                src_ref=slot, dst_ref=slot, send_sem=send_sems.at[r - 1], recv_sem=recv_sems.at[r - 1],
                device_id=(px, py, pc), device_id_type=MESH)

        sends = [copy(r) for r in range(1, N_DEV)]
        for cp in sends:
            cp.start()
        buf[me] = v_ref[...]
        for r in range(1, N_DEV):
            arrival(r).wait_recv()
        for cp in sends:
            cp.wait_send()
        total = buf[0]
        for d in range(1, N_DEV):
            total = total + buf[d]
        o_ref[...] = total

    vmem = pl.BlockSpec(memory_space=pltpu.VMEM)
    return pl.pallas_call(
        body, name=name, out_shape=jax.ShapeDtypeStruct((1, n), F32),
        in_specs=[vmem], out_specs=vmem,
        scratch_shapes=[pltpu.VMEM((N_DEV, 1, n), F32), pltpu.SemaphoreType.DMA((7,)),
                        pltpu.SemaphoreType.DMA((7,))],
    )(vec)


def _mm(pairs, mode, out_dtype, *, name, tm=None, tn=None, tk=None, alpha=1.0, res=None):
    a0, b0 = pairs[0]
    if mode == "nn":
        (m, k), n = a0.shape, b0.shape[1]
    elif mode == "nt":
        (m, k), n = a0.shape, b0.shape[0]
    else:
        (k, m), n = a0.shape, b0.shape[1]
    dims = {"nn": NN_DIMS, "nt": NT_DIMS, "tn": TN_DIMS}[mode]
    tm = tm or _tile(m, 1056, LANES if mode == "tn" else 16)
    tn = tn or _tile(n, 512, LANES)
    tk = tk or _tile(k, 2048 if mode != "tn" else 1056, LANES if mode != "tn" else 16)
    nk = k // tk
    npairs = len(pairs)
    a_spec = pl.BlockSpec((tk, tm), lambda i, j, kk: (kk, i)) if mode == "tn" else pl.BlockSpec((tm, tk), lambda i, j, kk: (i, kk))
    b_spec = pl.BlockSpec((tn, tk), lambda i, j, kk: (j, kk)) if mode == "nt" else pl.BlockSpec((tk, tn), lambda i, j, kk: (kk, j))
    o_spec = pl.BlockSpec((tm, tn), lambda i, j, kk: (i, j))

    def body(*refs):
        ab = refs[:2 * npairs]
        rest = refs[2 * npairs:]
        res_ref = rest[0] if res is not None else None
        o_ref = rest[1] if res is not None else rest[0]
        acc_ref = rest[-1] if nk > 1 else None
        part = None
        for p in range(npairs):
            d = _dot(ab[2 * p][...].astype(BF16), ab[2 * p + 1][...].astype(BF16), dims)
            part = d if part is None else part + d

        def finish(total):
            val = total * alpha if alpha != 1.0 else total
            if res_ref is not None:
                val = res_ref[...] + val
            o_ref[...] = val.astype(out_dtype)

        if nk == 1:
            finish(part)
        else:
            kk = pl.program_id(2)

            @pl.when(kk == 0)
            def _():
                acc_ref[...] = part

            @pl.when(kk > 0)
            def _():
                acc_ref[...] += part

            @pl.when(kk == nk - 1)
            def _():
                finish(acc_ref[...])

    operands, in_specs = [], []
    for a, b in pairs:
        operands += [a, b]
        in_specs += [a_spec, b_spec]
    if res is not None:
        operands.append(res)
        in_specs.append(o_spec)
    return pl.pallas_call(
        body, name=name, grid=(m // tm, n // tn, nk),
        out_shape=jax.ShapeDtypeStruct((m, n), out_dtype),
        in_specs=in_specs, out_specs=o_spec,
        scratch_shapes=[pltpu.VMEM((tm, tn), F32)] if nk > 1 else [],
        compiler_params=_params("parallel", "parallel", "arbitrary"),
    )(*operands)


def _rms_fwd(h, gain, name):
    m, d = h.shape
    tm = _tile(m, 528, 16)

    def body(h_ref, g_ref, o_ref):
        hv = h_ref[...]
        r = lax.rsqrt(jnp.mean(hv * hv, axis=-1, keepdims=True) + EPS)
        o_ref[...] = (hv * r * g_ref[...]).astype(BF16)

    return pl.pallas_call(
        body, name=name, grid=(m // tm,), out_shape=jax.ShapeDtypeStruct((m, d), BF16),
        in_specs=[pl.BlockSpec((tm, d), lambda i: (i, 0)), pl.BlockSpec((1, d), lambda i: (0, 0))],
        out_specs=pl.BlockSpec((tm, d), lambda i: (i, 0)),
        compiler_params=_params("parallel"),
    )(h, gain)


def _rms_bwd(dxn, h, gain, dres, name):
    m, d = h.shape
    tm = _tile(m, 264, 8)

    def body(dxn_ref, h_ref, g_ref, dres_ref, dh_ref, dg_ref):
        hv = h_ref[...]
        r = lax.rsqrt(jnp.mean(hv * hv, axis=-1, keepdims=True) + EPS)
        xhat = hv * r
        dxn_v = dxn_ref[...]
        t = dxn_v * g_ref[...]
        dh_ref[...] = dres_ref[...] + r * (t - xhat * jnp.mean(t * xhat, axis=-1, keepdims=True))
        part = jnp.sum(dxn_v * xhat, axis=0, keepdims=True)

        @pl.when(pl.program_id(0) == 0)
        def _():
            dg_ref[...] = part

        @pl.when(pl.program_id(0) > 0)
        def _():
            dg_ref[...] += part

    row = pl.BlockSpec((tm, d), lambda i: (i, 0))
    vec = pl.BlockSpec((1, d), lambda i: (0, 0))
    return pl.pallas_call(
        body, name=name, grid=(m // tm,),
        out_shape=[jax.ShapeDtypeStruct((m, d), F32), jax.ShapeDtypeStruct((1, d), F32)],
        in_specs=[row, row, vec, row], out_specs=[row, vec],
        compiler_params=_params("arbitrary"),
    )(dxn, h, gain, dres)


def _loss_head(h, gain, target, x_off, name):
    m, d = h.shape
    tm = ROW_BLOCK
    first = x_off // tm

    def body(h_ref, g_ref, t_ref, dh_ref, dg_ref, loss_ref):
        i = pl.program_id(0)

        @pl.when(i == 0)
        def _():
            dg_ref[...] = jnp.zeros_like(dg_ref)
            loss_ref[...] = jnp.zeros_like(loss_ref)

        @pl.when(i < first)
        def _():
            dh_ref[...] = jnp.zeros_like(dh_ref)

        @pl.when(i >= first)
        def _():
            hv = h_ref[...]
            g = g_ref[...]
            r = lax.rsqrt(jnp.mean(hv * hv, axis=-1, keepdims=True) + EPS)
            xhat = hv * r
            err = xhat * g - t_ref[...]
            loss_ref[...] += 0.5 * jnp.sum(jnp.mean(err * err, axis=-1, keepdims=True))
            dy = err * (1.0 / d)
            t = dy * g
            dh_ref[...] = r * (t - xhat * jnp.mean(t * xhat, axis=-1, keepdims=True))
            dg_ref[...] += jnp.sum(dy * xhat, axis=0, keepdims=True)

    row = pl.BlockSpec((tm, d), lambda i: (i, 0))
    vec = pl.BlockSpec((1, d), lambda i: (0, 0))
    return pl.pallas_call(
        body, name=name, grid=(m // tm,),
        out_shape=[jax.ShapeDtypeStruct((m, d), F32), jax.ShapeDtypeStruct((1, d), F32),
                   jax.ShapeDtypeStruct((8, LANES), F32)],
        in_specs=[row, vec, pl.BlockSpec((tm, d), lambda i: (jnp.maximum(i - first, 0), 0))],
        out_specs=[row, vec, pl.BlockSpec((8, LANES), lambda i: (0, 0))],
        compiler_params=_params("arbitrary"),
    )(h, gain, target)


def _sigmoid(z):
    return 1.0 / (1.0 + jnp.exp(-z))


def _ffn_up(xn, wg, wu, name):
    m, d = xn.shape
    ff = wg.shape[1]
    tm, tn = _tile(m, 1056, 16), _tile(ff, 512, LANES)

    def body(x_ref, wg_ref, wu_ref, g_ref, u_ref, a_ref):
        xv = x_ref[...]
        g = _dot(xv, wg_ref[...])
        u = _dot(xv, wu_ref[...])
        g_ref[...] = g.astype(BF16)
        u_ref[...] = u.astype(BF16)
        a_ref[...] = (g * _sigmoid(g) * u).astype(BF16)

    out = pl.BlockSpec((tm, tn), lambda i, j: (i, j))
    w = pl.BlockSpec((d, tn), lambda i, j: (0, j))
    return pl.pallas_call(
        body, name=name, grid=(m // tm, ff // tn),
        out_shape=[jax.ShapeDtypeStruct((m, ff), BF16)] * 3,
        in_specs=[pl.BlockSpec((tm, d), lambda i, j: (i, 0)), w, w], out_specs=[out, out, out],
        compiler_params=_params("parallel", "parallel"),
    )(xn, wg, wu)


def _ffn_bwd_act(dh, wd, g, u, name):
    m, d = dh.shape
    ff = wd.shape[0]
    tm, tn = _tile(m, 528, 16), _tile(ff, 512, LANES)

    def body(dh_ref, wd_ref, g_ref, u_ref, dg_ref, du_ref):
        dact = 0.5 * _dot(dh_ref[...].astype(BF16), wd_ref[...], NT_DIMS)
        gv = g_ref[...].astype(F32)
        uv = u_ref[...].astype(F32)
        sig = _sigmoid(gv)
        du_ref[...] = (dact * gv * sig).astype(BF16)
        dg_ref[...] = (dact * uv * sig * (1.0 + gv * (1.0 - sig))).astype(BF16)

    blk = pl.BlockSpec((tm, tn), lambda i, j: (i, j))
    return pl.pallas_call(
        body, name=name, grid=(m // tm, ff // tn),
        out_shape=[jax.ShapeDtypeStruct((m, ff), BF16)] * 2,
        in_specs=[pl.BlockSpec((tm, d), lambda i, j: (i, 0)), pl.BlockSpec((tn, d), lambda i, j: (j, 0)), blk, blk],
        out_specs=[blk, blk],
        compiler_params=_params("parallel", "parallel"),
    )(dh, wd, g, u)


def _dot3(tri, x):
    h1 = x.astype(BF16)
    r1 = x - h1.astype(F32)
    h2 = r1.astype(BF16)
    h3 = (r1 - h2.astype(F32)).astype(BF16)
    return _dot(tri, h1) + _dot(tri, h2) + _dot(tri, h3)


def _log_sigmoid(z):
    return jnp.minimum(z, 0.0) - jnp.log(1.0 + jnp.exp(-jnp.abs(z)))


def _forget_cumsum(fl, bias, n_heads, pad, name):
    m = fl.shape[0]
    nb = m // ROW_BLOCK

    def body(fl_ref, b_ref, c_ref):
        tri = (lax.broadcasted_iota(jnp.int32, (ROW_BLOCK, ROW_BLOCK), 0)
               >= lax.broadcasted_iota(jnp.int32, (ROW_BLOCK, ROW_BLOCK), 1)).astype(BF16)
        lane_ok = lax.broadcasted_iota(jnp.int32, (ROW_BLOCK, LANES), 1) < n_heads
        rows = lax.broadcasted_iota(jnp.int32, (ROW_BLOCK, LANES), 0)

        def step(b, carry):
            off = pl.multiple_of(b * ROW_BLOCK, ROW_BLOCK)
            lf = _log_sigmoid(fl_ref[pl.ds(off, ROW_BLOCK), :] + b_ref[...])
            lf = jnp.where(lane_ok & (rows + off >= pad), lf, 0.0)
            cs = _dot3(tri, lf) + carry
            c_ref[pl.ds(off, ROW_BLOCK), :] = cs
            return cs[ROW_BLOCK - 1:ROW_BLOCK, :]

        lax.fori_loop(0, nb, step, jnp.zeros((1, LANES), F32))

    vmem = pl.BlockSpec(memory_space=pltpu.VMEM)
    return pl.pallas_call(
        body, name=name, out_shape=jax.ShapeDtypeStruct((m, LANES), F32),
        in_specs=[vmem, vmem], out_specs=vmem,
        compiler_params=pltpu.CompilerParams(vmem_limit_bytes=VMEM_LIMIT_BYTES),
    )(fl, bias)


def _forget_cumsum_bwd(dc_a, dc_b, fl, bias, n_heads, pad, name):
    m = fl.shape[0]
    nb = m // ROW_BLOCK

    def body(da_ref, db_ref, fl_ref, b_ref, dfl_ref, dbias_ref):
        tri = (lax.broadcasted_iota(jnp.int32, (ROW_BLOCK, ROW_BLOCK), 0)
               <= lax.broadcasted_iota(jnp.int32, (ROW_BLOCK, ROW_BLOCK), 1)).astype(BF16)
        lane_ok = lax.broadcasted_iota(jnp.int32, (ROW_BLOCK, LANES), 1) < n_heads
        rows = lax.broadcasted_iota(jnp.int32, (ROW_BLOCK, LANES), 0)

        def step(bb, carry):
            tail, dbias = carry
            off = pl.multiple_of((nb - 1 - bb) * ROW_BLOCK, ROW_BLOCK)
            dc = da_ref[pl.ds(off, ROW_BLOCK), :] + db_ref[pl.ds(off, ROW_BLOCK), :]
            dlf = _dot3(tri, dc) + tail
            z = fl_ref[pl.ds(off, ROW_BLOCK), :] + b_ref[...]
            dfl = jnp.where(lane_ok & (rows + off >= pad), dlf * _sigmoid(-z), 0.0)
            dfl_ref[pl.ds(off, ROW_BLOCK), :] = dfl
            return dlf[0:1, :], dbias + jnp.sum(dfl, axis=0, keepdims=True)

        zero = jnp.zeros((1, LANES), F32)
        _, dbias = lax.fori_loop(0, nb, step, (zero, zero))
        dbias_ref[...] = dbias

    vmem = pl.BlockSpec(memory_space=pltpu.VMEM)
    return pl.pallas_call(
        body, name=name,
        out_shape=[jax.ShapeDtypeStruct((m, LANES), F32), jax.ShapeDtypeStruct((1, LANES), F32)],
        in_specs=[vmem] * 4, out_specs=[vmem, vmem],
        compiler_params=pltpu.CompilerParams(vmem_limit_bytes=VMEM_LIMIT_BYTES),
    )(dc_a, dc_b, fl, bias)


def _head_norm(o, gain):
    r = lax.rsqrt(jnp.mean(o * o, axis=-1, keepdims=True) + EPS)
    return o * r * gain


def _head_norm_bwd(o, d_on, gain):
    r = lax.rsqrt(jnp.mean(o * o, axis=-1, keepdims=True) + EPS)
    ohat = o * r
    t = d_on * gain
    d_o = r * (t - ohat * jnp.mean(t * ohat, axis=-1, keepdims=True))
    return d_o, jnp.sum(d_on * ohat, axis=0, keepdims=True)


def _qkv_specs(t, m, h, first_col_block):
    q = pl.BlockSpec((t, HEAD_DIM), lambda hd, i: (i, first_col_block + hd))
    k = pl.BlockSpec((m, HEAD_DIM), lambda hd, i: (0, first_col_block + h + hd))
    v = pl.BlockSpec((m, HEAD_DIM), lambda hd, i: (0, first_col_block + 2 * h + hd))
    return q, k, v


def _fox_fwd(qkv, ccol, crow, gain, n_heads, pad, name):
    m = qkv.shape[0]
    t = ROW_BLOCK
    nq = m // t
    scale = HEAD_DIM ** -0.5
    hw = n_heads * HEAD_DIM

    def body(q_ref, k_ref, v_ref, ccol_ref, crow_ref, g_ref, o_ref, on_ref, lse_ref):
        i = pl.program_id(1)
        q = q_ref[...]
        ci = ccol_ref[...]
        qpos = i * t + lax.broadcasted_iota(jnp.int32, (t, 1), 0)

        def step(j, carry):
            mx, l, acc = carry
            off = pl.multiple_of(j * t, t)
            k = k_ref[pl.ds(off, t), :]
            v = v_ref[pl.ds(off, t), :]
            s = _dot(q, k, NT_DIMS) * scale + ci - crow_ref[j]
            kpos = off + lax.broadcasted_iota(jnp.int32, (1, t), 1)
            s = jnp.where((kpos <= qpos) & (kpos >= pad), s, NEG)
            mx_new = jnp.maximum(mx, jnp.max(s, axis=-1, keepdims=True))
            p = jnp.exp(s - mx_new)
            a = jnp.exp(mx - mx_new)
            return mx_new, a * l + jnp.sum(p, axis=-1, keepdims=True), a * acc + _dot(p.astype(BF16), v)

        init = (jnp.full((t, 1), NEG, F32), jnp.zeros((t, 1), F32), jnp.zeros((t, HEAD_DIM), F32))
        mx, l, acc = lax.fori_loop(0, i + 1, step, init)
        valid = qpos >= pad
        o = jnp.where(valid, acc / l, 0.0)
        o_ref[...] = o
        on_ref[...] = _head_norm(o, g_ref[...]).astype(BF16)
        lse_ref[...] = jnp.where(valid, mx + jnp.log(l), 0.0)

    q_spec, k_spec, v_spec = _qkv_specs(t, m, n_heads, 0)
    col = pl.BlockSpec((None, t, 1), lambda hd, i: (hd, i, 0))
    head = pl.BlockSpec((t, HEAD_DIM), lambda hd, i: (i, hd))
    return pl.pallas_call(
        body, name=name, grid=(n_heads, nq),
        out_shape=[jax.ShapeDtypeStruct((m, hw), F32), jax.ShapeDtypeStruct((m, hw), BF16),
                   jax.ShapeDtypeStruct((n_heads, m, 1), F32)],
        in_specs=[q_spec, k_spec, v_spec, col,
                  pl.BlockSpec((None, nq, 1, t), lambda hd, i: (hd, 0, 0, 0)),
                  pl.BlockSpec((1, HEAD_DIM), lambda hd, i: (0, hd))],
        out_specs=[head, head, col],
        compiler_params=_params("arbitrary", "arbitrary"),
    )(qkv, qkv, qkv, ccol, crow, gain)


def _fox_bwd(qkv, o, d_on, gain, lse, ccol, crow, n_heads, pad, name):
    m = qkv.shape[0]
    t = ROW_BLOCK
    nq = m // t
    scale = HEAD_DIM ** -0.5
    hw = n_heads * HEAD_DIM

    def body(q_ref, k_ref, v_ref, o_ref, don_ref, g_ref, lse_ref, ccol_ref, crow_ref,
             dq_ref, dk_ref, dv_ref, dg_ref, dccol_ref, dcrow_ref):
        i = pl.program_id(1)

        @pl.when(i == 0)
        def _():
            dk_ref[...] = jnp.zeros_like(dk_ref)
            dv_ref[...] = jnp.zeros_like(dv_ref)
            dg_ref[...] = jnp.zeros_like(dg_ref)
            dcrow_ref[...] = jnp.zeros_like(dcrow_ref)

        q = q_ref[...]
        o = o_ref[...]
        d_o, dgain = _head_norm_bwd(o, don_ref[...], g_ref[...])
        dg_ref[...] += dgain
        delta = jnp.sum(d_o * o, axis=-1, keepdims=True)
        d_ob = d_o.astype(BF16)
        ci = ccol_ref[...]
        lse_i = lse_ref[...]
        qpos = i * t + lax.broadcasted_iota(jnp.int32, (t, 1), 0)

        def step(j, carry):
            dq, dci = carry
            off = pl.multiple_of(j * t, t)
            k = k_ref[pl.ds(off, t), :]
            v = v_ref[pl.ds(off, t), :]
            s = _dot(q, k, NT_DIMS) * scale + ci - crow_ref[j]
            kpos = off + lax.broadcasted_iota(jnp.int32, (1, t), 1)
            ok = (kpos <= qpos) & (kpos >= pad)
            p = jnp.where(ok, jnp.exp(jnp.where(ok, s - lse_i, 0.0)), 0.0)
            ds = p * (_dot(d_ob, v, NT_DIMS) - delta)
            dsb = ds.astype(BF16)
            dk_ref[pl.ds(off, t), :] += _dot(dsb, q, TN_DIMS) * scale
            dv_ref[pl.ds(off, t), :] += _dot(p.astype(BF16), d_ob, TN_DIMS)
            dcrow_ref[j] -= jnp.sum(ds, axis=0, keepdims=True)
            return dq + _dot(dsb, k), dci + jnp.sum(ds, axis=-1, keepdims=True)

        dq, dci = lax.fori_loop(0, i + 1, step, (jnp.zeros((t, HEAD_DIM), F32), jnp.zeros((t, 1), F32)))
        dq_ref[...] = (dq * scale).astype(BF16)
        dccol_ref[...] = dci

    q_spec, k_spec, v_spec = _qkv_specs(t, m, n_heads, 0)
    col = pl.BlockSpec((None, t, 1), lambda hd, i: (hd, i, 0))
    rowc = pl.BlockSpec((None, nq, 1, t), lambda hd, i: (hd, 0, 0, 0))
    head = pl.BlockSpec((t, HEAD_DIM), lambda hd, i: (i, hd))
    whole = pl.BlockSpec((m, HEAD_DIM), lambda hd, i: (0, hd))
    gvec = pl.BlockSpec((1, HEAD_DIM), lambda hd, i: (0, hd))
    return pl.pallas_call(
        body, name=name, grid=(n_heads, nq),
        out_shape=[jax.ShapeDtypeStruct((m, hw), BF16), jax.ShapeDtypeStruct((m, hw), F32),
                   jax.ShapeDtypeStruct((m, hw), F32), jax.ShapeDtypeStruct((1, hw), F32),
                   jax.ShapeDtypeStruct((n_heads, m, 1), F32), jax.ShapeDtypeStruct((n_heads, nq, 1, t), F32)],
        in_specs=[q_spec, k_spec, v_spec, head, head, gvec, col, col, rowc],
        out_specs=[head, whole, whole, gvec, col, rowc],
        compiler_params=_params("arbitrary", "arbitrary"),
    )(qkv, qkv, qkv, o, d_on, gain, lse, ccol, crow)


def _sb_scores(q, k, scale, ok):
    z = _dot(q, k, NT_DIMS) * scale
    e = jnp.exp(-jnp.abs(z))
    lp = jnp.log(1.0 + e)
    ls_pos = jnp.minimum(z, 0.0) - lp
    log_1m = jnp.where(ok, jnp.minimum(-z, 0.0) - lp, 0.0)
    return z, e, ls_pos, log_1m


def _sb_fwd(qkv, gain, n_heads, pad, name):
    m = qkv.shape[0]
    t = ROW_BLOCK
    nq = m // t
    assert nq <= LANES
    scale = HEAD_DIM ** -0.5
    hw = n_heads * HEAD_DIM

    def body(q_ref, k_ref, v_ref, g_ref, o_ref, on_ref, run_ref):
        i = pl.program_id(1)
        q = q_ref[...]
        qpos = i * t + lax.broadcasted_iota(jnp.int32, (t, 1), 0)
        after = (lax.broadcasted_iota(jnp.int32, (t, t), 0) > lax.broadcasted_iota(jnp.int32, (t, t), 1)).astype(BF16)
        lane = lax.broadcasted_iota(jnp.int32, (t, LANES), 1)

        def step(jj, carry):
            run, acc = carry
            j = i - jj
            off = pl.multiple_of(j * t, t)
            k = k_ref[pl.ds(off, t), :]
            v = v_ref[pl.ds(off, t), :]
            kpos = off + lax.broadcasted_iota(jnp.int32, (1, t), 1)
            ok = (kpos < qpos) & (kpos >= pad)
            _, _, ls_pos, log_1m = _sb_scores(q, k, scale, ok)
            later = _dot_split(log_1m, after) + run
            a = jnp.where(ok, jnp.exp(ls_pos + later), 0.0)
            run_ref[...] = jnp.where(lane == j, run, run_ref[...])
            return run + jnp.sum(log_1m, axis=-1, keepdims=True), acc + _dot(a.astype(BF16), v)

        run_ref[...] = jnp.zeros_like(run_ref)
        _, o = lax.fori_loop(0, i + 1, step, (jnp.zeros((t, 1), F32), jnp.zeros((t, HEAD_DIM), F32)))
        o_ref[...] = o
        on_ref[...] = _head_norm(o, g_ref[...]).astype(BF16)

    q_spec, k_spec, v_spec = _qkv_specs(t, m, n_heads, 3 * n_heads)
    head = pl.BlockSpec((t, HEAD_DIM), lambda hd, i: (i, hd))
    return pl.pallas_call(
        body, name=name, grid=(n_heads, nq),
        out_shape=[jax.ShapeDtypeStruct((m, hw), F32), jax.ShapeDtypeStruct((m, hw), BF16),
                   jax.ShapeDtypeStruct((n_heads, m, LANES), F32)],
        in_specs=[q_spec, k_spec, v_spec, pl.BlockSpec((1, HEAD_DIM), lambda hd, i: (0, hd))],
        out_specs=[head, head, pl.BlockSpec((None, t, LANES), lambda hd, i: (hd, i, 0))],
        compiler_params=_params("arbitrary", "arbitrary"),
    )(qkv, qkv, qkv, gain)


def _sb_bwd(qkv, o, d_on, gain, runs, n_heads, pad, name):
    m = qkv.shape[0]
    t = ROW_BLOCK
    nq = m // t
    scale = HEAD_DIM ** -0.5
    hw = n_heads * HEAD_DIM

    def body(q_ref, k_ref, v_ref, o_ref, don_ref, g_ref, run_ref, dq_ref, dk_ref, dv_ref, dg_ref):
        i = pl.program_id(1)

        @pl.when(i == 0)
        def _():
            dk_ref[...] = jnp.zeros_like(dk_ref)
            dv_ref[...] = jnp.zeros_like(dv_ref)
            dg_ref[...] = jnp.zeros_like(dg_ref)

        q = q_ref[...]
        d_o, dgain = _head_norm_bwd(o_ref[...], don_ref[...], g_ref[...])
        dg_ref[...] += dgain
        d_ob = d_o.astype(BF16)
        runs_i = run_ref[...]
        qpos = i * t + lax.broadcasted_iota(jnp.int32, (t, 1), 0)
        r_idx = lax.broadcasted_iota(jnp.int32, (t, t), 0)
        c_idx = lax.broadcasted_iota(jnp.int32, (t, t), 1)
        after = (r_idx > c_idx).astype(BF16)
        before = (r_idx < c_idx).astype(BF16)
        lane = lax.broadcasted_iota(jnp.int32, (t, LANES), 1)

        def step(j, carry):
            g_run, dq = carry
            off = pl.multiple_of(j * t, t)
            k = k_ref[pl.ds(off, t), :]
            v = v_ref[pl.ds(off, t), :]
            kpos = off + lax.broadcasted_iota(jnp.int32, (1, t), 1)
            ok = (kpos < qpos) & (kpos >= pad)
            z, e, ls_pos, log_1m = _sb_scores(q, k, scale, ok)
            run = jnp.sum(jnp.where(lane == j, runs_i, 0.0), axis=-1, keepdims=True)
            later = _dot_split(log_1m, after) + run
            a = jnp.where(ok, jnp.exp(ls_pos + later), 0.0)
            g = a * _dot(d_ob, v, NT_DIMS)
            prefix = _dot_split(g, before) + g_run
            rcp = 1.0 / (1.0 + e)
            beta = jnp.where(z >= 0.0, rcp, e * rcp)
            one_m = jnp.where(z >= 0.0, e * rcp, rcp)
            dz = jnp.where(ok, g * one_m - beta * prefix, 0.0)
            dzb = dz.astype(BF16)
            dk_ref[pl.ds(off, t), :] += _dot(dzb, q, TN_DIMS) * scale
            dv_ref[pl.ds(off, t), :] += _dot(a.astype(BF16), d_ob, TN_DIMS)
            return g_run + jnp.sum(g, axis=-1, keepdims=True), dq + _dot(dzb, k)

        _, dq = lax.fori_loop(0, i + 1, step, (jnp.zeros((t, 1), F32), jnp.zeros((t, HEAD_DIM), F32)))
        dq_ref[...] = (dq * scale).astype(BF16)

    q_spec, k_spec, v_spec = _qkv_specs(t, m, n_heads, 3 * n_heads)
    head = pl.BlockSpec((t, HEAD_DIM), lambda hd, i: (i, hd))
    whole = pl.BlockSpec((m, HEAD_DIM), lambda hd, i: (0, hd))
    gvec = pl.BlockSpec((1, HEAD_DIM), lambda hd, i: (0, hd))
    return pl.pallas_call(
        body, name=name, grid=(n_heads, nq),
        out_shape=[jax.ShapeDtypeStruct((m, hw), BF16), jax.ShapeDtypeStruct((m, hw), F32),
                   jax.ShapeDtypeStruct((m, hw), F32), jax.ShapeDtypeStruct((1, hw), F32)],
        in_specs=[q_spec, k_spec, v_spec, head, head, gvec, pl.BlockSpec((None, t, LANES), lambda hd, i: (hd, i, 0))],
        out_specs=[head, whole, whole, gvec],
        compiler_params=_params("arbitrary", "arbitrary"),
    )(qkv, qkv, qkv, o, d_on, gain, runs)


def _adamw(parts, w, m1, v2, name):
    n_parts, nl, r, c = parts.shape
    lanes_padded = -(-c // LANES) * LANES
    tr = _tile(r, max(8, (128 * 1024) // lanes_padded), 8)
    bias1 = 1.0 / (1.0 - ADAM_B1 ** ADAM_STEP)
    bias2 = 1.0 / (1.0 - ADAM_B2 ** ADAM_STEP)

    def body(p_ref, w_ref, m_ref, v_ref, g_ref, d_ref, nm_ref, nv_ref):
        g = p_ref[0]
        for s in range(1, n_parts):
            g = g + p_ref[s]
        m_new = ADAM_B1 * m_ref[...] + (1.0 - ADAM_B1) * g
        v_new = ADAM_B2 * v_ref[...] + (1.0 - ADAM_B2) * (g * g)
        g_ref[...] = g
        nm_ref[...] = m_new
        nv_ref[...] = v_new
        d_ref[...] = -ADAM_LR * ((m_new * bias1) / (jnp.sqrt(v_new * bias2) + ADAM_EPS) + ADAM_WD * w_ref[...])

    blk = pl.BlockSpec((None, tr, c), lambda l, i: (l, i, 0))
    return pl.pallas_call(
        body, name=name, grid=(nl, r // tr),
        out_shape=[jax.ShapeDtypeStruct((nl, r, c), F32)] * 4,
        in_specs=[pl.BlockSpec((n_parts, None, tr, c), lambda l, i: (0, l, i, 0)), blk, blk, blk],
        out_specs=[blk] * 4,
        compiler_params=_params("parallel", "parallel"),
    )(parts, w, m1, v2)


BIG_WEIGHTS = ("ffn1_w_gate", "ffn1_w_up", "ffn1_w_down", "w_in", "w_out", "ffn2_w_gate", "ffn2_w_up", "ffn2_w_down")
SMALL_WEIGHTS = ("ffn1_norm", "mix_norm", "b_forget", "g_fox", "g_sb", "ffn2_norm", "final_norm")
WEIGHT_ORDER = ("meta_tokens", "ffn1_norm", "ffn1_w_gate", "ffn1_w_up", "ffn1_w_down", "mix_norm", "w_in", "b_forget",
                "g_fox", "g_sb", "w_out", "ffn2_norm", "ffn2_w_gate", "ffn2_w_up", "ffn2_w_down", "final_norm")


def _pad_lanes(a):
    extra = (-a.shape[-1]) % LANES
    return a if extra == 0 else jnp.pad(a, [(0, 0)] * (a.ndim - 1) + [(0, extra)])


def _ffn_forward(h, gain, wg, wu, wd, tag):
    xn = _rms_fwd(h, gain, f"{tag}_norm")
    g, u, act = _ffn_up(xn, wg, wu, f"{tag}_up")
    h_out = _mm([(act, wd)], "nn", F32, name=f"{tag}_down", alpha=0.5, res=h)
    return h_out, (h, xn, g, u, act)


def _ffn_backward(dh_out, saved, gain, wg, wu, wd, tag):
    h, xn, g, u, act = saved
    dg, du = _ffn_bwd_act(dh_out, wd, g, u, f"{tag}_bwd_act")
    d_wd = _mm([(act, dh_out)], "tn", F32, name=f"{tag}_dwd", alpha=0.5)
    d_wg = _mm([(xn, dg)], "tn", F32, name=f"{tag}_dwg")
    d_wu = _mm([(xn, du)], "tn", F32, name=f"{tag}_dwu")
    dxn = _mm([(dg, wg), (du, wu)], "nt", F32, name=f"{tag}_dxn")
    dh, d_gain = _rms_bwd(dxn, h, gain, dh_out, f"{tag}_norm_bwd")
    return dh, d_gain, d_wg, d_wu, d_wd


def _mixer_forward(h, gain, w_in_pad, bias, g_fox, g_sb, w_out, n_heads, pad, tag):
    m = h.shape[0]
    nq = m // ROW_BLOCK
    hw = n_heads * HEAD_DIM
    xn = _rms_fwd(h, gain, f"{tag}_norm")
    qkv = _mm([(xn, w_in_pad[:, :6 * hw])], "nn", BF16, name=f"{tag}_qkv")
    fl = _mm([(xn, w_in_pad[:, 6 * hw:])], "nn", F32, name=f"{tag}_forget")
    c = _forget_cumsum(fl, bias, n_heads, pad, f"{tag}_cumsum")
    c_heads = c[:, :n_heads].T
    ccol = c_heads[:, :, None]
    crow = c_heads.reshape(n_heads, nq, 1, ROW_BLOCK)
    o_f, on_f, lse = _fox_fwd(qkv, ccol, crow, g_fox, n_heads, pad, f"{tag}_fox")
    o_s, on_s, runs = _sb_fwd(qkv, g_sb, n_heads, pad, f"{tag}_sb")
    h_out = _mm([(on_f, w_out[:hw]), (on_s, w_out[hw:])], "nn", F32, name=f"{tag}_out", res=h)
    return h_out, (h, xn, qkv, fl, ccol, crow, o_f, on_f, lse, o_s, on_s, runs)


def _mixer_backward(dh_out, saved, gain, w_in_pad, bias, g_fox, g_sb, w_out, n_heads, pad, tag):
    h, xn, qkv, fl, ccol, crow, o_f, on_f, lse, o_s, on_s, runs = saved
    m = h.shape[0]
    hw = n_heads * HEAD_DIM
    d_on_f = _mm([(dh_out, w_out[:hw])], "nt", F32, name=f"{tag}_don_f")
    d_on_s = _mm([(dh_out, w_out[hw:])], "nt", F32, name=f"{tag}_don_s")
    d_wout = jnp.concatenate([_mm([(on_f, dh_out)], "tn", F32, name=f"{tag}_dwout_f"),
                              _mm([(on_s, dh_out)], "tn", F32, name=f"{tag}_dwout_s")], axis=0)
    dq_f, dk_f, dv_f, dg_fox, dccol, dcrow = _fox_bwd(qkv, o_f, d_on_f, g_fox, lse, ccol, crow, n_heads, pad, f"{tag}_fox_bwd")
    dq_s, dk_s, dv_s, dg_sb = _sb_bwd(qkv, o_s, d_on_s, g_sb, runs, n_heads, pad, f"{tag}_sb_bwd")
    dc_a = _pad_lanes(dccol[:, :, 0].T)
    dc_b = _pad_lanes(dcrow.reshape(n_heads, m).T)
    dfl, dbias = _forget_cumsum_bwd(dc_a, dc_b, fl, bias, n_heads, pad, f"{tag}_cumsum_bwd")
    dproj = jnp.concatenate([dq_f, dk_f.astype(BF16), dv_f.astype(BF16), dq_s, dk_s.astype(BF16), dv_s.astype(BF16),
                             dfl.astype(BF16)], axis=1)
    d_win = _mm([(xn, dproj)], "tn", F32, name=f"{tag}_dwin")
    dxn = _mm([(dproj, w_in_pad)], "nt", F32, name=f"{tag}_dxn")
    dh, d_gain = _rms_bwd(dxn, h, gain, dh_out, f"{tag}_norm_bwd")
    return dh, d_gain, d_win, dbias, dg_fox, dg_sb, d_wout


def kernel(x, meta_tokens, ffn1_norm, ffn1_w_gate, ffn1_w_up, ffn1_w_down, mix_norm, w_in, b_forget, g_fox, g_sb, w_out, ffn2_norm, ffn2_w_gate, ffn2_w_up, ffn2_w_down, final_norm, loss_target, m_meta_tokens, m_ffn1_norm, m_ffn1_w_gate, m_ffn1_w_up, m_ffn1_w_down, m_mix_norm, m_w_in, m_b_forget, m_g_fox, m_g_sb, m_w_out, m_ffn2_norm, m_ffn2_w_gate, m_ffn2_w_up, m_ffn2_w_down, m_final_norm, v_meta_tokens, v_ffn1_norm, v_ffn1_w_gate, v_ffn1_w_up, v_ffn1_w_down, v_mix_norm, v_w_in, v_b_forget, v_g_fox, v_g_sb, v_w_out, v_ffn2_norm, v_ffn2_w_gate, v_ffn2_w_up, v_ffn2_w_down, v_final_norm):
    given = dict(locals())
    seq, d = x.shape[1], x.shape[2]
    depth = ffn1_norm.shape[0]
    ff = N_DEV * ffn1_w_gate.shape[2]
    d_in = N_DEV * w_in.shape[2]
    n_heads = g_fox.shape[1] // HEAD_DIM
    hw = n_heads * HEAD_DIM
    assert seq % ROW_BLOCK == 0 and d_in == 6 * hw + n_heads and n_heads <= LANES
    pad = (-(seq + N_META)) % ROW_BLOCK
    x_off = pad + N_META
    m = x_off + seq

    gathered = _all_gather([given[n].astype(BF16) for n in BIG_WEIGHTS] + [meta_tokens], "gather_weights")
    full = dict(zip(BIG_WEIGHTS + ("meta_tokens",), gathered))

    def columns(name, l):
        g = full[name][:, l]
        return g.transpose(1, 0, 2).reshape(g.shape[1], N_DEV * g.shape[2])

    def rows(name, l):
        g = full[name][:, l]
        return g.reshape(N_DEV * g.shape[1], g.shape[2])

    meta_full = full["meta_tokens"].transpose(1, 0, 2).reshape(N_META, d)
    layers = []
    for l in range(depth):
        win = columns("w_in", l)
        layers.append(dict(
            n1=ffn1_norm[l:l + 1], wg1=columns("ffn1_w_gate", l), wu1=columns("ffn1_w_up", l), wd1=rows("ffn1_w_down", l),
            nm=mix_norm[l:l + 1], w_in_pad=_pad_lanes(win), bias=_pad_lanes(b_forget[l:l + 1]),
            g_fox=g_fox[l:l + 1], g_sb=g_sb[l:l + 1], w_out=rows("w_out", l),
            n2=ffn2_norm[l:l + 1], wg2=columns("ffn2_w_gate", l), wu2=columns("ffn2_w_up", l), wd2=rows("ffn2_w_down", l)))

    h = jnp.concatenate([jnp.zeros((pad, d), F32), meta_full, x[0]], axis=0)
    saved = []
    for w in layers:
        h, s1 = _ffn_forward(h, w["n1"], w["wg1"], w["wu1"], w["wd1"], "ffn1")
        h, sm = _mixer_forward(h, w["nm"], w["w_in_pad"], w["bias"], w["g_fox"], w["g_sb"], w["w_out"], n_heads, pad, "mix")
        h, s2 = _ffn_forward(h, w["n2"], w["wg2"], w["wu2"], w["wd2"], "ffn2")
        saved.append((s1, sm, s2))

    dh, d_final, loss_arr = _loss_head(h, final_norm[None, :], loss_target[0], x_off, "loss_head")
    grads = {n: [None] * depth for n in WEIGHT_ORDER}
    for l in reversed(range(depth)):
        w = layers[l]
        s1, sm, s2 = saved[l]
        dh, grads["ffn2_norm"][l], grads["ffn2_w_gate"][l], grads["ffn2_w_up"][l], grads["ffn2_w_down"][l] = _ffn_backward(
            dh, s2, w["n2"], w["wg2"], w["wu2"], w["wd2"], "ffn2")
        dh, grads["mix_norm"][l], d_win, dbias, grads["g_fox"][l], grads["g_sb"][l], grads["w_out"][l] = _mixer_backward(
            dh, sm, w["nm"], w["w_in_pad"], w["bias"], w["g_fox"], w["g_sb"], w["w_out"], n_heads, pad, "mix")
        grads["w_in"][l] = d_win[:, :d_in]
        grads["b_forget"][l] = dbias[:, :n_heads]
        dh, grads["ffn1_norm"][l], grads["ffn1_w_gate"][l], grads["ffn1_w_up"][l], grads["ffn1_w_down"][l] = _ffn_backward(
            dh, s1, w["n1"], w["wg1"], w["wu1"], w["wd1"], "ffn1")
    grad_x = dh[x_off:][None]
    d_meta = dh[pad:x_off]

    def split_columns(name):
        g = jnp.stack(grads[name])
        return g.reshape(g.shape[0], g.shape[1], N_DEV, -1).transpose(2, 0, 1, 3)

    def split_rows(name):
        g = jnp.stack(grads[name])
        return g.reshape(g.shape[0], N_DEV, -1, g.shape[2]).transpose(1, 0, 2, 3)

    by_rows = ("ffn1_w_down", "w_out", "ffn2_w_down")
    parts = [split_rows(n) if n in by_rows else split_columns(n) for n in BIG_WEIGHTS]
    parts.append(d_meta.reshape(1, N_META, N_DEV, -1).transpose(2, 0, 1, 3))
    received = dict(zip(BIG_WEIGHTS + ("meta_tokens",), _exchange_partials(parts, "exchange_grads")))

    small = [loss_arr[0:1, :]] + [_pad_lanes(jnp.concatenate(grads[n], axis=0).reshape(1, -1)) for n in SMALL_WEIGHTS[:-1]]
    small.append(d_final)
    sizes = [a.shape[1] for a in small]
    summed = _all_reduce_small(jnp.concatenate(small, axis=1), "reduce_small")
    loss = summed[0, 0]

    def packed(prefix):
        cols = [jnp.zeros((1, LANES), F32)]
        cols += [_pad_lanes(given[prefix + n].reshape(1, -1)) for n in SMALL_WEIGHTS]
        return jnp.concatenate(cols, axis=1)[None]

    small_out = _adamw(summed[None, None], packed(""), packed("m_"), packed("v_"), "adamw_small")

    out = {}
    for n in BIG_WEIGHTS + ("meta_tokens",):
        wv, mv, vv = given[n], given["m_" + n], given["v_" + n]
        shape = wv.shape
        as3 = (lambda a: a[None]) if wv.ndim == 2 else (lambda a: a)
        res = _adamw(received[n], as3(wv), as3(mv), as3(vv), "adamw_" + n)
        out[n] = [r.reshape(shape) for r in res]
    offset = sizes[0]
    for n, size in zip(SMALL_WEIGHTS, sizes[1:]):
        shape = given[n].shape
        count = given[n].size
        out[n] = [r[0, 0, offset:offset + count].reshape(shape) for r in small_out]
        offset += size

    result = [loss, grad_x]
    for k in range(4):
        result += [out[n][k] for n in WEIGHT_ORDER]
    return tuple(result)
```

```python
import functools

import jax
import jax.numpy as jnp
from jax import lax
from jax.experimental import pallas as pl
from jax.experimental.pallas import tpu as pltpu

F32 = jnp.float32
BF16 = jnp.bfloat16

N_DEV = 8
N_META = 16
HEAD_DIM = 128
ROW_BLOCK = 128
LANES = 128
EPS = 1e-6
NEG = -1e30
ADAM_LR = 0.001
ADAM_B1 = 0.9
ADAM_B2 = 0.999
ADAM_EPS = 1e-08
ADAM_WD = 0.01
ADAM_STEP = 10
VMEM_LIMIT_BYTES = 56 * 1024 * 1024
MESH = pl.DeviceIdType.MESH

NT_DIMS = (((1,), (1,)), ((), ()))
TN_DIMS = (((0,), (0,)), ((), ()))
NN_DIMS = (((1,), (0,)), ((), ()))


def _tile(n, cap, align):
    best = None
    for d in range(align, min(n, cap) + 1, align):
        if n % d == 0:
            best = d
    return best if best is not None else n


def _params(*sem):
    return pltpu.CompilerParams(dimension_semantics=sem, vmem_limit_bytes=VMEM_LIMIT_BYTES)


def _dot(a, b, dims=NN_DIMS):
    return lax.dot_general(a, b, dims, preferred_element_type=F32)


def _dot_split(x, u):
    hi = x.astype(BF16)
    lo = (x - hi.astype(F32)).astype(BF16)
    return _dot(hi, u) + _dot(lo, u)


def _my_position():
    return lax.axis_index("x"), lax.axis_index("y"), lax.axis_index("c")


def _all_gather(arrs, name):
    n = len(arrs)

    def body(*refs):
        ins, outs = refs[:n], refs[n:2 * n]
        send_sems, recv_sems, local_sems = refs[2 * n:]
        x, y, c = _my_position()
        me, sibling = (x, y, c), (x, y, 1 - c)
        chips = [(1 - x, y), (x, 1 - y), (1 - x, 1 - y)]

        def copy(a, k, block, to, src=None):
            slot = outs[a].at[4 * block[0] + 2 * block[1] + block[2]]
            return pltpu.make_async_remote_copy(
                src_ref=slot if src is None else src, dst_ref=slot,
                send_sem=send_sems.at[a, k], recv_sem=recv_sems.at[a, k],
                device_id=to, device_id_type=MESH)

        started = []
        for a in range(n):
            mine = pltpu.make_async_copy(ins[a], outs[a].at[4 * x + 2 * y + c], local_sems.at[a])
            mine.start()
            started.append(mine)
        sends = []
        for a in range(n):
            first = [copy(a, 0, me, sibling, src=ins[a])]
            first += [copy(a, 1 + j, me, (*chip, c), src=ins[a]) for j, chip in enumerate(chips)]
            for cp in first:
                cp.start()
            sends += first
        for a in range(n):
            for j, chip in enumerate(chips):
                copy(a, 1 + j, (*chip, c), me).wait_recv()
                passed = copy(a, 4 + j, (*chip, c), sibling)
                passed.start()
                sends.append(passed)
        for a in range(n):
            copy(a, 0, sibling, me).wait_recv()
            for j, chip in enumerate(chips):
                copy(a, 4 + j, (*chip, 1 - c), me).wait_recv()
        for cp in sends:
            cp.wait_send()
        for mine in started:
            mine.wait()

    any_spec = pl.BlockSpec(memory_space=pl.ANY)
    return pl.pallas_call(
        body, name=name,
        out_shape=[jax.ShapeDtypeStruct((N_DEV,) + a.shape, a.dtype) for a in arrs],
        in_specs=[any_spec] * n, out_specs=[any_spec] * n,
        scratch_shapes=[pltpu.SemaphoreType.DMA((n, 7)), pltpu.SemaphoreType.DMA((n, 7)),
                        pltpu.SemaphoreType.DMA((n,))],
    )(*arrs)


def _exchange_partials(parts, name):
    n = len(parts)

    def body(*refs):
        ins, outs = refs[:n], refs[n:2 * n]
        send_sems, recv_sems, local_sems = refs[2 * n:]
        x, y, c = _my_position()
        me = 4 * x + 2 * y + c

        def peer_of(r):
            return (x ^ ((r >> 2) & 1), y ^ ((r >> 1) & 1), c ^ (r & 1))

        def copy(a, r):
            px, py, pc = peer_of(r)
            return pltpu.make_async_remote_copy(
                src_ref=ins[a].at[4 * px + 2 * py + pc], dst_ref=outs[a].at[me],
                send_sem=send_sems.at[a, r - 1], recv_sem=recv_sems.at[a, r - 1],
                device_id=(px, py, pc), device_id_type=MESH)

        def arrival(a, r):
            px, py, pc = peer_of(r)
            slot = outs[a].at[4 * px + 2 * py + pc]
            return pltpu.make_async_remote_copy(
                src_ref=slot, dst_ref=slot, send_sem=send_sems.at[a, r - 1], recv_sem=recv_sems.at[a, r - 1],
                device_id=(px, py, pc), device_id_type=MESH)

        local = []
        for a in range(n):
            mine = pltpu.make_async_copy(ins[a].at[me], outs[a].at[me], local_sems.at[a])
            mine.start()
            local.append(mine)
        sends = [copy(a, r) for a in range(n) for r in range(1, N_DEV)]
        for cp in sends:
            cp.start()
        for a in range(n):
            for r in range(1, N_DEV):
                arrival(a, r).wait_recv()
        for cp in sends:
            cp.wait_send()
        for mine in local:
            mine.wait()

    any_spec = pl.BlockSpec(memory_space=pl.ANY)
    return pl.pallas_call(
        body, name=name,
        out_shape=[jax.ShapeDtypeStruct(p.shape, p.dtype) for p in parts],
        in_specs=[any_spec] * n, out_specs=[any_spec] * n,
        scratch_shapes=[pltpu.SemaphoreType.DMA((n, 7)), pltpu.SemaphoreType.DMA((n, 7)),
                        pltpu.SemaphoreType.DMA((n,))],
    )(*parts)


def _all_reduce_small(vec, name):
    n = vec.shape[1]

    def body(v_ref, o_ref, buf, send_sems, recv_sems):
        x, y, c = _my_position()
        me = 4 * x + 2 * y + c

        def peer_of(r):
            return (x ^ ((r >> 2) & 1), y ^ ((r >> 1) & 1), c ^ (r & 1))

        def copy(r):
            px, py, pc = peer_of(r)
            return pltpu.make_async_remote_copy(
                src_ref=v_ref, dst_ref=buf.at[me], send_sem=send_sems.at[r - 1], recv_sem=recv_sems.at[r - 1],
                device_id=(px, py, pc), device_id_type=MESH)

        def arrival(r):
            px, py, pc = peer_of(r)
            slot = buf.at[4 * px + 2 * py + pc]
            return pltpu.make_async_remote_copy(
                src_ref=slot, dst_ref=slot, send_sem=send_sems.at[r - 1], recv_sem=recv_sems.at[r - 1],
                device_id=(px, py, pc), device_id_type=MESH)

        sends = [copy(r) for r in range(1, N_DEV)]
        for cp in sends:
            cp.start()
        buf[me] = v_ref[...]
        for r in range(1, N_DEV):
            arrival(r).wait_recv()
        for cp in sends:
            cp.wait_send()
        total = buf[0]
        for d in range(1, N_DEV):
            total = total + buf[d]
        o_ref[...] = total

    vmem = pl.BlockSpec(memory_space=pltpu.VMEM)
    return pl.pallas_call(
        body, name=name, out_shape=jax.ShapeDtypeStruct((1, n), F32),
        in_specs=[vmem], out_specs=vmem,
        scratch_shapes=[pltpu.VMEM((N_DEV, 1, n), F32), pltpu.SemaphoreType.DMA((7,)),
                        pltpu.SemaphoreType.DMA((7,))],
    )(vec)


def _mm(pairs, mode, out_dtype, *, name, tm=None, tn=None, tk=None, alpha=1.0, res=None):
    a0, b0 = pairs[0]
    if mode == "nn":
        (m, k), n = a0.shape, b0.shape[1]
    elif mode == "nt":
        (m, k), n = a0.shape, b0.shape[0]
    else:
        (k, m), n = a0.shape, b0.shape[1]
    dims = {"nn": NN_DIMS, "nt": NT_DIMS, "tn": TN_DIMS}[mode]
    tm = tm or _tile(m, 1056, LANES if mode == "tn" else 16)
    tn = tn or _tile(n, 1024, LANES)
    tk = tk or _tile(k, 2048 if mode != "tn" else 1056, LANES if mode != "tn" else 16)
    nk = k // tk
    npairs = len(pairs)
    a_spec = pl.BlockSpec((tk, tm), lambda i, j, kk: (kk, i)) if mode == "tn" else pl.BlockSpec((tm, tk), lambda i, j, kk: (i, kk))
    b_spec = pl.BlockSpec((tn, tk), lambda i, j, kk: (j, kk)) if mode == "nt" else pl.BlockSpec((tk, tn), lambda i, j, kk: (kk, j))
    o_spec = pl.BlockSpec((tm, tn), lambda i, j, kk: (i, j))

    def body(*refs):
        ab = refs[:2 * npairs]
        rest = refs[2 * npairs:]
        res_ref = rest[0] if res is not None else None
        o_ref = rest[1] if res is not None else rest[0]
        acc_ref = rest[-1] if nk > 1 else None
        part = None
        for p in range(npairs):
            d = _dot(ab[2 * p][...].astype(BF16), ab[2 * p + 1][...].astype(BF16), dims)
            part = d if part is None else part + d

        def finish(total):
            val = total * alpha if alpha != 1.0 else total
            if res_ref is not None:
                val = res_ref[...] + val
            o_ref[...] = val.astype(out_dtype)

        if nk == 1:
            finish(part)
        else:
            kk = pl.program_id(2)

            @pl.when(kk == 0)
            def _():
                acc_ref[...] = part

            @pl.when(kk > 0)
            def _():
                acc_ref[...] += part

            @pl.when(kk == nk - 1)
            def _():
                finish(acc_ref[...])

    operands, in_specs = [], []
    for a, b in pairs:
        operands += [a, b]
        in_specs += [a_spec, b_spec]
    if res is not None:
        operands.append(res)
        in_specs.append(o_spec)
    return pl.pallas_call(
        body, name=name, grid=(m // tm, n // tn, nk),
        out_shape=jax.ShapeDtypeStruct((m, n), out_dtype),
        in_specs=in_specs, out_specs=o_spec,
        scratch_shapes=[pltpu.VMEM((tm, tn), F32)] if nk > 1 else [],
        compiler_params=_params("parallel", "parallel", "arbitrary"),
    )(*operands)


def _rms_fwd(h, gain, name):
    m, d = h.shape
    tm = _tile(m, 528, 16)

    def body(h_ref, g_ref, o_ref):
        hv = h_ref[...]
        r = lax.rsqrt(jnp.mean(hv * hv, axis=-1, keepdims=True) + EPS)
        o_ref[...] = (hv * r * g_ref[...]).astype(BF16)

    return pl.pallas_call(
        body, name=name, grid=(m // tm,), out_shape=jax.ShapeDtypeStruct((m, d), BF16),
        in_specs=[pl.BlockSpec((tm, d), lambda i: (i, 0)), pl.BlockSpec((1, d), lambda i: (0, 0))],
        out_specs=pl.BlockSpec((tm, d), lambda i: (i, 0)),
        compiler_params=_params("parallel"),
    )(h, gain)


def _rms_bwd(dxn, h, gain, dres, name):
    m, d = h.shape
    tm = _tile(m, 264, 8)

    def body(dxn_ref, h_ref, g_ref, dres_ref, dh_ref, dg_ref):
        hv = h_ref[...]
        r = lax.rsqrt(jnp.mean(hv * hv, axis=-1, keepdims=True) + EPS)
        xhat = hv * r
        dxn_v = dxn_ref[...]
        t = dxn_v * g_ref[...]
        dh_ref[...] = dres_ref[...] + r * (t - xhat * jnp.mean(t * xhat, axis=-1, keepdims=True))
        part = jnp.sum(dxn_v * xhat, axis=0, keepdims=True)

        @pl.when(pl.program_id(0) == 0)
        def _():
            dg_ref[...] = part

        @pl.when(pl.program_id(0) > 0)
        def _():
            dg_ref[...] += part

    row = pl.BlockSpec((tm, d), lambda i: (i, 0))
    vec = pl.BlockSpec((1, d), lambda i: (0, 0))
    return pl.pallas_call(
        body, name=name, grid=(m // tm,),
        out_shape=[jax.ShapeDtypeStruct((m, d), F32), jax.ShapeDtypeStruct((1, d), F32)],
        in_specs=[row, row, vec, row], out_specs=[row, vec],
        compiler_params=_params("arbitrary"),
    )(dxn, h, gain, dres)


def _loss_head(h, gain, target, x_off, name):
    m, d = h.shape
    tm = ROW_BLOCK
    first = x_off // tm

    def body(h_ref, g_ref, t_ref, dh_ref, dg_ref, loss_ref):
        i = pl.program_id(0)

        @pl.when(i == 0)
        def _():
            dg_ref[...] = jnp.zeros_like(dg_ref)
            loss_ref[...] = jnp.zeros_like(loss_ref)

        @pl.when(i < first)
        def _():
            dh_ref[...] = jnp.zeros_like(dh_ref)

        @pl.when(i >= first)
        def _():
            hv = h_ref[...]
            g = g_ref[...]
            r = lax.rsqrt(jnp.mean(hv * hv, axis=-1, keepdims=True) + EPS)
            xhat = hv * r
            err = xhat * g - t_ref[...]
            loss_ref[...] += 0.5 * jnp.sum(jnp.mean(err * err, axis=-1, keepdims=True))
            dy = err * (1.0 / d)
            t = dy * g
            dh_ref[...] = r * (t - xhat * jnp.mean(t * xhat, axis=-1, keepdims=True))
            dg_ref[...] += jnp.sum(dy * xhat, axis=0, keepdims=True)

    row = pl.BlockSpec((tm, d), lambda i: (i, 0))
    vec = pl.BlockSpec((1, d), lambda i: (0, 0))
    return pl.pallas_call(
        body, name=name, grid=(m // tm,),
        out_shape=[jax.ShapeDtypeStruct((m, d), F32), jax.ShapeDtypeStruct((1, d), F32),
                   jax.ShapeDtypeStruct((8, LANES), F32)],
        in_specs=[row, vec, pl.BlockSpec((tm, d), lambda i: (jnp.maximum(i - first, 0), 0))],
        out_specs=[row, vec, pl.BlockSpec((8, LANES), lambda i: (0, 0))],
        compiler_params=_params("arbitrary"),
    )(h, gain, target)


def _sigmoid(z):
    return 1.0 / (1.0 + jnp.exp(-z))


def _ffn_up(xn, wg, wu, name):
    m, d = xn.shape
    ff = wg.shape[1]
    tm, tn = _tile(m, 1056, 16), _tile(ff, 512, LANES)

    def body(x_ref, wg_ref, wu_ref, g_ref, u_ref, a_ref):
        xv = x_ref[...]
        g = _dot(xv, wg_ref[...])
        u = _dot(xv, wu_ref[...])
        g_ref[...] = g.astype(BF16)
        u_ref[...] = u.astype(BF16)
        a_ref[...] = (g * _sigmoid(g) * u).astype(BF16)

    out = pl.BlockSpec((tm, tn), lambda i, j: (i, j))
    w = pl.BlockSpec((d, tn), lambda i, j: (0, j))
    return pl.pallas_call(
        body, name=name, grid=(m // tm, ff // tn),
        out_shape=[jax.ShapeDtypeStruct((m, ff), BF16)] * 3,
        in_specs=[pl.BlockSpec((tm, d), lambda i, j: (i, 0)), w, w], out_specs=[out, out, out],
        compiler_params=_params("parallel", "parallel"),
    )(xn, wg, wu)


def _ffn_bwd_act(dh, wd, g, u, name):
    m, d = dh.shape
    ff = wd.shape[0]
    tm, tn = _tile(m, 528, 16), _tile(ff, 512, LANES)

    def body(dh_ref, wd_ref, g_ref, u_ref, dg_ref, du_ref):
        dact = 0.5 * _dot(dh_ref[...].astype(BF16), wd_ref[...], NT_DIMS)
        gv = g_ref[...].astype(F32)
        uv = u_ref[...].astype(F32)
        sig = _sigmoid(gv)
        du_ref[...] = (dact * gv * sig).astype(BF16)
        dg_ref[...] = (dact * uv * sig * (1.0 + gv * (1.0 - sig))).astype(BF16)

    blk = pl.BlockSpec((tm, tn), lambda i, j: (i, j))
    return pl.pallas_call(
        body, name=name, grid=(m // tm, ff // tn),
        out_shape=[jax.ShapeDtypeStruct((m, ff), BF16)] * 2,
        in_specs=[pl.BlockSpec((tm, d), lambda i, j: (i, 0)), pl.BlockSpec((tn, d), lambda i, j: (j, 0)), blk, blk],
        out_specs=[blk, blk],
        compiler_params=_params("parallel", "parallel"),
    )(dh, wd, g, u)


def _dot3(tri, x):
    h1 = x.astype(BF16)
    r1 = x - h1.astype(F32)
    h2 = r1.astype(BF16)
    h3 = (r1 - h2.astype(F32)).astype(BF16)
    return _dot(tri, h1) + _dot(tri, h2) + _dot(tri, h3)


def _log_sigmoid(z):
    return jnp.minimum(z, 0.0) - jnp.log(1.0 + jnp.exp(-jnp.abs(z)))


def _forget_cumsum(fl, bias, n_heads, pad, name):
    m = fl.shape[0]
    nb = m // ROW_BLOCK

    def body(fl_ref, b_ref, c_ref):
        tri = (lax.broadcasted_iota(jnp.int32, (ROW_BLOCK, ROW_BLOCK), 0)
               >= lax.broadcasted_iota(jnp.int32, (ROW_BLOCK, ROW_BLOCK), 1)).astype(BF16)
        lane_ok = lax.broadcasted_iota(jnp.int32, (ROW_BLOCK, LANES), 1) < n_heads
        rows = lax.broadcasted_iota(jnp.int32, (ROW_BLOCK, LANES), 0)

        def step(b, carry):
            off = pl.multiple_of(b * ROW_BLOCK, ROW_BLOCK)
            lf = _log_sigmoid(fl_ref[pl.ds(off, ROW_BLOCK), :] + b_ref[...])
            lf = jnp.where(lane_ok & (rows + off >= pad), lf, 0.0)
            cs = _dot3(tri, lf) + carry
            c_ref[pl.ds(off, ROW_BLOCK), :] = cs
            return cs[ROW_BLOCK - 1:ROW_BLOCK, :]

        lax.fori_loop(0, nb, step, jnp.zeros((1, LANES), F32))

    vmem = pl.BlockSpec(memory_space=pltpu.VMEM)
    return pl.pallas_call(
        body, name=name, out_shape=jax.ShapeDtypeStruct((m, LANES), F32),
        in_specs=[vmem, vmem], out_specs=vmem,
        compiler_params=pltpu.CompilerParams(vmem_limit_bytes=VMEM_LIMIT_BYTES),
    )(fl, bias)


def _forget_cumsum_bwd(dc_a, dc_b, fl, bias, n_heads, pad, name):
    m = fl.shape[0]
    nb = m // ROW_BLOCK

    def body(da_ref, db_ref, fl_ref, b_ref, dfl_ref, dbias_ref):
        tri = (lax.broadcasted_iota(jnp.int32, (ROW_BLOCK, ROW_BLOCK), 0)
               <= lax.broadcasted_iota(jnp.int32, (ROW_BLOCK, ROW_BLOCK), 1)).astype(BF16)
        lane_ok = lax.broadcasted_iota(jnp.int32, (ROW_BLOCK, LANES), 1) < n_heads
        rows = lax.broadcasted_iota(jnp.int32, (ROW_BLOCK, LANES), 0)

        def step(bb, carry):
            tail, dbias = carry
            off = pl.multiple_of((nb - 1 - bb) * ROW_BLOCK, ROW_BLOCK)
            dc = da_ref[pl.ds(off, ROW_BLOCK), :] + db_ref[pl.ds(off, ROW_BLOCK), :]
            dlf = _dot3(tri, dc) + tail
            z = fl_ref[pl.ds(off, ROW_BLOCK), :] + b_ref[...]
            dfl = jnp.where(lane_ok & (rows + off >= pad), dlf * _sigmoid(-z), 0.0)
            dfl_ref[pl.ds(off, ROW_BLOCK), :] = dfl
            return dlf[0:1, :], dbias + jnp.sum(dfl, axis=0, keepdims=True)

        zero = jnp.zeros((1, LANES), F32)
        _, dbias = lax.fori_loop(0, nb, step, (zero, zero))
        dbias_ref[...] = dbias

    vmem = pl.BlockSpec(memory_space=pltpu.VMEM)
    return pl.pallas_call(
        body, name=name,
        out_shape=[jax.ShapeDtypeStruct((m, LANES), F32), jax.ShapeDtypeStruct((1, LANES), F32)],
        in_specs=[vmem] * 4, out_specs=[vmem, vmem],
        compiler_params=pltpu.CompilerParams(vmem_limit_bytes=VMEM_LIMIT_BYTES),
    )(dc_a, dc_b, fl, bias)


def _attn_block(m):
    return 3 * ROW_BLOCK if m % (3 * ROW_BLOCK) == 0 else ROW_BLOCK


def _triangle(t, cmp):
    return cmp(lax.broadcasted_iota(jnp.int32, (t, t), 0), lax.broadcasted_iota(jnp.int32, (t, t), 1)).astype(BF16)


def _head_norm(o, gain):
    r = lax.rsqrt(jnp.mean(o * o, axis=-1, keepdims=True) + EPS)
    return o * r * gain


def _head_norm_bwd(o, d_on, gain):
    r = lax.rsqrt(jnp.mean(o * o, axis=-1, keepdims=True) + EPS)
    ohat = o * r
    t = d_on * gain
    d_o = r * (t - ohat * jnp.mean(t * ohat, axis=-1, keepdims=True))
    return d_o, jnp.sum(d_on * ohat, axis=0, keepdims=True)


def _qkv_specs(t, m, h, first_col_block):
    q = pl.BlockSpec((t, HEAD_DIM), lambda hd, i: (i, first_col_block + hd))
    k = pl.BlockSpec((m, HEAD_DIM), lambda hd, i: (0, first_col_block + h + hd))
    v = pl.BlockSpec((m, HEAD_DIM), lambda hd, i: (0, first_col_block + 2 * h + hd))
    return q, k, v


def _fox_fwd(qkv, ccol, crow, gain, n_heads, pad, name):
    m = qkv.shape[0]
    t = _attn_block(m)
    nq = m // t
    scale = HEAD_DIM ** -0.5
    hw = n_heads * HEAD_DIM

    def body(q_ref, k_ref, v_ref, ccol_ref, crow_ref, g_ref, o_ref, on_ref, lse_ref):
        i = pl.program_id(1)
        q = q_ref[...]
        ci = ccol_ref[...]
        qpos = i * t + lax.broadcasted_iota(jnp.int32, (t, 1), 0)

        def step(j, carry):
            mx, l, acc = carry
            off = pl.multiple_of(j * t, t)
            k = k_ref[pl.ds(off, t), :]
            v = v_ref[pl.ds(off, t), :]
            s = _dot(q, k, NT_DIMS) * scale + ci - crow_ref[j]
            kpos = off + lax.broadcasted_iota(jnp.int32, (1, t), 1)
            s = jnp.where((kpos <= qpos) & (kpos >= pad), s, NEG)
            mx_new = jnp.maximum(mx, jnp.max(s, axis=-1, keepdims=True))
            p = jnp.exp(s - mx_new)
            a = jnp.exp(mx - mx_new)
            return mx_new, a * l + jnp.sum(p, axis=-1, keepdims=True), a * acc + _dot(p.astype(BF16), v)

        init = (jnp.full((t, 1), NEG, F32), jnp.zeros((t, 1), F32), jnp.zeros((t, HEAD_DIM), F32))
        mx, l, acc = lax.fori_loop(0, i + 1, step, init)
        valid = qpos >= pad
        o = jnp.where(valid, acc / l, 0.0)
        o_ref[...] = o
        on_ref[...] = _head_norm(o, g_ref[...]).astype(BF16)
        lse_ref[...] = jnp.where(valid, mx + jnp.log(l), 0.0)

    q_spec, k_spec, v_spec = _qkv_specs(t, m, n_heads, 0)
    col = pl.BlockSpec((None, t, 1), lambda hd, i: (hd, i, 0))
    head = pl.BlockSpec((t, HEAD_DIM), lambda hd, i: (i, hd))
    return pl.pallas_call(
        body, name=name, grid=(n_heads, nq),
        out_shape=[jax.ShapeDtypeStruct((m, hw), F32), jax.ShapeDtypeStruct((m, hw), BF16),
                   jax.ShapeDtypeStruct((n_heads, m, 1), F32)],
        in_specs=[q_spec, k_spec, v_spec, col,
                  pl.BlockSpec((None, nq, 1, t), lambda hd, i: (hd, 0, 0, 0)),
                  pl.BlockSpec((1, HEAD_DIM), lambda hd, i: (0, hd))],
        out_specs=[head, head, col],
        compiler_params=_params("arbitrary", "arbitrary"),
    )(qkv, qkv, qkv, ccol, crow, gain)


def _fox_bwd(qkv, o, d_on, gain, lse, ccol, crow, n_heads, pad, name):
    m = qkv.shape[0]
    t = _attn_block(m)
    nq = m // t
    scale = HEAD_DIM ** -0.5
    hw = n_heads * HEAD_DIM

    def body(q_ref, k_ref, v_ref, o_ref, don_ref, g_ref, lse_ref, ccol_ref, crow_ref,
             dq_ref, dk_ref, dv_ref, dg_ref, dccol_ref, dcrow_ref):
        i = pl.program_id(1)

        @pl.when(i == 0)
        def _():
            dk_ref[...] = jnp.zeros_like(dk_ref)
            dv_ref[...] = jnp.zeros_like(dv_ref)
            dg_ref[...] = jnp.zeros_like(dg_ref)
            dcrow_ref[...] = jnp.zeros_like(dcrow_ref)

        q = q_ref[...]
        o = o_ref[...]
        d_o, dgain = _head_norm_bwd(o, don_ref[...], g_ref[...])
        dg_ref[...] += dgain
        delta = jnp.sum(d_o * o, axis=-1, keepdims=True)
        d_ob = d_o.astype(BF16)
        ci = ccol_ref[...]
        lse_i = lse_ref[...]
        qpos = i * t + lax.broadcasted_iota(jnp.int32, (t, 1), 0)

        def step(j, carry):
            dq, dci = carry
            off = pl.multiple_of(j * t, t)
            k = k_ref[pl.ds(off, t), :]
            v = v_ref[pl.ds(off, t), :]
            s = _dot(q, k, NT_DIMS) * scale + ci - crow_ref[j]
            kpos = off + lax.broadcasted_iota(jnp.int32, (1, t), 1)
            ok = (kpos <= qpos) & (kpos >= pad)
            p = jnp.where(ok, jnp.exp(jnp.where(ok, s - lse_i, 0.0)), 0.0)
            ds = p * (_dot(d_ob, v, NT_DIMS) - delta)
            dsb = ds.astype(BF16)
            dk_ref[pl.ds(off, t), :] += _dot(dsb, q, TN_DIMS) * scale
            dv_ref[pl.ds(off, t), :] += _dot(p.astype(BF16), d_ob, TN_DIMS)
            dcrow_ref[j] -= jnp.sum(ds, axis=0, keepdims=True)
            return dq + _dot(dsb, k), dci + jnp.sum(ds, axis=-1, keepdims=True)

        dq, dci = lax.fori_loop(0, i + 1, step, (jnp.zeros((t, HEAD_DIM), F32), jnp.zeros((t, 1), F32)))
        dq_ref[...] = (dq * scale).astype(BF16)
        dccol_ref[...] = dci

    q_spec, k_spec, v_spec = _qkv_specs(t, m, n_heads, 0)
    col = pl.BlockSpec((None, t, 1), lambda hd, i: (hd, i, 0))
    rowc = pl.BlockSpec((None, nq, 1, t), lambda hd, i: (hd, 0, 0, 0))
    head = pl.BlockSpec((t, HEAD_DIM), lambda hd, i: (i, hd))
    whole = pl.BlockSpec((m, HEAD_DIM), lambda hd, i: (0, hd))
    gvec = pl.BlockSpec((1, HEAD_DIM), lambda hd, i: (0, hd))
    return pl.pallas_call(
        body, name=name, grid=(n_heads, nq),
        out_shape=[jax.ShapeDtypeStruct((m, hw), BF16), jax.ShapeDtypeStruct((m, hw), F32),
                   jax.ShapeDtypeStruct((m, hw), F32), jax.ShapeDtypeStruct((1, hw), F32),
                   jax.ShapeDtypeStruct((n_heads, m, 1), F32), jax.ShapeDtypeStruct((n_heads, nq, 1, t), F32)],
        in_specs=[q_spec, k_spec, v_spec, head, head, gvec, col, col, rowc],
        out_specs=[head, whole, whole, gvec, col, rowc],
        compiler_params=_params("arbitrary", "arbitrary"),
    )(qkv, qkv, qkv, o, d_on, gain, lse, ccol, crow)


def _sb_scores(q, k, scale, ok):
    z = _dot(q, k, NT_DIMS) * scale
    e = jnp.exp(-jnp.abs(z))
    lp = jnp.log(1.0 + e)
    ls_pos = jnp.minimum(z, 0.0) - lp
    log_1m = jnp.where(ok, jnp.minimum(-z, 0.0) - lp, 0.0)
    return z, e, ls_pos, log_1m


def _sb_fwd(qkv, gain, n_heads, pad, name):
    m = qkv.shape[0]
    t = _attn_block(m)
    nq = m // t
    assert nq <= LANES
    scale = HEAD_DIM ** -0.5
    hw = n_heads * HEAD_DIM

    def body(q_ref, k_ref, v_ref, g_ref, after_ref, o_ref, on_ref, run_ref):
        i = pl.program_id(1)
        q = q_ref[...]
        qpos = i * t + lax.broadcasted_iota(jnp.int32, (t, 1), 0)
        after = after_ref[...]
        lane = lax.broadcasted_iota(jnp.int32, (t, LANES), 1)

        def step(jj, carry):
            run, acc = carry
            j = i - jj
            off = pl.multiple_of(j * t, t)
            k = k_ref[pl.ds(off, t), :]
            v = v_ref[pl.ds(off, t), :]
            kpos = off + lax.broadcasted_iota(jnp.int32, (1, t), 1)
            ok = (kpos < qpos) & (kpos >= pad)
            _, _, ls_pos, log_1m = _sb_scores(q, k, scale, ok)
            later = _dot_split(log_1m, after) + run
            a = jnp.where(ok, jnp.exp(ls_pos + later), 0.0)
            run_ref[...] = jnp.where(lane == j, run, run_ref[...])
            return run + jnp.sum(log_1m, axis=-1, keepdims=True), acc + _dot(a.astype(BF16), v)

        run_ref[...] = jnp.zeros_like(run_ref)
        _, o = lax.fori_loop(0, i + 1, step, (jnp.zeros((t, 1), F32), jnp.zeros((t, HEAD_DIM), F32)))
        o_ref[...] = o
        on_ref[...] = _head_norm(o, g_ref[...]).astype(BF16)

    q_spec, k_spec, v_spec = _qkv_specs(t, m, n_heads, 3 * n_heads)
    head = pl.BlockSpec((t, HEAD_DIM), lambda hd, i: (i, hd))
    return pl.pallas_call(
        body, name=name, grid=(n_heads, nq),
        out_shape=[jax.ShapeDtypeStruct((m, hw), F32), jax.ShapeDtypeStruct((m, hw), BF16),
                   jax.ShapeDtypeStruct((n_heads, m, LANES), F32)],
        in_specs=[q_spec, k_spec, v_spec, pl.BlockSpec((1, HEAD_DIM), lambda hd, i: (0, hd)),
                  pl.BlockSpec((t, t), lambda hd, i: (0, 0))],
        out_specs=[head, head, pl.BlockSpec((None, t, LANES), lambda hd, i: (hd, i, 0))],
        compiler_params=_params("arbitrary", "arbitrary"),
    )(qkv, qkv, qkv, gain, _triangle(t, lambda r, c: r > c))


def _sb_bwd(qkv, o, d_on, gain, runs, n_heads, pad, name):
    m = qkv.shape[0]
    t = _attn_block(m)
    nq = m // t
    scale = HEAD_DIM ** -0.5
    hw = n_heads * HEAD_DIM

    def body(q_ref, k_ref, v_ref, o_ref, don_ref, g_ref, run_ref, after_ref, before_ref, dq_ref, dk_ref, dv_ref, dg_ref):
        i = pl.program_id(1)

        @pl.when(i == 0)
        def _():
            dk_ref[...] = jnp.zeros_like(dk_ref)
            dv_ref[...] = jnp.zeros_like(dv_ref)
            dg_ref[...] = jnp.zeros_like(dg_ref)

        q = q_ref[...]
        d_o, dgain = _head_norm_bwd(o_ref[...], don_ref[...], g_ref[...])
        dg_ref[...] += dgain
        d_ob = d_o.astype(BF16)
        runs_i = run_ref[...]
        qpos = i * t + lax.broadcasted_iota(jnp.int32, (t, 1), 0)
        after = after_ref[...]
        before = before_ref[...]
        lane = lax.broadcasted_iota(jnp.int32, (t, LANES), 1)

        def step(j, carry):
            g_run, dq = carry
            off = pl.multiple_of(j * t, t)
            k = k_ref[pl.ds(off, t), :]
            v = v_ref[pl.ds(off, t), :]
            kpos = off + lax.broadcasted_iota(jnp.int32, (1, t), 1)
            ok = (kpos < qpos) & (kpos >= pad)
            z, e, ls_pos, log_1m = _sb_scores(q, k, scale, ok)
            run = jnp.sum(jnp.where(lane == j, runs_i, 0.0), axis=-1, keepdims=True)
            later = _dot_split(log_1m, after) + run
            a = jnp.where(ok, jnp.exp(ls_pos + later), 0.0)
            g = a * _dot(d_ob, v, NT_DIMS)
            prefix = _dot_split(g, before) + g_run
            rcp = 1.0 / (1.0 + e)
            beta = jnp.where(z >= 0.0, rcp, e * rcp)
            one_m = jnp.where(z >= 0.0, e * rcp, rcp)
            dz = jnp.where(ok, g * one_m - beta * prefix, 0.0)
            dzb = dz.astype(BF16)
            dk_ref[pl.ds(off, t), :] += _dot(dzb, q, TN_DIMS) * scale
            dv_ref[pl.ds(off, t), :] += _dot(a.astype(BF16), d_ob, TN_DIMS)
            return g_run + jnp.sum(g, axis=-1, keepdims=True), dq + _dot(dzb, k)

        _, dq = lax.fori_loop(0, i + 1, step, (jnp.zeros((t, 1), F32), jnp.zeros((t, HEAD_DIM), F32)))
        dq_ref[...] = (dq * scale).astype(BF16)

    q_spec, k_spec, v_spec = _qkv_specs(t, m, n_heads, 3 * n_heads)
    head = pl.BlockSpec((t, HEAD_DIM), lambda hd, i: (i, hd))
    whole = pl.BlockSpec((m, HEAD_DIM), lambda hd, i: (0, hd))
    gvec = pl.BlockSpec((1, HEAD_DIM), lambda hd, i: (0, hd))
    tri = pl.BlockSpec((t, t), lambda hd, i: (0, 0))
    return pl.pallas_call(
        body, name=name, grid=(n_heads, nq),
        out_shape=[jax.ShapeDtypeStruct((m, hw), BF16), jax.ShapeDtypeStruct((m, hw), F32),
                   jax.ShapeDtypeStruct((m, hw), F32), jax.ShapeDtypeStruct((1, hw), F32)],
        in_specs=[q_spec, k_spec, v_spec, head, head, gvec, pl.BlockSpec((None, t, LANES), lambda hd, i: (hd, i, 0)),
                  tri, tri],
        out_specs=[head, whole, whole, gvec],
        compiler_params=_params("arbitrary", "arbitrary"),
    )(qkv, qkv, qkv, o, d_on, gain, runs, _triangle(t, lambda r, c: r > c), _triangle(t, lambda r, c: r < c))


def _adamw(parts, w, m1, v2, name):
    n_parts, nl, r, c = parts.shape
    lanes_padded = -(-c // LANES) * LANES
    tr = _tile(r, max(8, (128 * 1024) // lanes_padded), 8)
    bias1 = 1.0 / (1.0 - ADAM_B1 ** ADAM_STEP)
    bias2 = 1.0 / (1.0 - ADAM_B2 ** ADAM_STEP)

    def body(p_ref, w_ref, m_ref, v_ref, g_ref, d_ref, nm_ref, nv_ref):
        g = p_ref[0]
        for s in range(1, n_parts):
            g = g + p_ref[s]
        m_new = ADAM_B1 * m_ref[...] + (1.0 - ADAM_B1) * g
        v_new = ADAM_B2 * v_ref[...] + (1.0 - ADAM_B2) * (g * g)
        g_ref[...] = g
        nm_ref[...] = m_new
        nv_ref[...] = v_new
        d_ref[...] = -ADAM_LR * ((m_new * bias1) / (jnp.sqrt(v_new * bias2) + ADAM_EPS) + ADAM_WD * w_ref[...])

    blk = pl.BlockSpec((None, tr, c), lambda l, i: (l, i, 0))
    return pl.pallas_call(
        body, name=name, grid=(nl, r // tr),
        out_shape=[jax.ShapeDtypeStruct((nl, r, c), F32)] * 4,
        in_specs=[pl.BlockSpec((n_parts, None, tr, c), lambda l, i: (0, l, i, 0)), blk, blk, blk],
        out_specs=[blk] * 4,
        compiler_params=_params("parallel", "parallel"),
    )(parts, w, m1, v2)


BIG_WEIGHTS = ("ffn1_w_gate", "ffn1_w_up", "ffn1_w_down", "w_in", "w_out", "ffn2_w_gate", "ffn2_w_up", "ffn2_w_down")
SMALL_WEIGHTS = ("ffn1_norm", "mix_norm", "b_forget", "g_fox", "g_sb", "ffn2_norm", "final_norm")
WEIGHT_ORDER = ("meta_tokens", "ffn1_norm", "ffn1_w_gate", "ffn1_w_up", "ffn1_w_down", "mix_norm", "w_in", "b_forget",
                "g_fox", "g_sb", "w_out", "ffn2_norm", "ffn2_w_gate", "ffn2_w_up", "ffn2_w_down", "final_norm")


def _pad_lanes(a):
    extra = (-a.shape[-1]) % LANES
    return a if extra == 0 else jnp.pad(a, [(0, 0)] * (a.ndim - 1) + [(0, extra)])


def _ffn_forward(h, gain, wg, wu, wd, tag):
    xn = _rms_fwd(h, gain, f"{tag}_norm")
    g, u, act = _ffn_up(xn, wg, wu, f"{tag}_up")
    h_out = _mm([(act, wd)], "nn", F32, name=f"{tag}_down", alpha=0.5, res=h)
    return h_out, (h, xn, g, u, act)


def _ffn_backward(dh_out, saved, gain, wg, wu, wd, tag):
    h, xn, g, u, act = saved
    dg, du = _ffn_bwd_act(dh_out, wd, g, u, f"{tag}_bwd_act")
    d_wd = _mm([(act, dh_out)], "tn", F32, name=f"{tag}_dwd", alpha=0.5)
    d_wg = _mm([(xn, dg)], "tn", F32, name=f"{tag}_dwg")
    d_wu = _mm([(xn, du)], "tn", F32, name=f"{tag}_dwu")
    dxn = _mm([(dg, wg), (du, wu)], "nt", F32, name=f"{tag}_dxn")
    dh, d_gain = _rms_bwd(dxn, h, gain, dh_out, f"{tag}_norm_bwd")
    return dh, d_gain, d_wg, d_wu, d_wd


def _mixer_forward(h, gain, w_in_pad, bias, g_fox, g_sb, w_out, n_heads, pad, tag):
    m = h.shape[0]
    t = _attn_block(m)
    hw = n_heads * HEAD_DIM
    xn = _rms_fwd(h, gain, f"{tag}_norm")
    qkv = _mm([(xn, w_in_pad[:, :6 * hw])], "nn", BF16, name=f"{tag}_qkv")
    fl = _mm([(xn, w_in_pad[:, 6 * hw:])], "nn", F32, name=f"{tag}_forget")
    c = _forget_cumsum(fl, bias, n_heads, pad, f"{tag}_cumsum")
    c_heads = c[:, :n_heads].T
    ccol = c_heads[:, :, None]
    crow = c_heads.reshape(n_heads, m // t, 1, t)
    o_f, on_f, lse = _fox_fwd(qkv, ccol, crow, g_fox, n_heads, pad, f"{tag}_fox")
    o_s, on_s, runs = _sb_fwd(qkv, g_sb, n_heads, pad, f"{tag}_sb")
    h_out = _mm([(on_f, w_out[:hw]), (on_s, w_out[hw:])], "nn", F32, name=f"{tag}_out", res=h)
    return h_out, (h, xn, qkv, fl, ccol, crow, o_f, on_f, lse, o_s, on_s, runs)


def _mixer_backward(dh_out, saved, gain, w_in_pad, bias, g_fox, g_sb, w_out, n_heads, pad, tag):
    h, xn, qkv, fl, ccol, crow, o_f, on_f, lse, o_s, on_s, runs = saved
    m = h.shape[0]
    hw = n_heads * HEAD_DIM
    d_on_f = _mm([(dh_out, w_out[:hw])], "nt", F32, name=f"{tag}_don_f")
    d_on_s = _mm([(dh_out, w_out[hw:])], "nt", F32, name=f"{tag}_don_s")
    d_wout = jnp.concatenate([_mm([(on_f, dh_out)], "tn", F32, name=f"{tag}_dwout_f"),
                              _mm([(on_s, dh_out)], "tn", F32, name=f"{tag}_dwout_s")], axis=0)
    dq_f, dk_f, dv_f, dg_fox, dccol, dcrow = _fox_bwd(qkv, o_f, d_on_f, g_fox, lse, ccol, crow, n_heads, pad, f"{tag}_fox_bwd")
    dq_s, dk_s, dv_s, dg_sb = _sb_bwd(qkv, o_s, d_on_s, g_sb, runs, n_heads, pad, f"{tag}_sb_bwd")
    dc_a = _pad_lanes(dccol[:, :, 0].T)
    dc_b = _pad_lanes(dcrow.reshape(n_heads, m).T)
    dfl, dbias = _forget_cumsum_bwd(dc_a, dc_b, fl, bias, n_heads, pad, f"{tag}_cumsum_bwd")
    dproj = jnp.concatenate([dq_f, dk_f.astype(BF16), dv_f.astype(BF16), dq_s, dk_s.astype(BF16), dv_s.astype(BF16),
                             dfl.astype(BF16)], axis=1)
    d_win = _mm([(xn, dproj)], "tn", F32, name=f"{tag}_dwin")
    dxn = _mm([(dproj, w_in_pad)], "nt", F32, name=f"{tag}_dxn")
    dh, d_gain = _rms_bwd(dxn, h, gain, dh_out, f"{tag}_norm_bwd")
    return dh, d_gain, d_win, dbias, dg_fox, dg_sb, d_wout


def kernel(x, meta_tokens, ffn1_norm, ffn1_w_gate, ffn1_w_up, ffn1_w_down, mix_norm, w_in, b_forget, g_fox, g_sb, w_out, ffn2_norm, ffn2_w_gate, ffn2_w_up, ffn2_w_down, final_norm, loss_target, m_meta_tokens, m_ffn1_norm, m_ffn1_w_gate, m_ffn1_w_up, m_ffn1_w_down, m_mix_norm, m_w_in, m_b_forget, m_g_fox, m_g_sb, m_w_out, m_ffn2_norm, m_ffn2_w_gate, m_ffn2_w_up, m_ffn2_w_down, m_final_norm, v_meta_tokens, v_ffn1_norm, v_ffn1_w_gate, v_ffn1_w_up, v_ffn1_w_down, v_mix_norm, v_w_in, v_b_forget, v_g_fox, v_g_sb, v_w_out, v_ffn2_norm, v_ffn2_w_gate, v_ffn2_w_up, v_ffn2_w_down, v_final_norm):
    given = dict(locals())
    seq, d = x.shape[1], x.shape[2]
    depth = ffn1_norm.shape[0]
    ff = N_DEV * ffn1_w_gate.shape[2]
    d_in = N_DEV * w_in.shape[2]
    n_heads = g_fox.shape[1] // HEAD_DIM
    hw = n_heads * HEAD_DIM
    assert seq % ROW_BLOCK == 0 and d_in == 6 * hw + n_heads and n_heads <= LANES
    pad = (-(seq + N_META)) % ROW_BLOCK
    x_off = pad + N_META
    m = x_off + seq

    gathered = _all_gather([given[n].astype(BF16) for n in BIG_WEIGHTS] + [meta_tokens], "gather_weights")
    full = dict(zip(BIG_WEIGHTS + ("meta_tokens",), gathered))

    def columns(name, l):
        g = full[name][:, l]
        return g.transpose(1, 0, 2).reshape(g.shape[1], N_DEV * g.shape[2])

    def rows(name, l):
        g = full[name][:, l]
        return g.reshape(N_DEV * g.shape[1], g.shape[2])

    meta_full = full["meta_tokens"].transpose(1, 0, 2).reshape(N_META, d)
    layers = []
    for l in range(depth):
        win = columns("w_in", l)
        layers.append(dict(
            n1=ffn1_norm[l:l + 1], wg1=columns("ffn1_w_gate", l), wu1=columns("ffn1_w_up", l), wd1=rows("ffn1_w_down", l),
            nm=mix_norm[l:l + 1], w_in_pad=_pad_lanes(win), bias=_pad_lanes(b_forget[l:l + 1]),
            g_fox=g_fox[l:l + 1], g_sb=g_sb[l:l + 1], w_out=rows("w_out", l),
            n2=ffn2_norm[l:l + 1], wg2=columns("ffn2_w_gate", l), wu2=columns("ffn2_w_up", l), wd2=rows("ffn2_w_down", l)))

    h = jnp.concatenate([jnp.zeros((pad, d), F32), meta_full, x[0]], axis=0)
    saved = []
    for w in layers:
        h, s1 = _ffn_forward(h, w["n1"], w["wg1"], w["wu1"], w["wd1"], "ffn1")
        h, sm = _mixer_forward(h, w["nm"], w["w_in_pad"], w["bias"], w["g_fox"], w["g_sb"], w["w_out"], n_heads, pad, "mix")
        h, s2 = _ffn_forward(h, w["n2"], w["wg2"], w["wu2"], w["wd2"], "ffn2")
        saved.append((s1, sm, s2))

    dh, d_final, loss_arr = _loss_head(h, final_norm[None, :], loss_target[0], x_off, "loss_head")
    grads = {n: [None] * depth for n in WEIGHT_ORDER}
    for l in reversed(range(depth)):
        w = layers[l]
        s1, sm, s2 = saved[l]
        dh, grads["ffn2_norm"][l], grads["ffn2_w_gate"][l], grads["ffn2_w_up"][l], grads["ffn2_w_down"][l] = _ffn_backward(
            dh, s2, w["n2"], w["wg2"], w["wu2"], w["wd2"], "ffn2")
        dh, grads["mix_norm"][l], d_win, dbias, grads["g_fox"][l], grads["g_sb"][l], grads["w_out"][l] = _mixer_backward(
            dh, sm, w["nm"], w["w_in_pad"], w["bias"], w["g_fox"], w["g_sb"], w["w_out"], n_heads, pad, "mix")
        grads["w_in"][l] = d_win[:, :d_in]
        grads["b_forget"][l] = dbias[:, :n_heads]
        dh, grads["ffn1_norm"][l], grads["ffn1_w_gate"][l], grads["ffn1_w_up"][l], grads["ffn1_w_down"][l] = _ffn_backward(
            dh, s1, w["n1"], w["wg1"], w["wu1"], w["wd1"], "ffn1")
    grad_x = dh[x_off:][None]
    d_meta = dh[pad:x_off]

    def split_columns(name):
        g = jnp.stack(grads[name])
        return g.reshape(g.shape[0], g.shape[1], N_DEV, -1).transpose(2, 0, 1, 3)

    def split_rows(name):
        g = jnp.stack(grads[name])
        return g.reshape(g.shape[0], N_DEV, -1, g.shape[2]).transpose(1, 0, 2, 3)

    by_rows = ("ffn1_w_down", "w_out", "ffn2_w_down")
    parts = [split_rows(n) if n in by_rows else split_columns(n) for n in BIG_WEIGHTS]
    parts.append(d_meta.reshape(1, N_META, N_DEV, -1).transpose(2, 0, 1, 3))
    received = dict(zip(BIG_WEIGHTS + ("meta_tokens",), _exchange_partials(parts, "exchange_grads")))

    small = [loss_arr[0:1, :]] + [_pad_lanes(jnp.concatenate(grads[n], axis=0).reshape(1, -1)) for n in SMALL_WEIGHTS[:-1]]
    small.append(d_final)
    sizes = [a.shape[1] for a in small]
    summed = _all_reduce_small(jnp.concatenate(small, axis=1), "reduce_small")
    loss = summed[0, 0]

    def packed(prefix):
        cols = [jnp.zeros((1, LANES), F32)]
        cols += [_pad_lanes(given[prefix + n].reshape(1, -1)) for n in SMALL_WEIGHTS]
        return jnp.concatenate(cols, axis=1)[None]

    small_out = _adamw(summed[None, None], packed(""), packed("m_"), packed("v_"), "adamw_small")

    out = {}
    for n in BIG_WEIGHTS + ("meta_tokens",):
        wv, mv, vv = given[n], given["m_" + n], given["v_" + n]
        shape = wv.shape
        as3 = (lambda a: a[None]) if wv.ndim == 2 else (lambda a: a)
        res = _adamw(received[n], as3(wv), as3(mv), as3(vv), "adamw_" + n)
        out[n] = [r.reshape(shape) for r in res]
    offset = sizes[0]
    for n, size in zip(SMALL_WEIGHTS, sizes[1:]):
        shape = given[n].shape
        count = given[n].size
        out[n] = [r[0, 0, offset:offset + count].reshape(shape) for r in small_out]
        offset += size

    result = [loss, grad_x]
    for k in range(4):
        result += [out[n][k] for n in WEIGHT_ORDER]
    return tuple(result)
```

```python
import math

import jax
import jax.numpy as jnp
from jax import lax
from jax.experimental import pallas as pl
from jax.experimental.pallas import tpu as pltpu

F32 = jnp.float32
BF16 = jnp.bfloat16

N_DEV = 8
N_META = 16
HEAD_DIM = 128
ROW_BLOCK = 128
LANES = 128
EPS = 1e-6
NEG = -1e30
ADAM_LR = 0.001
ADAM_B1 = 0.9
ADAM_B2 = 0.999
ADAM_EPS = 1e-08
ADAM_WD = 0.01
ADAM_STEP = 10
VMEM_LIMIT_BYTES = 56 * 1024 * 1024
MESH = pl.DeviceIdType.MESH

NT_DIMS = (((1,), (1,)), ((), ()))
TN_DIMS = (((0,), (0,)), ((), ()))
NN_DIMS = (((1,), (0,)), ((), ()))
ANY = pl.BlockSpec(memory_space=pl.ANY)


def _tile(n, cap, align):
    best = None
    for d in range(align, min(n, cap) + 1, align):
        if n % d == 0:
            best = d
    return best if best is not None else n


def _dot(a, b, dims=NN_DIMS):
    return lax.dot_general(a, b, dims, preferred_element_type=F32)


def _dot_split(x, u):
    hi = x.astype(BF16)
    lo = (x - hi.astype(F32)).astype(BF16)
    return _dot(hi, u) + _dot(lo, u)


def _my_position():
    return lax.axis_index("x"), lax.axis_index("y"), lax.axis_index("c")


class _Gather:
    n_phases = 3

    def __init__(self, arrs):
        self.arrs = list(arrs)
        n = len(self.arrs)
        self.out_shapes = [jax.ShapeDtypeStruct((N_DEV,) + a.shape, a.dtype) for a in self.arrs]
        self.scratch = [pltpu.SemaphoreType.DMA((n, 7)), pltpu.SemaphoreType.DMA((n, 7)), pltpu.SemaphoreType.DMA((n,))]

    def phase(self, p, ins, outs, sems):
        send_sems, recv_sems, local_sems = sems
        n = len(self.arrs)
        x, y, c = _my_position()
        me, sibling = (x, y, c), (x, y, 1 - c)
        chips = [(1 - x, y), (x, 1 - y), (1 - x, 1 - y)]

        def copy(a, k, block, to, src=None):
            slot = outs[a].at[4 * block[0] + 2 * block[1] + block[2]]
            return pltpu.make_async_remote_copy(
                src_ref=slot if src is None else src, dst_ref=slot,
                send_sem=send_sems.at[a, k], recv_sem=recv_sems.at[a, k], device_id=to, device_id_type=MESH)

        def local(a):
            return pltpu.make_async_copy(ins[a], outs[a].at[4 * x + 2 * y + c], local_sems.at[a])

        def first(a):
            return [copy(a, 0, me, sibling, src=ins[a])] + [copy(a, 1 + j, me, (*chip, c), src=ins[a]) for j, chip in enumerate(chips)]

        def passed(a, j):
            return copy(a, 4 + j, (*chips[j], c), sibling)

        if p == 0:
            for a in range(n):
                local(a).start()
            for a in range(n):
                for cp in first(a):
                    cp.start()
        elif p == 1:
            for a in range(n):
                for j, chip in enumerate(chips):
                    copy(a, 1 + j, (*chip, c), me).wait_recv()
                    passed(a, j).start()
        else:
            for a in range(n):
                copy(a, 0, sibling, me).wait_recv()
                for j, chip in enumerate(chips):
                    copy(a, 4 + j, (*chip, 1 - c), me).wait_recv()
            for a in range(n):
                for cp in first(a) + [passed(a, j) for j in range(3)]:
                    cp.wait_send()
                local(a).wait()


class _Exchange:
    n_phases = 2

    def __init__(self, arrs):
        self.arrs = list(arrs)
        n = len(self.arrs)
        self.out_shapes = [jax.ShapeDtypeStruct(a.shape, a.dtype) for a in self.arrs]
        self.scratch = [pltpu.SemaphoreType.DMA((n, 7)), pltpu.SemaphoreType.DMA((n, 7)), pltpu.SemaphoreType.DMA((n,))]

    def phase(self, p, ins, outs, sems):
        send_sems, recv_sems, local_sems = sems
        n = len(self.arrs)
        x, y, c = _my_position()
        me = 4 * x + 2 * y + c

        def peer_of(r):
            return (x ^ ((r >> 2) & 1), y ^ ((r >> 1) & 1), c ^ (r & 1))

        def copy(a, r):
            px, py, pc = peer_of(r)
            return pltpu.make_async_remote_copy(
                src_ref=ins[a].at[4 * px + 2 * py + pc], dst_ref=outs[a].at[me],
                send_sem=send_sems.at[a, r - 1], recv_sem=recv_sems.at[a, r - 1],
                device_id=(px, py, pc), device_id_type=MESH)

        def arrival(a, r):
            px, py, pc = peer_of(r)
            slot = outs[a].at[4 * px + 2 * py + pc]
            return pltpu.make_async_remote_copy(
                src_ref=slot, dst_ref=slot, send_sem=send_sems.at[a, r - 1], recv_sem=recv_sems.at[a, r - 1],
                device_id=(px, py, pc), device_id_type=MESH)

        def local(a):
            return pltpu.make_async_copy(ins[a].at[me], outs[a].at[me], local_sems.at[a])

        if p == 0:
            for a in range(n):
                local(a).start()
            for a in range(n):
                for r in range(1, N_DEV):
                    copy(a, r).start()
        else:
            for a in range(n):
                for r in range(1, N_DEV):
                    arrival(a, r).wait_recv()
            for a in range(n):
                for r in range(1, N_DEV):
                    copy(a, r).wait_send()
                local(a).wait()


def _run_alone(comm, name):
    n = len(comm.arrs)

    def body(*refs):
        for p in range(comm.n_phases):
            comm.phase(p, refs[:n], refs[n:2 * n], refs[2 * n:])

    return pl.pallas_call(body, name=name, out_shape=comm.out_shapes, in_specs=[ANY] * n, out_specs=[ANY] * n,
                          scratch_shapes=comm.scratch)(*comm.arrs)


def _call(body, *, name, grid, in_specs, out_specs, out_shape, operands, scratch_shapes=(), comm=None):
    scratch_shapes = list(scratch_shapes)
    params = pltpu.CompilerParams(dimension_semantics=("arbitrary",) * len(grid), vmem_limit_bytes=VMEM_LIMIT_BYTES)
    if comm is None:
        res = pl.pallas_call(body, name=name, grid=grid, in_specs=in_specs, out_specs=out_specs, out_shape=out_shape,
                             scratch_shapes=scratch_shapes, compiler_params=params)(*operands)
        return res, None
    n_in, n_out, n_sc = len(in_specs), len(out_specs), len(scratch_shapes)
    nc = len(comm.arrs)
    total = math.prod(grid)
    at = {0: 0, comm.n_phases - 1: total - 1}
    for p in range(1, comm.n_phases - 1):
        at[p] = (total * p) // (comm.n_phases - 1)

    def wrapped(*refs):
        ins, cins = refs[:n_in], refs[n_in:n_in + nc]
        outs, couts = refs[n_in + nc:n_in + nc + n_out], refs[n_in + nc + n_out:n_in + 2 * nc + n_out]
        rest = refs[n_in + 2 * nc + n_out:]
        scratch, sems = rest[:n_sc], rest[n_sc:]
        step = 0
        for axis, size in enumerate(grid):
            step = step * size + pl.program_id(axis)
        for p in range(comm.n_phases - 1):
            @pl.when(step == at[p])
            def _(p=p):
                comm.phase(p, cins, couts, sems)
        body(*ins, *outs, *scratch)

        @pl.when(step == total - 1)
        def _():
            comm.phase(comm.n_phases - 1, cins, couts, sems)

    res = pl.pallas_call(
        wrapped, name=name, grid=grid, in_specs=list(in_specs) + [ANY] * nc, out_specs=list(out_specs) + [ANY] * nc,
        out_shape=list(out_shape) + comm.out_shapes, scratch_shapes=scratch_shapes + comm.scratch,
        compiler_params=params)(*operands, *comm.arrs)
    return res[:n_out], res[n_out:]


def _all_reduce_small(vec, name):
    n = vec.shape[1]

    def body(v_ref, o_ref, buf, send_sems, recv_sems):
        x, y, c = _my_position()
        me = 4 * x + 2 * y + c

        def peer_of(r):
            return (x ^ ((r >> 2) & 1), y ^ ((r >> 1) & 1), c ^ (r & 1))

        def copy(r):
            px, py, pc = peer_of(r)
            return pltpu.make_async_remote_copy(
                src_ref=v_ref, dst_ref=buf.at[me], send_sem=send_sems.at[r - 1], recv_sem=recv_sems.at[r - 1],
                device_id=(px, py, pc), device_id_type=MESH)

        def arrival(r):
            px, py, pc = peer_of(r)
            slot = buf.at[4 * px + 2 * py + pc]
            return pltpu.make_async_remote_copy(
                src_ref=slot, dst_ref=slot, send_sem=send_sems.at[r - 1], recv_sem=recv_sems.at[r - 1],
                device_id=(px, py, pc), device_id_type=MESH)

        sends = [copy(r) for r in range(1, N_DEV)]
        for cp in sends:
            cp.start()
        buf[me] = v_ref[...]
        for r in range(1, N_DEV):
            arrival(r).wait_recv()
        for cp in sends:
            cp.wait_send()
        total = buf[0]
        for d in range(1, N_DEV):
            total = total + buf[d]
        o_ref[...] = total

    vmem = pl.BlockSpec(memory_space=pltpu.VMEM)
    return pl.pallas_call(
        body, name=name, out_shape=jax.ShapeDtypeStruct((1, n), F32), in_specs=[vmem], out_specs=vmem,
        scratch_shapes=[pltpu.VMEM((N_DEV, 1, n), F32), pltpu.SemaphoreType.DMA((7,)), pltpu.SemaphoreType.DMA((7,))],
    )(vec)


def _mm_core(pairs, dims, out_dtype, *, name, grid, out_shape, out_spec, acc_shape, alpha=1.0, res=None):
    nk = grid[2]
    npairs = len(pairs)

    def body(*refs):
        ab = refs[:2 * npairs]
        rest = refs[2 * npairs:]
        res_ref = rest[0] if res is not None else None
        o_ref = rest[1] if res is not None else rest[0]
        acc_ref = rest[-1] if nk > 1 else None
        part = None
        for p in range(npairs):
            d = _dot(ab[2 * p][...].astype(BF16), ab[2 * p + 1][...].astype(BF16), dims)
            part = d if part is None else part + d

        def finish(total):
            val = total * alpha if alpha != 1.0 else total
            if res_ref is not None:
                val = res_ref[...] + val
            o_ref[...] = val.astype(out_dtype)

        if nk == 1:
            finish(part)
        else:
            kk = pl.program_id(2)

            @pl.when(kk == 0)
            def _():
                acc_ref[...] = part

            @pl.when(kk > 0)
            def _():
                acc_ref[...] += part

            @pl.when(kk == nk - 1)
            def _():
                finish(acc_ref[...])

    operands, in_specs = [], []
    for (a, a_spec), (b, b_spec) in pairs:
        operands += [a, b]
        in_specs += [a_spec, b_spec]
    if res is not None:
        operands.append(res[0])
        in_specs.append(res[1])
    out, _ = _call(body, name=name, grid=grid, in_specs=in_specs, out_specs=[out_spec],
                   out_shape=[jax.ShapeDtypeStruct(out_shape, out_dtype)], operands=operands,
                   scratch_shapes=[pltpu.VMEM(acc_shape, F32)] if nk > 1 else [])
    return out[0]


def _mm(pairs, mode, out_dtype, *, name, alpha=1.0, res=None):
    a0, b0 = pairs[0]
    if mode == "nn":
        (m, k), n = a0.shape, b0.shape[1]
    elif mode == "nt":
        (m, k), n = a0.shape, b0.shape[0]
    else:
        (k, m), n = a0.shape, b0.shape[1]
    dims = {"nn": NN_DIMS, "nt": NT_DIMS, "tn": TN_DIMS}[mode]
    tm = _tile(m, 1056, LANES if mode == "tn" else 16)
    tn = _tile(n, 1024, LANES)
    tk = _tile(k, 2048 if mode != "tn" else 1056, LANES if mode != "tn" else 16)
    a_spec = pl.BlockSpec((tk, tm), lambda i, j, kk: (kk, i)) if mode == "tn" else pl.BlockSpec((tm, tk), lambda i, j, kk: (i, kk))
    b_spec = pl.BlockSpec((tn, tk), lambda i, j, kk: (j, kk)) if mode == "nt" else pl.BlockSpec((tk, tn), lambda i, j, kk: (kk, j))
    o_spec = pl.BlockSpec((tm, tn), lambda i, j, kk: (i, j))
    return _mm_core([((a, a_spec), (b, b_spec)) for a, b in pairs], dims, out_dtype, name=name,
                    grid=(m // tm, n // tn, k // tk), out_shape=(m, n), out_spec=o_spec, acc_shape=(tm, tn),
                    alpha=alpha, res=None if res is None else (res, o_spec))


def _rms_fwd(h, gain, name):
    m, d = h.shape
    tm = _tile(m, 528, 16)

    def body(h_ref, g_ref, o_ref):
        hv = h_ref[...]
        r = lax.rsqrt(jnp.mean(hv * hv, axis=-1, keepdims=True) + EPS)
        o_ref[...] = (hv * r * g_ref[...]).astype(BF16)

    row = pl.BlockSpec((tm, d), lambda i: (i, 0))
    out, _ = _call(body, name=name, grid=(m // tm,), in_specs=[row, pl.BlockSpec((1, d), lambda i: (0, 0))],
                   out_specs=[row], out_shape=[jax.ShapeDtypeStruct((m, d), BF16)], operands=[h, gain])
    return out[0]


def _rms_bwd(dxn, h, gain, dres, name):
    m, d = h.shape
    tm = _tile(m, 264, 8)

    def body(dxn_ref, h_ref, g_ref, dres_ref, dh_ref, dg_ref):
        hv = h_ref[...]
        r = lax.rsqrt(jnp.mean(hv * hv, axis=-1, keepdims=True) + EPS)
        xhat = hv * r
        dxn_v = dxn_ref[...]
        t = dxn_v * g_ref[...]
        dh_ref[...] = dres_ref[...] + r * (t - xhat * jnp.mean(t * xhat, axis=-1, keepdims=True))
        part = jnp.sum(dxn_v * xhat, axis=0, keepdims=True)

        @pl.when(pl.program_id(0) == 0)
        def _():
            dg_ref[...] = part

        @pl.when(pl.program_id(0) > 0)
        def _():
            dg_ref[...] += part

    row = pl.BlockSpec((tm, d), lambda i: (i, 0))
    vec = pl.BlockSpec((1, d), lambda i: (0, 0))
    out, _ = _call(body, name=name, grid=(m // tm,), in_specs=[row, row, vec, row], out_specs=[row, vec],
                   out_shape=[jax.ShapeDtypeStruct((m, d), F32), jax.ShapeDtypeStruct((1, d), F32)],
                   operands=[dxn, h, gain, dres])
    return out


def _loss_head(h, gain, target, x_off, name):
    m, d = h.shape
    tm = ROW_BLOCK
    first = x_off // tm

    def body(h_ref, g_ref, t_ref, dh_ref, dg_ref, loss_ref):
        i = pl.program_id(0)

        @pl.when(i == 0)
        def _():
            dg_ref[...] = jnp.zeros_like(dg_ref)
            loss_ref[...] = jnp.zeros_like(loss_ref)

        @pl.when(i < first)
        def _():
            dh_ref[...] = jnp.zeros_like(dh_ref)

        @pl.when(i >= first)
        def _():
            hv = h_ref[...]
            g = g_ref[...]
            r = lax.rsqrt(jnp.mean(hv * hv, axis=-1, keepdims=True) + EPS)
            xhat = hv * r
            err = xhat * g - t_ref[...]
            loss_ref[...] += 0.5 * jnp.sum(jnp.mean(err * err, axis=-1, keepdims=True))
            dy = err * (1.0 / d)
            t = dy * g
            dh_ref[...] = r * (t - xhat * jnp.mean(t * xhat, axis=-1, keepdims=True))
            dg_ref[...] += jnp.sum(dy * xhat, axis=0, keepdims=True)

    row = pl.BlockSpec((tm, d), lambda i: (i, 0))
    vec = pl.BlockSpec((1, d), lambda i: (0, 0))
    out, _ = _call(body, name=name, grid=(m // tm,),
                   in_specs=[row, vec, pl.BlockSpec((tm, d), lambda i: (jnp.maximum(i - first, 0), 0))],
                   out_specs=[row, vec, pl.BlockSpec((8, LANES), lambda i: (0, 0))],
                   out_shape=[jax.ShapeDtypeStruct((m, d), F32), jax.ShapeDtypeStruct((1, d), F32),
                              jax.ShapeDtypeStruct((8, LANES), F32)],
                   operands=[h, gain, target])
    return out


def _sigmoid(z):
    return 1.0 / (1.0 + jnp.exp(-z))


def _ffn_up(xn, wg, wu, name, comm=None):
    m, d = xn.shape
    nsh, _, c = wg.shape
    tm = _tile(m, 1056, 16)

    def body(x_ref, wg_ref, wu_ref, g_ref, u_ref, a_ref):
        xv = x_ref[...]
        g = _dot(xv, wg_ref[...])
        u = _dot(xv, wu_ref[...])
        g_ref[...] = g.astype(BF16)
        u_ref[...] = u.astype(BF16)
        a_ref[...] = (g * _sigmoid(g) * u).astype(BF16)

    out = pl.BlockSpec((None, tm, c), lambda i, j: (j, i, 0))
    w = pl.BlockSpec((None, d, c), lambda i, j: (j, 0, 0))
    return _call(body, name=name, grid=(m // tm, nsh), in_specs=[pl.BlockSpec((tm, d), lambda i, j: (i, 0)), w, w],
                 out_specs=[out, out, out], out_shape=[jax.ShapeDtypeStruct((nsh, m, c), BF16)] * 3,
                 operands=[xn, wg, wu], comm=comm)


def _ffn_down(act, wd, h, name):
    nsh, m, c = act.shape
    d = wd.shape[2]
    tm, tn = _tile(m, 1056, 16), _tile(d, 1024, LANES)
    o_spec = pl.BlockSpec((tm, tn), lambda i, j, kk: (i, j))
    return _mm_core([((act, pl.BlockSpec((None, tm, c), lambda i, j, kk: (kk, i, 0))),
                      (wd, pl.BlockSpec((None, c, tn), lambda i, j, kk: (kk, 0, j))))],
                    NN_DIMS, F32, name=name, grid=(m // tm, d // tn, nsh), out_shape=(m, d), out_spec=o_spec,
                    acc_shape=(tm, tn), alpha=0.5, res=(h, o_spec))


def _ffn_bwd_act(dh, wd, g, u, name):
    m, d = dh.shape
    nsh, c, _ = wd.shape
    tm = _tile(m, 528, 16)

    def body(dh_ref, wd_ref, g_ref, u_ref, dg_ref, du_ref):
        dact = 0.5 * _dot(dh_ref[...].astype(BF16), wd_ref[...], NT_DIMS)
        gv = g_ref[...].astype(F32)
        uv = u_ref[...].astype(F32)
        sig = _sigmoid(gv)
        du_ref[...] = (dact * gv * sig).astype(BF16)
        dg_ref[...] = (dact * uv * sig * (1.0 + gv * (1.0 - sig))).astype(BF16)

    blk = pl.BlockSpec((None, tm, c), lambda i, j: (j, i, 0))
    out, _ = _call(body, name=name, grid=(m // tm, nsh),
                   in_specs=[pl.BlockSpec((tm, d), lambda i, j: (i, 0)), pl.BlockSpec((None, c, d), lambda i, j: (j, 0, 0)), blk, blk],
                   out_specs=[blk, blk], out_shape=[jax.ShapeDtypeStruct((nsh, m, c), BF16)] * 2, operands=[dh, wd, g, u])
    return out


def _ffn_dwd(act, dh, name):
    nsh, m, c = act.shape
    d = dh.shape[1]
    tn, tk = _tile(d, 1024, LANES), _tile(m, 1056, 16)
    return _mm_core([((act, pl.BlockSpec((None, tk, c), lambda i, j, kk: (i, kk, 0))),
                      (dh, pl.BlockSpec((tk, tn), lambda i, j, kk: (kk, j))))],
                    TN_DIMS, BF16, name=name, grid=(nsh, d // tn, m // tk), out_shape=(nsh, c, d),
                    out_spec=pl.BlockSpec((None, c, tn), lambda i, j, kk: (i, 0, j)), acc_shape=(c, tn), alpha=0.5)


def _ffn_dw_in(xn, dz, name):
    m, d = xn.shape
    nsh, _, c = dz.shape
    tm, tk = _tile(d, 1024, LANES), _tile(m, 1056, 16)
    return _mm_core([((xn, pl.BlockSpec((tk, tm), lambda i, j, kk: (kk, i))),
                      (dz, pl.BlockSpec((None, tk, c), lambda i, j, kk: (j, kk, 0))))],
                    TN_DIMS, BF16, name=name, grid=(d // tm, nsh, m // tk), out_shape=(nsh, d, c),
                    out_spec=pl.BlockSpec((None, tm, c), lambda i, j, kk: (j, i, 0)), acc_shape=(tm, c))


def _ffn_dxn(dg, wg, du, wu, name):
    nsh, m, c = dg.shape
    d = wg.shape[1]
    tm, tn = _tile(m, 1056, 16), _tile(d, 1024, LANES)
    a_spec = pl.BlockSpec((None, tm, c), lambda i, j, kk: (kk, i, 0))
    b_spec = pl.BlockSpec((None, tn, c), lambda i, j, kk: (kk, j, 0))
    return _mm_core([((dg, a_spec), (wg, b_spec)), ((du, a_spec), (wu, b_spec))], NT_DIMS, F32, name=name,
                    grid=(m // tm, d // tn, nsh), out_shape=(m, d), out_spec=pl.BlockSpec((tm, tn), lambda i, j, kk: (i, j)),
                    acc_shape=(tm, tn))


def _dot3(tri, x):
    h1 = x.astype(BF16)
    r1 = x - h1.astype(F32)
    h2 = r1.astype(BF16)
    h3 = (r1 - h2.astype(F32)).astype(BF16)
    return _dot(tri, h1) + _dot(tri, h2) + _dot(tri, h3)


def _log_sigmoid(z):
    return jnp.minimum(z, 0.0) - jnp.log(1.0 + jnp.exp(-jnp.abs(z)))


def _triangle(t, cmp):
    return cmp(lax.broadcasted_iota(jnp.int32, (t, t), 0), lax.broadcasted_iota(jnp.int32, (t, t), 1)).astype(BF16)


def _forget_cumsum(fl, bias, n_heads, pad, name):
    m = fl.shape[0]
    nb = m // ROW_BLOCK

    def body(fl_ref, b_ref, c_ref):
        tri = _triangle(ROW_BLOCK, lambda r, c: r >= c)
        lane_ok = lax.broadcasted_iota(jnp.int32, (ROW_BLOCK, LANES), 1) < n_heads
        rows = lax.broadcasted_iota(jnp.int32, (ROW_BLOCK, LANES), 0)

        def step(b, carry):
            off = pl.multiple_of(b * ROW_BLOCK, ROW_BLOCK)
            lf = _log_sigmoid(fl_ref[pl.ds(off, ROW_BLOCK), :] + b_ref[...])
            lf = jnp.where(lane_ok & (rows + off >= pad), lf, 0.0)
            cs = _dot3(tri, lf) + carry
            c_ref[pl.ds(off, ROW_BLOCK), :] = cs
            return cs[ROW_BLOCK - 1:ROW_BLOCK, :]

        lax.fori_loop(0, nb, step, jnp.zeros((1, LANES), F32))

    vmem = pl.BlockSpec(memory_space=pltpu.VMEM)
    return pl.pallas_call(
        body, name=name, out_shape=jax.ShapeDtypeStruct((m, LANES), F32), in_specs=[vmem, vmem], out_specs=vmem,
        compiler_params=pltpu.CompilerParams(vmem_limit_bytes=VMEM_LIMIT_BYTES),
    )(fl, bias)


def _forget_cumsum_bwd(dc_a, dc_b, fl, bias, n_heads, pad, name):
    m = fl.shape[0]
    nb = m // ROW_BLOCK

    def body(da_ref, db_ref, fl_ref, b_ref, dfl_ref, dbias_ref):
        tri = _triangle(ROW_BLOCK, lambda r, c: r <= c)
        lane_ok = lax.broadcasted_iota(jnp.int32, (ROW_BLOCK, LANES), 1) < n_heads
        rows = lax.broadcasted_iota(jnp.int32, (ROW_BLOCK, LANES), 0)

        def step(bb, carry):
            tail, dbias = carry
            off = pl.multiple_of((nb - 1 - bb) * ROW_BLOCK, ROW_BLOCK)
            dc = da_ref[pl.ds(off, ROW_BLOCK), :] + db_ref[pl.ds(off, ROW_BLOCK), :]
            dlf = _dot3(tri, dc) + tail
            z = fl_ref[pl.ds(off, ROW_BLOCK), :] + b_ref[...]
            dfl = jnp.where(lane_ok & (rows + off >= pad), dlf * _sigmoid(-z), 0.0)
            dfl_ref[pl.ds(off, ROW_BLOCK), :] = dfl
            return dlf[0:1, :], dbias + jnp.sum(dfl, axis=0, keepdims=True)

        zero = jnp.zeros((1, LANES), F32)
        _, dbias = lax.fori_loop(0, nb, step, (zero, zero))
        dbias_ref[...] = dbias

    vmem = pl.BlockSpec(memory_space=pltpu.VMEM)
    return pl.pallas_call(
        body, name=name,
        out_shape=[jax.ShapeDtypeStruct((m, LANES), F32), jax.ShapeDtypeStruct((1, LANES), F32)],
        in_specs=[vmem] * 4, out_specs=[vmem, vmem],
        compiler_params=pltpu.CompilerParams(vmem_limit_bytes=VMEM_LIMIT_BYTES),
    )(dc_a, dc_b, fl, bias)


def _attn_block(m):
    return 3 * ROW_BLOCK if m % (3 * ROW_BLOCK) == 0 else ROW_BLOCK


def _head_norm(o, gain):
    r = lax.rsqrt(jnp.mean(o * o, axis=-1, keepdims=True) + EPS)
    return o * r * gain


def _head_norm_bwd(o, d_on, gain):
    r = lax.rsqrt(jnp.mean(o * o, axis=-1, keepdims=True) + EPS)
    ohat = o * r
    t = d_on * gain
    d_o = r * (t - ohat * jnp.mean(t * ohat, axis=-1, keepdims=True))
    return d_o, jnp.sum(d_on * ohat, axis=0, keepdims=True)


def _qkv_specs(t, m, h, first_col_block):
    q = pl.BlockSpec((t, HEAD_DIM), lambda hd, i: (i, first_col_block + hd))
    k = pl.BlockSpec((m, HEAD_DIM), lambda hd, i: (0, first_col_block + h + hd))
    v = pl.BlockSpec((m, HEAD_DIM), lambda hd, i: (0, first_col_block + 2 * h + hd))
    return q, k, v


def _fox_fwd(qkv, ccol, crow, gain, n_heads, pad, name, comm=None):
    m = qkv.shape[0]
    t = _attn_block(m)
    nq = m // t
    scale = HEAD_DIM ** -0.5
    hw = n_heads * HEAD_DIM

    def body(q_ref, k_ref, v_ref, ccol_ref, crow_ref, g_ref, o_ref, on_ref, lse_ref):
        i = pl.program_id(1)
        q = q_ref[...]
        ci = ccol_ref[...]
        qpos = i * t + lax.broadcasted_iota(jnp.int32, (t, 1), 0)

        def step(j, carry):
            mx, l, acc = carry
            off = pl.multiple_of(j * t, t)
            k = k_ref[pl.ds(off, t), :]
            v = v_ref[pl.ds(off, t), :]
            s = _dot(q, k, NT_DIMS) * scale + ci - crow_ref[j]
            kpos = off + lax.broadcasted_iota(jnp.int32, (1, t), 1)
            s = jnp.where((kpos <= qpos) & (kpos >= pad), s, NEG)
            mx_new = jnp.maximum(mx, jnp.max(s, axis=-1, keepdims=True))
            p = jnp.exp(s - mx_new)
            a = jnp.exp(mx - mx_new)
            return mx_new, a * l + jnp.sum(p, axis=-1, keepdims=True), a * acc + _dot(p.astype(BF16), v)

        init = (jnp.full((t, 1), NEG, F32), jnp.zeros((t, 1), F32), jnp.zeros((t, HEAD_DIM), F32))
        mx, l, acc = lax.fori_loop(0, i + 1, step, init)
        valid = qpos >= pad
        o = jnp.where(valid, acc / l, 0.0)
        o_ref[...] = o
        on_ref[...] = _head_norm(o, g_ref[...]).astype(BF16)
        lse_ref[...] = jnp.where(valid, mx + jnp.log(l), 0.0)

    q_spec, k_spec, v_spec = _qkv_specs(t, m, n_heads, 0)
    col = pl.BlockSpec((None, t, 1), lambda hd, i: (hd, i, 0))
    head = pl.BlockSpec((t, HEAD_DIM), lambda hd, i: (i, hd))
    return _call(body, name=name, grid=(n_heads, nq),
                 in_specs=[q_spec, k_spec, v_spec, col, pl.BlockSpec((None, nq, 1, t), lambda hd, i: (hd, 0, 0, 0)),
                           pl.BlockSpec((1, HEAD_DIM), lambda hd, i: (0, hd))],
                 out_specs=[head, head, col],
                 out_shape=[jax.ShapeDtypeStruct((m, hw), F32), jax.ShapeDtypeStruct((m, hw), BF16),
                            jax.ShapeDtypeStruct((n_heads, m, 1), F32)],
                 operands=[qkv, qkv, qkv, ccol, crow, gain], comm=comm)


def _fox_bwd(qkv, o, d_on, gain, lse, ccol, crow, n_heads, pad, name, comm=None):
    m = qkv.shape[0]
    t = _attn_block(m)
    nq = m // t
    scale = HEAD_DIM ** -0.5
    hw = n_heads * HEAD_DIM

    def body(q_ref, k_ref, v_ref, o_ref, don_ref, g_ref, lse_ref, ccol_ref, crow_ref,
             dq_ref, dk_ref, dv_ref, dg_ref, dccol_ref, dcrow_ref):
        i = pl.program_id(1)

        @pl.when(i == 0)
        def _():
            dk_ref[...] = jnp.zeros_like(dk_ref)
            dv_ref[...] = jnp.zeros_like(dv_ref)
            dg_ref[...] = jnp.zeros_like(dg_ref)
            dcrow_ref[...] = jnp.zeros_like(dcrow_ref)

        q = q_ref[...]
        o = o_ref[...]
        d_o, dgain = _head_norm_bwd(o, don_ref[...], g_ref[...])
        dg_ref[...] += dgain
        delta = jnp.sum(d_o * o, axis=-1, keepdims=True)
        d_ob = d_o.astype(BF16)
        ci = ccol_ref[...]
        lse_i = lse_ref[...]
        qpos = i * t + lax.broadcasted_iota(jnp.int32, (t, 1), 0)

        def step(j, carry):
            dq, dci = carry
            off = pl.multiple_of(j * t, t)
            k = k_ref[pl.ds(off, t), :]
            v = v_ref[pl.ds(off, t), :]
            s = _dot(q, k, NT_DIMS) * scale + ci - crow_ref[j]
            kpos = off + lax.broadcasted_iota(jnp.int32, (1, t), 1)
            ok = (kpos <= qpos) & (kpos >= pad)
            p = jnp.where(ok, jnp.exp(jnp.where(ok, s - lse_i, 0.0)), 0.0)
            ds = p * (_dot(d_ob, v, NT_DIMS) - delta)
            dsb = ds.astype(BF16)
            dk_ref[pl.ds(off, t), :] += _dot(dsb, q, TN_DIMS) * scale
            dv_ref[pl.ds(off, t), :] += _dot(p.astype(BF16), d_ob, TN_DIMS)
            dcrow_ref[j] -= jnp.sum(ds, axis=0, keepdims=True)
            return dq + _dot(dsb, k), dci + jnp.sum(ds, axis=-1, keepdims=True)

        dq, dci = lax.fori_loop(0, i + 1, step, (jnp.zeros((t, HEAD_DIM), F32), jnp.zeros((t, 1), F32)))
        dq_ref[...] = (dq * scale).astype(BF16)
        dccol_ref[...] = dci

    q_spec, k_spec, v_spec = _qkv_specs(t, m, n_heads, 0)
    col = pl.BlockSpec((None, t, 1), lambda hd, i: (hd, i, 0))
    rowc = pl.BlockSpec((None, nq, 1, t), lambda hd, i: (hd, 0, 0, 0))
    head = pl.BlockSpec((t, HEAD_DIM), lambda hd, i: (i, hd))
    whole = pl.BlockSpec((m, HEAD_DIM), lambda hd, i: (0, hd))
    gvec = pl.BlockSpec((1, HEAD_DIM), lambda hd, i: (0, hd))
    return _call(body, name=name, grid=(n_heads, nq),
                 in_specs=[q_spec, k_spec, v_spec, head, head, gvec, col, col, rowc],
                 out_specs=[head, whole, whole, gvec, col, rowc],
                 out_shape=[jax.ShapeDtypeStruct((m, hw), BF16), jax.ShapeDtypeStruct((m, hw), F32),
                            jax.ShapeDtypeStruct((m, hw), F32), jax.ShapeDtypeStruct((1, hw), F32),
                            jax.ShapeDtypeStruct((n_heads, m, 1), F32), jax.ShapeDtypeStruct((n_heads, nq, 1, t), F32)],
                 operands=[qkv, qkv, qkv, o, d_on, gain, lse, ccol, crow], comm=comm)


def _sb_scores(q, k, scale, ok):
    z = _dot(q, k, NT_DIMS) * scale
    e = jnp.exp(-jnp.abs(z))
    lp = jnp.log(1.0 + e)
    ls_pos = jnp.minimum(z, 0.0) - lp
    log_1m = jnp.where(ok, jnp.minimum(-z, 0.0) - lp, 0.0)
    return z, e, ls_pos, log_1m


def _sb_fwd(qkv, gain, n_heads, pad, name, comm=None):
    m = qkv.shape[0]
    t = _attn_block(m)
    nq = m // t
    assert nq <= LANES
    scale = HEAD_DIM ** -0.5
    hw = n_heads * HEAD_DIM

    def body(q_ref, k_ref, v_ref, g_ref, after_ref, o_ref, on_ref, run_ref):
        i = pl.program_id(1)
        q = q_ref[...]
        qpos = i * t + lax.broadcasted_iota(jnp.int32, (t, 1), 0)
        after = after_ref[...]
        lane = lax.broadcasted_iota(jnp.int32, (t, LANES), 1)

        def step(jj, carry):
            run, acc = carry
            j = i - jj
            off = pl.multiple_of(j * t, t)
            k = k_ref[pl.ds(off, t), :]
            v = v_ref[pl.ds(off, t), :]
            kpos = off + lax.broadcasted_iota(jnp.int32, (1, t), 1)
            ok = (kpos < qpos) & (kpos >= pad)
            _, _, ls_pos, log_1m = _sb_scores(q, k, scale, ok)
            later = _dot_split(log_1m, after) + run
            a = jnp.where(ok, jnp.exp(ls_pos + later), 0.0)
            run_ref[...] = jnp.where(lane == j, run, run_ref[...])
            return run + jnp.sum(log_1m, axis=-1, keepdims=True), acc + _dot(a.astype(BF16), v)

        run_ref[...] = jnp.zeros_like(run_ref)
        _, o = lax.fori_loop(0, i + 1, step, (jnp.zeros((t, 1), F32), jnp.zeros((t, HEAD_DIM), F32)))
        o_ref[...] = o
        on_ref[...] = _head_norm(o, g_ref[...]).astype(BF16)

    q_spec, k_spec, v_spec = _qkv_specs(t, m, n_heads, 3 * n_heads)
    head = pl.BlockSpec((t, HEAD_DIM), lambda hd, i: (i, hd))
    return _call(body, name=name, grid=(n_heads, nq),
                 in_specs=[q_spec, k_spec, v_spec, pl.BlockSpec((1, HEAD_DIM), lambda hd, i: (0, hd)),
                           pl.BlockSpec((t, t), lambda hd, i: (0, 0))],
                 out_specs=[head, head, pl.BlockSpec((None, t, LANES), lambda hd, i: (hd, i, 0))],
                 out_shape=[jax.ShapeDtypeStruct((m, hw), F32), jax.ShapeDtypeStruct((m, hw), BF16),
                            jax.ShapeDtypeStruct((n_heads, m, LANES), F32)],
                 operands=[qkv, qkv, qkv, gain, _triangle(t, lambda r, c: r > c)], comm=comm)


def _sb_bwd(qkv, o, d_on, gain, runs, n_heads, pad, name, comm=None):
    m = qkv.shape[0]
    t = _attn_block(m)
    nq = m // t
    scale = HEAD_DIM ** -0.5
    hw = n_heads * HEAD_DIM

    def body(q_ref, k_ref, v_ref, o_ref, don_ref, g_ref, run_ref, after_ref, before_ref, dq_ref, dk_ref, dv_ref, dg_ref):
        i = pl.program_id(1)

        @pl.when(i == 0)
        def _():
            dk_ref[...] = jnp.zeros_like(dk_ref)
            dv_ref[...] = jnp.zeros_like(dv_ref)
            dg_ref[...] = jnp.zeros_like(dg_ref)

        q = q_ref[...]
        d_o, dgain = _head_norm_bwd(o_ref[...], don_ref[...], g_ref[...])
        dg_ref[...] += dgain
        d_ob = d_o.astype(BF16)
        runs_i = run_ref[...]
        qpos = i * t + lax.broadcasted_iota(jnp.int32, (t, 1), 0)
        after = after_ref[...]
        before = before_ref[...]
        lane = lax.broadcasted_iota(jnp.int32, (t, LANES), 1)

        def step(j, carry):
            g_run, dq = carry
            off = pl.multiple_of(j * t, t)
            k = k_ref[pl.ds(off, t), :]
            v = v_ref[pl.ds(off, t), :]
            kpos = off + lax.broadcasted_iota(jnp.int32, (1, t), 1)
            ok = (kpos < qpos) & (kpos >= pad)
            z, e, ls_pos, log_1m = _sb_scores(q, k, scale, ok)
            run = jnp.sum(jnp.where(lane == j, runs_i, 0.0), axis=-1, keepdims=True)
            later = _dot_split(log_1m, after) + run
            a = jnp.where(ok, jnp.exp(ls_pos + later), 0.0)
            g = a * _dot(d_ob, v, NT_DIMS)
            prefix = _dot_split(g, before) + g_run
            rcp = 1.0 / (1.0 + e)
            beta = jnp.where(z >= 0.0, rcp, e * rcp)
            one_m = jnp.where(z >= 0.0, e * rcp, rcp)
            dz = jnp.where(ok, g * one_m - beta * prefix, 0.0)
            dzb = dz.astype(BF16)
            dk_ref[pl.ds(off, t), :] += _dot(dzb, q, TN_DIMS) * scale
            dv_ref[pl.ds(off, t), :] += _dot(a.astype(BF16), d_ob, TN_DIMS)
            return g_run + jnp.sum(g, axis=-1, keepdims=True), dq + _dot(dzb, k)

        _, dq = lax.fori_loop(0, i + 1, step, (jnp.zeros((t, 1), F32), jnp.zeros((t, HEAD_DIM), F32)))
        dq_ref[...] = (dq * scale).astype(BF16)

    q_spec, k_spec, v_spec = _qkv_specs(t, m, n_heads, 3 * n_heads)
    head = pl.BlockSpec((t, HEAD_DIM), lambda hd, i: (i, hd))
    whole = pl.BlockSpec((m, HEAD_DIM), lambda hd, i: (0, hd))
    gvec = pl.BlockSpec((1, HEAD_DIM), lambda hd, i: (0, hd))
    tri = pl.BlockSpec((t, t), lambda hd, i: (0, 0))
    return _call(body, name=name, grid=(n_heads, nq),
                 in_specs=[q_spec, k_spec, v_spec, head, head, gvec, pl.BlockSpec((None, t, LANES), lambda hd, i: (hd, i, 0)),
                           tri, tri],
                 out_specs=[head, whole, whole, gvec],
                 out_shape=[jax.ShapeDtypeStruct((m, hw), BF16), jax.ShapeDtypeStruct((m, hw), F32),
                            jax.ShapeDtypeStruct((m, hw), F32), jax.ShapeDtypeStruct((1, hw), F32)],
                 operands=[qkv, qkv, qkv, o, d_on, gain, runs, _triangle(t, lambda r, c: r > c), _triangle(t, lambda r, c: r < c)],
                 comm=comm)


def _adamw(parts, w, m1, v2, name):
    nl, r, c = w.shape
    assert len(parts) == nl
    n_parts = parts[0].shape[0]
    lanes_padded = -(-c // LANES) * LANES
    tr = _tile(r, max(8, (128 * 1024) // lanes_padded), 8)
    nt = r // tr
    bias1 = 1.0 / (1.0 - ADAM_B1 ** ADAM_STEP)
    bias2 = 1.0 / (1.0 - ADAM_B2 ** ADAM_STEP)

    def body(*refs):
        p_refs = refs[:nl]
        w_ref, m_ref, v_ref, g_ref, d_ref, nm_ref, nv_ref = refs[nl:]

        def update(p_ref):
            g = p_ref[0].astype(F32)
            for s in range(1, n_parts):
                g = g + p_ref[s].astype(F32)
            m_new = ADAM_B1 * m_ref[...] + (1.0 - ADAM_B1) * g
            v_new = ADAM_B2 * v_ref[...] + (1.0 - ADAM_B2) * (g * g)
            g_ref[...] = g
            nm_ref[...] = m_new
            nv_ref[...] = v_new
            d_ref[...] = -ADAM_LR * ((m_new * bias1) / (jnp.sqrt(v_new * bias2) + ADAM_EPS) + ADAM_WD * w_ref[...])

        for ll in range(nl):
            @pl.when(pl.program_id(0) == ll)
            def _(ll=ll):
                update(p_refs[ll])

    def part_spec(ll):
        return pl.BlockSpec((n_parts, tr, c), lambda l, i: (0, jnp.where(l == ll, i, jnp.where(l < ll, 0, nt - 1)), 0))

    blk = pl.BlockSpec((None, tr, c), lambda l, i: (l, i, 0))
    out, _ = _call(body, name=name, grid=(nl, nt), in_specs=[part_spec(ll) for ll in range(nl)] + [blk, blk, blk],
                   out_specs=[blk] * 4, out_shape=[jax.ShapeDtypeStruct((nl, r, c), F32)] * 4,
                   operands=list(parts) + [w, m1, v2])
    return out


SMALL_WEIGHTS = ("ffn1_norm", "mix_norm", "b_forget", "g_fox", "g_sb", "ffn2_norm", "final_norm")
WEIGHT_ORDER = ("meta_tokens", "ffn1_norm", "ffn1_w_gate", "ffn1_w_up", "ffn1_w_down", "mix_norm", "w_in", "b_forget",
                "g_fox", "g_sb", "w_out", "ffn2_norm", "ffn2_w_gate", "ffn2_w_up", "ffn2_w_down", "final_norm")
GROUPS = {"ffn1": ("ffn1_w_gate", "ffn1_w_up", "ffn1_w_down"), "mix": ("w_in", "w_out"),
          "ffn2": ("ffn2_w_gate", "ffn2_w_up", "ffn2_w_down")}


def _pad_lanes(a):
    extra = (-a.shape[-1]) % LANES
    return a if extra == 0 else jnp.pad(a, [(0, 0)] * (a.ndim - 1) + [(0, extra)])


def _ffn_forward(h, gain, wg, wu, wd, tag, comm=None):
    xn = _rms_fwd(h, gain, f"{tag}_norm")
    (g, u, act), got = _ffn_up(xn, wg, wu, f"{tag}_up", comm)
    return _ffn_down(act, wd, h, f"{tag}_down"), (h, xn, g, u, act), got


def _ffn_backward(dh_out, saved, gain, wg, wu, wd, tag):
    h, xn, g, u, act = saved
    dg, du = _ffn_bwd_act(dh_out, wd, g, u, f"{tag}_bwd_act")
    d_wd = _ffn_dwd(act, dh_out, f"{tag}_dwd")
    d_wg = _ffn_dw_in(xn, dg, f"{tag}_dwg")
    d_wu = _ffn_dw_in(xn, du, f"{tag}_dwu")
    dxn = _ffn_dxn(dg, wg, du, wu, f"{tag}_dxn")
    dh, d_gain = _rms_bwd(dxn, h, gain, dh_out, f"{tag}_norm_bwd")
    return dh, d_gain, d_wg, d_wu, d_wd


def _mixer_forward(h, gain, w_in_pad, bias, g_fox, g_sb, w_out, n_heads, pad, tag, comm_fox=None, comm_sb=None):
    m = h.shape[0]
    t = _attn_block(m)
    hw = n_heads * HEAD_DIM
    xn = _rms_fwd(h, gain, f"{tag}_norm")
    qkv = _mm([(xn, w_in_pad[:, :6 * hw])], "nn", BF16, name=f"{tag}_qkv")
    fl = _mm([(xn, w_in_pad[:, 6 * hw:])], "nn", F32, name=f"{tag}_forget")
    c = _forget_cumsum(fl, bias, n_heads, pad, f"{tag}_cumsum")
    c_heads = c[:, :n_heads].T
    ccol = c_heads[:, :, None]
    crow = c_heads.reshape(n_heads, m // t, 1, t)
    (o_f, on_f, lse), got_fox = _fox_fwd(qkv, ccol, crow, g_fox, n_heads, pad, f"{tag}_fox", comm_fox)
    (o_s, on_s, runs), got_sb = _sb_fwd(qkv, g_sb, n_heads, pad, f"{tag}_sb", comm_sb)
    h_out = _mm([(on_f, w_out[:hw]), (on_s, w_out[hw:])], "nn", F32, name=f"{tag}_out", res=h)
    return h_out, (h, xn, qkv, fl, ccol, crow, o_f, on_f, lse, o_s, on_s, runs), got_fox, got_sb


def _mixer_backward(dh_out, saved, gain, w_in_pad, bias, g_fox, g_sb, w_out, n_heads, pad, tag, comm_fox=None, comm_sb=None):
    h, xn, qkv, fl, ccol, crow, o_f, on_f, lse, o_s, on_s, runs = saved
    m = h.shape[0]
    hw = n_heads * HEAD_DIM
    d_on_f = _mm([(dh_out, w_out[:hw])], "nt", F32, name=f"{tag}_don_f")
    d_on_s = _mm([(dh_out, w_out[hw:])], "nt", F32, name=f"{tag}_don_s")
    d_wout = jnp.concatenate([_mm([(on_f, dh_out)], "tn", BF16, name=f"{tag}_dwout_f"),
                              _mm([(on_s, dh_out)], "tn", BF16, name=f"{tag}_dwout_s")], axis=0)
    (dq_f, dk_f, dv_f, dg_fox, dccol, dcrow), got_fox = _fox_bwd(
        qkv, o_f, d_on_f, g_fox, lse, ccol, crow, n_heads, pad, f"{tag}_fox_bwd", comm_fox)
    (dq_s, dk_s, dv_s, dg_sb), got_sb = _sb_bwd(qkv, o_s, d_on_s, g_sb, runs, n_heads, pad, f"{tag}_sb_bwd", comm_sb)
    dc_a = _pad_lanes(dccol[:, :, 0].T)
    dc_b = _pad_lanes(dcrow.reshape(n_heads, m).T)
    dfl, dbias = _forget_cumsum_bwd(dc_a, dc_b, fl, bias, n_heads, pad, f"{tag}_cumsum_bwd")
    dproj = jnp.concatenate([dq_f, dk_f.astype(BF16), dv_f.astype(BF16), dq_s, dk_s.astype(BF16), dv_s.astype(BF16),
                             dfl.astype(BF16)], axis=1)
    d_win = _mm([(xn, dproj)], "tn", BF16, name=f"{tag}_dwin")
    dxn = _mm([(dproj, w_in_pad)], "nt", F32, name=f"{tag}_dxn")
    dh, d_gain = _rms_bwd(dxn, h, gain, dh_out, f"{tag}_norm_bwd")
    return dh, d_gain, d_win, dbias, dg_fox, dg_sb, d_wout, got_fox, got_sb


def kernel(x, meta_tokens, ffn1_norm, ffn1_w_gate, ffn1_w_up, ffn1_w_down, mix_norm, w_in, b_forget, g_fox, g_sb, w_out, ffn2_norm, ffn2_w_gate, ffn2_w_up, ffn2_w_down, final_norm, loss_target, m_meta_tokens, m_ffn1_norm, m_ffn1_w_gate, m_ffn1_w_up, m_ffn1_w_down, m_mix_norm, m_w_in, m_b_forget, m_g_fox, m_g_sb, m_w_out, m_ffn2_norm, m_ffn2_w_gate, m_ffn2_w_up, m_ffn2_w_down, m_final_norm, v_meta_tokens, v_ffn1_norm, v_ffn1_w_gate, v_ffn1_w_up, v_ffn1_w_down, v_mix_norm, v_w_in, v_b_forget, v_g_fox, v_g_sb, v_w_out, v_ffn2_norm, v_ffn2_w_gate, v_ffn2_w_up, v_ffn2_w_down, v_final_norm):
    given = dict(locals())
    seq, d = x.shape[1], x.shape[2]
    depth = ffn1_norm.shape[0]
    d_in = N_DEV * w_in.shape[2]
    n_heads = g_fox.shape[1] // HEAD_DIM
    hw = n_heads * HEAD_DIM
    assert seq % ROW_BLOCK == 0 and d_in == 6 * hw + n_heads and n_heads <= LANES
    pad = (-(seq + N_META)) % ROW_BLOCK
    x_off = pad + N_META

    def shards(group, l):
        return [given[n][l].astype(BF16) for n in GROUPS[group]]

    first = _run_alone(_Gather(shards("ffn1", 0) + [meta_tokens]), "gather_first")
    full = {("ffn1", 0): first[:3]}
    meta_full = first[3].transpose(1, 0, 2).reshape(N_META, d)
    h = jnp.concatenate([jnp.zeros((pad, d), F32), meta_full, x[0]], axis=0)
    weights, saved = [], []
    for l in range(depth):
        wg1, wu1, wd1 = full[("ffn1", l)]
        h, s1, full[("mix", l)] = _ffn_forward(h, ffn1_norm[l:l + 1], wg1, wu1, wd1, "ffn1", _Gather(shards("mix", l)))
        win3, wout3 = full[("mix", l)]
        w_in_pad = _pad_lanes(win3.transpose(1, 0, 2).reshape(d, d_in))
        w_out_full = wout3.reshape(N_DEV * wout3.shape[1], d)
        bias = _pad_lanes(b_forget[l:l + 1])
        nxt = _Gather(shards("ffn1", l + 1)) if l + 1 < depth else None
        h, sm, full[("ffn2", l)], got = _mixer_forward(
            h, mix_norm[l:l + 1], w_in_pad, bias, g_fox[l:l + 1], g_sb[l:l + 1], w_out_full, n_heads, pad, "mix",
            _Gather(shards("ffn2", l)), nxt)
        if nxt is not None:
            full[("ffn1", l + 1)] = got
        wg2, wu2, wd2 = full[("ffn2", l)]
        h, s2, _ = _ffn_forward(h, ffn2_norm[l:l + 1], wg2, wu2, wd2, "ffn2")
        weights.append((w_in_pad, w_out_full, bias))
        saved.append((s1, sm, s2))

    dh, d_final, loss_arr = _loss_head(h, final_norm[None, :], loss_target[0], x_off, "loss_head")
    small = {n: [None] * depth for n in SMALL_WEIGHTS[:-1]}
    partial, received = {}, {}

    def names(group, l):
        return [(n, l) for n in GROUPS[group]]

    for l in reversed(range(depth)):
        w_in_pad, w_out_full, bias = weights[l]
        s1, sm, s2 = saved[l]
        wg2, wu2, wd2 = full[("ffn2", l)]
        dh, small["ffn2_norm"][l], partial[("ffn2_w_gate", l)], partial[("ffn2_w_up", l)], partial[("ffn2_w_down", l)] = (
            _ffn_backward(dh, s2, ffn2_norm[l:l + 1], wg2, wu2, wd2, "ffn2"))
        in_fox = names("mix", l + 1) if l + 1 < depth else []
        in_sb = names("ffn2", l) + (names("ffn1", l + 1) if l + 1 < depth else [])
        dh, small["mix_norm"][l], d_win, dbias, small["g_fox"][l], small["g_sb"][l], d_wout, got_fox, got_sb = _mixer_backward(
            dh, sm, mix_norm[l:l + 1], w_in_pad, bias, g_fox[l:l + 1], g_sb[l:l + 1], w_out_full, n_heads, pad, "mix",
            _Exchange([partial[k] for k in in_fox]) if in_fox else None, _Exchange([partial[k] for k in in_sb]))
        received.update(zip(in_fox, got_fox or []))
        received.update(zip(in_sb, got_sb))
        partial[("w_in", l)] = d_win[:, :d_in].reshape(d, N_DEV, -1).transpose(1, 0, 2)
        partial[("w_out", l)] = d_wout.reshape(N_DEV, -1, d)
        small["b_forget"][l] = dbias[:, :n_heads]
        wg1, wu1, wd1 = full[("ffn1", l)]
        dh, small["ffn1_norm"][l], partial[("ffn1_w_gate", l)], partial[("ffn1_w_up", l)], partial[("ffn1_w_down", l)] = (
            _ffn_backward(dh, s1, ffn1_norm[l:l + 1], wg1, wu1, wd1, "ffn1"))
    grad_x = dh[x_off:][None]
    last = names("mix", 0) + names("ffn1", 0)
    d_meta = dh[pad:x_off].reshape(N_META, N_DEV, -1).transpose(1, 0, 2)
    got = _run_alone(_Exchange([partial[k] for k in last] + [d_meta]), "exchange_last")
    received.update(zip(last, got[:-1]))
    received[("meta_tokens", 0)] = got[-1]

    vec = [loss_arr[0:1, :]] + [_pad_lanes(jnp.concatenate(small[n], axis=0).reshape(1, -1)) for n in SMALL_WEIGHTS[:-1]]
    vec.append(d_final)
    sizes = [a.shape[1] for a in vec]
    summed = _all_reduce_small(jnp.concatenate(vec, axis=1), "reduce_small")
    loss = summed[0, 0]

    def packed(prefix):
        cols = [jnp.zeros((1, LANES), F32)]
        cols += [_pad_lanes(given[prefix + n].reshape(1, -1)) for n in SMALL_WEIGHTS]
        return jnp.concatenate(cols, axis=1)[None]

    small_out = _adamw([summed[None]], packed(""), packed("m_"), packed("v_"), "adamw_small")

    out = {}
    for n in [w for g in GROUPS.values() for w in g]:
        out[n] = _adamw([received[(n, l)] for l in range(depth)], given[n], given["m_" + n], given["v_" + n], "adamw_" + n)
    out["meta_tokens"] = [r[0] for r in _adamw([received[("meta_tokens", 0)]], meta_tokens[None], m_meta_tokens[None],
                                               v_meta_tokens[None], "adamw_meta_tokens")]
    offset = sizes[0]
    for n, size in zip(SMALL_WEIGHTS, sizes[1:]):
        shape, count = given[n].shape, given[n].size
        out[n] = [r[0, 0, offset:offset + count].reshape(shape) for r in small_out]
        offset += size

    result = [loss, grad_x]
    for k in range(4):
        result += [out[n][k] for n in WEIGHT_ORDER]
    return tuple(result)
```

```python
import math

import jax
import jax.numpy as jnp
from jax import lax
from jax.experimental import pallas as pl
from jax.experimental.pallas import tpu as pltpu

F32 = jnp.float32
BF16 = jnp.bfloat16

N_DEV = 8
N_META = 16
HEAD_DIM = 128
ROW_BLOCK = 128
LANES = 128
EPS = 1e-6
NEG = -1e30
ADAM_LR = 0.001
ADAM_B1 = 0.9
ADAM_B2 = 0.999
ADAM_EPS = 1e-08
ADAM_WD = 0.01
ADAM_STEP = 10
VMEM_LIMIT_BYTES = 56 * 1024 * 1024
MESH = pl.DeviceIdType.MESH

NT_DIMS = (((1,), (1,)), ((), ()))
TN_DIMS = (((0,), (0,)), ((), ()))
NN_DIMS = (((1,), (0,)), ((), ()))
ANY = pl.BlockSpec(memory_space=pl.ANY)


def _tile(n, cap, align):
    best = None
    for d in range(align, min(n, cap) + 1, align):
        if n % d == 0:
            best = d
    return best if best is not None else n


def _dot(a, b, dims=NN_DIMS):
    return lax.dot_general(a, b, dims, preferred_element_type=F32)


def _dot_split(x, u):
    hi = x.astype(BF16)
    lo = (x - hi.astype(F32)).astype(BF16)
    return _dot(hi, u) + _dot(lo, u)


def _my_position():
    return lax.axis_index("x"), lax.axis_index("y"), lax.axis_index("c")


class _Gather:
    n_phases = 3

    def __init__(self, arrs):
        self.arrs = list(arrs)
        n = len(self.arrs)
        self.out_shapes = [jax.ShapeDtypeStruct((N_DEV,) + a.shape, a.dtype) for a in self.arrs]
        self.scratch = [pltpu.SemaphoreType.DMA((n, 7)), pltpu.SemaphoreType.DMA((n, 7)), pltpu.SemaphoreType.DMA((n,))]

    def phase(self, p, ins, outs, sems):
        send_sems, recv_sems, local_sems = sems
        n = len(self.arrs)
        x, y, c = _my_position()
        me, sibling = (x, y, c), (x, y, 1 - c)
        chips = [(1 - x, y), (x, 1 - y), (1 - x, 1 - y)]

        def copy(a, k, block, to, src=None):
            slot = outs[a].at[4 * block[0] + 2 * block[1] + block[2]]
            return pltpu.make_async_remote_copy(
                src_ref=slot if src is None else src, dst_ref=slot,
                send_sem=send_sems.at[a, k], recv_sem=recv_sems.at[a, k], device_id=to, device_id_type=MESH)

        def local(a):
            return pltpu.make_async_copy(ins[a], outs[a].at[4 * x + 2 * y + c], local_sems.at[a])

        def first(a):
            return [copy(a, 0, me, sibling, src=ins[a])] + [copy(a, 1 + j, me, (*chip, c), src=ins[a]) for j, chip in enumerate(chips)]

        def passed(a, j):
            return copy(a, 4 + j, (*chips[j], c), sibling)

        if p == 0:
            for a in range(n):
                local(a).start()
            for a in range(n):
                for cp in first(a):
                    cp.start()
        elif p == 1:
            for a in range(n):
                for j, chip in enumerate(chips):
                    copy(a, 1 + j, (*chip, c), me).wait_recv()
                    passed(a, j).start()
        else:
            for a in range(n):
                copy(a, 0, sibling, me).wait_recv()
                for j, chip in enumerate(chips):
                    copy(a, 4 + j, (*chip, 1 - c), me).wait_recv()
            for a in range(n):
                for cp in first(a) + [passed(a, j) for j in range(3)]:
                    cp.wait_send()
                local(a).wait()


class _Exchange:
    n_phases = 2

    def __init__(self, arrs):
        self.arrs = list(arrs)
        n = len(self.arrs)
        self.out_shapes = [jax.ShapeDtypeStruct(a.shape, a.dtype) for a in self.arrs]
        self.scratch = [pltpu.SemaphoreType.DMA((n, 7)), pltpu.SemaphoreType.DMA((n, 7)), pltpu.SemaphoreType.DMA((n,))]

    def phase(self, p, ins, outs, sems):
        send_sems, recv_sems, local_sems = sems
        n = len(self.arrs)
        x, y, c = _my_position()
        me = 4 * x + 2 * y + c

        def peer_of(r):
            return (x ^ ((r >> 2) & 1), y ^ ((r >> 1) & 1), c ^ (r & 1))

        def copy(a, r):
            px, py, pc = peer_of(r)
            return pltpu.make_async_remote_copy(
                src_ref=ins[a].at[4 * px + 2 * py + pc], dst_ref=outs[a].at[me],
                send_sem=send_sems.at[a, r - 1], recv_sem=recv_sems.at[a, r - 1],
                device_id=(px, py, pc), device_id_type=MESH)

        def arrival(a, r):
            px, py, pc = peer_of(r)
            slot = outs[a].at[4 * px + 2 * py + pc]
            return pltpu.make_async_remote_copy(
                src_ref=slot, dst_ref=slot, send_sem=send_sems.at[a, r - 1], recv_sem=recv_sems.at[a, r - 1],
                device_id=(px, py, pc), device_id_type=MESH)

        def local(a):
            return pltpu.make_async_copy(ins[a].at[me], outs[a].at[me], local_sems.at[a])

        if p == 0:
            for a in range(n):
                local(a).start()
            for a in range(n):
                for r in range(1, N_DEV):
                    copy(a, r).start()
        else:
            for a in range(n):
                for r in range(1, N_DEV):
                    arrival(a, r).wait_recv()
            for a in range(n):
                for r in range(1, N_DEV):
                    copy(a, r).wait_send()
                local(a).wait()


def _run_alone(comm, name):
    n = len(comm.arrs)

    def body(*refs):
        for p in range(comm.n_phases):
            comm.phase(p, refs[:n], refs[n:2 * n], refs[2 * n:])

    return pl.pallas_call(body, name=name, out_shape=comm.out_shapes, in_specs=[ANY] * n, out_specs=[ANY] * n,
                          scratch_shapes=comm.scratch)(*comm.arrs)


def _call(body, *, name, grid, in_specs, out_specs, out_shape, operands, scratch_shapes=(), comm=None):
    scratch_shapes = list(scratch_shapes)
    params = pltpu.CompilerParams(dimension_semantics=("arbitrary",) * len(grid), vmem_limit_bytes=VMEM_LIMIT_BYTES)
    if comm is None:
        res = pl.pallas_call(body, name=name, grid=grid, in_specs=in_specs, out_specs=out_specs, out_shape=out_shape,
                             scratch_shapes=scratch_shapes, compiler_params=params)(*operands)
        return res, None
    n_in, n_out, n_sc = len(in_specs), len(out_specs), len(scratch_shapes)
    nc = len(comm.arrs)
    total = math.prod(grid)
    at = {0: 0, comm.n_phases - 1: total - 1}
    for p in range(1, comm.n_phases - 1):
        at[p] = (total * 7) // 8

    def wrapped(*refs):
        ins, cins = refs[:n_in], refs[n_in:n_in + nc]
        outs, couts = refs[n_in + nc:n_in + nc + n_out], refs[n_in + nc + n_out:n_in + 2 * nc + n_out]
        rest = refs[n_in + 2 * nc + n_out:]
        scratch, sems = rest[:n_sc], rest[n_sc:]
        step = 0
        for axis, size in enumerate(grid):
            step = step * size + pl.program_id(axis)
        for p in range(comm.n_phases - 1):
            @pl.when(step == at[p])
            def _(p=p):
                comm.phase(p, cins, couts, sems)
        body(*ins, *outs, *scratch)

        @pl.when(step == total - 1)
        def _():
            comm.phase(comm.n_phases - 1, cins, couts, sems)

    res = pl.pallas_call(
        wrapped, name=name, grid=grid, in_specs=list(in_specs) + [ANY] * nc, out_specs=list(out_specs) + [ANY] * nc,
        out_shape=list(out_shape) + comm.out_shapes, scratch_shapes=scratch_shapes + comm.scratch,
        compiler_params=params)(*operands, *comm.arrs)
    return res[:n_out], res[n_out:]


def _all_reduce_small(vec, name):
    n = vec.shape[1]

    def body(v_ref, o_ref, buf, send_sems, recv_sems):
        x, y, c = _my_position()
        me = 4 * x + 2 * y + c

        def peer_of(r):
            return (x ^ ((r >> 2) & 1), y ^ ((r >> 1) & 1), c ^ (r & 1))

        def copy(r):
            px, py, pc = peer_of(r)
            return pltpu.make_async_remote_copy(
                src_ref=v_ref, dst_ref=buf.at[me], send_sem=send_sems.at[r - 1], recv_sem=recv_sems.at[r - 1],
                device_id=(px, py, pc), device_id_type=MESH)

        def arrival(r):
            px, py, pc = peer_of(r)
            slot = buf.at[4 * px + 2 * py + pc]
            return pltpu.make_async_remote_copy(
                src_ref=slot, dst_ref=slot, send_sem=send_sems.at[r - 1], recv_sem=recv_sems.at[r - 1],
                device_id=(px, py, pc), device_id_type=MESH)

        sends = [copy(r) for r in range(1, N_DEV)]
        for cp in sends:
            cp.start()
        buf[me] = v_ref[...]
        for r in range(1, N_DEV):
            arrival(r).wait_recv()
        for cp in sends:
            cp.wait_send()
        total = buf[0]
        for d in range(1, N_DEV):
            total = total + buf[d]
        o_ref[...] = total

    vmem = pl.BlockSpec(memory_space=pltpu.VMEM)
    return pl.pallas_call(
        body, name=name, out_shape=jax.ShapeDtypeStruct((1, n), F32), in_specs=[vmem], out_specs=vmem,
        scratch_shapes=[pltpu.VMEM((N_DEV, 1, n), F32), pltpu.SemaphoreType.DMA((7,)), pltpu.SemaphoreType.DMA((7,))],
    )(vec)


def _mm_core(pairs, dims, out_dtype, *, name, grid, out_shape, out_spec, acc_shape, alpha=1.0, res=None, comm=None):
    nk = grid[2]
    npairs = len(pairs)

    def body(*refs):
        ab = refs[:2 * npairs]
        rest = refs[2 * npairs:]
        res_ref = rest[0] if res is not None else None
        o_ref = rest[1] if res is not None else rest[0]
        acc_ref = rest[-1] if nk > 1 else None
        part = None
        for p in range(npairs):
            d = _dot(ab[2 * p][...].astype(BF16), ab[2 * p + 1][...].astype(BF16), dims)
            part = d if part is None else part + d

        def finish(total):
            val = total * alpha if alpha != 1.0 else total
            if res_ref is not None:
                val = res_ref[...] + val
            o_ref[...] = val.astype(out_dtype)

        if nk == 1:
            finish(part)
        else:
            kk = pl.program_id(2)

            @pl.when(kk == 0)
            def _():
                acc_ref[...] = part

            @pl.when(kk > 0)
            def _():
                acc_ref[...] += part

            @pl.when(kk == nk - 1)
            def _():
                finish(acc_ref[...])

    operands, in_specs = [], []
    for (a, a_spec), (b, b_spec) in pairs:
        operands += [a, b]
        in_specs += [a_spec, b_spec]
    if res is not None:
        operands.append(res[0])
        in_specs.append(res[1])
    out, got = _call(body, name=name, grid=grid, in_specs=in_specs, out_specs=[out_spec],
                     out_shape=[jax.ShapeDtypeStruct(out_shape, out_dtype)], operands=operands,
                     scratch_shapes=[pltpu.VMEM(acc_shape, F32)] if nk > 1 else [], comm=comm)
    return out[0] if comm is None else (out[0], got)


def _mm(pairs, mode, out_dtype, *, name, alpha=1.0, res=None):
    a0, b0 = pairs[0]
    if mode == "nn":
        (m, k), n = a0.shape, b0.shape[1]
    elif mode == "nt":
        (m, k), n = a0.shape, b0.shape[0]
    else:
        (k, m), n = a0.shape, b0.shape[1]
    dims = {"nn": NN_DIMS, "nt": NT_DIMS, "tn": TN_DIMS}[mode]
    tm = _tile(m, 1056, LANES if mode == "tn" else 16)
    tn = _tile(n, 1024, LANES)
    tk = _tile(k, 2048 if mode != "tn" else 1056, LANES if mode != "tn" else 16)
    a_spec = pl.BlockSpec((tk, tm), lambda i, j, kk: (kk, i)) if mode == "tn" else pl.BlockSpec((tm, tk), lambda i, j, kk: (i, kk))
    b_spec = pl.BlockSpec((tn, tk), lambda i, j, kk: (j, kk)) if mode == "nt" else pl.BlockSpec((tk, tn), lambda i, j, kk: (kk, j))
    o_spec = pl.BlockSpec((tm, tn), lambda i, j, kk: (i, j))
    return _mm_core([((a, a_spec), (b, b_spec)) for a, b in pairs], dims, out_dtype, name=name,
                    grid=(m // tm, n // tn, k // tk), out_shape=(m, n), out_spec=o_spec, acc_shape=(tm, tn),
                    alpha=alpha, res=None if res is None else (res, o_spec))


def _rms_fwd(h, gain, name):
    m, d = h.shape
    tm = _tile(m, 528, 16)

    def body(h_ref, g_ref, o_ref):
        hv = h_ref[...]
        r = lax.rsqrt(jnp.mean(hv * hv, axis=-1, keepdims=True) + EPS)
        o_ref[...] = (hv * r * g_ref[...]).astype(BF16)

    row = pl.BlockSpec((tm, d), lambda i: (i, 0))
    out, _ = _call(body, name=name, grid=(m // tm,), in_specs=[row, pl.BlockSpec((1, d), lambda i: (0, 0))],
                   out_specs=[row], out_shape=[jax.ShapeDtypeStruct((m, d), BF16)], operands=[h, gain])
    return out[0]


def _rms_bwd(dxn, h, gain, dres, name):
    m, d = h.shape
    tm = _tile(m, 264, 8)

    def body(dxn_ref, h_ref, g_ref, dres_ref, dh_ref, dg_ref):
        hv = h_ref[...]
        r = lax.rsqrt(jnp.mean(hv * hv, axis=-1, keepdims=True) + EPS)
        xhat = hv * r
        dxn_v = dxn_ref[...]
        t = dxn_v * g_ref[...]
        dh_ref[...] = dres_ref[...] + r * (t - xhat * jnp.mean(t * xhat, axis=-1, keepdims=True))
        part = jnp.sum(dxn_v * xhat, axis=0, keepdims=True)

        @pl.when(pl.program_id(0) == 0)
        def _():
            dg_ref[...] = part

        @pl.when(pl.program_id(0) > 0)
        def _():
            dg_ref[...] += part

    row = pl.BlockSpec((tm, d), lambda i: (i, 0))
    vec = pl.BlockSpec((1, d), lambda i: (0, 0))
    out, _ = _call(body, name=name, grid=(m // tm,), in_specs=[row, row, vec, row], out_specs=[row, vec],
                   out_shape=[jax.ShapeDtypeStruct((m, d), F32), jax.ShapeDtypeStruct((1, d), F32)],
                   operands=[dxn, h, gain, dres])
    return out


def _loss_head(h, gain, target, x_off, name):
    m, d = h.shape
    tm = ROW_BLOCK
    first = x_off // tm

    def body(h_ref, g_ref, t_ref, dh_ref, dg_ref, loss_ref):
        i = pl.program_id(0)

        @pl.when(i == 0)
        def _():
            dg_ref[...] = jnp.zeros_like(dg_ref)
            loss_ref[...] = jnp.zeros_like(loss_ref)

        @pl.when(i < first)
        def _():
            dh_ref[...] = jnp.zeros_like(dh_ref)

        @pl.when(i >= first)
        def _():
            hv = h_ref[...]
            g = g_ref[...]
            r = lax.rsqrt(jnp.mean(hv * hv, axis=-1, keepdims=True) + EPS)
            xhat = hv * r
            err = xhat * g - t_ref[...]
            loss_ref[...] += 0.5 * jnp.sum(jnp.mean(err * err, axis=-1, keepdims=True))
            dy = err * (1.0 / d)
            t = dy * g
            dh_ref[...] = r * (t - xhat * jnp.mean(t * xhat, axis=-1, keepdims=True))
            dg_ref[...] += jnp.sum(dy * xhat, axis=0, keepdims=True)

    row = pl.BlockSpec((tm, d), lambda i: (i, 0))
    vec = pl.BlockSpec((1, d), lambda i: (0, 0))
    out, _ = _call(body, name=name, grid=(m // tm,),
                   in_specs=[row, vec, pl.BlockSpec((tm, d), lambda i: (jnp.maximum(i - first, 0), 0))],
                   out_specs=[row, vec, pl.BlockSpec((8, LANES), lambda i: (0, 0))],
                   out_shape=[jax.ShapeDtypeStruct((m, d), F32), jax.ShapeDtypeStruct((1, d), F32),
                              jax.ShapeDtypeStruct((8, LANES), F32)],
                   operands=[h, gain, target])
    return out


def _sigmoid(z):
    return 1.0 / (1.0 + jnp.exp(-z))


def _ffn_up(xn, wg, wu, name, comm=None):
    m, d = xn.shape
    nsh, _, c = wg.shape
    tm = _tile(m, 1056, 16)

    def body(x_ref, wg_ref, wu_ref, g_ref, u_ref, a_ref):
        xv = x_ref[...]
        g = _dot(xv, wg_ref[...])
        u = _dot(xv, wu_ref[...])
        g_ref[...] = g.astype(BF16)
        u_ref[...] = u.astype(BF16)
        a_ref[...] = (g * _sigmoid(g) * u).astype(BF16)

    out = pl.BlockSpec((None, tm, c), lambda i, j: (j, i, 0))
    w = pl.BlockSpec((None, d, c), lambda i, j: (j, 0, 0))
    return _call(body, name=name, grid=(m // tm, nsh), in_specs=[pl.BlockSpec((tm, d), lambda i, j: (i, 0)), w, w],
                 out_specs=[out, out, out], out_shape=[jax.ShapeDtypeStruct((nsh, m, c), BF16)] * 3,
                 operands=[xn, wg, wu], comm=comm)


def _ffn_down(act, wd, h, name, comm=None):
    nsh, m, c = act.shape
    d = wd.shape[2]
    tm, tn = _tile(m, 1056, 16), _tile(d, 1024, LANES)
    o_spec = pl.BlockSpec((tm, tn), lambda i, j, kk: (i, j))
    return _mm_core([((act, pl.BlockSpec((None, tm, c), lambda i, j, kk: (kk, i, 0))),
                      (wd, pl.BlockSpec((None, c, tn), lambda i, j, kk: (kk, 0, j))))],
                    NN_DIMS, F32, name=name, grid=(m // tm, d // tn, nsh), out_shape=(m, d), out_spec=o_spec,
                    acc_shape=(tm, tn), alpha=0.5, res=(h, o_spec), comm=comm)


def _ffn_bwd_act(dh, wd, g, u, name, comm=None):
    m, d = dh.shape
    nsh, c, _ = wd.shape
    tm = _tile(m, 528, 16)

    def body(dh_ref, wd_ref, g_ref, u_ref, dg_ref, du_ref):
        dact = 0.5 * _dot(dh_ref[...].astype(BF16), wd_ref[...], NT_DIMS)
        gv = g_ref[...].astype(F32)
        uv = u_ref[...].astype(F32)
        sig = _sigmoid(gv)
        du_ref[...] = (dact * gv * sig).astype(BF16)
        dg_ref[...] = (dact * uv * sig * (1.0 + gv * (1.0 - sig))).astype(BF16)

    blk = pl.BlockSpec((None, tm, c), lambda i, j: (j, i, 0))
    return _call(body, name=name, grid=(m // tm, nsh),
                 in_specs=[pl.BlockSpec((tm, d), lambda i, j: (i, 0)), pl.BlockSpec((None, c, d), lambda i, j: (j, 0, 0)), blk, blk],
                 out_specs=[blk, blk], out_shape=[jax.ShapeDtypeStruct((nsh, m, c), BF16)] * 2, operands=[dh, wd, g, u],
                 comm=comm)


def _ffn_dwd(act, dh, name):
    nsh, m, c = act.shape
    d = dh.shape[1]
    tn, tk = _tile(d, 1024, LANES), _tile(m, 1056, 16)
    return _mm_core([((act, pl.BlockSpec((None, tk, c), lambda i, j, kk: (i, kk, 0))),
                      (dh, pl.BlockSpec((tk, tn), lambda i, j, kk: (kk, j))))],
                    TN_DIMS, BF16, name=name, grid=(nsh, d // tn, m // tk), out_shape=(nsh, c, d),
                    out_spec=pl.BlockSpec((None, c, tn), lambda i, j, kk: (i, 0, j)), acc_shape=(c, tn), alpha=0.5)


def _ffn_dw_in(xn, dz, name, comm=None):
    m, d = xn.shape
    nsh, _, c = dz.shape
    tm, tk = _tile(d, 1024, LANES), _tile(m, 1056, 16)
    return _mm_core([((xn, pl.BlockSpec((tk, tm), lambda i, j, kk: (kk, i))),
                      (dz, pl.BlockSpec((None, tk, c), lambda i, j, kk: (j, kk, 0))))],
                    TN_DIMS, BF16, name=name, grid=(d // tm, nsh, m // tk), out_shape=(nsh, d, c),
                    out_spec=pl.BlockSpec((None, tm, c), lambda i, j, kk: (j, i, 0)), acc_shape=(tm, c), comm=comm)


def _ffn_dxn(dg, wg, du, wu, name, comm=None):
    nsh, m, c = dg.shape
    d = wg.shape[1]
    tm, tn = _tile(m, 1056, 16), _tile(d, 1024, LANES)
    a_spec = pl.BlockSpec((None, tm, c), lambda i, j, kk: (kk, i, 0))
    b_spec = pl.BlockSpec((None, tn, c), lambda i, j, kk: (kk, j, 0))
    return _mm_core([((dg, a_spec), (wg, b_spec)), ((du, a_spec), (wu, b_spec))], NT_DIMS, F32, name=name,
                    grid=(m // tm, d // tn, nsh), out_shape=(m, d), out_spec=pl.BlockSpec((tm, tn), lambda i, j, kk: (i, j)),
                    acc_shape=(tm, tn), comm=comm)


def _dot3(tri, x):
    h1 = x.astype(BF16)
    r1 = x - h1.astype(F32)
    h2 = r1.astype(BF16)
    h3 = (r1 - h2.astype(F32)).astype(BF16)
    return _dot(tri, h1) + _dot(tri, h2) + _dot(tri, h3)


def _log_sigmoid(z):
    return jnp.minimum(z, 0.0) - jnp.log(1.0 + jnp.exp(-jnp.abs(z)))


def _triangle(t, cmp):
    return cmp(lax.broadcasted_iota(jnp.int32, (t, t), 0), lax.broadcasted_iota(jnp.int32, (t, t), 1)).astype(BF16)


def _forget_cumsum(fl, bias, n_heads, pad, name):
    m = fl.shape[0]
    nb = m // ROW_BLOCK

    def body(fl_ref, b_ref, c_ref):
        tri = _triangle(ROW_BLOCK, lambda r, c: r >= c)
        lane_ok = lax.broadcasted_iota(jnp.int32, (ROW_BLOCK, LANES), 1) < n_heads
        rows = lax.broadcasted_iota(jnp.int32, (ROW_BLOCK, LANES), 0)

        def step(b, carry):
            off = pl.multiple_of(b * ROW_BLOCK, ROW_BLOCK)
            lf = _log_sigmoid(fl_ref[pl.ds(off, ROW_BLOCK), :] + b_ref[...])
            lf = jnp.where(lane_ok & (rows + off >= pad), lf, 0.0)
            cs = _dot3(tri, lf) + carry
            c_ref[pl.ds(off, ROW_BLOCK), :] = cs
            return cs[ROW_BLOCK - 1:ROW_BLOCK, :]

        lax.fori_loop(0, nb, step, jnp.zeros((1, LANES), F32))

    vmem = pl.BlockSpec(memory_space=pltpu.VMEM)
    return pl.pallas_call(
        body, name=name, out_shape=jax.ShapeDtypeStruct((m, LANES), F32), in_specs=[vmem, vmem], out_specs=vmem,
        compiler_params=pltpu.CompilerParams(vmem_limit_bytes=VMEM_LIMIT_BYTES),
    )(fl, bias)


def _forget_cumsum_bwd(dc_a, dc_b, fl, bias, n_heads, pad, name):
    m = fl.shape[0]
    nb = m // ROW_BLOCK

    def body(da_ref, db_ref, fl_ref, b_ref, dfl_ref, dbias_ref):
        tri = _triangle(ROW_BLOCK, lambda r, c: r <= c)
        lane_ok = lax.broadcasted_iota(jnp.int32, (ROW_BLOCK, LANES), 1) < n_heads
        rows = lax.broadcasted_iota(jnp.int32, (ROW_BLOCK, LANES), 0)

        def step(bb, carry):
            tail, dbias = carry
            off = pl.multiple_of((nb - 1 - bb) * ROW_BLOCK, ROW_BLOCK)
            dc = da_ref[pl.ds(off, ROW_BLOCK), :] + db_ref[pl.ds(off, ROW_BLOCK), :]
            dlf = _dot3(tri, dc) + tail
            z = fl_ref[pl.ds(off, ROW_BLOCK), :] + b_ref[...]
            dfl = jnp.where(lane_ok & (rows + off >= pad), dlf * _sigmoid(-z), 0.0)
            dfl_ref[pl.ds(off, ROW_BLOCK), :] = dfl
            return dlf[0:1, :], dbias + jnp.sum(dfl, axis=0, keepdims=True)

        zero = jnp.zeros((1, LANES), F32)
        _, dbias = lax.fori_loop(0, nb, step, (zero, zero))
        dbias_ref[...] = dbias

    vmem = pl.BlockSpec(memory_space=pltpu.VMEM)
    return pl.pallas_call(
        body, name=name,
        out_shape=[jax.ShapeDtypeStruct((m, LANES), F32), jax.ShapeDtypeStruct((1, LANES), F32)],
        in_specs=[vmem] * 4, out_specs=[vmem, vmem],
        compiler_params=pltpu.CompilerParams(vmem_limit_bytes=VMEM_LIMIT_BYTES),
    )(dc_a, dc_b, fl, bias)


def _attn_block(m):
    return 3 * ROW_BLOCK if m % (3 * ROW_BLOCK) == 0 else ROW_BLOCK


def _head_norm(o, gain):
    r = lax.rsqrt(jnp.mean(o * o, axis=-1, keepdims=True) + EPS)
    return o * r * gain


def _head_norm_bwd(o, d_on, gain):
    r = lax.rsqrt(jnp.mean(o * o, axis=-1, keepdims=True) + EPS)
    ohat = o * r
    t = d_on * gain
    d_o = r * (t - ohat * jnp.mean(t * ohat, axis=-1, keepdims=True))
    return d_o, jnp.sum(d_on * ohat, axis=0, keepdims=True)


def _qkv_specs(t, m, h, first_col_block):
    q = pl.BlockSpec((t, HEAD_DIM), lambda hd, i: (i, first_col_block + hd))
    k = pl.BlockSpec((m, HEAD_DIM), lambda hd, i: (0, first_col_block + h + hd))
    v = pl.BlockSpec((m, HEAD_DIM), lambda hd, i: (0, first_col_block + 2 * h + hd))
    return q, k, v


def _fox_fwd(qkv, ccol, crow, gain, n_heads, pad, name, comm=None):
    m = qkv.shape[0]
    t = _attn_block(m)
    nq = m // t
    scale = HEAD_DIM ** -0.5
    hw = n_heads * HEAD_DIM

    def body(q_ref, k_ref, v_ref, ccol_ref, crow_ref, g_ref, o_ref, on_ref, lse_ref):
        i = pl.program_id(1)
        q = q_ref[...]
        ci = ccol_ref[...]
        qpos = i * t + lax.broadcasted_iota(jnp.int32, (t, 1), 0)

        def step(j, carry, masked):
            mx, l, acc = carry
            off = pl.multiple_of(j * t, t)
            k = k_ref[pl.ds(off, t), :]
            v = v_ref[pl.ds(off, t), :]
            s = _dot(q, k, NT_DIMS) * scale + ci - crow_ref[j]
            if masked:
                kpos = off + lax.broadcasted_iota(jnp.int32, (1, t), 1)
                s = jnp.where((kpos <= qpos) & (kpos >= pad), s, NEG)
            mx_new = jnp.maximum(mx, jnp.max(s, axis=-1, keepdims=True))
            p = jnp.exp(s - mx_new)
            a = jnp.exp(mx - mx_new)
            return mx_new, a * l + jnp.sum(p, axis=-1, keepdims=True), a * acc + _dot(p.astype(BF16), v)

        carry = step(0, (jnp.full((t, 1), NEG, F32), jnp.zeros((t, 1), F32), jnp.zeros((t, HEAD_DIM), F32)), True)
        carry = lax.fori_loop(1, i, lambda j, c: step(j, c, False), carry)
        mx, l, acc = lax.fori_loop(0, jnp.minimum(i, 1), lambda _, c: step(i, c, True), carry)
        valid = qpos >= pad
        o = jnp.where(valid, acc / l, 0.0)
        o_ref[...] = o
        on_ref[...] = _head_norm(o, g_ref[...]).astype(BF16)
        lse_ref[...] = jnp.where(valid, mx + jnp.log(l), 0.0)

    q_spec, k_spec, v_spec = _qkv_specs(t, m, n_heads, 0)
    col = pl.BlockSpec((None, t, 1), lambda hd, i: (hd, i, 0))
    head = pl.BlockSpec((t, HEAD_DIM), lambda hd, i: (i, hd))
    return _call(body, name=name, grid=(n_heads, nq),
                 in_specs=[q_spec, k_spec, v_spec, col, pl.BlockSpec((None, nq, 1, t), lambda hd, i: (hd, 0, 0, 0)),
                           pl.BlockSpec((1, HEAD_DIM), lambda hd, i: (0, hd))],
                 out_specs=[head, head, col],
                 out_shape=[jax.ShapeDtypeStruct((m, hw), F32), jax.ShapeDtypeStruct((m, hw), BF16),
                            jax.ShapeDtypeStruct((n_heads, m, 1), F32)],
                 operands=[qkv, qkv, qkv, ccol, crow, gain], comm=comm)


def _fox_bwd(qkv, o, d_on, gain, lse, ccol, crow, n_heads, pad, name, comm=None):
    m = qkv.shape[0]
    t = _attn_block(m)
    nq = m // t
    scale = HEAD_DIM ** -0.5
    hw = n_heads * HEAD_DIM

    def body(q_ref, k_ref, v_ref, o_ref, don_ref, g_ref, lse_ref, ccol_ref, crow_ref,
             dq_ref, dk_ref, dv_ref, dg_ref, dccol_ref, dcrow_ref):
        i = pl.program_id(1)

        @pl.when(i == 0)
        def _():
            dk_ref[...] = jnp.zeros_like(dk_ref)
            dv_ref[...] = jnp.zeros_like(dv_ref)
            dg_ref[...] = jnp.zeros_like(dg_ref)
            dcrow_ref[...] = jnp.zeros_like(dcrow_ref)

        q = q_ref[...]
        o = o_ref[...]
        d_o, dgain = _head_norm_bwd(o, don_ref[...], g_ref[...])
        dg_ref[...] += dgain
        delta = jnp.sum(d_o * o, axis=-1, keepdims=True)
        d_ob = d_o.astype(BF16)
        ci = ccol_ref[...]
        lse_i = lse_ref[...]
        qpos = i * t + lax.broadcasted_iota(jnp.int32, (t, 1), 0)

        def step(j, carry, masked):
            dq, dci = carry
            off = pl.multiple_of(j * t, t)
            k = k_ref[pl.ds(off, t), :]
            v = v_ref[pl.ds(off, t), :]
            s = _dot(q, k, NT_DIMS) * scale + ci - crow_ref[j]
            if masked:
                kpos = off + lax.broadcasted_iota(jnp.int32, (1, t), 1)
                ok = (kpos <= qpos) & (kpos >= pad)
                p = jnp.where(ok, jnp.exp(jnp.where(ok, s - lse_i, 0.0)), 0.0)
            else:
                p = jnp.exp(s - lse_i)
            ds = p * (_dot(d_ob, v, NT_DIMS) - delta)
            dsb = ds.astype(BF16)
            dk_ref[pl.ds(off, t), :] += _dot(dsb, q, TN_DIMS) * scale
            dv_ref[pl.ds(off, t), :] += _dot(p.astype(BF16), d_ob, TN_DIMS)
            dcrow_ref[j] -= jnp.sum(ds, axis=0, keepdims=True)
            return dq + _dot(dsb, k), dci + jnp.sum(ds, axis=-1, keepdims=True)

        carry = step(0, (jnp.zeros((t, HEAD_DIM), F32), jnp.zeros((t, 1), F32)), True)
        carry = lax.fori_loop(1, i, lambda j, c: step(j, c, False), carry)
        dq, dci = lax.fori_loop(0, jnp.minimum(i, 1), lambda _, c: step(i, c, True), carry)
        dq_ref[...] = (dq * scale).astype(BF16)
        dccol_ref[...] = dci

    q_spec, k_spec, v_spec = _qkv_specs(t, m, n_heads, 0)
    col = pl.BlockSpec((None, t, 1), lambda hd, i: (hd, i, 0))
    rowc = pl.BlockSpec((None, nq, 1, t), lambda hd, i: (hd, 0, 0, 0))
    head = pl.BlockSpec((t, HEAD_DIM), lambda hd, i: (i, hd))
    whole = pl.BlockSpec((m, HEAD_DIM), lambda hd, i: (0, hd))
    gvec = pl.BlockSpec((1, HEAD_DIM), lambda hd, i: (0, hd))
    return _call(body, name=name, grid=(n_heads, nq),
                 in_specs=[q_spec, k_spec, v_spec, head, head, gvec, col, col, rowc],
                 out_specs=[head, whole, whole, gvec, col, rowc],
                 out_shape=[jax.ShapeDtypeStruct((m, hw), BF16), jax.ShapeDtypeStruct((m, hw), F32),
                            jax.ShapeDtypeStruct((m, hw), F32), jax.ShapeDtypeStruct((1, hw), F32),
                            jax.ShapeDtypeStruct((n_heads, m, 1), F32), jax.ShapeDtypeStruct((n_heads, nq, 1, t), F32)],
                 operands=[qkv, qkv, qkv, o, d_on, gain, lse, ccol, crow], comm=comm)


def _sb_scores(q, k, scale):
    z = _dot(q, k, NT_DIMS) * scale
    lp = jnp.log(1.0 + jnp.exp(-jnp.abs(z)))
    return jnp.minimum(z, 0.0) - lp, jnp.minimum(-z, 0.0) - lp


def _sb_fwd(qkv, gain, n_heads, pad, name, comm=None):
    m = qkv.shape[0]
    t = _attn_block(m)
    nq = m // t
    assert nq <= LANES
    scale = HEAD_DIM ** -0.5
    hw = n_heads * HEAD_DIM

    def body(q_ref, k_ref, v_ref, g_ref, after_ref, o_ref, on_ref, run_ref):
        i = pl.program_id(1)
        q = q_ref[...]
        qpos = i * t + lax.broadcasted_iota(jnp.int32, (t, 1), 0)
        after = after_ref[...]
        lane = lax.broadcasted_iota(jnp.int32, (t, LANES), 1)

        def step(j, carry, masked):
            run, acc = carry
            off = pl.multiple_of(j * t, t)
            k = k_ref[pl.ds(off, t), :]
            v = v_ref[pl.ds(off, t), :]
            ls_pos, log_1m = _sb_scores(q, k, scale)
            if masked:
                kpos = off + lax.broadcasted_iota(jnp.int32, (1, t), 1)
                ok = (kpos < qpos) & (kpos >= pad)
                log_1m = jnp.where(ok, log_1m, 0.0)
            a = jnp.exp(ls_pos + _dot_split(log_1m, after) + run)
            if masked:
                a = jnp.where(ok, a, 0.0)
            run_ref[...] = jnp.where(lane == j, run, run_ref[...])
            return run + jnp.sum(log_1m, axis=-1, keepdims=True), acc + _dot(a.astype(BF16), v)

        run_ref[...] = jnp.zeros_like(run_ref)
        carry = step(i, (jnp.zeros((t, 1), F32), jnp.zeros((t, HEAD_DIM), F32)), True)
        carry = lax.fori_loop(1, i, lambda jj, c: step(i - jj, c, False), carry)
        _, o = lax.fori_loop(0, jnp.minimum(i, 1), lambda _, c: step(0, c, True), carry)
        o_ref[...] = o
        on_ref[...] = _head_norm(o, g_ref[...]).astype(BF16)

    q_spec, k_spec, v_spec = _qkv_specs(t, m, n_heads, 3 * n_heads)
    head = pl.BlockSpec((t, HEAD_DIM), lambda hd, i: (i, hd))
    return _call(body, name=name, grid=(n_heads, nq),
                 in_specs=[q_spec, k_spec, v_spec, pl.BlockSpec((1, HEAD_DIM), lambda hd, i: (0, hd)),
                           pl.BlockSpec((t, t), lambda hd, i: (0, 0))],
                 out_specs=[head, head, pl.BlockSpec((None, t, LANES), lambda hd, i: (hd, i, 0))],
                 out_shape=[jax.ShapeDtypeStruct((m, hw), F32), jax.ShapeDtypeStruct((m, hw), BF16),
                            jax.ShapeDtypeStruct((n_heads, m, LANES), F32)],
                 operands=[qkv, qkv, qkv, gain, _triangle(t, lambda r, c: r > c)], comm=comm)


def _sb_bwd(qkv, o, d_on, gain, runs, n_heads, pad, name, comm=None):
    m = qkv.shape[0]
    t = _attn_block(m)
    nq = m // t
    scale = HEAD_DIM ** -0.5
    hw = n_heads * HEAD_DIM

    def body(q_ref, k_ref, v_ref, o_ref, don_ref, g_ref, run_ref, after_ref, before_ref, dq_ref, dk_ref, dv_ref, dg_ref):
        i = pl.program_id(1)

        @pl.when(i == 0)
        def _():
            dk_ref[...] = jnp.zeros_like(dk_ref)
            dv_ref[...] = jnp.zeros_like(dv_ref)
            dg_ref[...] = jnp.zeros_like(dg_ref)

        q = q_ref[...]
        d_o, dgain = _head_norm_bwd(o_ref[...], don_ref[...], g_ref[...])
        dg_ref[...] += dgain
        d_ob = d_o.astype(BF16)
        runs_i = run_ref[...]
        after = after_ref[...]
        before = before_ref[...]
        rc = t
        lane = lax.broadcasted_iota(jnp.int32, (rc, LANES), 1)
        chunks = [slice(r, r + rc) for r in range(0, t, rc)]

        def step(j, carry, masked):
            g_run, dq = carry
            off = pl.multiple_of(j * t, t)
            k = k_ref[pl.ds(off, t), :]
            v = v_ref[pl.ds(off, t), :]
            kpos = off + lax.broadcasted_iota(jnp.int32, (1, t), 1)
            dz_all, a_all, g_new, dq_new = [], [], [], []
            for ch, rows in enumerate(chunks):
                ls_pos, ls_neg = _sb_scores(q[rows], k, scale)
                if masked:
                    qpos = i * t + rows.start + lax.broadcasted_iota(jnp.int32, (rc, 1), 0)
                    ok = (kpos < qpos) & (kpos >= pad)
                    log_1m = jnp.where(ok, ls_neg, 0.0)
                else:
                    log_1m = ls_neg
                run = jnp.sum(jnp.where(lane == j, runs_i[rows], 0.0), axis=-1, keepdims=True)
                a = jnp.exp(ls_pos + _dot_split(log_1m, after) + run)
                if masked:
                    a = jnp.where(ok, a, 0.0)
                g = a * _dot(d_ob[rows], v, NT_DIMS)
                prefix = _dot(g.astype(BF16), before) + g_run[ch]
                dz = g * jnp.exp(ls_neg) - jnp.exp(ls_pos) * prefix
                if masked:
                    dz = jnp.where(ok, dz, 0.0)
                dzb = dz.astype(BF16)
                dz_all.append(dzb)
                a_all.append(a.astype(BF16))
                g_new.append(g_run[ch] + jnp.sum(g, axis=-1, keepdims=True))
                dq_new.append(dq[ch] + _dot(dzb, k))
            dk_ref[pl.ds(off, t), :] += _dot(jnp.concatenate(dz_all, axis=0), q, TN_DIMS) * scale
            dv_ref[pl.ds(off, t), :] += _dot(jnp.concatenate(a_all, axis=0), d_ob, TN_DIMS)
            return tuple(g_new), tuple(dq_new)

        carry = (tuple(jnp.zeros((rc, 1), F32) for _ in chunks), tuple(jnp.zeros((rc, HEAD_DIM), F32) for _ in chunks))
        carry = step(0, carry, True)
        carry = lax.fori_loop(1, i, lambda j, c: step(j, c, False), carry)
        carry = lax.fori_loop(0, jnp.minimum(i, 1), lambda _, c: step(i, c, True), carry)
        for rows, dq_c in zip(chunks, carry[1]):
            dq_ref[rows, :] = (dq_c * scale).astype(BF16)

    q_spec, k_spec, v_spec = _qkv_specs(t, m, n_heads, 3 * n_heads)
    head = pl.BlockSpec((t, HEAD_DIM), lambda hd, i: (i, hd))
    whole = pl.BlockSpec((m, HEAD_DIM), lambda hd, i: (0, hd))
    gvec = pl.BlockSpec((1, HEAD_DIM), lambda hd, i: (0, hd))
    tri = pl.BlockSpec((t, t), lambda hd, i: (0, 0))
    return _call(body, name=name, grid=(n_heads, nq),
                 in_specs=[q_spec, k_spec, v_spec, head, head, gvec, pl.BlockSpec((None, t, LANES), lambda hd, i: (hd, i, 0)),
                           tri, tri],
                 out_specs=[head, whole, whole, gvec],
                 out_shape=[jax.ShapeDtypeStruct((m, hw), BF16), jax.ShapeDtypeStruct((m, hw), F32),
                            jax.ShapeDtypeStruct((m, hw), F32), jax.ShapeDtypeStruct((1, hw), F32)],
                 operands=[qkv, qkv, qkv, o, d_on, gain, runs, _triangle(t, lambda r, c: r > c), _triangle(t, lambda r, c: r < c)],
                 comm=comm)


def _adamw(parts, w, m1, v2, name):
    nl, r, c = w.shape
    assert len(parts) == nl
    n_parts = parts[0].shape[0]
    lanes_padded = -(-c // LANES) * LANES
    tr = _tile(r, max(8, (128 * 1024) // lanes_padded), 8)
    nt = r // tr
    bias1 = 1.0 / (1.0 - ADAM_B1 ** ADAM_STEP)
    bias2 = 1.0 / (1.0 - ADAM_B2 ** ADAM_STEP)

    def body(*refs):
        p_refs = refs[:nl]
        w_ref, m_ref, v_ref, g_ref, d_ref, nm_ref, nv_ref = refs[nl:]

        def update(p_ref):
            g = p_ref[0].astype(F32)
            for s in range(1, n_parts):
                g = g + p_ref[s].astype(F32)
            m_new = ADAM_B1 * m_ref[...] + (1.0 - ADAM_B1) * g
            v_new = ADAM_B2 * v_ref[...] + (1.0 - ADAM_B2) * (g * g)
            g_ref[...] = g
            nm_ref[...] = m_new
            nv_ref[...] = v_new
            d_ref[...] = -ADAM_LR * ((m_new * bias1) / (jnp.sqrt(v_new * bias2) + ADAM_EPS) + ADAM_WD * w_ref[...])

        for ll in range(nl):
            @pl.when(pl.program_id(0) == ll)
            def _(ll=ll):
                update(p_refs[ll])

    def part_spec(ll):
        return pl.BlockSpec((n_parts, tr, c), lambda l, i: (0, jnp.where(l == ll, i, jnp.where(l < ll, 0, nt - 1)), 0))

    blk = pl.BlockSpec((None, tr, c), lambda l, i: (l, i, 0))
    out, _ = _call(body, name=name, grid=(nl, nt), in_specs=[part_spec(ll) for ll in range(nl)] + [blk, blk, blk],
                   out_specs=[blk] * 4, out_shape=[jax.ShapeDtypeStruct((nl, r, c), F32)] * 4,
                   operands=list(parts) + [w, m1, v2])
    return out


SMALL_WEIGHTS = ("ffn1_norm", "mix_norm", "b_forget", "g_fox", "g_sb", "ffn2_norm", "final_norm")
WEIGHT_ORDER = ("meta_tokens", "ffn1_norm", "ffn1_w_gate", "ffn1_w_up", "ffn1_w_down", "mix_norm", "w_in", "b_forget",
                "g_fox", "g_sb", "w_out", "ffn2_norm", "ffn2_w_gate", "ffn2_w_up", "ffn2_w_down", "final_norm")
GROUPS = {"ffn1": ("ffn1_w_gate", "ffn1_w_up", "ffn1_w_down"), "mix": ("w_in", "w_out"),
          "ffn2": ("ffn2_w_gate", "ffn2_w_up", "ffn2_w_down")}


def _pad_lanes(a):
    extra = (-a.shape[-1]) % LANES
    return a if extra == 0 else jnp.pad(a, [(0, 0)] * (a.ndim - 1) + [(0, extra)])


def _ffn_backward(dh_out, saved, gain, wg, wu, wd, tag):
    h, xn, g, u, act = saved
    (dg, du), _ = _ffn_bwd_act(dh_out, wd, g, u, f"{tag}_bwd_act")
    d_wd = _ffn_dwd(act, dh_out, f"{tag}_dwd")
    d_wg = _ffn_dw_in(xn, dg, f"{tag}_dwg")
    d_wu = _ffn_dw_in(xn, du, f"{tag}_dwu")
    dxn = _ffn_dxn(dg, wg, du, wu, f"{tag}_dxn")
    dh, d_gain = _rms_bwd(dxn, h, gain, dh_out, f"{tag}_norm_bwd")
    return dh, d_gain, d_wg, d_wu, d_wd


def _ffn_backward_last(dh_out, saved, gain, wg, wu, wd, tag, others):
    h, xn, g, u, act = saved
    (dg, du), got_others = _ffn_bwd_act(dh_out, wd, g, u, f"{tag}_bwd_act", _Exchange(others))
    d_wd = _ffn_dwd(act, dh_out, f"{tag}_dwd")
    d_wg, got_wd = _ffn_dw_in(xn, dg, f"{tag}_dwg", _Exchange([d_wd]))
    d_wu, got_wg = _ffn_dw_in(xn, du, f"{tag}_dwu", _Exchange([d_wg]))
    dxn, got_wu = _ffn_dxn(dg, wg, du, wu, f"{tag}_dxn", _Exchange([d_wu]))
    dh, d_gain = _rms_bwd(dxn, h, gain, dh_out, f"{tag}_norm_bwd")
    return dh, d_gain, got_others, got_wg[0], got_wu[0], got_wd[0]


def _mixer_forward(h, gain, w_in_pad, bias, g_fox, g_sb, w_out, n_heads, pad, tag, comm_fox=None, comm_sb=None):
    m = h.shape[0]
    t = _attn_block(m)
    hw = n_heads * HEAD_DIM
    xn = _rms_fwd(h, gain, f"{tag}_norm")
    qkv = _mm([(xn, w_in_pad[:, :6 * hw])], "nn", BF16, name=f"{tag}_qkv")
    fl = _mm([(xn, w_in_pad[:, 6 * hw:])], "nn", F32, name=f"{tag}_forget")
    c = _forget_cumsum(fl, bias, n_heads, pad, f"{tag}_cumsum")
    c_heads = c[:, :n_heads].T
    ccol = c_heads[:, :, None]
    crow = c_heads.reshape(n_heads, m // t, 1, t)
    (o_f, on_f, lse), got_fox = _fox_fwd(qkv, ccol, crow, g_fox, n_heads, pad, f"{tag}_fox", comm_fox)
    (o_s, on_s, runs), got_sb = _sb_fwd(qkv, g_sb, n_heads, pad, f"{tag}_sb", comm_sb)
    h_out = _mm([(on_f, w_out[:hw]), (on_s, w_out[hw:])], "nn", F32, name=f"{tag}_out", res=h)
    return h_out, (h, xn, qkv, fl, ccol, crow, o_f, on_f, lse, o_s, on_s, runs), got_fox, got_sb


def _mixer_backward(dh_out, saved, gain, w_in_pad, bias, g_fox, g_sb, w_out, n_heads, pad, tag, comm_fox=None, comm_sb=None):
    h, xn, qkv, fl, ccol, crow, o_f, on_f, lse, o_s, on_s, runs = saved
    m = h.shape[0]
    hw = n_heads * HEAD_DIM
    d_on_f = _mm([(dh_out, w_out[:hw])], "nt", F32, name=f"{tag}_don_f")
    d_on_s = _mm([(dh_out, w_out[hw:])], "nt", F32, name=f"{tag}_don_s")
    d_wout = jnp.concatenate([_mm([(on_f, dh_out)], "tn", BF16, name=f"{tag}_dwout_f"),
                              _mm([(on_s, dh_out)], "tn", BF16, name=f"{tag}_dwout_s")], axis=0)
    (dq_f, dk_f, dv_f, dg_fox, dccol, dcrow), got_fox = _fox_bwd(
        qkv, o_f, d_on_f, g_fox, lse, ccol, crow, n_heads, pad, f"{tag}_fox_bwd", comm_fox)
    (dq_s, dk_s, dv_s, dg_sb), got_sb = _sb_bwd(qkv, o_s, d_on_s, g_sb, runs, n_heads, pad, f"{tag}_sb_bwd", comm_sb)
    dc_a = _pad_lanes(dccol[:, :, 0].T)
    dc_b = _pad_lanes(dcrow.reshape(n_heads, m).T)
    dfl, dbias = _forget_cumsum_bwd(dc_a, dc_b, fl, bias, n_heads, pad, f"{tag}_cumsum_bwd")
    dproj = jnp.concatenate([dq_f, dk_f.astype(BF16), dv_f.astype(BF16), dq_s, dk_s.astype(BF16), dv_s.astype(BF16),
                             dfl.astype(BF16)], axis=1)
    d_win = _mm([(xn, dproj)], "tn", BF16, name=f"{tag}_dwin")
    dxn = _mm([(dproj, w_in_pad)], "nt", F32, name=f"{tag}_dxn")
    dh, d_gain = _rms_bwd(dxn, h, gain, dh_out, f"{tag}_norm_bwd")
    return dh, d_gain, d_win, dbias, dg_fox, dg_sb, d_wout, got_fox, got_sb


def kernel(x, meta_tokens, ffn1_norm, ffn1_w_gate, ffn1_w_up, ffn1_w_down, mix_norm, w_in, b_forget, g_fox, g_sb, w_out, ffn2_norm, ffn2_w_gate, ffn2_w_up, ffn2_w_down, final_norm, loss_target, m_meta_tokens, m_ffn1_norm, m_ffn1_w_gate, m_ffn1_w_up, m_ffn1_w_down, m_mix_norm, m_w_in, m_b_forget, m_g_fox, m_g_sb, m_w_out, m_ffn2_norm, m_ffn2_w_gate, m_ffn2_w_up, m_ffn2_w_down, m_final_norm, v_meta_tokens, v_ffn1_norm, v_ffn1_w_gate, v_ffn1_w_up, v_ffn1_w_down, v_mix_norm, v_w_in, v_b_forget, v_g_fox, v_g_sb, v_w_out, v_ffn2_norm, v_ffn2_w_gate, v_ffn2_w_up, v_ffn2_w_down, v_final_norm):
    given = dict(locals())
    seq, d = x.shape[1], x.shape[2]
    depth = ffn1_norm.shape[0]
    d_in = N_DEV * w_in.shape[2]
    n_heads = g_fox.shape[1] // HEAD_DIM
    hw = n_heads * HEAD_DIM
    assert seq % ROW_BLOCK == 0 and d_in == 6 * hw + n_heads and n_heads <= LANES
    pad = (-(seq + N_META)) % ROW_BLOCK
    x_off = pad + N_META

    def shards(group, l):
        return [given[n][l].astype(BF16) for n in GROUPS[group]]

    sh = shards("ffn1", 0)
    first = _run_alone(_Gather(sh[:2] + [meta_tokens]), "gather_first")
    full = {}
    meta_full = first[2].transpose(1, 0, 2).reshape(N_META, d)
    h = jnp.concatenate([jnp.zeros((pad, d), F32), meta_full, x[0]], axis=0)
    weights, saved = [], []
    for l in range(depth):
        xn = _rms_fwd(h, ffn1_norm[l:l + 1], "ffn1_norm")
        if l == 0:
            (g, u, act), (wd1,) = _ffn_up(xn, first[0], first[1], "ffn1_up", _Gather(sh[2:]))
            full[("ffn1", 0)] = (first[0], first[1], wd1)
            h_out, full[("mix", 0)] = _ffn_down(act, wd1, h, "ffn1_down", _Gather(shards("mix", 0)))
        else:
            wg1, wu1, wd1 = full[("ffn1", l)]
            (g, u, act), full[("mix", l)] = _ffn_up(xn, wg1, wu1, "ffn1_up", _Gather(shards("mix", l)))
            h_out = _ffn_down(act, wd1, h, "ffn1_down")
        s1 = (h, xn, g, u, act)
        h = h_out
        win3, wout3 = full[("mix", l)]
        w_in_pad = _pad_lanes(win3.transpose(1, 0, 2).reshape(d, d_in))
        w_out_full = wout3.reshape(N_DEV * wout3.shape[1], d)
        bias = _pad_lanes(b_forget[l:l + 1])
        nxt = _Gather(shards("ffn1", l + 1)) if l + 1 < depth else None
        h, sm, full[("ffn2", l)], got = _mixer_forward(
            h, mix_norm[l:l + 1], w_in_pad, bias, g_fox[l:l + 1], g_sb[l:l + 1], w_out_full, n_heads, pad, "mix",
            _Gather(shards("ffn2", l)), nxt)
        if nxt is not None:
            full[("ffn1", l + 1)] = got
        wg2, wu2, wd2 = full[("ffn2", l)]
        xn = _rms_fwd(h, ffn2_norm[l:l + 1], "ffn2_norm")
        (g, u, act), _ = _ffn_up(xn, wg2, wu2, "ffn2_up")
        s2 = (h, xn, g, u, act)
        h = _ffn_down(act, wd2, h, "ffn2_down")
        weights.append((w_in_pad, w_out_full, bias))
        saved.append((s1, sm, s2))

    dh, d_final, loss_arr = _loss_head(h, final_norm[None, :], loss_target[0], x_off, "loss_head")
    small = {n: [None] * depth for n in SMALL_WEIGHTS[:-1]}
    partial, received = {}, {}

    def names(group, l):
        return [(n, l) for n in GROUPS[group]]

    for l in reversed(range(depth)):
        w_in_pad, w_out_full, bias = weights[l]
        s1, sm, s2 = saved[l]
        wg2, wu2, wd2 = full[("ffn2", l)]
        dh, small["ffn2_norm"][l], partial[("ffn2_w_gate", l)], partial[("ffn2_w_up", l)], partial[("ffn2_w_down", l)] = (
            _ffn_backward(dh, s2, ffn2_norm[l:l + 1], wg2, wu2, wd2, "ffn2"))
        in_fox = names("mix", l + 1) if l + 1 < depth else []
        in_sb = names("ffn2", l) + (names("ffn1", l + 1) if l + 1 < depth else [])
        dh, small["mix_norm"][l], d_win, dbias, small["g_fox"][l], small["g_sb"][l], d_wout, got_fox, got_sb = _mixer_backward(
            dh, sm, mix_norm[l:l + 1], w_in_pad, bias, g_fox[l:l + 1], g_sb[l:l + 1], w_out_full, n_heads, pad, "mix",
            _Exchange([partial[k] for k in in_fox]) if in_fox else None, _Exchange([partial[k] for k in in_sb]))
        received.update(zip(in_fox, got_fox or []))
        received.update(zip(in_sb, got_sb))
        partial[("w_in", l)] = d_win[:, :d_in].reshape(d, N_DEV, -1).transpose(1, 0, 2)
        partial[("w_out", l)] = d_wout.reshape(N_DEV, -1, d)
        small["b_forget"][l] = dbias[:, :n_heads]
        wg1, wu1, wd1 = full[("ffn1", l)]
        if l > 0:
            dh, small["ffn1_norm"][l], partial[("ffn1_w_gate", l)], partial[("ffn1_w_up", l)], partial[("ffn1_w_down", l)] = (
                _ffn_backward(dh, s1, ffn1_norm[l:l + 1], wg1, wu1, wd1, "ffn1"))
        else:
            (dh, small["ffn1_norm"][0], got_mix, received[("ffn1_w_gate", 0)], received[("ffn1_w_up", 0)],
             received[("ffn1_w_down", 0)]) = _ffn_backward_last(
                dh, s1, ffn1_norm[0:1], wg1, wu1, wd1, "ffn1", [partial[k] for k in names("mix", 0)])
            received.update(zip(names("mix", 0), got_mix))
    grad_x = dh[x_off:][None]
    d_meta = dh[pad:x_off].reshape(N_META, N_DEV, -1).transpose(1, 0, 2)
    received[("meta_tokens", 0)] = _run_alone(_Exchange([d_meta]), "exchange_meta")[0]

    vec = [loss_arr[0:1, :]] + [_pad_lanes(jnp.concatenate(small[n], axis=0).reshape(1, -1)) for n in SMALL_WEIGHTS[:-1]]
    vec.append(d_final)
    sizes = [a.shape[1] for a in vec]
    summed = _all_reduce_small(jnp.concatenate(vec, axis=1), "reduce_small")
    loss = summed[0, 0]

    def packed(prefix):
        cols = [jnp.zeros((1, LANES), F32)]
        cols += [_pad_lanes(given[prefix + n].reshape(1, -1)) for n in SMALL_WEIGHTS]
        return jnp.concatenate(cols, axis=1)[None]

    small_out = _adamw([summed[None]], packed(""), packed("m_"), packed("v_"), "adamw_small")

    out = {}
    for n in [w for g in GROUPS.values() for w in g]:
        out[n] = _adamw([received[(n, l)] for l in range(depth)], given[n], given["m_" + n], given["v_" + n], "adamw_" + n)
    out["meta_tokens"] = [r[0] for r in _adamw([received[("meta_tokens", 0)]], meta_tokens[None], m_meta_tokens[None],
                                               v_meta_tokens[None], "adamw_meta_tokens")]
    offset = sizes[0]
    for n, size in zip(SMALL_WEIGHTS, sizes[1:]):
        shape, count = given[n].shape, given[n].size
        out[n] = [r[0, 0, offset:offset + count].reshape(shape) for r in small_out]
        offset += size

    result = [loss, grad_x]
    for k in range(4):
        result += [out[n][k] for n in WEIGHT_ORDER]
    return tuple(result)
```

```python
import math

import jax
import jax.numpy as jnp
from jax import lax
from jax.experimental import pallas as pl
from jax.experimental.pallas import tpu as pltpu

F32 = jnp.float32
BF16 = jnp.bfloat16

N_DEV = 8
N_META = 16
HEAD_DIM = 128
ROW_BLOCK = 128
LANES = 128
EPS = 1e-6
NEG = -1e30
ADAM_LR = 0.001
ADAM_B1 = 0.9
ADAM_B2 = 0.999
ADAM_EPS = 1e-08
ADAM_WD = 0.01
ADAM_STEP = 10
VMEM_LIMIT_BYTES = 56 * 1024 * 1024
MESH = pl.DeviceIdType.MESH

NT_DIMS = (((1,), (1,)), ((), ()))
TN_DIMS = (((0,), (0,)), ((), ()))
NN_DIMS = (((1,), (0,)), ((), ()))
ANY = pl.BlockSpec(memory_space=pl.ANY)


def _tile(n, cap, align):
    best = None
    for d in range(align, min(n, cap) + 1, align):
        if n % d == 0:
            best = d
    return best if best is not None else n


def _dot(a, b, dims=NN_DIMS):
    return lax.dot_general(a, b, dims, preferred_element_type=F32)


def _dot_split(x, u):
    hi = x.astype(BF16)
    lo = (x - hi.astype(F32)).astype(BF16)
    return _dot(hi, u) + _dot(lo, u)


def _my_position():
    return lax.axis_index("x"), lax.axis_index("y"), lax.axis_index("c")


class _Gather:
    n_phases = 3

    def __init__(self, arrs):
        self.arrs = list(arrs)
        n = len(self.arrs)
        self.out_shapes = [jax.ShapeDtypeStruct((N_DEV,) + a.shape, a.dtype) for a in self.arrs]
        self.scratch = [pltpu.SemaphoreType.DMA((n, 7)), pltpu.SemaphoreType.DMA((n, 7)), pltpu.SemaphoreType.DMA((n,))]

    def phase(self, p, ins, outs, sems):
        send_sems, recv_sems, local_sems = sems
        n = len(self.arrs)
        x, y, c = _my_position()
        me, sibling = (x, y, c), (x, y, 1 - c)
        chips = [(1 - x, y), (x, 1 - y), (1 - x, 1 - y)]

        def copy(a, k, block, to, src=None):
            slot = outs[a].at[4 * block[0] + 2 * block[1] + block[2]]
            return pltpu.make_async_remote_copy(
                src_ref=slot if src is None else src, dst_ref=slot,
                send_sem=send_sems.at[a, k], recv_sem=recv_sems.at[a, k], device_id=to, device_id_type=MESH)

        def local(a):
            return pltpu.make_async_copy(ins[a], outs[a].at[4 * x + 2 * y + c], local_sems.at[a])

        def first(a):
            return [copy(a, 0, me, sibling, src=ins[a])] + [copy(a, 1 + j, me, (*chip, c), src=ins[a]) for j, chip in enumerate(chips)]

        def passed(a, j):
            return copy(a, 4 + j, (*chips[j], c), sibling)

        if p == 0:
            for a in range(n):
                local(a).start()
            for a in range(n):
                for cp in first(a):
                    cp.start()
        elif p == 1:
            for a in range(n):
                for j, chip in enumerate(chips):
                    copy(a, 1 + j, (*chip, c), me).wait_recv()
                    passed(a, j).start()
        else:
            for a in range(n):
                copy(a, 0, sibling, me).wait_recv()
                for j, chip in enumerate(chips):
                    copy(a, 4 + j, (*chip, 1 - c), me).wait_recv()
            for a in range(n):
                for cp in first(a) + [passed(a, j) for j in range(3)]:
                    cp.wait_send()
                local(a).wait()


class _Exchange:
    n_phases = 2

    def __init__(self, arrs):
        self.arrs = list(arrs)
        n = len(self.arrs)
        self.out_shapes = [jax.ShapeDtypeStruct(a.shape, a.dtype) for a in self.arrs]
        self.scratch = [pltpu.SemaphoreType.DMA((n, 7)), pltpu.SemaphoreType.DMA((n, 7)), pltpu.SemaphoreType.DMA((n,))]

    def phase(self, p, ins, outs, sems):
        send_sems, recv_sems, local_sems = sems
        n = len(self.arrs)
        x, y, c = _my_position()
        me = 4 * x + 2 * y + c

        def peer_of(r):
            return (x ^ ((r >> 2) & 1), y ^ ((r >> 1) & 1), c ^ (r & 1))

        def copy(a, r):
            px, py, pc = peer_of(r)
            return pltpu.make_async_remote_copy(
                src_ref=ins[a].at[4 * px + 2 * py + pc], dst_ref=outs[a].at[me],
                send_sem=send_sems.at[a, r - 1], recv_sem=recv_sems.at[a, r - 1],
                device_id=(px, py, pc), device_id_type=MESH)

        def arrival(a, r):
            px, py, pc = peer_of(r)
            slot = outs[a].at[4 * px + 2 * py + pc]
            return pltpu.make_async_remote_copy(
                src_ref=slot, dst_ref=slot, send_sem=send_sems.at[a, r - 1], recv_sem=recv_sems.at[a, r - 1],
                device_id=(px, py, pc), device_id_type=MESH)

        def local(a):
            return pltpu.make_async_copy(ins[a].at[me], outs[a].at[me], local_sems.at[a])

        if p == 0:
            for a in range(n):
                local(a).start()
            for a in range(n):
                for r in range(1, N_DEV):
                    copy(a, r).start()
        else:
            for a in range(n):
                for r in range(1, N_DEV):
                    arrival(a, r).wait_recv()
            for a in range(n):
                for r in range(1, N_DEV):
                    copy(a, r).wait_send()
                local(a).wait()


def _run_alone(comm, name):
    n = len(comm.arrs)

    def body(*refs):
        for p in range(comm.n_phases):
            comm.phase(p, refs[:n], refs[n:2 * n], refs[2 * n:])

    return pl.pallas_call(body, name=name, out_shape=comm.out_shapes, in_specs=[ANY] * n, out_specs=[ANY] * n,
                          scratch_shapes=comm.scratch)(*comm.arrs)


def _call(body, *, name, grid, in_specs, out_specs, out_shape, operands, scratch_shapes=(), comm=None):
    scratch_shapes = list(scratch_shapes)
    params = pltpu.CompilerParams(dimension_semantics=("arbitrary",) * len(grid), vmem_limit_bytes=VMEM_LIMIT_BYTES)
    if comm is None:
        res = pl.pallas_call(body, name=name, grid=grid, in_specs=in_specs, out_specs=out_specs, out_shape=out_shape,
                             scratch_shapes=scratch_shapes, compiler_params=params)(*operands)
        return res, None
    n_in, n_out, n_sc = len(in_specs), len(out_specs), len(scratch_shapes)
    nc = len(comm.arrs)
    total = math.prod(grid)
    at = {0: 0, comm.n_phases - 1: total - 1}
    for p in range(1, comm.n_phases - 1):
        at[p] = (total * 7) // 8

    def wrapped(*refs):
        ins, cins = refs[:n_in], refs[n_in:n_in + nc]
        outs, couts = refs[n_in + nc:n_in + nc + n_out], refs[n_in + nc + n_out:n_in + 2 * nc + n_out]
        rest = refs[n_in + 2 * nc + n_out:]
        scratch, sems = rest[:n_sc], rest[n_sc:]
        step = 0
        for axis, size in enumerate(grid):
            step = step * size + pl.program_id(axis)
        for p in range(comm.n_phases - 1):
            @pl.when(step == at[p])
            def _(p=p):
                comm.phase(p, cins, couts, sems)
        body(*ins, *outs, *scratch)

        @pl.when(step == total - 1)
        def _():
            comm.phase(comm.n_phases - 1, cins, couts, sems)

    res = pl.pallas_call(
        wrapped, name=name, grid=grid, in_specs=list(in_specs) + [ANY] * nc, out_specs=list(out_specs) + [ANY] * nc,
        out_shape=list(out_shape) + comm.out_shapes, scratch_shapes=scratch_shapes + comm.scratch,
        compiler_params=params)(*operands, *comm.arrs)
    return res[:n_out], res[n_out:]


def _all_reduce_small(vec, name):
    n = vec.shape[1]

    def body(v_ref, o_ref, buf, send_sems, recv_sems):
        x, y, c = _my_position()
        me = 4 * x + 2 * y + c

        def peer_of(r):
            return (x ^ ((r >> 2) & 1), y ^ ((r >> 1) & 1), c ^ (r & 1))

        def copy(r):
            px, py, pc = peer_of(r)
            return pltpu.make_async_remote_copy(
                src_ref=v_ref, dst_ref=buf.at[me], send_sem=send_sems.at[r - 1], recv_sem=recv_sems.at[r - 1],
                device_id=(px, py, pc), device_id_type=MESH)

        def arrival(r):
            px, py, pc = peer_of(r)
            slot = buf.at[4 * px + 2 * py + pc]
            return pltpu.make_async_remote_copy(
                src_ref=slot, dst_ref=slot, send_sem=send_sems.at[r - 1], recv_sem=recv_sems.at[r - 1],
                device_id=(px, py, pc), device_id_type=MESH)

        sends = [copy(r) for r in range(1, N_DEV)]
        for cp in sends:
            cp.start()
        buf[me] = v_ref[...]
        for r in range(1, N_DEV):
            arrival(r).wait_recv()
        for cp in sends:
            cp.wait_send()
        total = buf[0]
        for d in range(1, N_DEV):
            total = total + buf[d]
        o_ref[...] = total

    vmem = pl.BlockSpec(memory_space=pltpu.VMEM)
    return pl.pallas_call(
        body, name=name, out_shape=jax.ShapeDtypeStruct((1, n), F32), in_specs=[vmem], out_specs=vmem,
        scratch_shapes=[pltpu.VMEM((N_DEV, 1, n), F32), pltpu.SemaphoreType.DMA((7,)), pltpu.SemaphoreType.DMA((7,))],
    )(vec)


def _mm_core(pairs, dims, out_dtype, *, name, grid, out_shape, out_spec, acc_shape, alpha=1.0, res=None, comm=None):
    nk = grid[2]
    npairs = len(pairs)

    def body(*refs):
        ab = refs[:2 * npairs]
        rest = refs[2 * npairs:]
        res_ref = rest[0] if res is not None else None
        o_ref = rest[1] if res is not None else rest[0]
        acc_ref = rest[-1] if nk > 1 else None
        part = None
        for p in range(npairs):
            a_ref, b_ref = ab[2 * p], ab[2 * p + 1]
            shards = [(a_ref[s], b_ref[s]) for s in range(a_ref.shape[0])] if len(a_ref.shape) == 3 else [(a_ref[...], b_ref[...])]
            for av, bv in shards:
                d = _dot(av.astype(BF16), bv.astype(BF16), dims)
                part = d if part is None else part + d

        def finish(total):
            val = total * alpha if alpha != 1.0 else total
            if res_ref is not None:
                val = res_ref[...] + val
            o_ref[...] = val.astype(out_dtype)

        if nk == 1:
            finish(part)
        else:
            kk = pl.program_id(2)

            @pl.when(kk == 0)
            def _():
                acc_ref[...] = part

            @pl.when(kk > 0)
            def _():
                acc_ref[...] += part

            @pl.when(kk == nk - 1)
            def _():
                finish(acc_ref[...])

    operands, in_specs = [], []
    for (a, a_spec), (b, b_spec) in pairs:
        operands += [a, b]
        in_specs += [a_spec, b_spec]
    if res is not None:
        operands.append(res[0])
        in_specs.append(res[1])
    out, got = _call(body, name=name, grid=grid, in_specs=in_specs, out_specs=[out_spec],
                     out_shape=[jax.ShapeDtypeStruct(out_shape, out_dtype)], operands=operands,
                     scratch_shapes=[pltpu.VMEM(acc_shape, F32)] if nk > 1 else [], comm=comm)
    return out[0] if comm is None else (out[0], got)


def _mm(pairs, mode, out_dtype, *, name, alpha=1.0, res=None):
    a0, b0 = pairs[0]
    if mode == "nn":
        (m, k), n = a0.shape, b0.shape[1]
    elif mode == "nt":
        (m, k), n = a0.shape, b0.shape[0]
    else:
        (k, m), n = a0.shape, b0.shape[1]
    dims = {"nn": NN_DIMS, "nt": NT_DIMS, "tn": TN_DIMS}[mode]
    tm = _tile(m, 1056, LANES if mode == "tn" else 16)
    tn = _tile(n, 1024, LANES)
    tk = _tile(k, 2048 if mode != "tn" else 2112, LANES if mode != "tn" else 16)
    a_spec = pl.BlockSpec((tk, tm), lambda i, j, kk: (kk, i)) if mode == "tn" else pl.BlockSpec((tm, tk), lambda i, j, kk: (i, kk))
    b_spec = pl.BlockSpec((tn, tk), lambda i, j, kk: (j, kk)) if mode == "nt" else pl.BlockSpec((tk, tn), lambda i, j, kk: (kk, j))
    o_spec = pl.BlockSpec((tm, tn), lambda i, j, kk: (i, j))
    return _mm_core([((a, a_spec), (b, b_spec)) for a, b in pairs], dims, out_dtype, name=name,
                    grid=(m // tm, n // tn, k // tk), out_shape=(m, n), out_spec=o_spec, acc_shape=(tm, tn),
                    alpha=alpha, res=None if res is None else (res, o_spec))


def _rms_fwd(h, gain, name):
    m, d = h.shape
    tm = _tile(m, 528, 16)

    def body(h_ref, g_ref, o_ref):
        hv = h_ref[...]
        r = lax.rsqrt(jnp.mean(hv * hv, axis=-1, keepdims=True) + EPS)
        o_ref[...] = (hv * r * g_ref[...]).astype(BF16)

    row = pl.BlockSpec((tm, d), lambda i: (i, 0))
    out, _ = _call(body, name=name, grid=(m // tm,), in_specs=[row, pl.BlockSpec((1, d), lambda i: (0, 0))],
                   out_specs=[row], out_shape=[jax.ShapeDtypeStruct((m, d), BF16)], operands=[h, gain])
    return out[0]


def _rms_bwd(dxn, h, gain, dres, name):
    m, d = h.shape
    tm = _tile(m, 264, 16)

    def body(dxn_ref, h_ref, g_ref, dres_ref, dh_ref, dhb_ref, dg_ref):
        hv = h_ref[...]
        r = lax.rsqrt(jnp.mean(hv * hv, axis=-1, keepdims=True) + EPS)
        xhat = hv * r
        dxn_v = dxn_ref[...]
        t = dxn_v * g_ref[...]
        dh = dres_ref[...] + r * (t - xhat * jnp.mean(t * xhat, axis=-1, keepdims=True))
        dh_ref[...] = dh
        dhb_ref[...] = dh.astype(BF16)
        part = jnp.sum(dxn_v * xhat, axis=0, keepdims=True)

        @pl.when(pl.program_id(0) == 0)
        def _():
            dg_ref[...] = part

        @pl.when(pl.program_id(0) > 0)
        def _():
            dg_ref[...] += part

    row = pl.BlockSpec((tm, d), lambda i: (i, 0))
    vec = pl.BlockSpec((1, d), lambda i: (0, 0))
    out, _ = _call(body, name=name, grid=(m // tm,), in_specs=[row, row, vec, row], out_specs=[row, row, vec],
                   out_shape=[jax.ShapeDtypeStruct((m, d), F32), jax.ShapeDtypeStruct((m, d), BF16),
                              jax.ShapeDtypeStruct((1, d), F32)],
                   operands=[dxn, h, gain, dres])
    return out


def _loss_head(h, gain, target, x_off, name):
    m, d = h.shape
    tm = ROW_BLOCK
    first = x_off // tm

    def body(h_ref, g_ref, t_ref, dh_ref, dhb_ref, dg_ref, loss_ref):
        i = pl.program_id(0)

        @pl.when(i == 0)
        def _():
            dg_ref[...] = jnp.zeros_like(dg_ref)
            loss_ref[...] = jnp.zeros_like(loss_ref)

        @pl.when(i < first)
        def _():
            dh_ref[...] = jnp.zeros_like(dh_ref)
            dhb_ref[...] = jnp.zeros_like(dhb_ref)

        @pl.when(i >= first)
        def _():
            hv = h_ref[...]
            g = g_ref[...]
            r = lax.rsqrt(jnp.mean(hv * hv, axis=-1, keepdims=True) + EPS)
            xhat = hv * r
            err = xhat * g - t_ref[...]
            loss_ref[...] += 0.5 * jnp.sum(jnp.mean(err * err, axis=-1, keepdims=True))
            dy = err * (1.0 / d)
            t = dy * g
            dh = r * (t - xhat * jnp.mean(t * xhat, axis=-1, keepdims=True))
            dh_ref[...] = dh
            dhb_ref[...] = dh.astype(BF16)
            dg_ref[...] += jnp.sum(dy * xhat, axis=0, keepdims=True)

    row = pl.BlockSpec((tm, d), lambda i: (i, 0))
    vec = pl.BlockSpec((1, d), lambda i: (0, 0))
    out, _ = _call(body, name=name, grid=(m // tm,),
                   in_specs=[row, vec, pl.BlockSpec((tm, d), lambda i: (jnp.maximum(i - first, 0), 0))],
                   out_specs=[row, row, vec, pl.BlockSpec((8, LANES), lambda i: (0, 0))],
                   out_shape=[jax.ShapeDtypeStruct((m, d), F32), jax.ShapeDtypeStruct((m, d), BF16),
                              jax.ShapeDtypeStruct((1, d), F32), jax.ShapeDtypeStruct((8, LANES), F32)],
                   operands=[h, gain, target])
    return out


def _sigmoid(z):
    return 1.0 / (1.0 + jnp.exp(-z))


def _ffn_up(xn, wg, wu, name, comm=None):
    m, d = xn.shape
    nsh, c, _ = wg.shape
    tm = _tile(m, 1056, 16)

    def body(x_ref, wg_ref, wu_ref, g_ref, u_ref, a_ref):
        xv = x_ref[...]
        g = _dot(xv, wg_ref[...], NT_DIMS)
        u = _dot(xv, wu_ref[...], NT_DIMS)
        g_ref[...] = g.astype(BF16)
        u_ref[...] = u.astype(BF16)
        a_ref[...] = (g * _sigmoid(g) * u).astype(BF16)

    out = pl.BlockSpec((None, tm, c), lambda i, j: (j, i, 0))
    w = pl.BlockSpec((None, c, d), lambda i, j: (j, 0, 0))
    return _call(body, name=name, grid=(m // tm, nsh), in_specs=[pl.BlockSpec((tm, d), lambda i, j: (i, 0)), w, w],
                 out_specs=[out, out, out], out_shape=[jax.ShapeDtypeStruct((nsh, m, c), BF16)] * 3,
                 operands=[xn, wg, wu], comm=comm)


def _ffn_contract(pairs, out_dtype, name, *, group, alpha=1.0, res=None, comm=None):
    nsh, m, c = pairs[0][0].shape
    d = pairs[0][1].shape[2]
    tm, tn = _tile(m, 1056, 16), _tile(d, 512, LANES)
    a_spec = pl.BlockSpec((group, tm, c), lambda i, j, kk: (kk, i, 0))
    b_spec = pl.BlockSpec((group, c, tn), lambda i, j, kk: (kk, 0, j))
    o_spec = pl.BlockSpec((tm, tn), lambda i, j, kk: (i, j))
    return _mm_core([((a, a_spec), (b, b_spec)) for a, b in pairs], NN_DIMS, out_dtype, name=name,
                    grid=(m // tm, d // tn, nsh // group), out_shape=(m, d), out_spec=o_spec, acc_shape=(tm, tn),
                    alpha=alpha, res=None if res is None else (res, o_spec), comm=comm)


def _ffn_bwd_act(dh, wd, g, u, name, comm=None):
    m, d = dh.shape
    nsh, c, _ = wd.shape
    tm = _tile(m, 1056, 16)

    def body(dh_ref, wd_ref, g_ref, u_ref, dg_ref, du_ref):
        dact = 0.5 * _dot(dh_ref[...], wd_ref[...], NT_DIMS)
        gv = g_ref[...].astype(F32)
        uv = u_ref[...].astype(F32)
        sig = _sigmoid(gv)
        du_ref[...] = (dact * gv * sig).astype(BF16)
        dg_ref[...] = (dact * uv * sig * (1.0 + gv * (1.0 - sig))).astype(BF16)

    blk = pl.BlockSpec((None, tm, c), lambda i, j: (j, i, 0))
    return _call(body, name=name, grid=(m // tm, nsh),
                 in_specs=[pl.BlockSpec((tm, d), lambda i, j: (i, 0)), pl.BlockSpec((None, c, d), lambda i, j: (j, 0, 0)), blk, blk],
                 out_specs=[blk, blk], out_shape=[jax.ShapeDtypeStruct((nsh, m, c), BF16)] * 2, operands=[dh, wd, g, u],
                 comm=comm)


def _ffn_dw(z, x, name, alpha=1.0, comm=None):
    nsh, m, c = z.shape
    d = x.shape[1]
    tn, tk = _tile(d, 1024, LANES), _tile(m, 2112, 16)
    return _mm_core([((z, pl.BlockSpec((None, tk, c), lambda i, j, kk: (i, kk, 0))),
                      (x, pl.BlockSpec((tk, tn), lambda i, j, kk: (kk, j))))],
                    TN_DIMS, BF16, name=name, grid=(nsh, d // tn, m // tk), out_shape=(nsh, c, d),
                    out_spec=pl.BlockSpec((None, c, tn), lambda i, j, kk: (i, 0, j)), acc_shape=(c, tn), alpha=alpha,
                    comm=comm)


def _dot3(tri, x):
    h1 = x.astype(BF16)
    r1 = x - h1.astype(F32)
    h2 = r1.astype(BF16)
    h3 = (r1 - h2.astype(F32)).astype(BF16)
    return _dot(tri, h1) + _dot(tri, h2) + _dot(tri, h3)


def _log_sigmoid(z):
    return jnp.minimum(z, 0.0) - jnp.log(1.0 + jnp.exp(-jnp.abs(z)))


def _triangle(t, cmp):
    return cmp(lax.broadcasted_iota(jnp.int32, (t, t), 0), lax.broadcasted_iota(jnp.int32, (t, t), 1)).astype(BF16)


def _forget_cumsum(fl, bias, n_heads, pad, name):
    m = fl.shape[0]
    nb = m // ROW_BLOCK

    def body(fl_ref, b_ref, c_ref):
        tri = _triangle(ROW_BLOCK, lambda r, c: r >= c)
        lane_ok = lax.broadcasted_iota(jnp.int32, (ROW_BLOCK, LANES), 1) < n_heads
        rows = lax.broadcasted_iota(jnp.int32, (ROW_BLOCK, LANES), 0)

        def step(b, carry):
            off = pl.multiple_of(b * ROW_BLOCK, ROW_BLOCK)
            lf = _log_sigmoid(fl_ref[pl.ds(off, ROW_BLOCK), :] + b_ref[...])
            lf = jnp.where(lane_ok & (rows + off >= pad), lf, 0.0)
            cs = _dot3(tri, lf) + carry
            c_ref[pl.ds(off, ROW_BLOCK), :] = cs
            return cs[ROW_BLOCK - 1:ROW_BLOCK, :]

        lax.fori_loop(0, nb, step, jnp.zeros((1, LANES), F32))

    vmem = pl.BlockSpec(memory_space=pltpu.VMEM)
    return pl.pallas_call(
        body, name=name, out_shape=jax.ShapeDtypeStruct((m, LANES), F32), in_specs=[vmem, vmem], out_specs=vmem,
        compiler_params=pltpu.CompilerParams(vmem_limit_bytes=VMEM_LIMIT_BYTES),
    )(fl, bias)


def _forget_cumsum_bwd(dc_a, dc_b, fl, bias, n_heads, pad, name):
    m = fl.shape[0]
    nb = m // ROW_BLOCK

    def body(da_ref, db_ref, fl_ref, b_ref, dfl_ref, dbias_ref):
        tri = _triangle(ROW_BLOCK, lambda r, c: r <= c)
        lane_ok = lax.broadcasted_iota(jnp.int32, (ROW_BLOCK, LANES), 1) < n_heads
        rows = lax.broadcasted_iota(jnp.int32, (ROW_BLOCK, LANES), 0)

        def step(bb, carry):
            tail, dbias = carry
            off = pl.multiple_of((nb - 1 - bb) * ROW_BLOCK, ROW_BLOCK)
            dc = da_ref[pl.ds(off, ROW_BLOCK), :] + db_ref[pl.ds(off, ROW_BLOCK), :]
            dlf = _dot3(tri, dc) + tail
            z = fl_ref[pl.ds(off, ROW_BLOCK), :] + b_ref[...]
            dfl = jnp.where(lane_ok & (rows + off >= pad), dlf * _sigmoid(-z), 0.0)
            dfl_ref[pl.ds(off, ROW_BLOCK), :] = dfl
            return dlf[0:1, :], dbias + jnp.sum(dfl, axis=0, keepdims=True)

        zero = jnp.zeros((1, LANES), F32)
        _, dbias = lax.fori_loop(0, nb, step, (zero, zero))
        dbias_ref[...] = dbias

    vmem = pl.BlockSpec(memory_space=pltpu.VMEM)
    return pl.pallas_call(
        body, name=name,
        out_shape=[jax.ShapeDtypeStruct((m, LANES), F32), jax.ShapeDtypeStruct((1, LANES), F32)],
        in_specs=[vmem] * 4, out_specs=[vmem, vmem],
        compiler_params=pltpu.CompilerParams(vmem_limit_bytes=VMEM_LIMIT_BYTES),
    )(dc_a, dc_b, fl, bias)


def _attn_block(m):
    return 3 * ROW_BLOCK if m % (3 * ROW_BLOCK) == 0 else ROW_BLOCK


def _head_norm(o, gain):
    r = lax.rsqrt(jnp.mean(o * o, axis=-1, keepdims=True) + EPS)
    return o * r * gain


def _head_norm_bwd(o, d_on, gain):
    r = lax.rsqrt(jnp.mean(o * o, axis=-1, keepdims=True) + EPS)
    ohat = o * r
    t = d_on * gain
    d_o = r * (t - ohat * jnp.mean(t * ohat, axis=-1, keepdims=True))
    return d_o, jnp.sum(d_on * ohat, axis=0, keepdims=True)


def _qkv_specs(t, m, h, first_col_block):
    q = pl.BlockSpec((t, HEAD_DIM), lambda hd, i: (i, first_col_block + hd))
    k = pl.BlockSpec((m, HEAD_DIM), lambda hd, i: (0, first_col_block + h + hd))
    v = pl.BlockSpec((m, HEAD_DIM), lambda hd, i: (0, first_col_block + 2 * h + hd))
    return q, k, v


def _fox_fwd(qkv, ccol, crow, gain, n_heads, pad, name, comm=None):
    m = qkv.shape[0]
    t = _attn_block(m)
    nq = m // t
    scale = HEAD_DIM ** -0.5
    hw = n_heads * HEAD_DIM

    def body(q_ref, k_ref, v_ref, ccol_ref, crow_ref, g_ref, o_ref, on_ref, lse_ref):
        i = pl.program_id(1)
        q = q_ref[...]
        ci = ccol_ref[...]
        qpos = i * t + lax.broadcasted_iota(jnp.int32, (t, 1), 0)

        def step(j, carry, masked):
            mx, l, acc = carry
            off = pl.multiple_of(j * t, t)
            k = k_ref[pl.ds(off, t), :]
            v = v_ref[pl.ds(off, t), :]
            s = _dot(q, k, NT_DIMS) * scale + ci - crow_ref[j]
            if masked:
                kpos = off + lax.broadcasted_iota(jnp.int32, (1, t), 1)
                s = jnp.where((kpos <= qpos) & (kpos >= pad), s, NEG)
            mx_new = jnp.maximum(mx, jnp.max(s, axis=-1, keepdims=True))
            p = jnp.exp(s - mx_new)
            a = jnp.exp(mx - mx_new)
            return mx_new, a * l + jnp.sum(p, axis=-1, keepdims=True), a * acc + _dot(p.astype(BF16), v)

        carry = step(0, (jnp.full((t, 1), NEG, F32), jnp.zeros((t, 1), F32), jnp.zeros((t, HEAD_DIM), F32)), True)
        carry = lax.fori_loop(1, i, lambda j, c: step(j, c, False), carry)
        mx, l, acc = lax.fori_loop(0, jnp.minimum(i, 1), lambda _, c: step(i, c, True), carry)
        valid = qpos >= pad
        o = jnp.where(valid, acc / l, 0.0)
        o_ref[...] = o
        on_ref[...] = _head_norm(o, g_ref[...]).astype(BF16)
        lse_ref[...] = jnp.where(valid, mx + jnp.log(l), 0.0)

    q_spec, k_spec, v_spec = _qkv_specs(t, m, n_heads, 0)
    col = pl.BlockSpec((None, t, 1), lambda hd, i: (hd, i, 0))
    head = pl.BlockSpec((t, HEAD_DIM), lambda hd, i: (i, hd))
    return _call(body, name=name, grid=(n_heads, nq),
                 in_specs=[q_spec, k_spec, v_spec, col, pl.BlockSpec((None, nq, 1, t), lambda hd, i: (hd, 0, 0, 0)),
                           pl.BlockSpec((1, HEAD_DIM), lambda hd, i: (0, hd))],
                 out_specs=[head, head, col],
                 out_shape=[jax.ShapeDtypeStruct((m, hw), F32), jax.ShapeDtypeStruct((m, hw), BF16),
                            jax.ShapeDtypeStruct((n_heads, m, 1), F32)],
                 operands=[qkv, qkv, qkv, ccol, crow, gain], comm=comm)


def _fox_bwd(qkv, o, d_on, gain, lse, ccol, crow, n_heads, pad, name, comm=None):
    m = qkv.shape[0]
    t = _attn_block(m)
    nq = m // t
    scale = HEAD_DIM ** -0.5
    hw = n_heads * HEAD_DIM

    def body(q_ref, k_ref, v_ref, o_ref, don_ref, g_ref, lse_ref, ccol_ref, crow_ref,
             dq_ref, dk_ref, dv_ref, dg_ref, dccol_ref, dcrow_ref):
        i = pl.program_id(1)

        @pl.when(i == 0)
        def _():
            dk_ref[...] = jnp.zeros_like(dk_ref)
            dv_ref[...] = jnp.zeros_like(dv_ref)
            dg_ref[...] = jnp.zeros_like(dg_ref)
            dcrow_ref[...] = jnp.zeros_like(dcrow_ref)

        q = q_ref[...]
        o = o_ref[...]
        d_o, dgain = _head_norm_bwd(o, don_ref[...], g_ref[...])
        dg_ref[...] += dgain
        delta = jnp.sum(d_o * o, axis=-1, keepdims=True)
        d_ob = d_o.astype(BF16)
        ci = ccol_ref[...]
        lse_i = lse_ref[...]
        qpos = i * t + lax.broadcasted_iota(jnp.int32, (t, 1), 0)

        def step(j, carry, masked):
            dq, dci = carry
            off = pl.multiple_of(j * t, t)
            k = k_ref[pl.ds(off, t), :]
            v = v_ref[pl.ds(off, t), :]
            s = _dot(q, k, NT_DIMS) * scale + ci - crow_ref[j]
            if masked:
                kpos = off + lax.broadcasted_iota(jnp.int32, (1, t), 1)
                ok = (kpos <= qpos) & (kpos >= pad)
                p = jnp.where(ok, jnp.exp(jnp.where(ok, s - lse_i, 0.0)), 0.0)
            else:
                p = jnp.exp(s - lse_i)
            ds = p * (_dot(d_ob, v, NT_DIMS) - delta)
            dsb = ds.astype(BF16)
            dk_ref[pl.ds(off, t), :] += _dot(dsb, q, TN_DIMS) * scale
            dv_ref[pl.ds(off, t), :] += _dot(p.astype(BF16), d_ob, TN_DIMS)
            dcrow_ref[j] -= jnp.sum(ds, axis=0, keepdims=True)
            return dq + _dot(dsb, k), dci + jnp.sum(ds, axis=-1, keepdims=True)

        carry = step(0, (jnp.zeros((t, HEAD_DIM), F32), jnp.zeros((t, 1), F32)), True)
        carry = lax.fori_loop(1, i, lambda j, c: step(j, c, False), carry)
        dq, dci = lax.fori_loop(0, jnp.minimum(i, 1), lambda _, c: step(i, c, True), carry)
        dq_ref[...] = (dq * scale).astype(BF16)
        dccol_ref[...] = dci

    q_spec, k_spec, v_spec = _qkv_specs(t, m, n_heads, 0)
    col = pl.BlockSpec((None, t, 1), lambda hd, i: (hd, i, 0))
    rowc = pl.BlockSpec((None, nq, 1, t), lambda hd, i: (hd, 0, 0, 0))
    head = pl.BlockSpec((t, HEAD_DIM), lambda hd, i: (i, hd))
    whole = pl.BlockSpec((m, HEAD_DIM), lambda hd, i: (0, hd))
    gvec = pl.BlockSpec((1, HEAD_DIM), lambda hd, i: (0, hd))
    return _call(body, name=name, grid=(n_heads, nq),
                 in_specs=[q_spec, k_spec, v_spec, head, head, gvec, col, col, rowc],
                 out_specs=[head, whole, whole, gvec, col, rowc],
                 out_shape=[jax.ShapeDtypeStruct((m, hw), BF16), jax.ShapeDtypeStruct((m, hw), F32),
                            jax.ShapeDtypeStruct((m, hw), F32), jax.ShapeDtypeStruct((1, hw), F32),
                            jax.ShapeDtypeStruct((n_heads, m, 1), F32), jax.ShapeDtypeStruct((n_heads, nq, 1, t), F32)],
                 operands=[qkv, qkv, qkv, o, d_on, gain, lse, ccol, crow], comm=comm)


def _sb_scores(q, k, scale):
    z = _dot(q, k, NT_DIMS) * scale
    lp = jnp.log(1.0 + jnp.exp(-jnp.abs(z)))
    return jnp.minimum(z, 0.0) - lp, jnp.minimum(-z, 0.0) - lp


def _sb_fwd(qkv, gain, n_heads, pad, name, comm=None):
    m = qkv.shape[0]
    t = _attn_block(m)
    nq = m // t
    assert nq <= LANES
    scale = HEAD_DIM ** -0.5
    hw = n_heads * HEAD_DIM

    def body(q_ref, k_ref, v_ref, g_ref, after_ref, o_ref, on_ref, run_ref):
        i = pl.program_id(1)
        q = q_ref[...]
        qpos = i * t + lax.broadcasted_iota(jnp.int32, (t, 1), 0)
        after = after_ref[...]
        lane = lax.broadcasted_iota(jnp.int32, (t, LANES), 1)

        def step(j, carry, masked):
            run, acc = carry
            off = pl.multiple_of(j * t, t)
            k = k_ref[pl.ds(off, t), :]
            v = v_ref[pl.ds(off, t), :]
            ls_pos, log_1m = _sb_scores(q, k, scale)
            if masked:
                kpos = off + lax.broadcasted_iota(jnp.int32, (1, t), 1)
                ok = (kpos < qpos) & (kpos >= pad)
                log_1m = jnp.where(ok, log_1m, 0.0)
            a = jnp.exp(ls_pos + _dot_split(log_1m, after) + run)
            if masked:
                a = jnp.where(ok, a, 0.0)
            run_ref[...] = jnp.where(lane == j, run, run_ref[...])
            return run + jnp.sum(log_1m, axis=-1, keepdims=True), acc + _dot(a.astype(BF16), v)

        run_ref[...] = jnp.zeros_like(run_ref)
        carry = step(i, (jnp.zeros((t, 1), F32), jnp.zeros((t, HEAD_DIM), F32)), True)
        carry = lax.fori_loop(1, i, lambda jj, c: step(i - jj, c, False), carry)
        _, o = lax.fori_loop(0, jnp.minimum(i, 1), lambda _, c: step(0, c, True), carry)
        o_ref[...] = o
        on_ref[...] = _head_norm(o, g_ref[...]).astype(BF16)

    q_spec, k_spec, v_spec = _qkv_specs(t, m, n_heads, 3 * n_heads)
    head = pl.BlockSpec((t, HEAD_DIM), lambda hd, i: (i, hd))
    return _call(body, name=name, grid=(n_heads, nq),
                 in_specs=[q_spec, k_spec, v_spec, pl.BlockSpec((1, HEAD_DIM), lambda hd, i: (0, hd)),
                           pl.BlockSpec((t, t), lambda hd, i: (0, 0))],
                 out_specs=[head, head, pl.BlockSpec((None, t, LANES), lambda hd, i: (hd, i, 0))],
                 out_shape=[jax.ShapeDtypeStruct((m, hw), F32), jax.ShapeDtypeStruct((m, hw), BF16),
                            jax.ShapeDtypeStruct((n_heads, m, LANES), F32)],
                 operands=[qkv, qkv, qkv, gain, _triangle(t, lambda r, c: r > c)], comm=comm)


def _sb_bwd(qkv, o, d_on, gain, runs, n_heads, pad, name, comm=None):
    m = qkv.shape[0]
    t = _attn_block(m)
    nq = m // t
    scale = HEAD_DIM ** -0.5
    hw = n_heads * HEAD_DIM

    def body(q_ref, k_ref, v_ref, o_ref, don_ref, g_ref, run_ref, after_ref, before_ref, dq_ref, dk_ref, dv_ref, dg_ref):
        i = pl.program_id(1)

        @pl.when(i == 0)
        def _():
            dk_ref[...] = jnp.zeros_like(dk_ref)
            dv_ref[...] = jnp.zeros_like(dv_ref)
            dg_ref[...] = jnp.zeros_like(dg_ref)

        q = q_ref[...]
        d_o, dgain = _head_norm_bwd(o_ref[...], don_ref[...], g_ref[...])
        dg_ref[...] += dgain
        d_ob = d_o.astype(BF16)
        runs_i = run_ref[...]
        qpos = i * t + lax.broadcasted_iota(jnp.int32, (t, 1), 0)
        after = after_ref[...]
        before = before_ref[...]
        lane = lax.broadcasted_iota(jnp.int32, (t, LANES), 1)

        def step(j, carry, masked):
            g_run, dq = carry
            off = pl.multiple_of(j * t, t)
            k = k_ref[pl.ds(off, t), :]
            v = v_ref[pl.ds(off, t), :]
            ls_pos, ls_neg = _sb_scores(q, k, scale)
            log_1m = ls_neg
            if masked:
                kpos = off + lax.broadcasted_iota(jnp.int32, (1, t), 1)
                ok = (kpos < qpos) & (kpos >= pad)
                log_1m = jnp.where(ok, ls_neg, 0.0)
            run = jnp.sum(jnp.where(lane == j, runs_i, 0.0), axis=-1, keepdims=True)
            a = jnp.exp(ls_pos + _dot_split(log_1m, after) + run)
            if masked:
                a = jnp.where(ok, a, 0.0)
            g = a * _dot(d_ob, v, NT_DIMS)
            prefix = _dot(g.astype(BF16), before) + g_run
            dz = g * jnp.exp(ls_neg) - jnp.exp(ls_pos) * prefix
            if masked:
                dz = jnp.where(ok, dz, 0.0)
            dzb = dz.astype(BF16)
            dk_ref[pl.ds(off, t), :] += _dot(dzb, q, TN_DIMS) * scale
            dv_ref[pl.ds(off, t), :] += _dot(a.astype(BF16), d_ob, TN_DIMS)
            return g_run + jnp.sum(g, axis=-1, keepdims=True), dq + _dot(dzb, k)

        carry = step(0, (jnp.zeros((t, 1), F32), jnp.zeros((t, HEAD_DIM), F32)), True)
        carry = lax.fori_loop(1, i, lambda j, c: step(j, c, False), carry)
        _, dq = lax.fori_loop(0, jnp.minimum(i, 1), lambda _, c: step(i, c, True), carry)
        dq_ref[...] = (dq * scale).astype(BF16)

    q_spec, k_spec, v_spec = _qkv_specs(t, m, n_heads, 3 * n_heads)
    head = pl.BlockSpec((t, HEAD_DIM), lambda hd, i: (i, hd))
    whole = pl.BlockSpec((m, HEAD_DIM), lambda hd, i: (0, hd))
    gvec = pl.BlockSpec((1, HEAD_DIM), lambda hd, i: (0, hd))
    tri = pl.BlockSpec((t, t), lambda hd, i: (0, 0))
    return _call(body, name=name, grid=(n_heads, nq),
                 in_specs=[q_spec, k_spec, v_spec, head, head, gvec, pl.BlockSpec((None, t, LANES), lambda hd, i: (hd, i, 0)),
                           tri, tri],
                 out_specs=[head, whole, whole, gvec],
                 out_shape=[jax.ShapeDtypeStruct((m, hw), BF16), jax.ShapeDtypeStruct((m, hw), F32),
                            jax.ShapeDtypeStruct((m, hw), F32), jax.ShapeDtypeStruct((1, hw), F32)],
                 operands=[qkv, qkv, qkv, o, d_on, gain, runs, _triangle(t, lambda r, c: r > c), _triangle(t, lambda r, c: r < c)],
                 comm=comm)


def _adamw(parts, w, m1, v2, name):
    nl, r, c = w.shape
    assert len(parts) == nl
    n_parts = parts[0].shape[0]
    block_elems = 128 * 1024
    if r % 8 == 0 or c % LANES != 0:
        tr, tc = _tile(r, max(8, block_elems // (-(-c // LANES) * LANES)), 8), c
    else:
        tr, tc = r, _tile(c, max(LANES, block_elems // r // LANES * LANES), LANES)
    nr, nc = r // tr, c // tc
    bias1 = 1.0 / (1.0 - ADAM_B1 ** ADAM_STEP)
    bias2 = 1.0 / (1.0 - ADAM_B2 ** ADAM_STEP)

    def body(*refs):
        p_refs = refs[:nl]
        w_ref, m_ref, v_ref, g_ref, d_ref, nm_ref, nv_ref = refs[nl:]

        def update(p_ref):
            g = p_ref[0].astype(F32)
            for s in range(1, n_parts):
                g = g + p_ref[s].astype(F32)
            m_new = ADAM_B1 * m_ref[...] + (1.0 - ADAM_B1) * g
            v_new = ADAM_B2 * v_ref[...] + (1.0 - ADAM_B2) * (g * g)
            g_ref[...] = g
            nm_ref[...] = m_new
            nv_ref[...] = v_new
            d_ref[...] = -ADAM_LR * ((m_new * bias1) / (jnp.sqrt(v_new * bias2) + ADAM_EPS) + ADAM_WD * w_ref[...])

        for ll in range(nl):
            @pl.when(pl.program_id(0) == ll)
            def _(ll=ll):
                update(p_refs[ll])

    def part_spec(ll):
        def index(l, i, j):
            pin = jnp.where(l < ll, 0, 1)
            return 0, jnp.where(l == ll, i, pin * (nr - 1)), jnp.where(l == ll, j, pin * (nc - 1))
        return pl.BlockSpec((n_parts, tr, tc), index)

    blk = pl.BlockSpec((None, tr, tc), lambda l, i, j: (l, i, j))
    out, _ = _call(body, name=name, grid=(nl, nr, nc), in_specs=[part_spec(ll) for ll in range(nl)] + [blk, blk, blk],
                   out_specs=[blk] * 4, out_shape=[jax.ShapeDtypeStruct((nl, r, c), F32)] * 4,
                   operands=list(parts) + [w, m1, v2])
    return out


SMALL_WEIGHTS = ("ffn1_norm", "mix_norm", "b_forget", "g_fox", "g_sb", "ffn2_norm", "final_norm")
WEIGHT_ORDER = ("meta_tokens", "ffn1_norm", "ffn1_w_gate", "ffn1_w_up", "ffn1_w_down", "mix_norm", "w_in", "b_forget",
                "g_fox", "g_sb", "w_out", "ffn2_norm", "ffn2_w_gate", "ffn2_w_up", "ffn2_w_down", "final_norm")
GROUPS = {"ffn1": ("ffn1_w_gate", "ffn1_w_up", "ffn1_w_down"), "mix": ("w_in", "w_out"),
          "ffn2": ("ffn2_w_gate", "ffn2_w_up", "ffn2_w_down")}
TRANSPOSED = ("ffn1_w_gate", "ffn1_w_up", "ffn2_w_gate", "ffn2_w_up")


def _pad_lanes(a):
    extra = (-a.shape[-1]) % LANES
    return a if extra == 0 else jnp.pad(a, [(0, 0)] * (a.ndim - 1) + [(0, extra)])


def _ffn_backward(dh_b, dh, saved, gain, wg, wu, wd, tag, carried=None):
    h, xn, g, u, act = saved
    (dg, du), _ = _ffn_bwd_act(dh_b, wd, g, u, f"{tag}_bwd_act")
    d_wd = _ffn_dw(act, dh_b, f"{tag}_dwd", alpha=0.5)
    d_wg = _ffn_dw(dg, xn, f"{tag}_dwg")
    d_wu = _ffn_dw(du, xn, f"{tag}_dwu")
    if carried:
        dxn, got = _ffn_contract([(dg, wg), (du, wu)], F32, f"{tag}_dxn", group=2, comm=_Exchange(carried))
    else:
        dxn, got = _ffn_contract([(dg, wg), (du, wu)], F32, f"{tag}_dxn", group=2), []
    dh_in, dh_in_b, d_gain = _rms_bwd(dxn, h, gain, dh, f"{tag}_norm_bwd")
    return dh_in, dh_in_b, d_gain, d_wg, d_wu, d_wd, got


def _ffn_backward_last(dh_b, dh, saved, gain, wg, wu, wd, tag, first, second):
    h, xn, g, u, act = saved
    (dg, du), got_first = _ffn_bwd_act(dh_b, wd, g, u, f"{tag}_bwd_act", _Exchange(first))
    d_wd, got_second = _ffn_dw(act, dh_b, f"{tag}_dwd", alpha=0.5, comm=_Exchange(second))
    d_wg, got_wd = _ffn_dw(dg, xn, f"{tag}_dwg", comm=_Exchange([d_wd]))
    d_wu, got_wg = _ffn_dw(du, xn, f"{tag}_dwu", comm=_Exchange([d_wg]))
    dxn, got_wu = _ffn_contract([(dg, wg), (du, wu)], F32, f"{tag}_dxn", group=2, comm=_Exchange([d_wu]))
    dh_in, dh_in_b, d_gain = _rms_bwd(dxn, h, gain, dh, f"{tag}_norm_bwd")
    return dh_in, dh_in_b, d_gain, got_first, got_second, got_wg[0], got_wu[0], got_wd[0]


def _mixer_forward(h, gain, w_in_t, bias, g_fox, g_sb, w_out, n_heads, pad, tag, comm_fox=None, comm_sb=None):
    m = h.shape[0]
    t = _attn_block(m)
    hw = n_heads * HEAD_DIM
    xn = _rms_fwd(h, gain, f"{tag}_norm")
    qkv = _mm([(xn, w_in_t[:6 * hw])], "nt", BF16, name=f"{tag}_qkv")
    fl = _mm([(xn, w_in_t[6 * hw:])], "nt", F32, name=f"{tag}_forget")
    c = _forget_cumsum(fl, bias, n_heads, pad, f"{tag}_cumsum")
    c_heads = c[:, :n_heads].T
    ccol = c_heads[:, :, None]
    crow = c_heads.reshape(n_heads, m // t, 1, t)
    (o_f, on_f, lse), got_fox = _fox_fwd(qkv, ccol, crow, g_fox, n_heads, pad, f"{tag}_fox", comm_fox)
    (o_s, on_s, runs), got_sb = _sb_fwd(qkv, g_sb, n_heads, pad, f"{tag}_sb", comm_sb)
    h_out = _mm([(on_f, w_out[:hw]), (on_s, w_out[hw:])], "nn", F32, name=f"{tag}_out", res=h)
    return h_out, (h, xn, qkv, fl, ccol, crow, o_f, on_f, lse, o_s, on_s, runs), got_fox, got_sb


def _mixer_backward(dh_b, dh_out, saved, gain, w_in_t, bias, g_fox, g_sb, w_out, n_heads, pad, tag, comm_fox=None, comm_sb=None):
    h, xn, qkv, fl, ccol, crow, o_f, on_f, lse, o_s, on_s, runs = saved
    m = h.shape[0]
    hw = n_heads * HEAD_DIM
    d_on_f = _mm([(dh_b, w_out[:hw])], "nt", F32, name=f"{tag}_don_f")
    d_on_s = _mm([(dh_b, w_out[hw:])], "nt", F32, name=f"{tag}_don_s")
    d_wout = jnp.concatenate([_mm([(on_f, dh_b)], "tn", BF16, name=f"{tag}_dwout_f"),
                              _mm([(on_s, dh_b)], "tn", BF16, name=f"{tag}_dwout_s")], axis=0)
    (dq_f, dk_f, dv_f, dg_fox, dccol, dcrow), got_fox = _fox_bwd(
        qkv, o_f, d_on_f, g_fox, lse, ccol, crow, n_heads, pad, f"{tag}_fox_bwd", comm_fox)
    (dq_s, dk_s, dv_s, dg_sb), got_sb = _sb_bwd(qkv, o_s, d_on_s, g_sb, runs, n_heads, pad, f"{tag}_sb_bwd", comm_sb)
    dc_a = _pad_lanes(dccol[:, :, 0].T)
    dc_b = _pad_lanes(dcrow.reshape(n_heads, m).T)
    dfl, dbias = _forget_cumsum_bwd(dc_a, dc_b, fl, bias, n_heads, pad, f"{tag}_cumsum_bwd")
    dproj = jnp.concatenate([dq_f, dk_f.astype(BF16), dv_f.astype(BF16), dq_s, dk_s.astype(BF16), dv_s.astype(BF16),
                             dfl.astype(BF16)], axis=1)
    d_win_t = _mm([(dproj, xn)], "tn", BF16, name=f"{tag}_dwin")
    dxn = _mm([(dproj, w_in_t)], "nn", F32, name=f"{tag}_dxn")
    dh, dh_in_b, d_gain = _rms_bwd(dxn, h, gain, dh_out, f"{tag}_norm_bwd")
    return dh, dh_in_b, d_gain, d_win_t, dbias, dg_fox, dg_sb, d_wout, got_fox, got_sb


def kernel(x, meta_tokens, ffn1_norm, ffn1_w_gate, ffn1_w_up, ffn1_w_down, mix_norm, w_in, b_forget, g_fox, g_sb, w_out, ffn2_norm, ffn2_w_gate, ffn2_w_up, ffn2_w_down, final_norm, loss_target, m_meta_tokens, m_ffn1_norm, m_ffn1_w_gate, m_ffn1_w_up, m_ffn1_w_down, m_mix_norm, m_w_in, m_b_forget, m_g_fox, m_g_sb, m_w_out, m_ffn2_norm, m_ffn2_w_gate, m_ffn2_w_up, m_ffn2_w_down, m_final_norm, v_meta_tokens, v_ffn1_norm, v_ffn1_w_gate, v_ffn1_w_up, v_ffn1_w_down, v_mix_norm, v_w_in, v_b_forget, v_g_fox, v_g_sb, v_w_out, v_ffn2_norm, v_ffn2_w_gate, v_ffn2_w_up, v_ffn2_w_down, v_final_norm):
    given = dict(locals())
    seq, d = x.shape[1], x.shape[2]
    depth = ffn1_norm.shape[0]
    d_in = N_DEV * w_in.shape[2]
    n_heads = g_fox.shape[1] // HEAD_DIM
    hw = n_heads * HEAD_DIM
    assert seq % ROW_BLOCK == 0 and d_in == 6 * hw + n_heads and n_heads <= LANES
    pad = (-(seq + N_META)) % ROW_BLOCK
    x_off = pad + N_META

    def view(n, a):
        if n in TRANSPOSED:
            return jnp.swapaxes(a, 1, 2)
        return a.transpose(2, 0, 1) if n == "w_in" else a

    def unview(n, a):
        if n in TRANSPOSED:
            return jnp.swapaxes(a, 1, 2)
        return a.transpose(1, 2, 0) if n == "w_in" else a

    def shard(n, l):
        v = view(n, given[n])
        return (v[:, l] if n == "w_in" else v[l]).astype(BF16)

    def shards(group, l):
        return [shard(n, l) for n in GROUPS[group]]

    sh = shards("ffn1", 0)
    first = _run_alone(_Gather(sh[:2] + [meta_tokens]), "gather_first")
    full = {}
    meta_full = first[2].transpose(1, 0, 2).reshape(N_META, d)
    h = jnp.concatenate([jnp.zeros((pad, d), F32), meta_full, x[0]], axis=0)
    weights, saved = [], []
    for l in range(depth):
        xn = _rms_fwd(h, ffn1_norm[l:l + 1], "ffn1_norm")
        if l == 0:
            (g, u, act), (wd1,) = _ffn_up(xn, first[0], first[1], "ffn1_up", _Gather(sh[2:]))
            full[("ffn1", 0)] = (first[0], first[1], wd1)
            h_out, full[("mix", 0)] = _ffn_contract([(act, wd1)], F32, "ffn1_down", group=4, alpha=0.5, res=h,
                                                    comm=_Gather(shards("mix", 0)))
        else:
            wg1, wu1, wd1 = full[("ffn1", l)]
            (g, u, act), full[("mix", l)] = _ffn_up(xn, wg1, wu1, "ffn1_up", _Gather(shards("mix", l)))
            h_out = _ffn_contract([(act, wd1)], F32, "ffn1_down", group=4, alpha=0.5, res=h)
        s1 = (h, xn, g, u, act)
        h = h_out
        win3, wout3 = full[("mix", l)]
        w_in_t = jnp.pad(win3.reshape(d_in, d), ((0, 6 * hw + LANES - d_in), (0, 0)))
        w_out_full = wout3.reshape(N_DEV * wout3.shape[1], d)
        bias = _pad_lanes(b_forget[l:l + 1])
        nxt = _Gather(shards("ffn1", l + 1)) if l + 1 < depth else None
        h, sm, full[("ffn2", l)], got = _mixer_forward(
            h, mix_norm[l:l + 1], w_in_t, bias, g_fox[l:l + 1], g_sb[l:l + 1], w_out_full, n_heads, pad, "mix",
            _Gather(shards("ffn2", l)), nxt)
        if nxt is not None:
            full[("ffn1", l + 1)] = got
        wg2, wu2, wd2 = full[("ffn2", l)]
        xn = _rms_fwd(h, ffn2_norm[l:l + 1], "ffn2_norm")
        (g, u, act), _ = _ffn_up(xn, wg2, wu2, "ffn2_up")
        s2 = (h, xn, g, u, act)
        h = _ffn_contract([(act, wd2)], F32, "ffn2_down", group=4, alpha=0.5, res=h)
        weights.append((w_in_t, w_out_full, bias))
        saved.append((s1, sm, s2))

    dh, dh_b, d_final, loss_arr = _loss_head(h, final_norm[None, :], loss_target[0], x_off, "loss_head")
    small = {n: [None] * depth for n in SMALL_WEIGHTS[:-1]}
    partial, received = {}, {}

    def names(group, l):
        return [(n, l) for n in GROUPS[group]]

    def send(keys):
        return [partial[k] for k in keys]

    for l in reversed(range(depth)):
        w_in_t, w_out_full, bias = weights[l]
        s1, sm, s2 = saved[l]
        wg2, wu2, wd2 = full[("ffn2", l)]
        up = l + 1 < depth
        in_dxn = [("ffn1_w_gate", l + 1)] if up else []
        in_fox = names("mix", l + 1) + [("ffn1_w_up", l + 1)] if up else []
        in_sb = names("ffn2", l) + ([("ffn1_w_down", l + 1)] if up else [])
        dh, dh_b, small["ffn2_norm"][l], partial[("ffn2_w_gate", l)], partial[("ffn2_w_up", l)], partial[("ffn2_w_down", l)], got = (
            _ffn_backward(dh_b, dh, s2, ffn2_norm[l:l + 1], wg2, wu2, wd2, "ffn2", send(in_dxn)))
        received.update(zip(in_dxn, got))
        dh, dh_b, small["mix_norm"][l], d_win_t, dbias, small["g_fox"][l], small["g_sb"][l], d_wout, got_fox, got_sb = _mixer_backward(
            dh_b, dh, sm, mix_norm[l:l + 1], w_in_t, bias, g_fox[l:l + 1], g_sb[l:l + 1], w_out_full, n_heads, pad, "mix",
            _Exchange(send(in_fox)) if in_fox else None, _Exchange(send(in_sb)))
        received.update(zip(in_fox, got_fox or []))
        received.update(zip(in_sb, got_sb))
        partial[("w_in", l)] = d_win_t[:d_in].reshape(N_DEV, -1, d)
        partial[("w_out", l)] = d_wout.reshape(N_DEV, -1, d)
        small["b_forget"][l] = dbias[:, :n_heads]
        wg1, wu1, wd1 = full[("ffn1", l)]
        if l > 0:
            dh, dh_b, small["ffn1_norm"][l], partial[("ffn1_w_gate", l)], partial[("ffn1_w_up", l)], partial[("ffn1_w_down", l)], _ = (
                _ffn_backward(dh_b, dh, s1, ffn1_norm[l:l + 1], wg1, wu1, wd1, "ffn1"))
        else:
            (dh, dh_b, small["ffn1_norm"][0], got_a, got_b, received[("ffn1_w_gate", 0)], received[("ffn1_w_up", 0)],
             received[("ffn1_w_down", 0)]) = _ffn_backward_last(
                dh_b, dh, s1, ffn1_norm[0:1], wg1, wu1, wd1, "ffn1", [partial[("w_out", 0)]], [partial[("w_in", 0)]])
            received[("w_out", 0)], received[("w_in", 0)] = got_a[0], got_b[0]
    grad_x = dh[x_off:][None]
    d_meta = dh[pad:x_off].reshape(N_META, N_DEV, -1).transpose(1, 0, 2)
    received[("meta_tokens", 0)] = _run_alone(_Exchange([d_meta]), "exchange_meta")[0]

    vec = [loss_arr[0:1, :]] + [_pad_lanes(jnp.concatenate(small[n], axis=0).reshape(1, -1)) for n in SMALL_WEIGHTS[:-1]]
    vec.append(d_final)
    sizes = [a.shape[1] for a in vec]
    summed = _all_reduce_small(jnp.concatenate(vec, axis=1), "reduce_small")
    loss = summed[0, 0]

    def packed(prefix):
        cols = [jnp.zeros((1, LANES), F32)]
        cols += [_pad_lanes(given[prefix + n].reshape(1, -1)) for n in SMALL_WEIGHTS]
        return jnp.concatenate(cols, axis=1)[None]

    small_out = _adamw([summed[None]], packed(""), packed("m_"), packed("v_"), "adamw_small")

    out = {}
    for n in [w for g in GROUPS.values() for w in g]:
        wv, mv, vv = (view(n, given[p + n]) for p in ("", "m_", "v_"))
        if n == "w_in":
            per_layer = [_adamw([received[(n, l)]], wv[:, l][None], mv[:, l][None], vv[:, l][None], "adamw_" + n)
                         for l in range(depth)]
            res = [jnp.stack([per_layer[l][k][0] for l in range(depth)], axis=1) for k in range(4)]
        else:
            res = _adamw([received[(n, l)] for l in range(depth)], wv, mv, vv, "adamw_" + n)
        out[n] = [unview(n, r) for r in res]
    out["meta_tokens"] = [r[0] for r in _adamw([received[("meta_tokens", 0)]], meta_tokens[None], m_meta_tokens[None],
                                               v_meta_tokens[None], "adamw_meta_tokens")]
    offset = sizes[0]
    for n, size in zip(SMALL_WEIGHTS, sizes[1:]):
        shape, count = given[n].shape, given[n].size
        out[n] = [r[0, 0, offset:offset + count].reshape(shape) for r in small_out]
        offset += size

    result = [loss, grad_x]
    for k in range(4):
        result += [out[n][k] for n in WEIGHT_ORDER]
    return tuple(result)
```

```python
import math

import jax
import jax.numpy as jnp
from jax import lax
from jax.experimental import pallas as pl
from jax.experimental.pallas import tpu as pltpu

F32 = jnp.float32
BF16 = jnp.bfloat16

N_DEV = 8
N_META = 16
HEAD_DIM = 128
ROW_BLOCK = 128
LANES = 128
EPS = 1e-6
NEG = -1e30
ADAM_LR = 0.001
ADAM_B1 = 0.9
ADAM_B2 = 0.999
ADAM_EPS = 1e-08
ADAM_WD = 0.01
ADAM_STEP = 10
VMEM_LIMIT_BYTES = 56 * 1024 * 1024
MESH = pl.DeviceIdType.MESH

NT_DIMS = (((1,), (1,)), ((), ()))
TN_DIMS = (((0,), (0,)), ((), ()))
NN_DIMS = (((1,), (0,)), ((), ()))
ANY = pl.BlockSpec(memory_space=pl.ANY)


def _tile(n, cap, align):
    best = None
    for d in range(align, min(n, cap) + 1, align):
        if n % d == 0:
            best = d
    return best if best is not None else n


def _dot(a, b, dims=NN_DIMS):
    return lax.dot_general(a, b, dims, preferred_element_type=F32)


def _dot_split(x, u):
    hi = x.astype(BF16)
    lo = (x - hi.astype(F32)).astype(BF16)
    return _dot(hi, u) + _dot(lo, u)


def _my_position():
    return lax.axis_index("x"), lax.axis_index("y"), lax.axis_index("c")


class _Gather:
    n_phases = 3

    def __init__(self, arrs):
        self.arrs = list(arrs)
        n = len(self.arrs)
        self.out_shapes = [jax.ShapeDtypeStruct((N_DEV,) + a.shape, a.dtype) for a in self.arrs]
        self.scratch = [pltpu.SemaphoreType.DMA((n, 7)), pltpu.SemaphoreType.DMA((n, 7)), pltpu.SemaphoreType.DMA((n,))]

    def phase(self, p, ins, outs, sems):
        send_sems, recv_sems, local_sems = sems
        n = len(self.arrs)
        x, y, c = _my_position()
        me, sibling = (x, y, c), (x, y, 1 - c)
        chips = [(1 - x, y), (x, 1 - y), (1 - x, 1 - y)]

        def copy(a, k, block, to, src=None):
            slot = outs[a].at[4 * block[0] + 2 * block[1] + block[2]]
            return pltpu.make_async_remote_copy(
                src_ref=slot if src is None else src, dst_ref=slot,
                send_sem=send_sems.at[a, k], recv_sem=recv_sems.at[a, k], device_id=to, device_id_type=MESH)

        def local(a):
            return pltpu.make_async_copy(ins[a], outs[a].at[4 * x + 2 * y + c], local_sems.at[a])

        def first(a):
            return [copy(a, 0, me, sibling, src=ins[a])] + [copy(a, 1 + j, me, (*chip, c), src=ins[a]) for j, chip in enumerate(chips)]

        def passed(a, j):
            return copy(a, 4 + j, (*chips[j], c), sibling)

        if p == 0:
            for a in range(n):
                local(a).start()
            for a in range(n):
                for cp in first(a):
                    cp.start()
        elif p == 1:
            for a in range(n):
                for j, chip in enumerate(chips):
                    copy(a, 1 + j, (*chip, c), me).wait_recv()
                    passed(a, j).start()
        else:
            for a in range(n):
                copy(a, 0, sibling, me).wait_recv()
                for j, chip in enumerate(chips):
                    copy(a, 4 + j, (*chip, 1 - c), me).wait_recv()
            for a in range(n):
                for cp in first(a) + [passed(a, j) for j in range(3)]:
                    cp.wait_send()
                local(a).wait()


class _Exchange:
    n_phases = 2

    def __init__(self, arrs):
        self.arrs = list(arrs)
        n = len(self.arrs)
        self.out_shapes = [jax.ShapeDtypeStruct(a.shape, a.dtype) for a in self.arrs]
        self.scratch = [pltpu.SemaphoreType.DMA((n, 7)), pltpu.SemaphoreType.DMA((n, 7)), pltpu.SemaphoreType.DMA((n,))]

    def phase(self, p, ins, outs, sems):
        send_sems, recv_sems, local_sems = sems
        n = len(self.arrs)
        x, y, c = _my_position()
        me = 4 * x + 2 * y + c

        def peer_of(r):
            return (x ^ ((r >> 2) & 1), y ^ ((r >> 1) & 1), c ^ (r & 1))

        def copy(a, r):
            px, py, pc = peer_of(r)
            return pltpu.make_async_remote_copy(
                src_ref=ins[a].at[4 * px + 2 * py + pc], dst_ref=outs[a].at[me],
                send_sem=send_sems.at[a, r - 1], recv_sem=recv_sems.at[a, r - 1],
                device_id=(px, py, pc), device_id_type=MESH)

        def arrival(a, r):
            px, py, pc = peer_of(r)
            slot = outs[a].at[4 * px + 2 * py + pc]
            return pltpu.make_async_remote_copy(
                src_ref=slot, dst_ref=slot, send_sem=send_sems.at[a, r - 1], recv_sem=recv_sems.at[a, r - 1],
                device_id=(px, py, pc), device_id_type=MESH)

        def local(a):
            return pltpu.make_async_copy(ins[a].at[me], outs[a].at[me], local_sems.at[a])

        if p == 0:
            for a in range(n):
                local(a).start()
            for a in range(n):
                for r in range(1, N_DEV):
                    copy(a, r).start()
        else:
            for a in range(n):
                for r in range(1, N_DEV):
                    arrival(a, r).wait_recv()
            for a in range(n):
                for r in range(1, N_DEV):
                    copy(a, r).wait_send()
                local(a).wait()


def _run_alone(comm, name):
    n = len(comm.arrs)

    def body(*refs):
        for p in range(comm.n_phases):
            comm.phase(p, refs[:n], refs[n:2 * n], refs[2 * n:])

    return pl.pallas_call(body, name=name, out_shape=comm.out_shapes, in_specs=[ANY] * n, out_specs=[ANY] * n,
                          scratch_shapes=comm.scratch)(*comm.arrs)


def _call(body, *, name, grid, in_specs, out_specs, out_shape, operands, scratch_shapes=(), comm=None):
    scratch_shapes = list(scratch_shapes)
    params = pltpu.CompilerParams(dimension_semantics=("arbitrary",) * len(grid), vmem_limit_bytes=VMEM_LIMIT_BYTES)
    if comm is None:
        res = pl.pallas_call(body, name=name, grid=grid, in_specs=in_specs, out_specs=out_specs, out_shape=out_shape,
                             scratch_shapes=scratch_shapes, compiler_params=params)(*operands)
        return res, None
    n_in, n_out, n_sc = len(in_specs), len(out_specs), len(scratch_shapes)
    nc = len(comm.arrs)
    total = math.prod(grid)
    at = {0: 0, comm.n_phases - 1: total - 1}
    for p in range(1, comm.n_phases - 1):
        at[p] = (total * 7) // 8

    def wrapped(*refs):
        ins, cins = refs[:n_in], refs[n_in:n_in + nc]
        outs, couts = refs[n_in + nc:n_in + nc + n_out], refs[n_in + nc + n_out:n_in + 2 * nc + n_out]
        rest = refs[n_in + 2 * nc + n_out:]
        scratch, sems = rest[:n_sc], rest[n_sc:]
        step = 0
        for axis, size in enumerate(grid):
            step = step * size + pl.program_id(axis)
        for p in range(comm.n_phases - 1):
            @pl.when(step == at[p])
            def _(p=p):
                comm.phase(p, cins, couts, sems)
        body(*ins, *outs, *scratch)

        @pl.when(step == total - 1)
        def _():
            comm.phase(comm.n_phases - 1, cins, couts, sems)

    res = pl.pallas_call(
        wrapped, name=name, grid=grid, in_specs=list(in_specs) + [ANY] * nc, out_specs=list(out_specs) + [ANY] * nc,
        out_shape=list(out_shape) + comm.out_shapes, scratch_shapes=scratch_shapes + comm.scratch,
        compiler_params=params)(*operands, *comm.arrs)
    return res[:n_out], res[n_out:]


def _all_reduce_small(vec, name):
    n = vec.shape[1]

    def body(v_ref, o_ref, buf, send_sems, recv_sems):
        x, y, c = _my_position()
        me = 4 * x + 2 * y + c

        def peer_of(r):
            return (x ^ ((r >> 2) & 1), y ^ ((r >> 1) & 1), c ^ (r & 1))

        def copy(r):
            px, py, pc = peer_of(r)
            return pltpu.make_async_remote_copy(
                src_ref=v_ref, dst_ref=buf.at[me], send_sem=send_sems.at[r - 1], recv_sem=recv_sems.at[r - 1],
                device_id=(px, py, pc), device_id_type=MESH)

        def arrival(r):
            px, py, pc = peer_of(r)
            slot = buf.at[4 * px + 2 * py + pc]
            return pltpu.make_async_remote_copy(
                src_ref=slot, dst_ref=slot, send_sem=send_sems.at[r - 1], recv_sem=recv_sems.at[r - 1],
                device_id=(px, py, pc), device_id_type=MESH)

        sends = [copy(r) for r in range(1, N_DEV)]
        for cp in sends:
            cp.start()
        buf[me] = v_ref[...]
        for r in range(1, N_DEV):
            arrival(r).wait_recv()
        for cp in sends:
            cp.wait_send()
        total = buf[0]
        for d in range(1, N_DEV):
            total = total + buf[d]
        o_ref[...] = total

    vmem = pl.BlockSpec(memory_space=pltpu.VMEM)
    return pl.pallas_call(
        body, name=name, out_shape=jax.ShapeDtypeStruct((1, n), F32), in_specs=[vmem], out_specs=vmem,
        scratch_shapes=[pltpu.VMEM((N_DEV, 1, n), F32), pltpu.SemaphoreType.DMA((7,)), pltpu.SemaphoreType.DMA((7,))],
    )(vec)


def _mm_core(pairs, dims, out_dtype, *, name, grid, out_shape, out_spec, acc_shape, alpha=1.0, res=None, comm=None):
    nk = grid[2]
    npairs = len(pairs)

    def body(*refs):
        ab = refs[:2 * npairs]
        rest = refs[2 * npairs:]
        res_ref = rest[0] if res is not None else None
        o_ref = rest[1] if res is not None else rest[0]
        acc_ref = rest[-1] if nk > 1 else None
        part = None
        for p in range(npairs):
            a_ref, b_ref = ab[2 * p], ab[2 * p + 1]
            shards = [(a_ref[s], b_ref[s]) for s in range(a_ref.shape[0])] if len(a_ref.shape) == 3 else [(a_ref[...], b_ref[...])]
            for av, bv in shards:
                d = _dot(av.astype(BF16), bv.astype(BF16), dims)
                part = d if part is None else part + d

        def finish(total):
            val = total * alpha if alpha != 1.0 else total
            if res_ref is not None:
                val = res_ref[...] + val
            o_ref[...] = val.astype(out_dtype)

        if nk == 1:
            finish(part)
        else:
            kk = pl.program_id(2)

            @pl.when(kk == 0)
            def _():
                acc_ref[...] = part

            @pl.when(kk > 0)
            def _():
                acc_ref[...] += part

            @pl.when(kk == nk - 1)
            def _():
                finish(acc_ref[...])

    operands, in_specs = [], []
    for (a, a_spec), (b, b_spec) in pairs:
        operands += [a, b]
        in_specs += [a_spec, b_spec]
    if res is not None:
        operands.append(res[0])
        in_specs.append(res[1])
    out, got = _call(body, name=name, grid=grid, in_specs=in_specs, out_specs=[out_spec],
                     out_shape=[jax.ShapeDtypeStruct(out_shape, out_dtype)], operands=operands,
                     scratch_shapes=[pltpu.VMEM(acc_shape, F32)] if nk > 1 else [], comm=comm)
    return out[0] if comm is None else (out[0], got)


def _mm(pairs, mode, out_dtype, *, name, alpha=1.0, res=None, comm=None, whole_k=False):
    a0, b0 = pairs[0]
    if mode == "nn":
        (m, k), n = a0.shape, b0.shape[1]
    elif mode == "nt":
        (m, k), n = a0.shape, b0.shape[0]
    else:
        (k, m), n = a0.shape, b0.shape[1]
    dims = {"nn": NN_DIMS, "nt": NT_DIMS, "tn": TN_DIMS}[mode]
    tm = _tile(m, 528 if whole_k else 1056, LANES if mode == "tn" else 16)
    tn = _tile(n, 512 if whole_k else 1024, LANES)
    tk = k if whole_k else _tile(k, 2048 if mode != "tn" else 2112, LANES if mode != "tn" else 16)
    a_spec = pl.BlockSpec((tk, tm), lambda i, j, kk: (kk, i)) if mode == "tn" else pl.BlockSpec((tm, tk), lambda i, j, kk: (i, kk))
    b_spec = pl.BlockSpec((tn, tk), lambda i, j, kk: (j, kk)) if mode == "nt" else pl.BlockSpec((tk, tn), lambda i, j, kk: (kk, j))
    o_spec = pl.BlockSpec((tm, tn), lambda i, j, kk: (i, j))
    return _mm_core([((a, a_spec), (b, b_spec)) for a, b in pairs], dims, out_dtype, name=name,
                    grid=(m // tm, n // tn, k // tk), out_shape=(m, n), out_spec=o_spec, acc_shape=(tm, tn),
                    alpha=alpha, res=None if res is None else (res, o_spec), comm=comm)


def _rms_fwd(h, gain, name):
    m, d = h.shape
    tm = _tile(m, 528, 16)

    def body(h_ref, g_ref, o_ref):
        hv = h_ref[...]
        r = lax.rsqrt(jnp.mean(hv * hv, axis=-1, keepdims=True) + EPS)
        o_ref[...] = (hv * r * g_ref[...]).astype(BF16)

    row = pl.BlockSpec((tm, d), lambda i: (i, 0))
    out, _ = _call(body, name=name, grid=(m // tm,), in_specs=[row, pl.BlockSpec((1, d), lambda i: (0, 0))],
                   out_specs=[row], out_shape=[jax.ShapeDtypeStruct((m, d), BF16)], operands=[h, gain])
    return out[0]


def _rms_bwd(dxn, h, gain, dres, name):
    m, d = h.shape
    tm = _tile(m, 264, 16)

    def body(dxn_ref, h_ref, g_ref, dres_ref, dh_ref, dhb_ref, dg_ref):
        hv = h_ref[...]
        r = lax.rsqrt(jnp.mean(hv * hv, axis=-1, keepdims=True) + EPS)
        xhat = hv * r
        dxn_v = dxn_ref[...]
        t = dxn_v * g_ref[...]
        dh = dres_ref[...] + r * (t - xhat * jnp.mean(t * xhat, axis=-1, keepdims=True))
        dh_ref[...] = dh
        dhb_ref[...] = dh.astype(BF16)
        part = jnp.sum(dxn_v * xhat, axis=0, keepdims=True)

        @pl.when(pl.program_id(0) == 0)
        def _():
            dg_ref[...] = part

        @pl.when(pl.program_id(0) > 0)
        def _():
            dg_ref[...] += part

    row = pl.BlockSpec((tm, d), lambda i: (i, 0))
    vec = pl.BlockSpec((1, d), lambda i: (0, 0))
    out, _ = _call(body, name=name, grid=(m // tm,), in_specs=[row, row, vec, row], out_specs=[row, row, vec],
                   out_shape=[jax.ShapeDtypeStruct((m, d), F32), jax.ShapeDtypeStruct((m, d), BF16),
                              jax.ShapeDtypeStruct((1, d), F32)],
                   operands=[dxn, h, gain, dres])
    return out


def _loss_head(h, gain, target, x_off, name):
    m, d = h.shape
    tm = ROW_BLOCK
    first = x_off // tm

    def body(h_ref, g_ref, t_ref, dh_ref, dhb_ref, dg_ref, loss_ref):
        i = pl.program_id(0)

        @pl.when(i == 0)
        def _():
            dg_ref[...] = jnp.zeros_like(dg_ref)
            loss_ref[...] = jnp.zeros_like(loss_ref)

        @pl.when(i < first)
        def _():
            dh_ref[...] = jnp.zeros_like(dh_ref)
            dhb_ref[...] = jnp.zeros_like(dhb_ref)

        @pl.when(i >= first)
        def _():
            hv = h_ref[...]
            g = g_ref[...]
            r = lax.rsqrt(jnp.mean(hv * hv, axis=-1, keepdims=True) + EPS)
            xhat = hv * r
            err = xhat * g - t_ref[...]
            loss_ref[...] += 0.5 * jnp.sum(jnp.mean(err * err, axis=-1, keepdims=True))
            dy = err * (1.0 / d)
            t = dy * g
            dh = r * (t - xhat * jnp.mean(t * xhat, axis=-1, keepdims=True))
            dh_ref[...] = dh
            dhb_ref[...] = dh.astype(BF16)
            dg_ref[...] += jnp.sum(dy * xhat, axis=0, keepdims=True)

    row = pl.BlockSpec((tm, d), lambda i: (i, 0))
    vec = pl.BlockSpec((1, d), lambda i: (0, 0))
    out, _ = _call(body, name=name, grid=(m // tm,),
                   in_specs=[row, vec, pl.BlockSpec((tm, d), lambda i: (jnp.maximum(i - first, 0), 0))],
                   out_specs=[row, row, vec, pl.BlockSpec((8, LANES), lambda i: (0, 0))],
                   out_shape=[jax.ShapeDtypeStruct((m, d), F32), jax.ShapeDtypeStruct((m, d), BF16),
                              jax.ShapeDtypeStruct((1, d), F32), jax.ShapeDtypeStruct((8, LANES), F32)],
                   operands=[h, gain, target])
    return out


def _sigmoid(z):
    return 1.0 / (1.0 + jnp.exp(-z))


def _ffn_up(xn, wg, wu, name, comm=None):
    m, d = xn.shape
    nsh, c, _ = wg.shape
    tm = _tile(m, 1056, 16)

    def body(x_ref, wg_ref, wu_ref, g_ref, u_ref, a_ref):
        xv = x_ref[...]
        g = _dot(xv, wg_ref[...], NT_DIMS)
        u = _dot(xv, wu_ref[...], NT_DIMS)
        g_ref[...] = g.astype(BF16)
        u_ref[...] = u.astype(BF16)
        a_ref[...] = (g * _sigmoid(g) * u).astype(BF16)

    out = pl.BlockSpec((None, tm, c), lambda i, j: (j, i, 0))
    w = pl.BlockSpec((None, c, d), lambda i, j: (j, 0, 0))
    return _call(body, name=name, grid=(m // tm, nsh), in_specs=[pl.BlockSpec((tm, d), lambda i, j: (i, 0)), w, w],
                 out_specs=[out, out, out], out_shape=[jax.ShapeDtypeStruct((nsh, m, c), BF16)] * 3,
                 operands=[xn, wg, wu], comm=comm)


def _ffn_contract(pairs, out_dtype, name, *, group, alpha=1.0, res=None, comm=None):
    nsh, m, c = pairs[0][0].shape
    d = pairs[0][1].shape[2]
    tm, tn = _tile(m, 1056, 16), _tile(d, 512, LANES)
    a_spec = pl.BlockSpec((group, tm, c), lambda i, j, kk: (kk, i, 0))
    b_spec = pl.BlockSpec((group, c, tn), lambda i, j, kk: (kk, 0, j))
    o_spec = pl.BlockSpec((tm, tn), lambda i, j, kk: (i, j))
    return _mm_core([((a, a_spec), (b, b_spec)) for a, b in pairs], NN_DIMS, out_dtype, name=name,
                    grid=(m // tm, d // tn, nsh // group), out_shape=(m, d), out_spec=o_spec, acc_shape=(tm, tn),
                    alpha=alpha, res=None if res is None else (res, o_spec), comm=comm)


def _ffn_bwd_act(dh, wd, g, u, name, comm=None):
    m, d = dh.shape
    nsh, c, _ = wd.shape
    tm = _tile(m, 1056, 16)

    def body(dh_ref, wd_ref, g_ref, u_ref, dg_ref, du_ref):
        dact = 0.5 * _dot(dh_ref[...], wd_ref[...], NT_DIMS)
        gv = g_ref[...].astype(F32)
        uv = u_ref[...].astype(F32)
        sig = _sigmoid(gv)
        du_ref[...] = (dact * gv * sig).astype(BF16)
        dg_ref[...] = (dact * uv * sig * (1.0 + gv * (1.0 - sig))).astype(BF16)

    blk = pl.BlockSpec((None, tm, c), lambda i, j: (j, i, 0))
    return _call(body, name=name, grid=(m // tm, nsh),
                 in_specs=[pl.BlockSpec((tm, d), lambda i, j: (i, 0)), pl.BlockSpec((None, c, d), lambda i, j: (j, 0, 0)), blk, blk],
                 out_specs=[blk, blk], out_shape=[jax.ShapeDtypeStruct((nsh, m, c), BF16)] * 2, operands=[dh, wd, g, u],
                 comm=comm)


def _ffn_dw(z, x, name, alpha=1.0, comm=None):
    nsh, m, c = z.shape
    d = x.shape[1]
    tn, tk = _tile(d, 1024, LANES), _tile(m, 2112, 16)
    return _mm_core([((z, pl.BlockSpec((None, tk, c), lambda i, j, kk: (i, kk, 0))),
                      (x, pl.BlockSpec((tk, tn), lambda i, j, kk: (kk, j))))],
                    TN_DIMS, BF16, name=name, grid=(nsh, d // tn, m // tk), out_shape=(nsh, c, d),
                    out_spec=pl.BlockSpec((None, c, tn), lambda i, j, kk: (i, 0, j)), acc_shape=(c, tn), alpha=alpha,
                    comm=comm)


def _dot3(tri, x):
    h1 = x.astype(BF16)
    r1 = x - h1.astype(F32)
    h2 = r1.astype(BF16)
    h3 = (r1 - h2.astype(F32)).astype(BF16)
    return _dot(tri, h1) + _dot(tri, h2) + _dot(tri, h3)


def _log_sigmoid(z):
    return jnp.minimum(z, 0.0) - jnp.log(1.0 + jnp.exp(-jnp.abs(z)))


def _triangle(t, cmp):
    return cmp(lax.broadcasted_iota(jnp.int32, (t, t), 0), lax.broadcasted_iota(jnp.int32, (t, t), 1)).astype(BF16)


def _forget_cumsum(fl, bias, n_heads, pad, name):
    m = fl.shape[0]
    nb = m // ROW_BLOCK

    def body(fl_ref, b_ref, c_ref):
        tri = _triangle(ROW_BLOCK, lambda r, c: r >= c)
        lane_ok = lax.broadcasted_iota(jnp.int32, (ROW_BLOCK, LANES), 1) < n_heads
        rows = lax.broadcasted_iota(jnp.int32, (ROW_BLOCK, LANES), 0)

        def step(b, carry):
            off = pl.multiple_of(b * ROW_BLOCK, ROW_BLOCK)
            lf = _log_sigmoid(fl_ref[pl.ds(off, ROW_BLOCK), :] + b_ref[...])
            lf = jnp.where(lane_ok & (rows + off >= pad), lf, 0.0)
            cs = _dot3(tri, lf) + carry
            c_ref[pl.ds(off, ROW_BLOCK), :] = cs
            return cs[ROW_BLOCK - 1:ROW_BLOCK, :]

        lax.fori_loop(0, nb, step, jnp.zeros((1, LANES), F32))

    vmem = pl.BlockSpec(memory_space=pltpu.VMEM)
    return pl.pallas_call(
        body, name=name, out_shape=jax.ShapeDtypeStruct((m, LANES), F32), in_specs=[vmem, vmem], out_specs=vmem,
        compiler_params=pltpu.CompilerParams(vmem_limit_bytes=VMEM_LIMIT_BYTES),
    )(fl, bias)


def _forget_cumsum_bwd(dc_a, dc_b, fl, bias, n_heads, pad, name):
    m = fl.shape[0]
    nb = m // ROW_BLOCK

    def body(da_ref, db_ref, fl_ref, b_ref, dfl_ref, dbias_ref):
        tri = _triangle(ROW_BLOCK, lambda r, c: r <= c)
        lane_ok = lax.broadcasted_iota(jnp.int32, (ROW_BLOCK, LANES), 1) < n_heads
        rows = lax.broadcasted_iota(jnp.int32, (ROW_BLOCK, LANES), 0)

        def step(bb, carry):
            tail, dbias = carry
            off = pl.multiple_of((nb - 1 - bb) * ROW_BLOCK, ROW_BLOCK)
            dc = da_ref[pl.ds(off, ROW_BLOCK), :] + db_ref[pl.ds(off, ROW_BLOCK), :]
            dlf = _dot3(tri, dc) + tail
            z = fl_ref[pl.ds(off, ROW_BLOCK), :] + b_ref[...]
            dfl = jnp.where(lane_ok & (rows + off >= pad), dlf * _sigmoid(-z), 0.0)
            dfl_ref[pl.ds(off, ROW_BLOCK), :] = dfl
            return dlf[0:1, :], dbias + jnp.sum(dfl, axis=0, keepdims=True)

        zero = jnp.zeros((1, LANES), F32)
        _, dbias = lax.fori_loop(0, nb, step, (zero, zero))
        dbias_ref[...] = dbias

    vmem = pl.BlockSpec(memory_space=pltpu.VMEM)
    return pl.pallas_call(
        body, name=name,
        out_shape=[jax.ShapeDtypeStruct((m, LANES), F32), jax.ShapeDtypeStruct((1, LANES), F32)],
        in_specs=[vmem] * 4, out_specs=[vmem, vmem],
        compiler_params=pltpu.CompilerParams(vmem_limit_bytes=VMEM_LIMIT_BYTES),
    )(dc_a, dc_b, fl, bias)


def _attn_block(m):
    return 3 * ROW_BLOCK if m % (3 * ROW_BLOCK) == 0 else ROW_BLOCK


def _block(ref, j, t):
    return ref[pl.ds(pl.multiple_of(j * t, t), t), :]


def _head_norm(o, gain):
    r = lax.rsqrt(jnp.mean(o * o, axis=-1, keepdims=True) + EPS)
    return o * r * gain


def _head_norm_bwd(o, d_on, gain):
    r = lax.rsqrt(jnp.mean(o * o, axis=-1, keepdims=True) + EPS)
    ohat = o * r
    t = d_on * gain
    d_o = r * (t - ohat * jnp.mean(t * ohat, axis=-1, keepdims=True))
    return d_o, jnp.sum(d_on * ohat, axis=0, keepdims=True)


def _qkv_specs(t, m, h, first_col_block):
    q = pl.BlockSpec((t, HEAD_DIM), lambda hd, i: (i, first_col_block + hd))
    k = pl.BlockSpec((m, HEAD_DIM), lambda hd, i: (0, first_col_block + h + hd))
    v = pl.BlockSpec((m, HEAD_DIM), lambda hd, i: (0, first_col_block + 2 * h + hd))
    return q, k, v


def _fox_fwd(qkv, ccol, crow, gain, n_heads, pad, name, comm=None):
    m = qkv.shape[0]
    t = _attn_block(m)
    nq = m // t
    scale = HEAD_DIM ** -0.5
    hw = n_heads * HEAD_DIM

    def body(q_ref, k_ref, v_ref, ccol_ref, crow_ref, g_ref, o_ref, on_ref, lse_ref):
        i = pl.program_id(1)
        q = q_ref[...]
        ci = ccol_ref[...]
        qpos = i * t + lax.broadcasted_iota(jnp.int32, (t, 1), 0)

        def step(j, j_prev, j_next, state, masked):
            mx, l, acc_scaled, p_prev, s_cur = state
            acc = acc_scaled + _dot(p_prev, _block(v_ref, j_prev, t))
            s_next = _dot(q, _block(k_ref, j_next, t), NT_DIMS)
            s = s_cur * scale + ci - crow_ref[j]
            if masked:
                kpos = j * t + lax.broadcasted_iota(jnp.int32, (1, t), 1)
                s = jnp.where((kpos <= qpos) & (kpos >= pad), s, NEG)
            mx_new = jnp.maximum(mx, jnp.max(s, axis=-1, keepdims=True))
            p = jnp.exp(s - mx_new)
            a = jnp.exp(mx - mx_new)
            return mx_new, a * l + jnp.sum(p, axis=-1, keepdims=True), a * acc, p.astype(BF16), s_next

        state = (jnp.full((t, 1), NEG, F32), jnp.zeros((t, 1), F32), jnp.zeros((t, HEAD_DIM), F32),
                 jnp.zeros((t, t), BF16), _dot(q, _block(k_ref, 0, t), NT_DIMS))
        state = step(0, 0, jnp.minimum(i, 1), state, True)
        state = lax.fori_loop(1, i, lambda j, st: step(j, j - 1, j + 1, st, False), state)
        mx, l, acc_scaled, p_last, _ = lax.fori_loop(0, jnp.minimum(i, 1), lambda _, st: step(i, i - 1, i, st, True), state)
        acc = acc_scaled + _dot(p_last, _block(v_ref, i, t))
        valid = qpos >= pad
        o = jnp.where(valid, acc / l, 0.0)
        o_ref[...] = o
        on_ref[...] = _head_norm(o, g_ref[...]).astype(BF16)
        lse_ref[...] = jnp.where(valid, mx + jnp.log(l), 0.0)

    q_spec, k_spec, v_spec = _qkv_specs(t, m, n_heads, 0)
    col = pl.BlockSpec((None, t, 1), lambda hd, i: (hd, i, 0))
    head = pl.BlockSpec((t, HEAD_DIM), lambda hd, i: (i, hd))
    return _call(body, name=name, grid=(n_heads, nq),
                 in_specs=[q_spec, k_spec, v_spec, col, pl.BlockSpec((None, nq, 1, t), lambda hd, i: (hd, 0, 0, 0)),
                           pl.BlockSpec((1, HEAD_DIM), lambda hd, i: (0, hd))],
                 out_specs=[head, head, col],
                 out_shape=[jax.ShapeDtypeStruct((m, hw), F32), jax.ShapeDtypeStruct((m, hw), BF16),
                            jax.ShapeDtypeStruct((n_heads, m, 1), F32)],
                 operands=[qkv, qkv, qkv, ccol, crow, gain], comm=comm)


def _fox_bwd(qkv, o, d_on, gain, lse, ccol, crow, n_heads, pad, name, comm=None):
    m = qkv.shape[0]
    t = _attn_block(m)
    nq = m // t
    scale = HEAD_DIM ** -0.5
    hw = n_heads * HEAD_DIM

    def body(q_ref, k_ref, v_ref, o_ref, don_ref, g_ref, lse_ref, ccol_ref, crow_ref,
             dq_ref, dk_ref, dv_ref, dg_ref, dccol_ref, dcrow_ref):
        i = pl.program_id(1)

        @pl.when(i == 0)
        def _():
            dk_ref[...] = jnp.zeros_like(dk_ref)
            dv_ref[...] = jnp.zeros_like(dv_ref)
            dg_ref[...] = jnp.zeros_like(dg_ref)
            dcrow_ref[...] = jnp.zeros_like(dcrow_ref)

        q = q_ref[...]
        o = o_ref[...]
        d_o, dgain = _head_norm_bwd(o, don_ref[...], g_ref[...])
        dg_ref[...] += dgain
        delta = jnp.sum(d_o * o, axis=-1, keepdims=True)
        d_ob = d_o.astype(BF16)
        ci = ccol_ref[...]
        lse_i = lse_ref[...]
        qpos = i * t + lax.broadcasted_iota(jnp.int32, (t, 1), 0)

        def step(j, carry, masked):
            dq, dci = carry
            k = _block(k_ref, j, t)
            v = _block(v_ref, j, t)
            off = pl.multiple_of(j * t, t)
            s = _dot(q, k, NT_DIMS) * scale + ci - crow_ref[j]
            if masked:
                kpos = off + lax.broadcasted_iota(jnp.int32, (1, t), 1)
                ok = (kpos <= qpos) & (kpos >= pad)
                p = jnp.where(ok, jnp.exp(jnp.where(ok, s - lse_i, 0.0)), 0.0)
            else:
                p = jnp.exp(s - lse_i)
            ds = p * (_dot(d_ob, v, NT_DIMS) - delta)
            dsb = ds.astype(BF16)
            dk_ref[pl.ds(off, t), :] += _dot(dsb, q, TN_DIMS) * scale
            dv_ref[pl.ds(off, t), :] += _dot(p.astype(BF16), d_ob, TN_DIMS)
            dcrow_ref[j] -= jnp.sum(ds, axis=0, keepdims=True)
            return dq + _dot(dsb, k), dci + jnp.sum(ds, axis=-1, keepdims=True)

        carry = step(0, (jnp.zeros((t, HEAD_DIM), F32), jnp.zeros((t, 1), F32)), True)
        carry = lax.fori_loop(1, i, lambda j, c: step(j, c, False), carry)
        dq, dci = lax.fori_loop(0, jnp.minimum(i, 1), lambda _, c: step(i, c, True), carry)
        dq_ref[...] = (dq * scale).astype(BF16)
        dccol_ref[...] = dci

    q_spec, k_spec, v_spec = _qkv_specs(t, m, n_heads, 0)
    col = pl.BlockSpec((None, t, 1), lambda hd, i: (hd, i, 0))
    rowc = pl.BlockSpec((None, nq, 1, t), lambda hd, i: (hd, 0, 0, 0))
    head = pl.BlockSpec((t, HEAD_DIM), lambda hd, i: (i, hd))
    whole = pl.BlockSpec((m, HEAD_DIM), lambda hd, i: (0, hd))
    gvec = pl.BlockSpec((1, HEAD_DIM), lambda hd, i: (0, hd))
    return _call(body, name=name, grid=(n_heads, nq),
                 in_specs=[q_spec, k_spec, v_spec, head, head, gvec, col, col, rowc],
                 out_specs=[head, whole, whole, gvec, col, rowc],
                 out_shape=[jax.ShapeDtypeStruct((m, hw), BF16), jax.ShapeDtypeStruct((m, hw), F32),
                            jax.ShapeDtypeStruct((m, hw), F32), jax.ShapeDtypeStruct((1, hw), F32),
                            jax.ShapeDtypeStruct((n_heads, m, 1), F32), jax.ShapeDtypeStruct((n_heads, nq, 1, t), F32)],
                 operands=[qkv, qkv, qkv, o, d_on, gain, lse, ccol, crow], comm=comm)


def _sb_scores(z):
    lp = jnp.log(1.0 + jnp.exp(-jnp.abs(z)))
    return jnp.minimum(z, 0.0) - lp, jnp.minimum(-z, 0.0) - lp


def _sb_fwd(qkv, gain, n_heads, pad, name, comm=None):
    m = qkv.shape[0]
    t = _attn_block(m)
    nq = m // t
    assert nq <= LANES
    scale = HEAD_DIM ** -0.5
    hw = n_heads * HEAD_DIM

    def body(q_ref, k_ref, v_ref, g_ref, after_ref, o_ref, on_ref, run_ref):
        i = pl.program_id(1)
        q = q_ref[...]
        qpos = i * t + lax.broadcasted_iota(jnp.int32, (t, 1), 0)
        after = after_ref[...]
        lane = lax.broadcasted_iota(jnp.int32, (t, LANES), 1)

        def step(j, j_prev, j_next, state, masked):
            run, acc, a_prev, s_cur = state
            acc = acc + _dot(a_prev, _block(v_ref, j_prev, t))
            s_next = _dot(q, _block(k_ref, j_next, t), NT_DIMS)
            ls_pos, log_1m = _sb_scores(s_cur * scale)
            if masked:
                kpos = j * t + lax.broadcasted_iota(jnp.int32, (1, t), 1)
                ok = (kpos < qpos) & (kpos >= pad)
                log_1m = jnp.where(ok, log_1m, 0.0)
            a = jnp.exp(ls_pos + _dot_split(log_1m, after) + run)
            if masked:
                a = jnp.where(ok, a, 0.0)
            run_ref[...] = jnp.where(lane == j, run, run_ref[...])
            return run + jnp.sum(log_1m, axis=-1, keepdims=True), acc, a.astype(BF16), s_next

        run_ref[...] = jnp.zeros_like(run_ref)
        state = (jnp.zeros((t, 1), F32), jnp.zeros((t, HEAD_DIM), F32), jnp.zeros((t, t), BF16),
                 _dot(q, _block(k_ref, i, t), NT_DIMS))
        state = step(i, i, jnp.maximum(i - 1, 0), state, True)
        state = lax.fori_loop(1, i, lambda jj, st: step(i - jj, i - jj + 1, i - jj - 1, st, False), state)
        _, acc, a_last, _ = lax.fori_loop(0, jnp.minimum(i, 1), lambda _, st: step(0, 1, 0, st, True), state)
        o = acc + _dot(a_last, _block(v_ref, 0, t))
        o_ref[...] = o
        on_ref[...] = _head_norm(o, g_ref[...]).astype(BF16)

    q_spec, k_spec, v_spec = _qkv_specs(t, m, n_heads, 3 * n_heads)
    head = pl.BlockSpec((t, HEAD_DIM), lambda hd, i: (i, hd))
    return _call(body, name=name, grid=(n_heads, nq),
                 in_specs=[q_spec, k_spec, v_spec, pl.BlockSpec((1, HEAD_DIM), lambda hd, i: (0, hd)),
                           pl.BlockSpec((t, t), lambda hd, i: (0, 0))],
                 out_specs=[head, head, pl.BlockSpec((None, t, LANES), lambda hd, i: (hd, i, 0))],
                 out_shape=[jax.ShapeDtypeStruct((m, hw), F32), jax.ShapeDtypeStruct((m, hw), BF16),
                            jax.ShapeDtypeStruct((n_heads, m, LANES), F32)],
                 operands=[qkv, qkv, qkv, gain, _triangle(t, lambda r, c: r > c)], comm=comm)


def _sb_bwd(qkv, o, d_on, gain, runs, n_heads, pad, name, comm=None):
    m = qkv.shape[0]
    t = _attn_block(m)
    nq = m // t
    scale = HEAD_DIM ** -0.5
    hw = n_heads * HEAD_DIM

    def body(q_ref, k_ref, v_ref, o_ref, don_ref, g_ref, run_ref, after_ref, before_ref, dq_ref, dk_ref, dv_ref, dg_ref):
        i = pl.program_id(1)

        @pl.when(i == 0)
        def _():
            dk_ref[...] = jnp.zeros_like(dk_ref)
            dv_ref[...] = jnp.zeros_like(dv_ref)
            dg_ref[...] = jnp.zeros_like(dg_ref)

        q = q_ref[...]
        d_o, dgain = _head_norm_bwd(o_ref[...], don_ref[...], g_ref[...])
        dg_ref[...] += dgain
        d_ob = d_o.astype(BF16)
        runs_i = run_ref[...]
        qpos = i * t + lax.broadcasted_iota(jnp.int32, (t, 1), 0)
        after = after_ref[...]
        before = before_ref[...]
        lane = lax.broadcasted_iota(jnp.int32, (t, LANES), 1)

        def step(j, carry, masked):
            g_run, dq = carry
            k = _block(k_ref, j, t)
            v = _block(v_ref, j, t)
            off = pl.multiple_of(j * t, t)
            ls_pos, ls_neg = _sb_scores(_dot(q, k, NT_DIMS) * scale)
            log_1m = ls_neg
            if masked:
                kpos = off + lax.broadcasted_iota(jnp.int32, (1, t), 1)
                ok = (kpos < qpos) & (kpos >= pad)
                log_1m = jnp.where(ok, ls_neg, 0.0)
            run = jnp.sum(jnp.where(lane == j, runs_i, 0.0), axis=-1, keepdims=True)
            a = jnp.exp(ls_pos + _dot_split(log_1m, after) + run)
            if masked:
                a = jnp.where(ok, a, 0.0)
            g = a * _dot(d_ob, v, NT_DIMS)
            prefix = _dot(g.astype(BF16), before) + g_run
            dz = g * jnp.exp(ls_neg) - jnp.exp(ls_pos) * prefix
            if masked:
                dz = jnp.where(ok, dz, 0.0)
            dzb = dz.astype(BF16)
            dk_ref[pl.ds(off, t), :] += _dot(dzb, q, TN_DIMS) * scale
            dv_ref[pl.ds(off, t), :] += _dot(a.astype(BF16), d_ob, TN_DIMS)
            return g_run + jnp.sum(g, axis=-1, keepdims=True), dq + _dot(dzb, k)

        carry = step(0, (jnp.zeros((t, 1), F32), jnp.zeros((t, HEAD_DIM), F32)), True)
        carry = lax.fori_loop(1, i, lambda j, c: step(j, c, False), carry)
        _, dq = lax.fori_loop(0, jnp.minimum(i, 1), lambda _, c: step(i, c, True), carry)
        dq_ref[...] = (dq * scale).astype(BF16)

    q_spec, k_spec, v_spec = _qkv_specs(t, m, n_heads, 3 * n_heads)
    head = pl.BlockSpec((t, HEAD_DIM), lambda hd, i: (i, hd))
    whole = pl.BlockSpec((m, HEAD_DIM), lambda hd, i: (0, hd))
    gvec = pl.BlockSpec((1, HEAD_DIM), lambda hd, i: (0, hd))
    tri = pl.BlockSpec((t, t), lambda hd, i: (0, 0))
    return _call(body, name=name, grid=(n_heads, nq),
                 in_specs=[q_spec, k_spec, v_spec, head, head, gvec, pl.BlockSpec((None, t, LANES), lambda hd, i: (hd, i, 0)),
                           tri, tri],
                 out_specs=[head, whole, whole, gvec],
                 out_shape=[jax.ShapeDtypeStruct((m, hw), BF16), jax.ShapeDtypeStruct((m, hw), F32),
                            jax.ShapeDtypeStruct((m, hw), F32), jax.ShapeDtypeStruct((1, hw), F32)],
                 operands=[qkv, qkv, qkv, o, d_on, gain, runs, _triangle(t, lambda r, c: r > c), _triangle(t, lambda r, c: r < c)],
                 comm=comm)


def _adamw(parts, w, m1, v2, name, comm=None):
    nl, r, c = w.shape
    assert len(parts) == nl
    n_parts = parts[0].shape[0]
    block_elems = 256 * 1024
    if r % 8 == 0 or c % LANES != 0:
        tr, tc = _tile(r, max(8, block_elems // (-(-c // LANES) * LANES)), 8), c
    else:
        tr, tc = r, _tile(c, max(LANES, block_elems // r // LANES * LANES), LANES)
    nr, nc = r // tr, c // tc
    bias1 = 1.0 / (1.0 - ADAM_B1 ** ADAM_STEP)
    bias2 = 1.0 / (1.0 - ADAM_B2 ** ADAM_STEP)

    def body(*refs):
        p_refs = refs[:nl]
        w_ref, m_ref, v_ref, g_ref, d_ref, nm_ref, nv_ref = refs[nl:]

        def update(p_ref):
            g = p_ref[0].astype(F32)
            for s in range(1, n_parts):
                g = g + p_ref[s].astype(F32)
            m_new = ADAM_B1 * m_ref[...] + (1.0 - ADAM_B1) * g
            v_new = ADAM_B2 * v_ref[...] + (1.0 - ADAM_B2) * (g * g)
            g_ref[...] = g
            nm_ref[...] = m_new
            nv_ref[...] = v_new
            d_ref[...] = -ADAM_LR * ((m_new * bias1) / (jnp.sqrt(v_new * bias2) + ADAM_EPS) + ADAM_WD * w_ref[...])

        for ll in range(nl):
            @pl.when(pl.program_id(0) == ll)
            def _(ll=ll):
                update(p_refs[ll])

    def part_spec(ll):
        def index(l, i, j):
            pin = jnp.where(l < ll, 0, 1)
            return 0, jnp.where(l == ll, i, pin * (nr - 1)), jnp.where(l == ll, j, pin * (nc - 1))
        return pl.BlockSpec((n_parts, tr, tc), index)

    blk = pl.BlockSpec((None, tr, tc), lambda l, i, j: (l, i, j))
    out, got = _call(body, name=name, grid=(nl, nr, nc), in_specs=[part_spec(ll) for ll in range(nl)] + [blk, blk, blk],
                     out_specs=[blk] * 4, out_shape=[jax.ShapeDtypeStruct((nl, r, c), F32)] * 4,
                     operands=list(parts) + [w, m1, v2], comm=comm)
    return out if comm is None else (out, got)


SMALL_WEIGHTS = ("ffn1_norm", "mix_norm", "b_forget", "g_fox", "g_sb", "ffn2_norm", "final_norm")
WEIGHT_ORDER = ("meta_tokens", "ffn1_norm", "ffn1_w_gate", "ffn1_w_up", "ffn1_w_down", "mix_norm", "w_in", "b_forget",
                "g_fox", "g_sb", "w_out", "ffn2_norm", "ffn2_w_gate", "ffn2_w_up", "ffn2_w_down", "final_norm")
GROUPS = {"ffn1": ("ffn1_w_gate", "ffn1_w_up", "ffn1_w_down"), "mix": ("w_in", "w_out"),
          "ffn2": ("ffn2_w_gate", "ffn2_w_up", "ffn2_w_down")}
TRANSPOSED = ("ffn1_w_gate", "ffn1_w_up", "ffn2_w_gate", "ffn2_w_up")


def _pad_lanes(a):
    extra = (-a.shape[-1]) % LANES
    return a if extra == 0 else jnp.pad(a, [(0, 0)] * (a.ndim - 1) + [(0, extra)])


def _ffn_backward(dh_b, dh, saved, gain, wg, wu, wd, tag, carried=None):
    h, xn, g, u, act = saved
    (dg, du), _ = _ffn_bwd_act(dh_b, wd, g, u, f"{tag}_bwd_act")
    d_wd = _ffn_dw(act, dh_b, f"{tag}_dwd", alpha=0.5)
    d_wg = _ffn_dw(dg, xn, f"{tag}_dwg")
    d_wu = _ffn_dw(du, xn, f"{tag}_dwu")
    if carried:
        dxn, got = _ffn_contract([(dg, wg), (du, wu)], F32, f"{tag}_dxn", group=2, comm=_Exchange(carried))
    else:
        dxn, got = _ffn_contract([(dg, wg), (du, wu)], F32, f"{tag}_dxn", group=2), []
    dh_in, dh_in_b, d_gain = _rms_bwd(dxn, h, gain, dh, f"{tag}_norm_bwd")
    return dh_in, dh_in_b, d_gain, d_wg, d_wu, d_wd, got


def _ffn_backward_last(dh_b, dh, saved, gain, wg, wu, wd, tag, first, second):
    h, xn, g, u, act = saved
    (dg, du), got_first = _ffn_bwd_act(dh_b, wd, g, u, f"{tag}_bwd_act", _Exchange(first))
    d_wd, got_second = _ffn_dw(act, dh_b, f"{tag}_dwd", alpha=0.5, comm=_Exchange(second))
    half = d_wd.shape[1] // 2 // 16 * 16
    d_wg, got_wd_a = _ffn_dw(dg, xn, f"{tag}_dwg", comm=_Exchange([d_wd[:, :half]]))
    d_wu, got_wd_b = _ffn_dw(du, xn, f"{tag}_dwu", comm=_Exchange([d_wd[:, half:]]))
    dxn, got_wg = _ffn_contract([(dg, wg), (du, wu)], F32, f"{tag}_dxn", group=2, comm=_Exchange([d_wg]))
    dh_in, dh_in_b, d_gain = _rms_bwd(dxn, h, gain, dh, f"{tag}_norm_bwd")
    got_wd = jnp.concatenate([got_wd_a[0], got_wd_b[0]], axis=1)
    return dh_in, dh_in_b, d_gain, got_first, got_second, got_wg[0], got_wd, d_wu


def _mixer_forward(h, gain, w_in_t, bias, g_fox, g_sb, w_out, n_heads, pad, tag, comm_fox=None, comm_sb=None):
    m = h.shape[0]
    t = _attn_block(m)
    hw = n_heads * HEAD_DIM
    xn = _rms_fwd(h, gain, f"{tag}_norm")
    qkv = _mm([(xn, w_in_t[:6 * hw])], "nt", BF16, name=f"{tag}_qkv")
    fl = _mm([(xn, w_in_t[6 * hw:])], "nt", F32, name=f"{tag}_forget")
    c = _forget_cumsum(fl, bias, n_heads, pad, f"{tag}_cumsum")
    c_heads = c[:, :n_heads].T
    ccol = c_heads[:, :, None]
    crow = c_heads.reshape(n_heads, m // t, 1, t)
    (o_f, on_f, lse), got_fox = _fox_fwd(qkv, ccol, crow, g_fox, n_heads, pad, f"{tag}_fox", comm_fox)
    (o_s, on_s, runs), got_sb = _sb_fwd(qkv, g_sb, n_heads, pad, f"{tag}_sb", comm_sb)
    h_out = _mm([(on_f, w_out[:hw]), (on_s, w_out[hw:])], "nn", F32, name=f"{tag}_out", res=h)
    return h_out, (h, xn, qkv, fl, ccol, crow, o_f, on_f, lse, o_s, on_s, runs), got_fox, got_sb


def _mixer_backward(dh_b, dh_out, saved, gain, w_in_t, bias, g_fox, g_sb, w_out, n_heads, pad, tag, comm_fox=None, comm_sb=None,
                    rows_in_dxn=0):
    h, xn, qkv, fl, ccol, crow, o_f, on_f, lse, o_s, on_s, runs = saved
    m = h.shape[0]
    hw = n_heads * HEAD_DIM
    d_on_f = _mm([(dh_b, w_out[:hw])], "nt", F32, name=f"{tag}_don_f")
    d_on_s = _mm([(dh_b, w_out[hw:])], "nt", F32, name=f"{tag}_don_s")
    d_wout = jnp.concatenate([_mm([(on_f, dh_b)], "tn", BF16, name=f"{tag}_dwout_f"),
                              _mm([(on_s, dh_b)], "tn", BF16, name=f"{tag}_dwout_s")], axis=0)
    (dq_f, dk_f, dv_f, dg_fox, dccol, dcrow), got_fox = _fox_bwd(
        qkv, o_f, d_on_f, g_fox, lse, ccol, crow, n_heads, pad, f"{tag}_fox_bwd", comm_fox)
    (dq_s, dk_s, dv_s, dg_sb), got_sb = _sb_bwd(qkv, o_s, d_on_s, g_sb, runs, n_heads, pad, f"{tag}_sb_bwd", comm_sb)
    dc_a = _pad_lanes(dccol[:, :, 0].T)
    dc_b = _pad_lanes(dcrow.reshape(n_heads, m).T)
    dfl, dbias = _forget_cumsum_bwd(dc_a, dc_b, fl, bias, n_heads, pad, f"{tag}_cumsum_bwd")
    dproj = jnp.concatenate([dq_f, dk_f.astype(BF16), dv_f.astype(BF16), dq_s, dk_s.astype(BF16), dv_s.astype(BF16),
                             dfl.astype(BF16)], axis=1)
    d_win = _mm([(dproj, xn)], "tn", BF16, name=f"{tag}_dwin")[:6 * hw + n_heads].reshape(N_DEV, -1, h.shape[1])
    got_win = None
    if rows_in_dxn:
        dxn, (got_win,) = _mm([(dproj, w_in_t)], "nn", F32, name=f"{tag}_dxn", whole_k=True,
                              comm=_Exchange([d_win[:, :rows_in_dxn]]))
    else:
        dxn = _mm([(dproj, w_in_t)], "nn", F32, name=f"{tag}_dxn", whole_k=True)
    dh, dh_in_b, d_gain = _rms_bwd(dxn, h, gain, dh_out, f"{tag}_norm_bwd")
    return dh, dh_in_b, d_gain, d_win, dbias, dg_fox, dg_sb, d_wout, got_fox, got_sb, got_win


def kernel(x, meta_tokens, ffn1_norm, ffn1_w_gate, ffn1_w_up, ffn1_w_down, mix_norm, w_in, b_forget, g_fox, g_sb, w_out, ffn2_norm, ffn2_w_gate, ffn2_w_up, ffn2_w_down, final_norm, loss_target, m_meta_tokens, m_ffn1_norm, m_ffn1_w_gate, m_ffn1_w_up, m_ffn1_w_down, m_mix_norm, m_w_in, m_b_forget, m_g_fox, m_g_sb, m_w_out, m_ffn2_norm, m_ffn2_w_gate, m_ffn2_w_up, m_ffn2_w_down, m_final_norm, v_meta_tokens, v_ffn1_norm, v_ffn1_w_gate, v_ffn1_w_up, v_ffn1_w_down, v_mix_norm, v_w_in, v_b_forget, v_g_fox, v_g_sb, v_w_out, v_ffn2_norm, v_ffn2_w_gate, v_ffn2_w_up, v_ffn2_w_down, v_final_norm):
    given = dict(locals())
    seq, d = x.shape[1], x.shape[2]
    depth = ffn1_norm.shape[0]
    d_in = N_DEV * w_in.shape[2]
    n_heads = g_fox.shape[1] // HEAD_DIM
    hw = n_heads * HEAD_DIM
    assert seq % ROW_BLOCK == 0 and d_in == 6 * hw + n_heads and n_heads <= LANES
    pad = (-(seq + N_META)) % ROW_BLOCK
    x_off = pad + N_META

    def view(n, a):
        if n in TRANSPOSED:
            return jnp.swapaxes(a, 1, 2)
        return a.transpose(2, 0, 1) if n == "w_in" else a

    def unview(n, a):
        if n in TRANSPOSED:
            return jnp.swapaxes(a, 1, 2)
        return a.transpose(1, 2, 0) if n == "w_in" else a

    def shard(n, l):
        v = view(n, given[n])
        return (v[:, l] if n == "w_in" else v[l]).astype(BF16)

    def shards(group, l):
        return [shard(n, l) for n in GROUPS[group]]

    sh = shards("ffn1", 0)
    first = _run_alone(_Gather(sh[:2] + [meta_tokens]), "gather_first")
    full = {}
    meta_full = first[2].transpose(1, 0, 2).reshape(N_META, d)
    h = jnp.concatenate([jnp.zeros((pad, d), F32), meta_full, x[0]], axis=0)
    weights, saved = [], []
    for l in range(depth):
        xn = _rms_fwd(h, ffn1_norm[l:l + 1], "ffn1_norm")
        if l == 0:
            (g, u, act), (wd1,) = _ffn_up(xn, first[0], first[1], "ffn1_up", _Gather(sh[2:]))
            full[("ffn1", 0)] = (first[0], first[1], wd1)
            h_out, full[("mix", 0)] = _ffn_contract([(act, wd1)], F32, "ffn1_down", group=4, alpha=0.5, res=h,
                                                    comm=_Gather(shards("mix", 0)))
        else:
            wg1, wu1, wd1 = full[("ffn1", l)]
            (g, u, act), full[("mix", l)] = _ffn_up(xn, wg1, wu1, "ffn1_up", _Gather(shards("mix", l)))
            h_out = _ffn_contract([(act, wd1)], F32, "ffn1_down", group=4, alpha=0.5, res=h)
        s1 = (h, xn, g, u, act)
        h = h_out
        win3, wout3 = full[("mix", l)]
        w_in_t = jnp.pad(win3.reshape(d_in, d), ((0, 6 * hw + LANES - d_in), (0, 0)))
        w_out_full = wout3.reshape(N_DEV * wout3.shape[1], d)
        bias = _pad_lanes(b_forget[l:l + 1])
        nxt = _Gather(shards("ffn1", l + 1)) if l + 1 < depth else None
        h, sm, full[("ffn2", l)], got = _mixer_forward(
            h, mix_norm[l:l + 1], w_in_t, bias, g_fox[l:l + 1], g_sb[l:l + 1], w_out_full, n_heads, pad, "mix",
            _Gather(shards("ffn2", l)), nxt)
        if nxt is not None:
            full[("ffn1", l + 1)] = got
        wg2, wu2, wd2 = full[("ffn2", l)]
        xn = _rms_fwd(h, ffn2_norm[l:l + 1], "ffn2_norm")
        (g, u, act), _ = _ffn_up(xn, wg2, wu2, "ffn2_up")
        s2 = (h, xn, g, u, act)
        h = _ffn_contract([(act, wd2)], F32, "ffn2_down", group=4, alpha=0.5, res=h)
        weights.append((w_in_t, w_out_full, bias))
        saved.append((s1, sm, s2))

    dh, dh_b, d_final, loss_arr = _loss_head(h, final_norm[None, :], loss_target[0], x_off, "loss_head")
    small = {n: [None] * depth for n in SMALL_WEIGHTS[:-1]}
    partial, received = {}, {}

    def names(group, l):
        return [(n, l) for n in GROUPS[group]]

    def send(keys):
        return [partial[k] for k in keys]

    for l in reversed(range(depth)):
        w_in_t, w_out_full, bias = weights[l]
        s1, sm, s2 = saved[l]
        wg2, wu2, wd2 = full[("ffn2", l)]
        up = l + 1 < depth
        in_dxn = [("ffn1_w_gate", l + 1)] if up else []
        in_fox = names("mix", l + 1) + [("ffn1_w_up", l + 1)] if up else []
        in_sb = names("ffn2", l) + ([("ffn1_w_down", l + 1)] if up else [])
        dh, dh_b, small["ffn2_norm"][l], partial[("ffn2_w_gate", l)], partial[("ffn2_w_up", l)], partial[("ffn2_w_down", l)], got = (
            _ffn_backward(dh_b, dh, s2, ffn2_norm[l:l + 1], wg2, wu2, wd2, "ffn2", send(in_dxn)))
        received.update(zip(in_dxn, got))
        win_rows = (d_in // N_DEV // 2 // 16 * 16) if l == 0 else 0
        (dh, dh_b, small["mix_norm"][l], partial[("w_in", l)], dbias, small["g_fox"][l], small["g_sb"][l], d_wout,
         got_fox, got_sb, got_win) = _mixer_backward(
            dh_b, dh, sm, mix_norm[l:l + 1], w_in_t, bias, g_fox[l:l + 1], g_sb[l:l + 1], w_out_full, n_heads, pad, "mix",
            _Exchange(send(in_fox)) if in_fox else None, _Exchange(send(in_sb)), win_rows)
        received.update(zip(in_fox, got_fox or []))
        received.update(zip(in_sb, got_sb))
        partial[("w_out", l)] = d_wout.reshape(N_DEV, -1, d)
        small["b_forget"][l] = dbias[:, :n_heads]
        wg1, wu1, wd1 = full[("ffn1", l)]
        if l > 0:
            dh, dh_b, small["ffn1_norm"][l], partial[("ffn1_w_gate", l)], partial[("ffn1_w_up", l)], partial[("ffn1_w_down", l)], _ = (
                _ffn_backward(dh_b, dh, s1, ffn1_norm[l:l + 1], wg1, wu1, wd1, "ffn1"))
        else:
            (dh, dh_b, small["ffn1_norm"][0], got_a, got_b, received[("ffn1_w_gate", 0)], received[("ffn1_w_down", 0)],
             last_grad) = _ffn_backward_last(
                dh_b, dh, s1, ffn1_norm[0:1], wg1, wu1, wd1, "ffn1", [partial[("w_out", 0)]],
                [partial[("w_in", 0)][:, win_rows:]])
            received[("w_out", 0)] = got_a[0]
            received[("w_in", 0)] = jnp.concatenate([got_win, got_b[0]], axis=1)
    grad_x = dh[x_off:][None]
    d_meta = dh[pad:x_off].reshape(N_META, N_DEV, -1).transpose(1, 0, 2)
    received[("meta_tokens", 0)] = _run_alone(_Exchange([d_meta]), "exchange_meta")[0]

    vec = [loss_arr[0:1, :]] + [_pad_lanes(jnp.concatenate(small[n], axis=0).reshape(1, -1)) for n in SMALL_WEIGHTS[:-1]]
    vec.append(d_final)
    sizes = [a.shape[1] for a in vec]
    summed = _all_reduce_small(jnp.concatenate(vec, axis=1), "reduce_small")
    loss = summed[0, 0]

    def packed(prefix):
        cols = [jnp.zeros((1, LANES), F32)]
        cols += [_pad_lanes(given[prefix + n].reshape(1, -1)) for n in SMALL_WEIGHTS]
        return jnp.concatenate(cols, axis=1)[None]

    small_out = _adamw([summed[None]], packed(""), packed("m_"), packed("v_"), "adamw_small")

    quarter = last_grad.shape[1] // 4 // 16 * 16
    pieces = [last_grad[:, k * quarter:(k + 1) * quarter if k < 3 else None] for k in range(4)]
    order = ["ffn2_w_gate", "ffn2_w_up", "ffn2_w_down", "ffn1_w_gate", "ffn1_w_down", "w_in", "w_out", "ffn1_w_up"]
    out, arrived = {}, []
    for n in order:
        wv, mv, vv = (view(n, given[p + n]) for p in ("", "m_", "v_"))
        if n == "ffn1_w_up":
            received[(n, 0)] = jnp.concatenate(arrived, axis=1)
        if n == "w_in":
            per_layer = [_adamw([received[(n, l)]], wv[:, l][None], mv[:, l][None], vv[:, l][None], "adamw_" + n)
                         for l in range(depth)]
            res = [jnp.stack([per_layer[l][k][0] for l in range(depth)], axis=1) for k in range(4)]
        elif len(arrived) < len(pieces):
            res, got = _adamw([received[(n, l)] for l in range(depth)], wv, mv, vv, "adamw_" + n,
                              _Exchange([pieces[len(arrived)]]))
            arrived.append(got[0])
        else:
            res = _adamw([received[(n, l)] for l in range(depth)], wv, mv, vv, "adamw_" + n)
        out[n] = [unview(n, r) for r in res]
    out["meta_tokens"] = [r[0] for r in _adamw([received[("meta_tokens", 0)]], meta_tokens[None], m_meta_tokens[None],
                                               v_meta_tokens[None], "adamw_meta_tokens")]
    offset = sizes[0]
    for n, size in zip(SMALL_WEIGHTS, sizes[1:]):
        shape, count = given[n].shape, given[n].size
        out[n] = [r[0, 0, offset:offset + count].reshape(shape) for r in small_out]
        offset += size

    result = [loss, grad_x]
    for k in range(4):
        result += [out[n][k] for n in WEIGHT_ORDER]
    return tuple(result)
```

```python
import math

import jax
import jax.numpy as jnp
from jax import lax
from jax.experimental import pallas as pl
from jax.experimental.pallas import tpu as pltpu

F32 = jnp.float32
BF16 = jnp.bfloat16

N_DEV = 8
N_META = 16
HEAD_DIM = 128
ROW_BLOCK = 128
LANES = 128
EPS = 1e-6
NEG = -1e30
ADAM_LR = 0.001
ADAM_B1 = 0.9
ADAM_B2 = 0.999
ADAM_EPS = 1e-08
ADAM_WD = 0.01
ADAM_STEP = 10
VMEM_LIMIT_BYTES = 56 * 1024 * 1024
MESH = pl.DeviceIdType.MESH

NT_DIMS = (((1,), (1,)), ((), ()))
TN_DIMS = (((0,), (0,)), ((), ()))
NN_DIMS = (((1,), (0,)), ((), ()))
ANY = pl.BlockSpec(memory_space=pl.ANY)


def _tile(n, cap, align):
    best = None
    for d in range(align, min(n, cap) + 1, align):
        if n % d == 0:
            best = d
    return best if best is not None else n


def _dot(a, b, dims=NN_DIMS):
    return lax.dot_general(a, b, dims, preferred_element_type=F32)


def _dot_split(x, u):
    hi = x.astype(BF16)
    lo = (x - hi.astype(F32)).astype(BF16)
    return _dot(hi, u) + _dot(lo, u)


def _my_position():
    return lax.axis_index("x"), lax.axis_index("y"), lax.axis_index("c")


class _Gather:
    n_phases = 3

    def __init__(self, arrs):
        self.arrs = list(arrs)
        n = len(self.arrs)
        self.out_shapes = [jax.ShapeDtypeStruct((N_DEV,) + a.shape, a.dtype) for a in self.arrs]
        self.scratch = [pltpu.SemaphoreType.DMA((n, 7)), pltpu.SemaphoreType.DMA((n, 7)), pltpu.SemaphoreType.DMA((n,))]

    def phase(self, p, ins, outs, sems):
        send_sems, recv_sems, local_sems = sems
        n = len(self.arrs)
        x, y, c = _my_position()
        me, sibling = (x, y, c), (x, y, 1 - c)
        chips = [(1 - x, y), (x, 1 - y), (1 - x, 1 - y)]

        def copy(a, k, block, to, src=None):
            slot = outs[a].at[4 * block[0] + 2 * block[1] + block[2]]
            return pltpu.make_async_remote_copy(
                src_ref=slot if src is None else src, dst_ref=slot,
                send_sem=send_sems.at[a, k], recv_sem=recv_sems.at[a, k], device_id=to, device_id_type=MESH)

        def local(a):
            return pltpu.make_async_copy(ins[a], outs[a].at[4 * x + 2 * y + c], local_sems.at[a])

        def first(a):
            return [copy(a, 0, me, sibling, src=ins[a])] + [copy(a, 1 + j, me, (*chip, c), src=ins[a]) for j, chip in enumerate(chips)]

        def passed(a, j):
            return copy(a, 4 + j, (*chips[j], c), sibling)

        if p == 0:
            for a in range(n):
                local(a).start()
            for a in range(n):
                for cp in first(a):
                    cp.start()
        elif p == 1:
            for a in range(n):
                for j, chip in enumerate(chips):
                    copy(a, 1 + j, (*chip, c), me).wait_recv()
                    passed(a, j).start()
        else:
            for a in range(n):
                copy(a, 0, sibling, me).wait_recv()
                for j, chip in enumerate(chips):
                    copy(a, 4 + j, (*chip, 1 - c), me).wait_recv()
            for a in range(n):
                for cp in first(a) + [passed(a, j) for j in range(3)]:
                    cp.wait_send()
                local(a).wait()


class _Exchange:
    n_phases = 2

    def __init__(self, arrs, rows=None, into=()):
        self.arrs = list(arrs)
        self.rows = rows
        self.into = list(into)
        n = len(self.arrs)
        self.out_shapes = [jax.ShapeDtypeStruct(a.shape, a.dtype) for a in self.arrs]
        self.scratch = [pltpu.SemaphoreType.DMA((n, 7)), pltpu.SemaphoreType.DMA((n, 7)), pltpu.SemaphoreType.DMA((n,))]

    def phase(self, p, ins, outs, sems):
        send_sems, recv_sems, local_sems = sems
        n = len(self.arrs)
        x, y, c = _my_position()
        me = 4 * x + 2 * y + c

        def peer_of(r):
            return (x ^ ((r >> 2) & 1), y ^ ((r >> 1) & 1), c ^ (r & 1))

        def part(ref, d):
            return ref.at[d] if self.rows is None else ref.at[d, pl.ds(self.rows[0], self.rows[1])]

        def copy(a, r):
            px, py, pc = peer_of(r)
            return pltpu.make_async_remote_copy(
                src_ref=part(ins[a], 4 * px + 2 * py + pc), dst_ref=part(outs[a], me),
                send_sem=send_sems.at[a, r - 1], recv_sem=recv_sems.at[a, r - 1],
                device_id=(px, py, pc), device_id_type=MESH)

        def arrival(a, r):
            px, py, pc = peer_of(r)
            slot = part(outs[a], 4 * px + 2 * py + pc)
            return pltpu.make_async_remote_copy(
                src_ref=slot, dst_ref=slot, send_sem=send_sems.at[a, r - 1], recv_sem=recv_sems.at[a, r - 1],
                device_id=(px, py, pc), device_id_type=MESH)

        def local(a):
            return pltpu.make_async_copy(part(ins[a], me), part(outs[a], me), local_sems.at[a])

        if p == 0:
            for a in range(n):
                local(a).start()
            for a in range(n):
                for r in range(1, N_DEV):
                    copy(a, r).start()
        else:
            for a in range(n):
                for r in range(1, N_DEV):
                    arrival(a, r).wait_recv()
            for a in range(n):
                for r in range(1, N_DEV):
                    copy(a, r).wait_send()
                local(a).wait()


def _run_alone(comm, name):
    n = len(comm.arrs)

    def body(*refs):
        for p in range(comm.n_phases):
            comm.phase(p, refs[:n], refs[n:2 * n], refs[2 * n:])

    return pl.pallas_call(body, name=name, out_shape=comm.out_shapes, in_specs=[ANY] * n, out_specs=[ANY] * n,
                          scratch_shapes=comm.scratch)(*comm.arrs)


def _call(body, *, name, grid, in_specs, out_specs, out_shape, operands, scratch_shapes=(), comm=None):
    scratch_shapes = list(scratch_shapes)
    params = pltpu.CompilerParams(dimension_semantics=("arbitrary",) * len(grid), vmem_limit_bytes=VMEM_LIMIT_BYTES)
    if comm is None:
        res = pl.pallas_call(body, name=name, grid=grid, in_specs=in_specs, out_specs=out_specs, out_shape=out_shape,
                             scratch_shapes=scratch_shapes, compiler_params=params)(*operands)
        return res, None
    n_in, n_out, n_sc = len(in_specs), len(out_specs), len(scratch_shapes)
    nc = len(comm.arrs)
    into = getattr(comm, "into", [])
    total = math.prod(grid)
    at = {0: 0, comm.n_phases - 1: total - 1}
    for p in range(1, comm.n_phases - 1):
        at[p] = (total * 7) // 8

    def wrapped(*refs):
        ins, cins = refs[:n_in], refs[n_in:n_in + nc]
        refs = refs[n_in + nc + len(into):]
        outs, couts = refs[:n_out], refs[n_out:n_out + nc]
        rest = refs[n_out + nc:]
        scratch, sems = rest[:n_sc], rest[n_sc:]
        step = 0
        for axis, size in enumerate(grid):
            step = step * size + pl.program_id(axis)
        for p in range(comm.n_phases - 1):
            @pl.when(step == at[p])
            def _(p=p):
                comm.phase(p, cins, couts, sems)
        body(*ins, *outs, *scratch)

        @pl.when(step == total - 1)
        def _():
            comm.phase(comm.n_phases - 1, cins, couts, sems)

    res = pl.pallas_call(
        wrapped, name=name, grid=grid, in_specs=list(in_specs) + [ANY] * (nc + len(into)),
        out_specs=list(out_specs) + [ANY] * nc, out_shape=list(out_shape) + comm.out_shapes,
        scratch_shapes=scratch_shapes + comm.scratch,
        input_output_aliases={n_in + nc + k: n_out + k for k in range(len(into))},
        compiler_params=params)(*operands, *comm.arrs, *into)
    return res[:n_out], res[n_out:]


def _all_reduce_small(vec, name):
    n = vec.shape[1]

    def body(v_ref, o_ref, buf, send_sems, recv_sems):
        x, y, c = _my_position()
        me = 4 * x + 2 * y + c

        def peer_of(r):
            return (x ^ ((r >> 2) & 1), y ^ ((r >> 1) & 1), c ^ (r & 1))

        def copy(r):
            px, py, pc = peer_of(r)
            return pltpu.make_async_remote_copy(
                src_ref=v_ref, dst_ref=buf.at[me], send_sem=send_sems.at[r - 1], recv_sem=recv_sems.at[r - 1],
                device_id=(px, py, pc), device_id_type=MESH)

        def arrival(r):
            px, py, pc = peer_of(r)
            slot = buf.at[4 * px + 2 * py + pc]
            return pltpu.make_async_remote_copy(
                src_ref=slot, dst_ref=slot, send_sem=send_sems.at[r - 1], recv_sem=recv_sems.at[r - 1],
                device_id=(px, py, pc), device_id_type=MESH)

        sends = [copy(r) for r in range(1, N_DEV)]
        for cp in sends:
            cp.start()
        buf[me] = v_ref[...]
        for r in range(1, N_DEV):
            arrival(r).wait_recv()
        for cp in sends:
            cp.wait_send()
        total = buf[0]
        for d in range(1, N_DEV):
            total = total + buf[d]
        o_ref[...] = total

    vmem = pl.BlockSpec(memory_space=pltpu.VMEM)
    return pl.pallas_call(
        body, name=name, out_shape=jax.ShapeDtypeStruct((1, n), F32), in_specs=[vmem], out_specs=vmem,
        scratch_shapes=[pltpu.VMEM((N_DEV, 1, n), F32), pltpu.SemaphoreType.DMA((7,)), pltpu.SemaphoreType.DMA((7,))],
    )(vec)


def _mm_core(pairs, dims, out_dtype, *, name, grid, out_shape, out_spec, acc_shape, alpha=1.0, res=None, comm=None):
    nk = grid[2]
    npairs = len(pairs)

    def body(*refs):
        ab = refs[:2 * npairs]
        rest = refs[2 * npairs:]
        res_ref = rest[0] if res is not None else None
        o_ref = rest[1] if res is not None else rest[0]
        acc_ref = rest[-1] if nk > 1 else None
        part = None
        for p in range(npairs):
            a_ref, b_ref = ab[2 * p], ab[2 * p + 1]
            shards = [(a_ref[s], b_ref[s]) for s in range(a_ref.shape[0])] if len(a_ref.shape) == 3 else [(a_ref[...], b_ref[...])]
            for av, bv in shards:
                d = _dot(av.astype(BF16), bv.astype(BF16), dims)
                part = d if part is None else part + d

        def finish(total):
            val = total * alpha if alpha != 1.0 else total
            if res_ref is not None:
                val = res_ref[...] + val
            o_ref[...] = val.astype(out_dtype)

        if nk == 1:
            finish(part)
        else:
            kk = pl.program_id(2)

            @pl.when(kk == 0)
            def _():
                acc_ref[...] = part

            @pl.when(kk > 0)
            def _():
                acc_ref[...] += part

            @pl.when(kk == nk - 1)
            def _():
                finish(acc_ref[...])

    operands, in_specs = [], []
    for (a, a_spec), (b, b_spec) in pairs:
        operands += [a, b]
        in_specs += [a_spec, b_spec]
    if res is not None:
        operands.append(res[0])
        in_specs.append(res[1])
    out, got = _call(body, name=name, grid=grid, in_specs=in_specs, out_specs=[out_spec],
                     out_shape=[jax.ShapeDtypeStruct(out_shape, out_dtype)], operands=operands,
                     scratch_shapes=[pltpu.VMEM(acc_shape, F32)] if nk > 1 else [], comm=comm)
    return out[0] if comm is None else (out[0], got)


def _mm(pairs, mode, out_dtype, *, name, alpha=1.0, res=None, comm=None, whole_k=False):
    a0, b0 = pairs[0]
    if mode == "nn":
        (m, k), n = a0.shape, b0.shape[1]
    elif mode == "nt":
        (m, k), n = a0.shape, b0.shape[0]
    else:
        (k, m), n = a0.shape, b0.shape[1]
    dims = {"nn": NN_DIMS, "nt": NT_DIMS, "tn": TN_DIMS}[mode]
    tm = _tile(m, 528 if whole_k else 1056, LANES if mode == "tn" else 16)
    tn = _tile(n, 512 if whole_k else 1024, LANES)
    tk = k if whole_k else _tile(k, 2048 if mode != "tn" else 2112, LANES if mode != "tn" else 16)
    a_spec = pl.BlockSpec((tk, tm), lambda i, j, kk: (kk, i)) if mode == "tn" else pl.BlockSpec((tm, tk), lambda i, j, kk: (i, kk))
    b_spec = pl.BlockSpec((tn, tk), lambda i, j, kk: (j, kk)) if mode == "nt" else pl.BlockSpec((tk, tn), lambda i, j, kk: (kk, j))
    o_spec = pl.BlockSpec((tm, tn), lambda i, j, kk: (i, j))
    return _mm_core([((a, a_spec), (b, b_spec)) for a, b in pairs], dims, out_dtype, name=name,
                    grid=(m // tm, n // tn, k // tk), out_shape=(m, n), out_spec=o_spec, acc_shape=(tm, tn),
                    alpha=alpha, res=None if res is None else (res, o_spec), comm=comm)


def _rms_fwd(h, gain, name):
    m, d = h.shape
    tm = _tile(m, 528, 16)

    def body(h_ref, g_ref, o_ref):
        hv = h_ref[...]
        r = lax.rsqrt(jnp.mean(hv * hv, axis=-1, keepdims=True) + EPS)
        o_ref[...] = (hv * r * g_ref[...]).astype(BF16)

    row = pl.BlockSpec((tm, d), lambda i: (i, 0))
    out, _ = _call(body, name=name, grid=(m // tm,), in_specs=[row, pl.BlockSpec((1, d), lambda i: (0, 0))],
                   out_specs=[row], out_shape=[jax.ShapeDtypeStruct((m, d), BF16)], operands=[h, gain])
    return out[0]


def _rms_bwd(dxn, h, gain, dres, name):
    m, d = h.shape
    tm = _tile(m, 264, 16)

    def body(dxn_ref, h_ref, g_ref, dres_ref, dh_ref, dhb_ref, dg_ref):
        hv = h_ref[...]
        r = lax.rsqrt(jnp.mean(hv * hv, axis=-1, keepdims=True) + EPS)
        xhat = hv * r
        dxn_v = dxn_ref[...]
        t = dxn_v * g_ref[...]
        dh = dres_ref[...] + r * (t - xhat * jnp.mean(t * xhat, axis=-1, keepdims=True))
        dh_ref[...] = dh
        dhb_ref[...] = dh.astype(BF16)
        part = jnp.sum(dxn_v * xhat, axis=0, keepdims=True)

        @pl.when(pl.program_id(0) == 0)
        def _():
            dg_ref[...] = part

        @pl.when(pl.program_id(0) > 0)
        def _():
            dg_ref[...] += part

    row = pl.BlockSpec((tm, d), lambda i: (i, 0))
    vec = pl.BlockSpec((1, d), lambda i: (0, 0))
    out, _ = _call(body, name=name, grid=(m // tm,), in_specs=[row, row, vec, row], out_specs=[row, row, vec],
                   out_shape=[jax.ShapeDtypeStruct((m, d), F32), jax.ShapeDtypeStruct((m, d), BF16),
                              jax.ShapeDtypeStruct((1, d), F32)],
                   operands=[dxn, h, gain, dres])
    return out


def _loss_head(h, gain, target, x_off, name):
    m, d = h.shape
    tm = ROW_BLOCK
    first = x_off // tm

    def body(h_ref, g_ref, t_ref, dh_ref, dhb_ref, dg_ref, loss_ref):
        i = pl.program_id(0)

        @pl.when(i == 0)
        def _():
            dg_ref[...] = jnp.zeros_like(dg_ref)
            loss_ref[...] = jnp.zeros_like(loss_ref)

        @pl.when(i < first)
        def _():
            dh_ref[...] = jnp.zeros_like(dh_ref)
            dhb_ref[...] = jnp.zeros_like(dhb_ref)

        @pl.when(i >= first)
        def _():
            hv = h_ref[...]
            g = g_ref[...]
            r = lax.rsqrt(jnp.mean(hv * hv, axis=-1, keepdims=True) + EPS)
            xhat = hv * r
            err = xhat * g - t_ref[...]
            loss_ref[...] += 0.5 * jnp.sum(jnp.mean(err * err, axis=-1, keepdims=True))
            dy = err * (1.0 / d)
            t = dy * g
            dh = r * (t - xhat * jnp.mean(t * xhat, axis=-1, keepdims=True))
            dh_ref[...] = dh
            dhb_ref[...] = dh.astype(BF16)
            dg_ref[...] += jnp.sum(dy * xhat, axis=0, keepdims=True)

    row = pl.BlockSpec((tm, d), lambda i: (i, 0))
    vec = pl.BlockSpec((1, d), lambda i: (0, 0))
    out, _ = _call(body, name=name, grid=(m // tm,),
                   in_specs=[row, vec, pl.BlockSpec((tm, d), lambda i: (jnp.maximum(i - first, 0), 0))],
                   out_specs=[row, row, vec, pl.BlockSpec((8, LANES), lambda i: (0, 0))],
                   out_shape=[jax.ShapeDtypeStruct((m, d), F32), jax.ShapeDtypeStruct((m, d), BF16),
                              jax.ShapeDtypeStruct((1, d), F32), jax.ShapeDtypeStruct((8, LANES), F32)],
                   operands=[h, gain, target])
    return out


def _sigmoid(z):
    return 1.0 / (1.0 + jnp.exp(-z))


def _ffn_up(xn, wg, wu, name, comm=None):
    m, d = xn.shape
    nsh, c, _ = wg.shape
    tm = _tile(m, 1056, 16)

    def body(x_ref, wg_ref, wu_ref, g_ref, u_ref, a_ref):
        xv = x_ref[...]
        g = _dot(xv, wg_ref[...], NT_DIMS)
        u = _dot(xv, wu_ref[...], NT_DIMS)
        g_ref[...] = g.astype(BF16)
        u_ref[...] = u.astype(BF16)
        a_ref[...] = (g * _sigmoid(g) * u).astype(BF16)

    out = pl.BlockSpec((None, tm, c), lambda i, j: (j, i, 0))
    w = pl.BlockSpec((None, c, d), lambda i, j: (j, 0, 0))
    return _call(body, name=name, grid=(m // tm, nsh), in_specs=[pl.BlockSpec((tm, d), lambda i, j: (i, 0)), w, w],
                 out_specs=[out, out, out], out_shape=[jax.ShapeDtypeStruct((nsh, m, c), BF16)] * 3,
                 operands=[xn, wg, wu], comm=comm)


def _ffn_contract(pairs, out_dtype, name, *, group, alpha=1.0, res=None, comm=None):
    nsh, m, c = pairs[0][0].shape
    d = pairs[0][1].shape[2]
    tm, tn = _tile(m, 1056, 16), _tile(d, 512, LANES)
    a_spec = pl.BlockSpec((group, tm, c), lambda i, j, kk: (kk, i, 0))
    b_spec = pl.BlockSpec((group, c, tn), lambda i, j, kk: (kk, 0, j))
    o_spec = pl.BlockSpec((tm, tn), lambda i, j, kk: (i, j))
    return _mm_core([((a, a_spec), (b, b_spec)) for a, b in pairs], NN_DIMS, out_dtype, name=name,
                    grid=(m // tm, d // tn, nsh // group), out_shape=(m, d), out_spec=o_spec, acc_shape=(tm, tn),
                    alpha=alpha, res=None if res is None else (res, o_spec), comm=comm)


def _ffn_bwd_act(dh, wd, g, u, name, comm=None):
    m, d = dh.shape
    nsh, c, _ = wd.shape
    tm = _tile(m, 1056, 16)

    def body(dh_ref, wd_ref, g_ref, u_ref, dg_ref, du_ref):
        dact = 0.5 * _dot(dh_ref[...], wd_ref[...], NT_DIMS)
        gv = g_ref[...].astype(F32)
        uv = u_ref[...].astype(F32)
        sig = _sigmoid(gv)
        du_ref[...] = (dact * gv * sig).astype(BF16)
        dg_ref[...] = (dact * uv * sig * (1.0 + gv * (1.0 - sig))).astype(BF16)

    blk = pl.BlockSpec((None, tm, c), lambda i, j: (j, i, 0))
    return _call(body, name=name, grid=(m // tm, nsh),
                 in_specs=[pl.BlockSpec((tm, d), lambda i, j: (i, 0)), pl.BlockSpec((None, c, d), lambda i, j: (j, 0, 0)), blk, blk],
                 out_specs=[blk, blk], out_shape=[jax.ShapeDtypeStruct((nsh, m, c), BF16)] * 2, operands=[dh, wd, g, u],
                 comm=comm)


def _ffn_dw(z, x, name, alpha=1.0, comm=None):
    nsh, m, c = z.shape
    d = x.shape[1]
    tn, tk = _tile(d, 1024, LANES), _tile(m, 2112, 16)
    return _mm_core([((z, pl.BlockSpec((None, tk, c), lambda i, j, kk: (i, kk, 0))),
                      (x, pl.BlockSpec((tk, tn), lambda i, j, kk: (kk, j))))],
                    TN_DIMS, BF16, name=name, grid=(nsh, d // tn, m // tk), out_shape=(nsh, c, d),
                    out_spec=pl.BlockSpec((None, c, tn), lambda i, j, kk: (i, 0, j)), acc_shape=(c, tn), alpha=alpha,
                    comm=comm)


def _dot3(tri, x):
    h1 = x.astype(BF16)
    r1 = x - h1.astype(F32)
    h2 = r1.astype(BF16)
    h3 = (r1 - h2.astype(F32)).astype(BF16)
    return _dot(tri, h1) + _dot(tri, h2) + _dot(tri, h3)


def _log_sigmoid(z):
    return jnp.minimum(z, 0.0) - jnp.log(1.0 + jnp.exp(-jnp.abs(z)))


def _triangle(t, cmp):
    return cmp(lax.broadcasted_iota(jnp.int32, (t, t), 0), lax.broadcasted_iota(jnp.int32, (t, t), 1)).astype(BF16)


def _forget_cumsum(fl, bias, n_heads, pad, name):
    m = fl.shape[0]
    nb = m // ROW_BLOCK

    def body(fl_ref, b_ref, c_ref):
        tri = _triangle(ROW_BLOCK, lambda r, c: r >= c)
        lane_ok = lax.broadcasted_iota(jnp.int32, (ROW_BLOCK, LANES), 1) < n_heads
        rows = lax.broadcasted_iota(jnp.int32, (ROW_BLOCK, LANES), 0)

        def step(b, carry):
            off = pl.multiple_of(b * ROW_BLOCK, ROW_BLOCK)
            lf = _log_sigmoid(fl_ref[pl.ds(off, ROW_BLOCK), :] + b_ref[...])
            lf = jnp.where(lane_ok & (rows + off >= pad), lf, 0.0)
            cs = _dot3(tri, lf) + carry
            c_ref[pl.ds(off, ROW_BLOCK), :] = cs
            return cs[ROW_BLOCK - 1:ROW_BLOCK, :]

        lax.fori_loop(0, nb, step, jnp.zeros((1, LANES), F32))

    vmem = pl.BlockSpec(memory_space=pltpu.VMEM)
    return pl.pallas_call(
        body, name=name, out_shape=jax.ShapeDtypeStruct((m, LANES), F32), in_specs=[vmem, vmem], out_specs=vmem,
        compiler_params=pltpu.CompilerParams(vmem_limit_bytes=VMEM_LIMIT_BYTES),
    )(fl, bias)


def _forget_cumsum_bwd(dc_a, dc_b, fl, bias, n_heads, pad, name):
    m = fl.shape[0]
    nb = m // ROW_BLOCK

    def body(da_ref, db_ref, fl_ref, b_ref, dfl_ref, dbias_ref):
        tri = _triangle(ROW_BLOCK, lambda r, c: r <= c)
        lane_ok = lax.broadcasted_iota(jnp.int32, (ROW_BLOCK, LANES), 1) < n_heads
        rows = lax.broadcasted_iota(jnp.int32, (ROW_BLOCK, LANES), 0)

        def step(bb, carry):
            tail, dbias = carry
            off = pl.multiple_of((nb - 1 - bb) * ROW_BLOCK, ROW_BLOCK)
            dc = da_ref[pl.ds(off, ROW_BLOCK), :] + db_ref[pl.ds(off, ROW_BLOCK), :]
            dlf = _dot3(tri, dc) + tail
            z = fl_ref[pl.ds(off, ROW_BLOCK), :] + b_ref[...]
            dfl = jnp.where(lane_ok & (rows + off >= pad), dlf * _sigmoid(-z), 0.0)
            dfl_ref[pl.ds(off, ROW_BLOCK), :] = dfl
            return dlf[0:1, :], dbias + jnp.sum(dfl, axis=0, keepdims=True)

        zero = jnp.zeros((1, LANES), F32)
        _, dbias = lax.fori_loop(0, nb, step, (zero, zero))
        dbias_ref[...] = dbias

    vmem = pl.BlockSpec(memory_space=pltpu.VMEM)
    return pl.pallas_call(
        body, name=name,
        out_shape=[jax.ShapeDtypeStruct((m, LANES), F32), jax.ShapeDtypeStruct((1, LANES), F32)],
        in_specs=[vmem] * 4, out_specs=[vmem, vmem],
        compiler_params=pltpu.CompilerParams(vmem_limit_bytes=VMEM_LIMIT_BYTES),
    )(dc_a, dc_b, fl, bias)


def _attn_block(m):
    return 3 * ROW_BLOCK if m % (3 * ROW_BLOCK) == 0 else ROW_BLOCK


def _block(ref, j, t):
    return ref[pl.ds(pl.multiple_of(j * t, t), t), :]


def _head_norm(o, gain):
    r = lax.rsqrt(jnp.mean(o * o, axis=-1, keepdims=True) + EPS)
    return o * r * gain


def _head_norm_bwd(o, d_on, gain):
    r = lax.rsqrt(jnp.mean(o * o, axis=-1, keepdims=True) + EPS)
    ohat = o * r
    t = d_on * gain
    d_o = r * (t - ohat * jnp.mean(t * ohat, axis=-1, keepdims=True))
    return d_o, jnp.sum(d_on * ohat, axis=0, keepdims=True)


def _qkv_specs(t, m, h, first_col_block):
    q = pl.BlockSpec((t, HEAD_DIM), lambda hd, i: (i, first_col_block + hd))
    k = pl.BlockSpec((m, HEAD_DIM), lambda hd, i: (0, first_col_block + h + hd))
    v = pl.BlockSpec((m, HEAD_DIM), lambda hd, i: (0, first_col_block + 2 * h + hd))
    return q, k, v


def _fox_fwd(qkv, ccol, crow, gain, n_heads, pad, name, comm=None):
    m = qkv.shape[0]
    t = _attn_block(m)
    nq = m // t
    scale = HEAD_DIM ** -0.5
    hw = n_heads * HEAD_DIM

    def body(q_ref, k_ref, v_ref, ccol_ref, crow_ref, g_ref, o_ref, on_ref, lse_ref):
        i = pl.program_id(1)
        q = q_ref[...]
        ci = ccol_ref[...]
        qpos = i * t + lax.broadcasted_iota(jnp.int32, (t, 1), 0)

        def step(j, j_prev, j_next, state, masked):
            mx, l, acc_scaled, p_prev, s_cur = state
            acc = acc_scaled + _dot(p_prev, _block(v_ref, j_prev, t))
            s_next = _dot(q, _block(k_ref, j_next, t), NT_DIMS)
            s = s_cur * scale + ci - crow_ref[j]
            if masked:
                kpos = j * t + lax.broadcasted_iota(jnp.int32, (1, t), 1)
                s = jnp.where((kpos <= qpos) & (kpos >= pad), s, NEG)
            mx_new = jnp.maximum(mx, jnp.max(s, axis=-1, keepdims=True))
            p = jnp.exp(s - mx_new)
            a = jnp.exp(mx - mx_new)
            return mx_new, a * l + jnp.sum(p, axis=-1, keepdims=True), a * acc, p.astype(BF16), s_next

        state = (jnp.full((t, 1), NEG, F32), jnp.zeros((t, 1), F32), jnp.zeros((t, HEAD_DIM), F32),
                 jnp.zeros((t, t), BF16), _dot(q, _block(k_ref, 0, t), NT_DIMS))
        state = step(0, 0, jnp.minimum(i, 1), state, True)
        state = lax.fori_loop(1, i, lambda j, st: step(j, j - 1, j + 1, st, False), state)
        mx, l, acc_scaled, p_last, _ = lax.fori_loop(0, jnp.minimum(i, 1), lambda _, st: step(i, i - 1, i, st, True), state)
        acc = acc_scaled + _dot(p_last, _block(v_ref, i, t))
        valid = qpos >= pad
        o = jnp.where(valid, acc / l, 0.0)
        o_ref[...] = o
        on_ref[...] = _head_norm(o, g_ref[...]).astype(BF16)
        lse_ref[...] = jnp.where(valid, mx + jnp.log(l), 0.0)

    q_spec, k_spec, v_spec = _qkv_specs(t, m, n_heads, 0)
    col = pl.BlockSpec((None, t, 1), lambda hd, i: (hd, i, 0))
    head = pl.BlockSpec((t, HEAD_DIM), lambda hd, i: (i, hd))
    return _call(body, name=name, grid=(n_heads, nq),
                 in_specs=[q_spec, k_spec, v_spec, col, pl.BlockSpec((None, nq, 1, t), lambda hd, i: (hd, 0, 0, 0)),
                           pl.BlockSpec((1, HEAD_DIM), lambda hd, i: (0, hd))],
                 out_specs=[head, head, col],
                 out_shape=[jax.ShapeDtypeStruct((m, hw), F32), jax.ShapeDtypeStruct((m, hw), BF16),
                            jax.ShapeDtypeStruct((n_heads, m, 1), F32)],
                 operands=[qkv, qkv, qkv, ccol, crow, gain], comm=comm)


def _fox_bwd(qkv, o, d_on, gain, lse, ccol, crow, n_heads, pad, name, comm=None):
    m = qkv.shape[0]
    t = _attn_block(m)
    nq = m // t
    scale = HEAD_DIM ** -0.5
    hw = n_heads * HEAD_DIM

    def body(q_ref, k_ref, v_ref, o_ref, don_ref, g_ref, lse_ref, ccol_ref, crow_ref,
             dq_ref, dk_ref, dv_ref, dg_ref, dccol_ref, dcrow_ref):
        i = pl.program_id(1)

        @pl.when(i == 0)
        def _():
            dk_ref[...] = jnp.zeros_like(dk_ref)
            dv_ref[...] = jnp.zeros_like(dv_ref)
            dg_ref[...] = jnp.zeros_like(dg_ref)
            dcrow_ref[...] = jnp.zeros_like(dcrow_ref)

        q = q_ref[...]
        o = o_ref[...]
        d_o, dgain = _head_norm_bwd(o, don_ref[...], g_ref[...])
        dg_ref[...] += dgain
        delta = jnp.sum(d_o * o, axis=-1, keepdims=True)
        d_ob = d_o.astype(BF16)
        ci = ccol_ref[...]
        lse_i = lse_ref[...]
        qpos = i * t + lax.broadcasted_iota(jnp.int32, (t, 1), 0)

        def step(j, carry, masked):
            dq, dci = carry
            k = _block(k_ref, j, t)
            v = _block(v_ref, j, t)
            off = pl.multiple_of(j * t, t)
            s = _dot(q, k, NT_DIMS) * scale + ci - crow_ref[j]
            if masked:
                kpos = off + lax.broadcasted_iota(jnp.int32, (1, t), 1)
                ok = (kpos <= qpos) & (kpos >= pad)
                p = jnp.where(ok, jnp.exp(jnp.where(ok, s - lse_i, 0.0)), 0.0)
            else:
                p = jnp.exp(s - lse_i)
            ds = p * (_dot(d_ob, v, NT_DIMS) - delta)
            dsb = ds.astype(BF16)
            dk_ref[pl.ds(off, t), :] += _dot(dsb, q, TN_DIMS) * scale
            dv_ref[pl.ds(off, t), :] += _dot(p.astype(BF16), d_ob, TN_DIMS)
            dcrow_ref[j] -= jnp.sum(ds, axis=0, keepdims=True)
            return dq + _dot(dsb, k), dci + jnp.sum(ds, axis=-1, keepdims=True)

        carry = step(0, (jnp.zeros((t, HEAD_DIM), F32), jnp.zeros((t, 1), F32)), True)
        carry = lax.fori_loop(1, i, lambda j, c: step(j, c, False), carry)
        dq, dci = lax.fori_loop(0, jnp.minimum(i, 1), lambda _, c: step(i, c, True), carry)
        dq_ref[...] = (dq * scale).astype(BF16)
        dccol_ref[...] = dci

    q_spec, k_spec, v_spec = _qkv_specs(t, m, n_heads, 0)
    col = pl.BlockSpec((None, t, 1), lambda hd, i: (hd, i, 0))
    rowc = pl.BlockSpec((None, nq, 1, t), lambda hd, i: (hd, 0, 0, 0))
    head = pl.BlockSpec((t, HEAD_DIM), lambda hd, i: (i, hd))
    whole = pl.BlockSpec((m, HEAD_DIM), lambda hd, i: (0, hd))
    gvec = pl.BlockSpec((1, HEAD_DIM), lambda hd, i: (0, hd))
    return _call(body, name=name, grid=(n_heads, nq),
                 in_specs=[q_spec, k_spec, v_spec, head, head, gvec, col, col, rowc],
                 out_specs=[head, whole, whole, gvec, col, rowc],
                 out_shape=[jax.ShapeDtypeStruct((m, hw), BF16), jax.ShapeDtypeStruct((m, hw), F32),
                            jax.ShapeDtypeStruct((m, hw), F32), jax.ShapeDtypeStruct((1, hw), F32),
                            jax.ShapeDtypeStruct((n_heads, m, 1), F32), jax.ShapeDtypeStruct((n_heads, nq, 1, t), F32)],
                 operands=[qkv, qkv, qkv, o, d_on, gain, lse, ccol, crow], comm=comm)


def _sb_scores(z):
    lp = jnp.log(1.0 + jnp.exp(-jnp.abs(z)))
    return jnp.minimum(z, 0.0) - lp, jnp.minimum(-z, 0.0) - lp


def _sb_fwd(qkv, gain, n_heads, pad, name, comm=None):
    m = qkv.shape[0]
    t = _attn_block(m)
    nq = m // t
    assert nq <= LANES
    scale = HEAD_DIM ** -0.5
    hw = n_heads * HEAD_DIM

    def body(q_ref, k_ref, v_ref, g_ref, after_ref, o_ref, on_ref, run_ref):
        i = pl.program_id(1)
        q = q_ref[...]
        qpos = i * t + lax.broadcasted_iota(jnp.int32, (t, 1), 0)
        after = after_ref[...]
        lane = lax.broadcasted_iota(jnp.int32, (t, LANES), 1)

        def step(j, j_prev, j_next, state, masked):
            run, acc, a_prev, s_cur = state
            acc = acc + _dot(a_prev, _block(v_ref, j_prev, t))
            s_next = _dot(q, _block(k_ref, j_next, t), NT_DIMS)
            ls_pos, log_1m = _sb_scores(s_cur * scale)
            if masked:
                kpos = j * t + lax.broadcasted_iota(jnp.int32, (1, t), 1)
                ok = (kpos < qpos) & (kpos >= pad)
                log_1m = jnp.where(ok, log_1m, 0.0)
            a = jnp.exp(ls_pos + _dot_split(log_1m, after) + run)
            if masked:
                a = jnp.where(ok, a, 0.0)
            run_ref[...] = jnp.where(lane == j, run, run_ref[...])
            return run + jnp.sum(log_1m, axis=-1, keepdims=True), acc, a.astype(BF16), s_next

        run_ref[...] = jnp.zeros_like(run_ref)
        state = (jnp.zeros((t, 1), F32), jnp.zeros((t, HEAD_DIM), F32), jnp.zeros((t, t), BF16),
                 _dot(q, _block(k_ref, i, t), NT_DIMS))
        state = step(i, i, jnp.maximum(i - 1, 0), state, True)
        state = lax.fori_loop(1, i, lambda jj, st: step(i - jj, i - jj + 1, i - jj - 1, st, False), state)
        _, acc, a_last, _ = lax.fori_loop(0, jnp.minimum(i, 1), lambda _, st: step(0, 1, 0, st, True), state)
        o = acc + _dot(a_last, _block(v_ref, 0, t))
        o_ref[...] = o
        on_ref[...] = _head_norm(o, g_ref[...]).astype(BF16)

    q_spec, k_spec, v_spec = _qkv_specs(t, m, n_heads, 3 * n_heads)
    head = pl.BlockSpec((t, HEAD_DIM), lambda hd, i: (i, hd))
    return _call(body, name=name, grid=(n_heads, nq),
                 in_specs=[q_spec, k_spec, v_spec, pl.BlockSpec((1, HEAD_DIM), lambda hd, i: (0, hd)),
                           pl.BlockSpec((t, t), lambda hd, i: (0, 0))],
                 out_specs=[head, head, pl.BlockSpec((None, t, LANES), lambda hd, i: (hd, i, 0))],
                 out_shape=[jax.ShapeDtypeStruct((m, hw), F32), jax.ShapeDtypeStruct((m, hw), BF16),
                            jax.ShapeDtypeStruct((n_heads, m, LANES), F32)],
                 operands=[qkv, qkv, qkv, gain, _triangle(t, lambda r, c: r > c)], comm=comm)


def _sb_bwd(qkv, o, d_on, gain, runs, n_heads, pad, name, comm=None):
    m = qkv.shape[0]
    t = _attn_block(m)
    nq = m // t
    scale = HEAD_DIM ** -0.5
    hw = n_heads * HEAD_DIM

    def body(q_ref, k_ref, v_ref, o_ref, don_ref, g_ref, run_ref, after_ref, before_ref, dq_ref, dk_ref, dv_ref, dg_ref):
        i = pl.program_id(1)

        @pl.when(i == 0)
        def _():
            dk_ref[...] = jnp.zeros_like(dk_ref)
            dv_ref[...] = jnp.zeros_like(dv_ref)
            dg_ref[...] = jnp.zeros_like(dg_ref)

        q = q_ref[...]
        d_o, dgain = _head_norm_bwd(o_ref[...], don_ref[...], g_ref[...])
        dg_ref[...] += dgain
        d_ob = d_o.astype(BF16)
        runs_i = run_ref[...]
        qpos = i * t + lax.broadcasted_iota(jnp.int32, (t, 1), 0)
        after = after_ref[...]
        before = before_ref[...]
        lane = lax.broadcasted_iota(jnp.int32, (t, LANES), 1)

        def step(j, carry, masked):
            g_run, dq = carry
            k = _block(k_ref, j, t)
            v = _block(v_ref, j, t)
            off = pl.multiple_of(j * t, t)
            ls_pos, ls_neg = _sb_scores(_dot(q, k, NT_DIMS) * scale)
            log_1m = ls_neg
            if masked:
                kpos = off + lax.broadcasted_iota(jnp.int32, (1, t), 1)
                ok = (kpos < qpos) & (kpos >= pad)
                log_1m = jnp.where(ok, ls_neg, 0.0)
            run = jnp.sum(jnp.where(lane == j, runs_i, 0.0), axis=-1, keepdims=True)
            a = jnp.exp(ls_pos + _dot_split(log_1m, after) + run)
            if masked:
                a = jnp.where(ok, a, 0.0)
            g = a * _dot(d_ob, v, NT_DIMS)
            prefix = _dot(g.astype(BF16), before) + g_run
            dz = g * jnp.exp(ls_neg) - jnp.exp(ls_pos) * prefix
            if masked:
                dz = jnp.where(ok, dz, 0.0)
            dzb = dz.astype(BF16)
            dk_ref[pl.ds(off, t), :] += _dot(dzb, q, TN_DIMS) * scale
            dv_ref[pl.ds(off, t), :] += _dot(a.astype(BF16), d_ob, TN_DIMS)
            return g_run + jnp.sum(g, axis=-1, keepdims=True), dq + _dot(dzb, k)

        carry = step(0, (jnp.zeros((t, 1), F32), jnp.zeros((t, HEAD_DIM), F32)), True)
        carry = lax.fori_loop(1, i, lambda j, c: step(j, c, False), carry)
        _, dq = lax.fori_loop(0, jnp.minimum(i, 1), lambda _, c: step(i, c, True), carry)
        dq_ref[...] = (dq * scale).astype(BF16)

    q_spec, k_spec, v_spec = _qkv_specs(t, m, n_heads, 3 * n_heads)
    head = pl.BlockSpec((t, HEAD_DIM), lambda hd, i: (i, hd))
    whole = pl.BlockSpec((m, HEAD_DIM), lambda hd, i: (0, hd))
    gvec = pl.BlockSpec((1, HEAD_DIM), lambda hd, i: (0, hd))
    tri = pl.BlockSpec((t, t), lambda hd, i: (0, 0))
    return _call(body, name=name, grid=(n_heads, nq),
                 in_specs=[q_spec, k_spec, v_spec, head, head, gvec, pl.BlockSpec((None, t, LANES), lambda hd, i: (hd, i, 0)),
                           tri, tri],
                 out_specs=[head, whole, whole, gvec],
                 out_shape=[jax.ShapeDtypeStruct((m, hw), BF16), jax.ShapeDtypeStruct((m, hw), F32),
                            jax.ShapeDtypeStruct((m, hw), F32), jax.ShapeDtypeStruct((1, hw), F32)],
                 operands=[qkv, qkv, qkv, o, d_on, gain, runs, _triangle(t, lambda r, c: r > c), _triangle(t, lambda r, c: r < c)],
                 comm=comm)


def _adamw(parts, w, m1, v2, name, comm=None):
    nl, r, c = w.shape
    assert len(parts) == nl
    n_parts = parts[0].shape[0]
    block_elems = 256 * 1024
    if r % 8 == 0 or c % LANES != 0:
        tr, tc = _tile(r, max(8, block_elems // (-(-c // LANES) * LANES)), 8), c
    else:
        tr, tc = r, _tile(c, max(LANES, block_elems // r // LANES * LANES), LANES)
    nr, nc = r // tr, c // tc
    bias1 = 1.0 / (1.0 - ADAM_B1 ** ADAM_STEP)
    bias2 = 1.0 / (1.0 - ADAM_B2 ** ADAM_STEP)

    def body(*refs):
        p_refs = refs[:nl]
        w_ref, m_ref, v_ref, g_ref, d_ref, nm_ref, nv_ref = refs[nl:]

        def update(p_ref):
            g = p_ref[0].astype(F32)
            for s in range(1, n_parts):
                g = g + p_ref[s].astype(F32)
            m_new = ADAM_B1 * m_ref[...] + (1.0 - ADAM_B1) * g
            v_new = ADAM_B2 * v_ref[...] + (1.0 - ADAM_B2) * (g * g)
            g_ref[...] = g
            nm_ref[...] = m_new
            nv_ref[...] = v_new
            d_ref[...] = -ADAM_LR * ((m_new * bias1) / (jnp.sqrt(v_new * bias2) + ADAM_EPS) + ADAM_WD * w_ref[...])

        for ll in range(nl):
            @pl.when(pl.program_id(0) == ll)
            def _(ll=ll):
                update(p_refs[ll])

    def part_spec(ll):
        def index(l, i, j):
            pin = jnp.where(l < ll, 0, 1)
            return 0, jnp.where(l == ll, i, pin * (nr - 1)), jnp.where(l == ll, j, pin * (nc - 1))
        return pl.BlockSpec((n_parts, tr, tc), index)

    blk = pl.BlockSpec((None, tr, tc), lambda l, i, j: (l, i, j))
    out, got = _call(body, name=name, grid=(nl, nr, nc), in_specs=[part_spec(ll) for ll in range(nl)] + [blk, blk, blk],
                     out_specs=[blk] * 4, out_shape=[jax.ShapeDtypeStruct((nl, r, c), F32)] * 4,
                     operands=list(parts) + [w, m1, v2], comm=comm)
    return out if comm is None else (out, got)


SMALL_WEIGHTS = ("ffn1_norm", "mix_norm", "b_forget", "g_fox", "g_sb", "ffn2_norm", "final_norm")
WEIGHT_ORDER = ("meta_tokens", "ffn1_norm", "ffn1_w_gate", "ffn1_w_up", "ffn1_w_down", "mix_norm", "w_in", "b_forget",
                "g_fox", "g_sb", "w_out", "ffn2_norm", "ffn2_w_gate", "ffn2_w_up", "ffn2_w_down", "final_norm")
GROUPS = {"ffn1": ("ffn1_w_gate", "ffn1_w_up", "ffn1_w_down"), "mix": ("w_in", "w_out"),
          "ffn2": ("ffn2_w_gate", "ffn2_w_up", "ffn2_w_down")}
TRANSPOSED = ("ffn1_w_gate", "ffn1_w_up", "ffn2_w_gate", "ffn2_w_up")


def _pad_lanes(a):
    extra = (-a.shape[-1]) % LANES
    return a if extra == 0 else jnp.pad(a, [(0, 0)] * (a.ndim - 1) + [(0, extra)])


def _ffn_backward(dh_b, dh, saved, gain, wg, wu, wd, tag, carried=None):
    h, xn, g, u, act = saved
    (dg, du), _ = _ffn_bwd_act(dh_b, wd, g, u, f"{tag}_bwd_act")
    d_wd = _ffn_dw(act, dh_b, f"{tag}_dwd", alpha=0.5)
    d_wg = _ffn_dw(dg, xn, f"{tag}_dwg")
    d_wu = _ffn_dw(du, xn, f"{tag}_dwu")
    if carried:
        dxn, got = _ffn_contract([(dg, wg), (du, wu)], F32, f"{tag}_dxn", group=2, comm=_Exchange(carried))
    else:
        dxn, got = _ffn_contract([(dg, wg), (du, wu)], F32, f"{tag}_dxn", group=2), []
    dh_in, dh_in_b, d_gain = _rms_bwd(dxn, h, gain, dh, f"{tag}_norm_bwd")
    return dh_in, dh_in_b, d_gain, d_wg, d_wu, d_wd, got


def _ffn_backward_last(dh_b, dh, saved, gain, wg, wu, wd, tag, first, second):
    h, xn, g, u, act = saved
    (dg, du), got_first = _ffn_bwd_act(dh_b, wd, g, u, f"{tag}_bwd_act", first)
    d_wd, got_second = _ffn_dw(act, dh_b, f"{tag}_dwd", alpha=0.5, comm=second)
    c = d_wd.shape[1]
    half = c // 2 // 16 * 16
    d_wg, got_wd = _ffn_dw(dg, xn, f"{tag}_dwg", comm=_Exchange([d_wd], rows=(0, half)))
    d_wu, got_wd = _ffn_dw(du, xn, f"{tag}_dwu", comm=_Exchange([d_wd], rows=(half, c - half), into=got_wd))
    dxn, got_wg = _ffn_contract([(dg, wg), (du, wu)], F32, f"{tag}_dxn", group=2, comm=_Exchange([d_wg]))
    dh_in, dh_in_b, d_gain = _rms_bwd(dxn, h, gain, dh, f"{tag}_norm_bwd")
    return dh_in, dh_in_b, d_gain, got_first, got_second, got_wg[0], got_wd[0], d_wu


def _mixer_forward(h, gain, w_in_t, bias, g_fox, g_sb, w_out, n_heads, pad, tag, comm_fox=None, comm_sb=None):
    m = h.shape[0]
    t = _attn_block(m)
    hw = n_heads * HEAD_DIM
    xn = _rms_fwd(h, gain, f"{tag}_norm")
    qkv = _mm([(xn, w_in_t[:6 * hw])], "nt", BF16, name=f"{tag}_qkv")
    fl = _mm([(xn, w_in_t[6 * hw:])], "nt", F32, name=f"{tag}_forget")
    c = _forget_cumsum(fl, bias, n_heads, pad, f"{tag}_cumsum")
    c_heads = c[:, :n_heads].T
    ccol = c_heads[:, :, None]
    crow = c_heads.reshape(n_heads, m // t, 1, t)
    (o_f, on_f, lse), got_fox = _fox_fwd(qkv, ccol, crow, g_fox, n_heads, pad, f"{tag}_fox", comm_fox)
    (o_s, on_s, runs), got_sb = _sb_fwd(qkv, g_sb, n_heads, pad, f"{tag}_sb", comm_sb)
    h_out = _mm([(on_f, w_out[:hw]), (on_s, w_out[hw:])], "nn", F32, name=f"{tag}_out", res=h)
    return h_out, (h, xn, qkv, fl, ccol, crow, o_f, on_f, lse, o_s, on_s, runs), got_fox, got_sb


def _mixer_backward(dh_b, dh_out, saved, gain, w_in_t, bias, g_fox, g_sb, w_out, n_heads, pad, tag, comm_fox=None, comm_sb=None,
                    rows_in_dxn=0):
    h, xn, qkv, fl, ccol, crow, o_f, on_f, lse, o_s, on_s, runs = saved
    m = h.shape[0]
    hw = n_heads * HEAD_DIM
    d_on_f = _mm([(dh_b, w_out[:hw])], "nt", F32, name=f"{tag}_don_f")
    d_on_s = _mm([(dh_b, w_out[hw:])], "nt", F32, name=f"{tag}_don_s")
    d_wout = jnp.concatenate([_mm([(on_f, dh_b)], "tn", BF16, name=f"{tag}_dwout_f"),
                              _mm([(on_s, dh_b)], "tn", BF16, name=f"{tag}_dwout_s")], axis=0)
    (dq_f, dk_f, dv_f, dg_fox, dccol, dcrow), got_fox = _fox_bwd(
        qkv, o_f, d_on_f, g_fox, lse, ccol, crow, n_heads, pad, f"{tag}_fox_bwd", comm_fox)
    (dq_s, dk_s, dv_s, dg_sb), got_sb = _sb_bwd(qkv, o_s, d_on_s, g_sb, runs, n_heads, pad, f"{tag}_sb_bwd", comm_sb)
    dc_a = _pad_lanes(dccol[:, :, 0].T)
    dc_b = _pad_lanes(dcrow.reshape(n_heads, m).T)
    dfl, dbias = _forget_cumsum_bwd(dc_a, dc_b, fl, bias, n_heads, pad, f"{tag}_cumsum_bwd")
    dproj = jnp.concatenate([dq_f, dk_f.astype(BF16), dv_f.astype(BF16), dq_s, dk_s.astype(BF16), dv_s.astype(BF16),
                             dfl.astype(BF16)], axis=1)
    d_win = _mm([(dproj, xn)], "tn", BF16, name=f"{tag}_dwin")[:6 * hw + n_heads].reshape(N_DEV, -1, h.shape[1])
    got_win = None
    if rows_in_dxn:
        dxn, got_win = _mm([(dproj, w_in_t)], "nn", F32, name=f"{tag}_dxn", whole_k=True,
                           comm=_Exchange([d_win], rows=(0, rows_in_dxn)))
    else:
        dxn = _mm([(dproj, w_in_t)], "nn", F32, name=f"{tag}_dxn", whole_k=True)
    dh, dh_in_b, d_gain = _rms_bwd(dxn, h, gain, dh_out, f"{tag}_norm_bwd")
    return dh, dh_in_b, d_gain, d_win, dbias, dg_fox, dg_sb, d_wout, got_fox, got_sb, got_win


def kernel(x, meta_tokens, ffn1_norm, ffn1_w_gate, ffn1_w_up, ffn1_w_down, mix_norm, w_in, b_forget, g_fox, g_sb, w_out, ffn2_norm, ffn2_w_gate, ffn2_w_up, ffn2_w_down, final_norm, loss_target, m_meta_tokens, m_ffn1_norm, m_ffn1_w_gate, m_ffn1_w_up, m_ffn1_w_down, m_mix_norm, m_w_in, m_b_forget, m_g_fox, m_g_sb, m_w_out, m_ffn2_norm, m_ffn2_w_gate, m_ffn2_w_up, m_ffn2_w_down, m_final_norm, v_meta_tokens, v_ffn1_norm, v_ffn1_w_gate, v_ffn1_w_up, v_ffn1_w_down, v_mix_norm, v_w_in, v_b_forget, v_g_fox, v_g_sb, v_w_out, v_ffn2_norm, v_ffn2_w_gate, v_ffn2_w_up, v_ffn2_w_down, v_final_norm):
    given = dict(locals())
    seq, d = x.shape[1], x.shape[2]
    depth = ffn1_norm.shape[0]
    d_in = N_DEV * w_in.shape[2]
    n_heads = g_fox.shape[1] // HEAD_DIM
    hw = n_heads * HEAD_DIM
    assert seq % ROW_BLOCK == 0 and d_in == 6 * hw + n_heads and n_heads <= LANES
    pad = (-(seq + N_META)) % ROW_BLOCK
    x_off = pad + N_META

    def view(n, a):
        if n in TRANSPOSED:
            return jnp.swapaxes(a, 1, 2)
        return a.transpose(2, 0, 1) if n == "w_in" else a

    def unview(n, a):
        if n in TRANSPOSED:
            return jnp.swapaxes(a, 1, 2)
        return a.transpose(1, 2, 0) if n == "w_in" else a

    def shard(n, l):
        v = view(n, given[n])
        return (v[:, l] if n == "w_in" else v[l]).astype(BF16)

    def shards(group, l):
        return [shard(n, l) for n in GROUPS[group]]

    sh = shards("ffn1", 0)
    first = _run_alone(_Gather(sh[:2] + [meta_tokens]), "gather_first")
    full = {}
    meta_full = first[2].transpose(1, 0, 2).reshape(N_META, d)
    h = jnp.concatenate([jnp.zeros((pad, d), F32), meta_full, x[0]], axis=0)
    weights, saved = [], []
    for l in range(depth):
        xn = _rms_fwd(h, ffn1_norm[l:l + 1], "ffn1_norm")
        if l == 0:
            (g, u, act), (wd1,) = _ffn_up(xn, first[0], first[1], "ffn1_up", _Gather(sh[2:]))
            full[("ffn1", 0)] = (first[0], first[1], wd1)
            h_out, full[("mix", 0)] = _ffn_contract([(act, wd1)], F32, "ffn1_down", group=4, alpha=0.5, res=h,
                                                    comm=_Gather(shards("mix", 0)))
        else:
            wg1, wu1, wd1 = full[("ffn1", l)]
            (g, u, act), full[("mix", l)] = _ffn_up(xn, wg1, wu1, "ffn1_up", _Gather(shards("mix", l)))
            h_out = _ffn_contract([(act, wd1)], F32, "ffn1_down", group=4, alpha=0.5, res=h)
        s1 = (h, xn, g, u, act)
        h = h_out
        win3, wout3 = full[("mix", l)]
        w_in_t = jnp.pad(win3.reshape(d_in, d), ((0, 6 * hw + LANES - d_in), (0, 0)))
        w_out_full = wout3.reshape(N_DEV * wout3.shape[1], d)
        bias = _pad_lanes(b_forget[l:l + 1])
        nxt = _Gather(shards("ffn1", l + 1)) if l + 1 < depth else None
        h, sm, full[("ffn2", l)], got = _mixer_forward(
            h, mix_norm[l:l + 1], w_in_t, bias, g_fox[l:l + 1], g_sb[l:l + 1], w_out_full, n_heads, pad, "mix",
            _Gather(shards("ffn2", l)), nxt)
        if nxt is not None:
            full[("ffn1", l + 1)] = got
        wg2, wu2, wd2 = full[("ffn2", l)]
        xn = _rms_fwd(h, ffn2_norm[l:l + 1], "ffn2_norm")
        (g, u, act), _ = _ffn_up(xn, wg2, wu2, "ffn2_up")
        s2 = (h, xn, g, u, act)
        h = _ffn_contract([(act, wd2)], F32, "ffn2_down", group=4, alpha=0.5, res=h)
        weights.append((w_in_t, w_out_full, bias))
        saved.append((s1, sm, s2))

    dh, dh_b, d_final, loss_arr = _loss_head(h, final_norm[None, :], loss_target[0], x_off, "loss_head")
    small = {n: [None] * depth for n in SMALL_WEIGHTS[:-1]}
    partial, received = {}, {}

    def names(group, l):
        return [(n, l) for n in GROUPS[group]]

    def send(keys):
        return [partial[k] for k in keys]

    for l in reversed(range(depth)):
        w_in_t, w_out_full, bias = weights[l]
        s1, sm, s2 = saved[l]
        wg2, wu2, wd2 = full[("ffn2", l)]
        up = l + 1 < depth
        in_dxn = [("ffn1_w_gate", l + 1)] if up else []
        in_fox = names("mix", l + 1) + [("ffn1_w_up", l + 1)] if up else []
        in_sb = names("ffn2", l) + ([("ffn1_w_down", l + 1)] if up else [])
        dh, dh_b, small["ffn2_norm"][l], partial[("ffn2_w_gate", l)], partial[("ffn2_w_up", l)], partial[("ffn2_w_down", l)], got = (
            _ffn_backward(dh_b, dh, s2, ffn2_norm[l:l + 1], wg2, wu2, wd2, "ffn2", send(in_dxn)))
        received.update(zip(in_dxn, got))
        win_rows = (d_in // N_DEV // 2 // 16 * 16) if l == 0 else 0
        (dh, dh_b, small["mix_norm"][l], partial[("w_in", l)], dbias, small["g_fox"][l], small["g_sb"][l], d_wout,
         got_fox, got_sb, got_win) = _mixer_backward(
            dh_b, dh, sm, mix_norm[l:l + 1], w_in_t, bias, g_fox[l:l + 1], g_sb[l:l + 1], w_out_full, n_heads, pad, "mix",
            _Exchange(send(in_fox)) if in_fox else None, _Exchange(send(in_sb)), win_rows)
        received.update(zip(in_fox, got_fox or []))
        received.update(zip(in_sb, got_sb))
        partial[("w_out", l)] = d_wout.reshape(N_DEV, -1, d)
        small["b_forget"][l] = dbias[:, :n_heads]
        wg1, wu1, wd1 = full[("ffn1", l)]
        if l > 0:
            dh, dh_b, small["ffn1_norm"][l], partial[("ffn1_w_gate", l)], partial[("ffn1_w_up", l)], partial[("ffn1_w_down", l)], _ = (
                _ffn_backward(dh_b, dh, s1, ffn1_norm[l:l + 1], wg1, wu1, wd1, "ffn1"))
        else:
            rest = (win_rows, d_in // N_DEV - win_rows)
            (dh, dh_b, small["ffn1_norm"][0], got_a, got_b, received[("ffn1_w_gate", 0)], received[("ffn1_w_down", 0)],
             last_grad) = _ffn_backward_last(
                dh_b, dh, s1, ffn1_norm[0:1], wg1, wu1, wd1, "ffn1", _Exchange([partial[("w_out", 0)]]),
                _Exchange([partial[("w_in", 0)]], rows=rest, into=got_win))
            received[("w_out", 0)], received[("w_in", 0)] = got_a[0], got_b[0]
    grad_x = dh[x_off:][None]
    d_meta = dh[pad:x_off].reshape(N_META, N_DEV, -1).transpose(1, 0, 2)
    received[("meta_tokens", 0)] = _run_alone(_Exchange([d_meta]), "exchange_meta")[0]

    vec = [loss_arr[0:1, :]] + [_pad_lanes(jnp.concatenate(small[n], axis=0).reshape(1, -1)) for n in SMALL_WEIGHTS[:-1]]
    vec.append(d_final)
    sizes = [a.shape[1] for a in vec]
    summed = _all_reduce_small(jnp.concatenate(vec, axis=1), "reduce_small")
    loss = summed[0, 0]

    def packed(prefix):
        cols = [jnp.zeros((1, LANES), F32)]
        cols += [_pad_lanes(given[prefix + n].reshape(1, -1)) for n in SMALL_WEIGHTS]
        return jnp.concatenate(cols, axis=1)[None]

    small_out = _adamw([summed[None]], packed(""), packed("m_"), packed("v_"), "adamw_small")

    c_last = last_grad.shape[1]
    quarter = c_last // 4 // 16 * 16
    pieces = [(k * quarter, quarter if k < 3 else c_last - 3 * quarter) for k in range(4)]
    order = ["ffn2_w_gate", "ffn2_w_up", "ffn2_w_down", "ffn1_w_gate", "ffn1_w_down", "w_in", "w_out", "ffn1_w_up"]
    out, arrived = {}, []
    for k, n in enumerate(order):
        wv, mv, vv = (view(n, given[p + n]) for p in ("", "m_", "v_"))
        if n == "ffn1_w_up":
            received[(n, 0)] = arrived[0]
        if n == "w_in":
            per_layer = [_adamw([received[(n, l)]], wv[:, l][None], mv[:, l][None], vv[:, l][None], "adamw_" + n)
                         for l in range(depth)]
            res = [jnp.stack([per_layer[l][k][0] for l in range(depth)], axis=1) for k in range(4)]
        elif k < len(pieces):
            res, arrived = _adamw([received[(n, l)] for l in range(depth)], wv, mv, vv, "adamw_" + n,
                                  _Exchange([last_grad], rows=pieces[k], into=arrived))
        else:
            res = _adamw([received[(n, l)] for l in range(depth)], wv, mv, vv, "adamw_" + n)
        out[n] = [unview(n, r) for r in res]
    out["meta_tokens"] = [r[0] for r in _adamw([received[("meta_tokens", 0)]], meta_tokens[None], m_meta_tokens[None],
                                               v_meta_tokens[None], "adamw_meta_tokens")]
    offset = sizes[0]
    for n, size in zip(SMALL_WEIGHTS, sizes[1:]):
        shape, count = given[n].shape, given[n].size
        out[n] = [r[0, 0, offset:offset + count].reshape(shape) for r in small_out]
        offset += size

    result = [loss, grad_x]
    for k in range(4):
        result += [out[n][k] for n in WEIGHT_ORDER]
    return tuple(result)
```

```python
import math

import jax
import jax.numpy as jnp
from jax import lax
from jax.experimental import pallas as pl
from jax.experimental.pallas import tpu as pltpu

F32 = jnp.float32
BF16 = jnp.bfloat16

N_DEV = 8
N_META = 16
HEAD_DIM = 128
ROW_BLOCK = 128
LANES = 128
EPS = 1e-6
NEG = -1e30
ADAM_LR = 0.001
ADAM_B1 = 0.9
ADAM_B2 = 0.999
ADAM_EPS = 1e-08
ADAM_WD = 0.01
ADAM_STEP = 10
VMEM_LIMIT_BYTES = 56 * 1024 * 1024
MESH = pl.DeviceIdType.MESH

NT_DIMS = (((1,), (1,)), ((), ()))
TN_DIMS = (((0,), (0,)), ((), ()))
NN_DIMS = (((1,), (0,)), ((), ()))
ANY = pl.BlockSpec(memory_space=pl.ANY)


def _tile(n, cap, align):
    best = None
    for d in range(align, min(n, cap) + 1, align):
        if n % d == 0:
            best = d
    return best if best is not None else n


def _dot(a, b, dims=NN_DIMS):
    return lax.dot_general(a, b, dims, preferred_element_type=F32)


def _dot_split(x, u):
    hi = x.astype(BF16)
    lo = (x - hi.astype(F32)).astype(BF16)
    return _dot(hi, u) + _dot(lo, u)


def _my_position():
    return lax.axis_index("x"), lax.axis_index("y"), lax.axis_index("c")


class _Gather:
    n_phases = 3

    def __init__(self, arrs):
        self.arrs = list(arrs)
        n = len(self.arrs)
        self.out_shapes = [jax.ShapeDtypeStruct((N_DEV,) + a.shape, a.dtype) for a in self.arrs]
        self.scratch = [pltpu.SemaphoreType.DMA((n, 7)), pltpu.SemaphoreType.DMA((n, 7)), pltpu.SemaphoreType.DMA((n,))]

    def phase(self, p, ins, outs, sems):
        send_sems, recv_sems, local_sems = sems
        n = len(self.arrs)
        x, y, c = _my_position()
        me, sibling = (x, y, c), (x, y, 1 - c)
        chips = [(1 - x, y), (x, 1 - y), (1 - x, 1 - y)]

        def copy(a, k, block, to, src=None):
            slot = outs[a].at[4 * block[0] + 2 * block[1] + block[2]]
            return pltpu.make_async_remote_copy(
                src_ref=slot if src is None else src, dst_ref=slot,
                send_sem=send_sems.at[a, k], recv_sem=recv_sems.at[a, k], device_id=to, device_id_type=MESH)

        def local(a):
            return pltpu.make_async_copy(ins[a], outs[a].at[4 * x + 2 * y + c], local_sems.at[a])

        def first(a):
            return [copy(a, 0, me, sibling, src=ins[a])] + [copy(a, 1 + j, me, (*chip, c), src=ins[a]) for j, chip in enumerate(chips)]

        def passed(a, j):
            return copy(a, 4 + j, (*chips[j], c), sibling)

        if p == 0:
            for a in range(n):
                local(a).start()
            for a in range(n):
                for cp in first(a):
                    cp.start()
        elif p == 1:
            for a in range(n):
                for j, chip in enumerate(chips):
                    copy(a, 1 + j, (*chip, c), me).wait_recv()
                    passed(a, j).start()
        else:
            for a in range(n):
                copy(a, 0, sibling, me).wait_recv()
                for j, chip in enumerate(chips):
                    copy(a, 4 + j, (*chip, 1 - c), me).wait_recv()
            for a in range(n):
                for cp in first(a) + [passed(a, j) for j in range(3)]:
                    cp.wait_send()
                local(a).wait()


class _Exchange:
    n_phases = 2

    def __init__(self, arrs, rows=None, into=()):
        self.arrs = list(arrs)
        self.rows = rows
        self.into = list(into)
        n = len(self.arrs)
        self.out_shapes = [jax.ShapeDtypeStruct(a.shape, a.dtype) for a in self.arrs]
        self.scratch = [pltpu.SemaphoreType.DMA((n, 7)), pltpu.SemaphoreType.DMA((n, 7)), pltpu.SemaphoreType.DMA((n,))]

    def phase(self, p, ins, outs, sems):
        send_sems, recv_sems, local_sems = sems
        n = len(self.arrs)
        x, y, c = _my_position()
        me = 4 * x + 2 * y + c

        def peer_of(r):
            return (x ^ ((r >> 2) & 1), y ^ ((r >> 1) & 1), c ^ (r & 1))

        def part(ref, d):
            return ref.at[d] if self.rows is None else ref.at[d, pl.ds(self.rows[0], self.rows[1])]

        def copy(a, r):
            px, py, pc = peer_of(r)
            return pltpu.make_async_remote_copy(
                src_ref=part(ins[a], 4 * px + 2 * py + pc), dst_ref=part(outs[a], me),
                send_sem=send_sems.at[a, r - 1], recv_sem=recv_sems.at[a, r - 1],
                device_id=(px, py, pc), device_id_type=MESH)

        def arrival(a, r):
            px, py, pc = peer_of(r)
            slot = part(outs[a], 4 * px + 2 * py + pc)
            return pltpu.make_async_remote_copy(
                src_ref=slot, dst_ref=slot, send_sem=send_sems.at[a, r - 1], recv_sem=recv_sems.at[a, r - 1],
                device_id=(px, py, pc), device_id_type=MESH)

        def local(a):
            return pltpu.make_async_copy(part(ins[a], me), part(outs[a], me), local_sems.at[a])

        if p == 0:
            for a in range(n):
                local(a).start()
            for a in range(n):
                for r in range(1, N_DEV):
                    copy(a, r).start()
        else:
            for a in range(n):
                for r in range(1, N_DEV):
                    arrival(a, r).wait_recv()
            for a in range(n):
                for r in range(1, N_DEV):
                    copy(a, r).wait_send()
                local(a).wait()


def _run_alone(comm, name):
    n = len(comm.arrs)

    def body(*refs):
        for p in range(comm.n_phases):
            comm.phase(p, refs[:n], refs[n:2 * n], refs[2 * n:])

    return pl.pallas_call(body, name=name, out_shape=comm.out_shapes, in_specs=[ANY] * n, out_specs=[ANY] * n,
                          scratch_shapes=comm.scratch)(*comm.arrs)


def _call(body, *, name, grid, in_specs, out_specs, out_shape, operands, scratch_shapes=(), comm=None):
    scratch_shapes = list(scratch_shapes)
    params = pltpu.CompilerParams(dimension_semantics=("arbitrary",) * len(grid), vmem_limit_bytes=VMEM_LIMIT_BYTES)
    if comm is None:
        res = pl.pallas_call(body, name=name, grid=grid, in_specs=in_specs, out_specs=out_specs, out_shape=out_shape,
                             scratch_shapes=scratch_shapes, compiler_params=params)(*operands)
        return res, None
    n_in, n_out, n_sc = len(in_specs), len(out_specs), len(scratch_shapes)
    nc = len(comm.arrs)
    into = getattr(comm, "into", [])
    total = math.prod(grid)
    at = {0: 0, comm.n_phases - 1: total - 1}
    for p in range(1, comm.n_phases - 1):
        at[p] = (total * 7) // 8

    def wrapped(*refs):
        ins, cins = refs[:n_in], refs[n_in:n_in + nc]
        refs = refs[n_in + nc + len(into):]
        outs, couts = refs[:n_out], refs[n_out:n_out + nc]
        rest = refs[n_out + nc:]
        scratch, sems = rest[:n_sc], rest[n_sc:]
        step = 0
        for axis, size in enumerate(grid):
            step = step * size + pl.program_id(axis)
        for p in range(comm.n_phases - 1):
            @pl.when(step == at[p])
            def _(p=p):
                comm.phase(p, cins, couts, sems)
        body(*ins, *outs, *scratch)

        @pl.when(step == total - 1)
        def _():
            comm.phase(comm.n_phases - 1, cins, couts, sems)

    res = pl.pallas_call(
        wrapped, name=name, grid=grid, in_specs=list(in_specs) + [ANY] * (nc + len(into)),
        out_specs=list(out_specs) + [ANY] * nc, out_shape=list(out_shape) + comm.out_shapes,
        scratch_shapes=scratch_shapes + comm.scratch,
        input_output_aliases={n_in + nc + k: n_out + k for k in range(len(into))},
        compiler_params=params)(*operands, *comm.arrs, *into)
    return res[:n_out], res[n_out:]


def _all_reduce_small(vec, name):
    n = vec.shape[1]

    def body(v_ref, o_ref, buf, send_sems, recv_sems):
        x, y, c = _my_position()
        me = 4 * x + 2 * y + c

        def peer_of(r):
            return (x ^ ((r >> 2) & 1), y ^ ((r >> 1) & 1), c ^ (r & 1))

        def copy(r):
            px, py, pc = peer_of(r)
            return pltpu.make_async_remote_copy(
                src_ref=v_ref, dst_ref=buf.at[me], send_sem=send_sems.at[r - 1], recv_sem=recv_sems.at[r - 1],
                device_id=(px, py, pc), device_id_type=MESH)

        def arrival(r):
            px, py, pc = peer_of(r)
            slot = buf.at[4 * px + 2 * py + pc]
            return pltpu.make_async_remote_copy(
                src_ref=slot, dst_ref=slot, send_sem=send_sems.at[r - 1], recv_sem=recv_sems.at[r - 1],
                device_id=(px, py, pc), device_id_type=MESH)

        sends = [copy(r) for r in range(1, N_DEV)]
        for cp in sends:
            cp.start()
        buf[me] = v_ref[...]
        for r in range(1, N_DEV):
            arrival(r).wait_recv()
        for cp in sends:
            cp.wait_send()
        total = buf[0]
        for d in range(1, N_DEV):
            total = total + buf[d]
        o_ref[...] = total

    vmem = pl.BlockSpec(memory_space=pltpu.VMEM)
    return pl.pallas_call(
        body, name=name, out_shape=jax.ShapeDtypeStruct((1, n), F32), in_specs=[vmem], out_specs=vmem,
        scratch_shapes=[pltpu.VMEM((N_DEV, 1, n), F32), pltpu.SemaphoreType.DMA((7,)), pltpu.SemaphoreType.DMA((7,))],
    )(vec)


def _mm_core(pairs, dims, out_dtype, *, name, grid, out_shape, out_spec, acc_shape, alpha=1.0, res=None, comm=None):
    nk = grid[2]
    npairs = len(pairs)

    def body(*refs):
        ab = refs[:2 * npairs]
        rest = refs[2 * npairs:]
        res_ref = rest[0] if res is not None else None
        o_ref = rest[1] if res is not None else rest[0]
        acc_ref = rest[-1] if nk > 1 else None
        part = None
        for p in range(npairs):
            a_ref, b_ref = ab[2 * p], ab[2 * p + 1]
            shards = [(a_ref[s], b_ref[s]) for s in range(a_ref.shape[0])] if len(a_ref.shape) == 3 else [(a_ref[...], b_ref[...])]
            for av, bv in shards:
                d = _dot(av.astype(BF16), bv.astype(BF16), dims)
                part = d if part is None else part + d

        def finish(total):
            val = total * alpha if alpha != 1.0 else total
            if res_ref is not None:
                val = res_ref[...] + val
            o_ref[...] = val.astype(out_dtype)

        if nk == 1:
            finish(part)
        else:
            kk = pl.program_id(2)

            @pl.when(kk == 0)
            def _():
                acc_ref[...] = part

            @pl.when(kk > 0)
            def _():
                acc_ref[...] += part

            @pl.when(kk == nk - 1)
            def _():
                finish(acc_ref[...])

    operands, in_specs = [], []
    for (a, a_spec), (b, b_spec) in pairs:
        operands += [a, b]
        in_specs += [a_spec, b_spec]
    if res is not None:
        operands.append(res[0])
        in_specs.append(res[1])
    out, got = _call(body, name=name, grid=grid, in_specs=in_specs, out_specs=[out_spec],
                     out_shape=[jax.ShapeDtypeStruct(out_shape, out_dtype)], operands=operands,
                     scratch_shapes=[pltpu.VMEM(acc_shape, F32)] if nk > 1 else [], comm=comm)
    return out[0] if comm is None else (out[0], got)


def _mm(pairs, mode, out_dtype, *, name, alpha=1.0, res=None, comm=None, whole_k=False):
    a0, b0 = pairs[0]
    if mode == "nn":
        (m, k), n = a0.shape, b0.shape[1]
    elif mode == "nt":
        (m, k), n = a0.shape, b0.shape[0]
    else:
        (k, m), n = a0.shape, b0.shape[1]
    dims = {"nn": NN_DIMS, "nt": NT_DIMS, "tn": TN_DIMS}[mode]
    tm = _tile(m, 528 if whole_k else 1056, LANES if mode == "tn" else 16)
    tn = _tile(n, 512 if whole_k else 1024, LANES)
    tk = k if whole_k else _tile(k, 2048 if mode != "tn" else 2112, LANES if mode != "tn" else 16)
    a_spec = pl.BlockSpec((tk, tm), lambda i, j, kk: (kk, i)) if mode == "tn" else pl.BlockSpec((tm, tk), lambda i, j, kk: (i, kk))
    b_spec = pl.BlockSpec((tn, tk), lambda i, j, kk: (j, kk)) if mode == "nt" else pl.BlockSpec((tk, tn), lambda i, j, kk: (kk, j))
    o_spec = pl.BlockSpec((tm, tn), lambda i, j, kk: (i, j))
    return _mm_core([((a, a_spec), (b, b_spec)) for a, b in pairs], dims, out_dtype, name=name,
                    grid=(m // tm, n // tn, k // tk), out_shape=(m, n), out_spec=o_spec, acc_shape=(tm, tn),
                    alpha=alpha, res=None if res is None else (res, o_spec), comm=comm)


def _rms_fwd(h, gain, name):
    m, d = h.shape
    tm = _tile(m, 528, 16)

    def body(h_ref, g_ref, o_ref):
        hv = h_ref[...]
        r = lax.rsqrt(jnp.mean(hv * hv, axis=-1, keepdims=True) + EPS)
        o_ref[...] = (hv * r * g_ref[...]).astype(BF16)

    row = pl.BlockSpec((tm, d), lambda i: (i, 0))
    out, _ = _call(body, name=name, grid=(m // tm,), in_specs=[row, pl.BlockSpec((1, d), lambda i: (0, 0))],
                   out_specs=[row], out_shape=[jax.ShapeDtypeStruct((m, d), BF16)], operands=[h, gain])
    return out[0]


def _rms_bwd(dxn, h, gain, dres, name):
    m, d = h.shape
    tm = _tile(m, 264, 16)

    def body(dxn_ref, h_ref, g_ref, dres_ref, dh_ref, dhb_ref, dg_ref):
        hv = h_ref[...]
        r = lax.rsqrt(jnp.mean(hv * hv, axis=-1, keepdims=True) + EPS)
        xhat = hv * r
        dxn_v = dxn_ref[...]
        t = dxn_v * g_ref[...]
        dh = dres_ref[...] + r * (t - xhat * jnp.mean(t * xhat, axis=-1, keepdims=True))
        dh_ref[...] = dh
        dhb_ref[...] = dh.astype(BF16)
        part = jnp.sum(dxn_v * xhat, axis=0, keepdims=True)

        @pl.when(pl.program_id(0) == 0)
        def _():
            dg_ref[...] = part

        @pl.when(pl.program_id(0) > 0)
        def _():
            dg_ref[...] += part

    row = pl.BlockSpec((tm, d), lambda i: (i, 0))
    vec = pl.BlockSpec((1, d), lambda i: (0, 0))
    out, _ = _call(body, name=name, grid=(m // tm,), in_specs=[row, row, vec, row], out_specs=[row, row, vec],
                   out_shape=[jax.ShapeDtypeStruct((m, d), F32), jax.ShapeDtypeStruct((m, d), BF16),
                              jax.ShapeDtypeStruct((1, d), F32)],
                   operands=[dxn, h, gain, dres])
    return out


def _loss_head(h, gain, target, x_off, name):
    m, d = h.shape
    tm = ROW_BLOCK
    first = x_off // tm

    def body(h_ref, g_ref, t_ref, dh_ref, dhb_ref, dg_ref, loss_ref):
        i = pl.program_id(0)

        @pl.when(i == 0)
        def _():
            dg_ref[...] = jnp.zeros_like(dg_ref)
            loss_ref[...] = jnp.zeros_like(loss_ref)

        @pl.when(i < first)
        def _():
            dh_ref[...] = jnp.zeros_like(dh_ref)
            dhb_ref[...] = jnp.zeros_like(dhb_ref)

        @pl.when(i >= first)
        def _():
            hv = h_ref[...]
            g = g_ref[...]
            r = lax.rsqrt(jnp.mean(hv * hv, axis=-1, keepdims=True) + EPS)
            xhat = hv * r
            err = xhat * g - t_ref[...]
            loss_ref[...] += 0.5 * jnp.sum(jnp.mean(err * err, axis=-1, keepdims=True))
            dy = err * (1.0 / d)
            t = dy * g
            dh = r * (t - xhat * jnp.mean(t * xhat, axis=-1, keepdims=True))
            dh_ref[...] = dh
            dhb_ref[...] = dh.astype(BF16)
            dg_ref[...] += jnp.sum(dy * xhat, axis=0, keepdims=True)

    row = pl.BlockSpec((tm, d), lambda i: (i, 0))
    vec = pl.BlockSpec((1, d), lambda i: (0, 0))
    out, _ = _call(body, name=name, grid=(m // tm,),
                   in_specs=[row, vec, pl.BlockSpec((tm, d), lambda i: (jnp.maximum(i - first, 0), 0))],
                   out_specs=[row, row, vec, pl.BlockSpec((8, LANES), lambda i: (0, 0))],
                   out_shape=[jax.ShapeDtypeStruct((m, d), F32), jax.ShapeDtypeStruct((m, d), BF16),
                              jax.ShapeDtypeStruct((1, d), F32), jax.ShapeDtypeStruct((8, LANES), F32)],
                   operands=[h, gain, target])
    return out


def _sigmoid(z):
    return 1.0 / (1.0 + jnp.exp(-z))


def _ffn_up(xn, wg, wu, name, comm=None):
    m, d = xn.shape
    nsh, c, _ = wg.shape
    tm = _tile(m, 1056, 16)

    def body(x_ref, wg_ref, wu_ref, g_ref, u_ref, a_ref):
        xv = x_ref[...]
        g = _dot(xv, wg_ref[...], NT_DIMS)
        u = _dot(xv, wu_ref[...], NT_DIMS)
        g_ref[...] = g.astype(BF16)
        u_ref[...] = u.astype(BF16)
        a_ref[...] = (g * _sigmoid(g) * u).astype(BF16)

    out = pl.BlockSpec((None, tm, c), lambda i, j: (j, i, 0))
    w = pl.BlockSpec((None, c, d), lambda i, j: (j, 0, 0))
    return _call(body, name=name, grid=(m // tm, nsh), in_specs=[pl.BlockSpec((tm, d), lambda i, j: (i, 0)), w, w],
                 out_specs=[out, out, out], out_shape=[jax.ShapeDtypeStruct((nsh, m, c), BF16)] * 3,
                 operands=[xn, wg, wu], comm=comm)


def _ffn_gate(xn, wg, name, comm=None):
    m, d = xn.shape
    nsh, c, _ = wg.shape
    tm = _tile(m, 1056, 16)

    def body(x_ref, wg_ref, g_ref):
        g_ref[...] = _dot(x_ref[...], wg_ref[...], NT_DIMS).astype(BF16)

    out, got = _call(body, name=name, grid=(m // tm, nsh),
                     in_specs=[pl.BlockSpec((tm, d), lambda i, j: (i, 0)), pl.BlockSpec((None, c, d), lambda i, j: (j, 0, 0))],
                     out_specs=[pl.BlockSpec((None, tm, c), lambda i, j: (j, i, 0))],
                     out_shape=[jax.ShapeDtypeStruct((nsh, m, c), BF16)], operands=[xn, wg], comm=comm)
    return out[0], got


def _ffn_up_given_gate(xn, wu, g, name, comm=None):
    m, d = xn.shape
    nsh, c, _ = wu.shape
    tm = _tile(m, 1056, 16)

    def body(x_ref, wu_ref, g_ref, u_ref, a_ref):
        u = _dot(x_ref[...], wu_ref[...], NT_DIMS)
        g = g_ref[...].astype(F32)
        u_ref[...] = u.astype(BF16)
        a_ref[...] = (g * _sigmoid(g) * u).astype(BF16)

    blk = pl.BlockSpec((None, tm, c), lambda i, j: (j, i, 0))
    return _call(body, name=name, grid=(m // tm, nsh),
                 in_specs=[pl.BlockSpec((tm, d), lambda i, j: (i, 0)), pl.BlockSpec((None, c, d), lambda i, j: (j, 0, 0)), blk],
                 out_specs=[blk, blk], out_shape=[jax.ShapeDtypeStruct((nsh, m, c), BF16)] * 2,
                 operands=[xn, wu, g], comm=comm)


def _ffn_contract(pairs, out_dtype, name, *, group, alpha=1.0, res=None, comm=None):
    nsh, m, c = pairs[0][0].shape
    d = pairs[0][1].shape[2]
    tm, tn = _tile(m, 1056, 16), _tile(d, 512, LANES)
    a_spec = pl.BlockSpec((group, tm, c), lambda i, j, kk: (kk, i, 0))
    b_spec = pl.BlockSpec((group, c, tn), lambda i, j, kk: (kk, 0, j))
    o_spec = pl.BlockSpec((tm, tn), lambda i, j, kk: (i, j))
    return _mm_core([((a, a_spec), (b, b_spec)) for a, b in pairs], NN_DIMS, out_dtype, name=name,
                    grid=(m // tm, d // tn, nsh // group), out_shape=(m, d), out_spec=o_spec, acc_shape=(tm, tn),
                    alpha=alpha, res=None if res is None else (res, o_spec), comm=comm)


def _ffn_bwd_act(dh, wd, g, u, name, comm=None):
    m, d = dh.shape
    nsh, c, _ = wd.shape
    tm = _tile(m, 1056, 16)

    def body(dh_ref, wd_ref, g_ref, u_ref, dg_ref, du_ref):
        dact = 0.5 * _dot(dh_ref[...], wd_ref[...], NT_DIMS)
        gv = g_ref[...].astype(F32)
        uv = u_ref[...].astype(F32)
        sig = _sigmoid(gv)
        du_ref[...] = (dact * gv * sig).astype(BF16)
        dg_ref[...] = (dact * uv * sig * (1.0 + gv * (1.0 - sig))).astype(BF16)

    blk = pl.BlockSpec((None, tm, c), lambda i, j: (j, i, 0))
    return _call(body, name=name, grid=(m // tm, nsh),
                 in_specs=[pl.BlockSpec((tm, d), lambda i, j: (i, 0)), pl.BlockSpec((None, c, d), lambda i, j: (j, 0, 0)), blk, blk],
                 out_specs=[blk, blk], out_shape=[jax.ShapeDtypeStruct((nsh, m, c), BF16)] * 2, operands=[dh, wd, g, u],
                 comm=comm)


def _ffn_dw(z, x, name, alpha=1.0, comm=None):
    nsh, m, c = z.shape
    d = x.shape[1]
    tn, tk = _tile(d, 1024, LANES), _tile(m, 2112, 16)
    return _mm_core([((z, pl.BlockSpec((None, tk, c), lambda i, j, kk: (i, kk, 0))),
                      (x, pl.BlockSpec((tk, tn), lambda i, j, kk: (kk, j))))],
                    TN_DIMS, BF16, name=name, grid=(nsh, d // tn, m // tk), out_shape=(nsh, c, d),
                    out_spec=pl.BlockSpec((None, c, tn), lambda i, j, kk: (i, 0, j)), acc_shape=(c, tn), alpha=alpha,
                    comm=comm)


def _dot3(tri, x):
    h1 = x.astype(BF16)
    r1 = x - h1.astype(F32)
    h2 = r1.astype(BF16)
    h3 = (r1 - h2.astype(F32)).astype(BF16)
    return _dot(tri, h1) + _dot(tri, h2) + _dot(tri, h3)


def _log_sigmoid(z):
    return jnp.minimum(z, 0.0) - jnp.log(1.0 + jnp.exp(-jnp.abs(z)))


def _triangle(t, cmp):
    return cmp(lax.broadcasted_iota(jnp.int32, (t, t), 0), lax.broadcasted_iota(jnp.int32, (t, t), 1)).astype(BF16)


def _forget_cumsum(fl, bias, n_heads, pad, name):
    m = fl.shape[0]
    nb = m // ROW_BLOCK

    def body(fl_ref, b_ref, c_ref):
        tri = _triangle(ROW_BLOCK, lambda r, c: r >= c)
        lane_ok = lax.broadcasted_iota(jnp.int32, (ROW_BLOCK, LANES), 1) < n_heads
        rows = lax.broadcasted_iota(jnp.int32, (ROW_BLOCK, LANES), 0)

        def step(b, carry):
            off = pl.multiple_of(b * ROW_BLOCK, ROW_BLOCK)
            lf = _log_sigmoid(fl_ref[pl.ds(off, ROW_BLOCK), :] + b_ref[...])
            lf = jnp.where(lane_ok & (rows + off >= pad), lf, 0.0)
            cs = _dot3(tri, lf) + carry
            c_ref[pl.ds(off, ROW_BLOCK), :] = cs
            return cs[ROW_BLOCK - 1:ROW_BLOCK, :]

        lax.fori_loop(0, nb, step, jnp.zeros((1, LANES), F32))

    vmem = pl.BlockSpec(memory_space=pltpu.VMEM)
    return pl.pallas_call(
        body, name=name, out_shape=jax.ShapeDtypeStruct((m, LANES), F32), in_specs=[vmem, vmem], out_specs=vmem,
        compiler_params=pltpu.CompilerParams(vmem_limit_bytes=VMEM_LIMIT_BYTES),
    )(fl, bias)


def _forget_cumsum_bwd(dc_a, dc_b, fl, bias, n_heads, pad, name):
    m = fl.shape[0]
    nb = m // ROW_BLOCK

    def body(da_ref, db_ref, fl_ref, b_ref, dfl_ref, dbias_ref):
        tri = _triangle(ROW_BLOCK, lambda r, c: r <= c)
        lane_ok = lax.broadcasted_iota(jnp.int32, (ROW_BLOCK, LANES), 1) < n_heads
        rows = lax.broadcasted_iota(jnp.int32, (ROW_BLOCK, LANES), 0)

        def step(bb, carry):
            tail, dbias = carry
            off = pl.multiple_of((nb - 1 - bb) * ROW_BLOCK, ROW_BLOCK)
            dc = da_ref[pl.ds(off, ROW_BLOCK), :] + db_ref[pl.ds(off, ROW_BLOCK), :]
            dlf = _dot3(tri, dc) + tail
            z = fl_ref[pl.ds(off, ROW_BLOCK), :] + b_ref[...]
            dfl = jnp.where(lane_ok & (rows + off >= pad), dlf * _sigmoid(-z), 0.0)
            dfl_ref[pl.ds(off, ROW_BLOCK), :] = dfl
            return dlf[0:1, :], dbias + jnp.sum(dfl, axis=0, keepdims=True)

        zero = jnp.zeros((1, LANES), F32)
        _, dbias = lax.fori_loop(0, nb, step, (zero, zero))
        dbias_ref[...] = dbias

    vmem = pl.BlockSpec(memory_space=pltpu.VMEM)
    return pl.pallas_call(
        body, name=name,
        out_shape=[jax.ShapeDtypeStruct((m, LANES), F32), jax.ShapeDtypeStruct((1, LANES), F32)],
        in_specs=[vmem] * 4, out_specs=[vmem, vmem],
        compiler_params=pltpu.CompilerParams(vmem_limit_bytes=VMEM_LIMIT_BYTES),
    )(dc_a, dc_b, fl, bias)


def _attn_block(m):
    return 3 * ROW_BLOCK if m % (3 * ROW_BLOCK) == 0 else ROW_BLOCK


def _block(ref, j, t):
    return ref[pl.ds(pl.multiple_of(j * t, t), t), :]


def _head_norm(o, gain):
    r = lax.rsqrt(jnp.mean(o * o, axis=-1, keepdims=True) + EPS)
    return o * r * gain


def _head_norm_bwd(o, d_on, gain):
    r = lax.rsqrt(jnp.mean(o * o, axis=-1, keepdims=True) + EPS)
    ohat = o * r
    t = d_on * gain
    d_o = r * (t - ohat * jnp.mean(t * ohat, axis=-1, keepdims=True))
    return d_o, jnp.sum(d_on * ohat, axis=0, keepdims=True)


def _qkv_specs(t, m, h, first_col_block):
    q = pl.BlockSpec((t, HEAD_DIM), lambda hd, i: (i, first_col_block + hd))
    k = pl.BlockSpec((m, HEAD_DIM), lambda hd, i: (0, first_col_block + h + hd))
    v = pl.BlockSpec((m, HEAD_DIM), lambda hd, i: (0, first_col_block + 2 * h + hd))
    return q, k, v


def _fox_fwd(qkv, ccol, crow, gain, n_heads, pad, name, comm=None):
    m = qkv.shape[0]
    t = _attn_block(m)
    nq = m // t
    scale = HEAD_DIM ** -0.5
    hw = n_heads * HEAD_DIM

    def body(q_ref, k_ref, v_ref, ccol_ref, crow_ref, g_ref, o_ref, on_ref, lse_ref):
        i = pl.program_id(1)
        q = q_ref[...]
        ci = ccol_ref[...]
        qpos = i * t + lax.broadcasted_iota(jnp.int32, (t, 1), 0)

        def step(j, j_prev, j_next, state, masked):
            mx, l, acc_scaled, p_prev, s_cur = state
            acc = acc_scaled + _dot(p_prev, _block(v_ref, j_prev, t))
            s_next = _dot(q, _block(k_ref, j_next, t), NT_DIMS)
            s = s_cur * scale + ci - crow_ref[j]
            if masked:
                kpos = j * t + lax.broadcasted_iota(jnp.int32, (1, t), 1)
                s = jnp.where((kpos <= qpos) & (kpos >= pad), s, NEG)
            mx_new = jnp.maximum(mx, jnp.max(s, axis=-1, keepdims=True))
            p = jnp.exp(s - mx_new)
            a = jnp.exp(mx - mx_new)
            return mx_new, a * l + jnp.sum(p, axis=-1, keepdims=True), a * acc, p.astype(BF16), s_next

        state = (jnp.full((t, 1), NEG, F32), jnp.zeros((t, 1), F32), jnp.zeros((t, HEAD_DIM), F32),
                 jnp.zeros((t, t), BF16), _dot(q, _block(k_ref, 0, t), NT_DIMS))
        state = step(0, 0, jnp.minimum(i, 1), state, True)
        state = lax.fori_loop(1, i, lambda j, st: step(j, j - 1, j + 1, st, False), state)
        mx, l, acc_scaled, p_last, _ = lax.fori_loop(0, jnp.minimum(i, 1), lambda _, st: step(i, i - 1, i, st, True), state)
        acc = acc_scaled + _dot(p_last, _block(v_ref, i, t))
        valid = qpos >= pad
        o = jnp.where(valid, acc / l, 0.0)
        o_ref[...] = o
        on_ref[...] = _head_norm(o, g_ref[...]).astype(BF16)
        lse_ref[...] = jnp.where(valid, mx + jnp.log(l), 0.0)

    q_spec, k_spec, v_spec = _qkv_specs(t, m, n_heads, 0)
    col = pl.BlockSpec((None, t, 1), lambda hd, i: (hd, i, 0))
    head = pl.BlockSpec((t, HEAD_DIM), lambda hd, i: (i, hd))
    return _call(body, name=name, grid=(n_heads, nq),
                 in_specs=[q_spec, k_spec, v_spec, col, pl.BlockSpec((None, nq, 1, t), lambda hd, i: (hd, 0, 0, 0)),
                           pl.BlockSpec((1, HEAD_DIM), lambda hd, i: (0, hd))],
                 out_specs=[head, head, col],
                 out_shape=[jax.ShapeDtypeStruct((m, hw), F32), jax.ShapeDtypeStruct((m, hw), BF16),
                            jax.ShapeDtypeStruct((n_heads, m, 1), F32)],
                 operands=[qkv, qkv, qkv, ccol, crow, gain], comm=comm)


def _fox_bwd(qkv, o, d_on, gain, lse, ccol, crow, n_heads, pad, name, comm=None):
    m = qkv.shape[0]
    t = _attn_block(m)
    nq = m // t
    scale = HEAD_DIM ** -0.5
    hw = n_heads * HEAD_DIM

    def body(q_ref, k_ref, v_ref, o_ref, don_ref, g_ref, lse_ref, ccol_ref, crow_ref,
             dq_ref, dk_ref, dv_ref, dg_ref, dccol_ref, dcrow_ref):
        i = pl.program_id(1)

        @pl.when(i == 0)
        def _():
            dk_ref[...] = jnp.zeros_like(dk_ref)
            dv_ref[...] = jnp.zeros_like(dv_ref)
            dg_ref[...] = jnp.zeros_like(dg_ref)
            dcrow_ref[...] = jnp.zeros_like(dcrow_ref)

        q = q_ref[...]
        o = o_ref[...]
        d_o, dgain = _head_norm_bwd(o, don_ref[...], g_ref[...])
        dg_ref[...] += dgain
        delta = jnp.sum(d_o * o, axis=-1, keepdims=True)
        d_ob = d_o.astype(BF16)
        ci = ccol_ref[...]
        lse_i = lse_ref[...]
        qpos = i * t + lax.broadcasted_iota(jnp.int32, (t, 1), 0)

        def step(j, carry, masked):
            dq, dci = carry
            k = _block(k_ref, j, t)
            v = _block(v_ref, j, t)
            off = pl.multiple_of(j * t, t)
            s = _dot(q, k, NT_DIMS) * scale + ci - crow_ref[j]
            if masked:
                kpos = off + lax.broadcasted_iota(jnp.int32, (1, t), 1)
                ok = (kpos <= qpos) & (kpos >= pad)
                p = jnp.where(ok, jnp.exp(jnp.where(ok, s - lse_i, 0.0)), 0.0)
            else:
                p = jnp.exp(s - lse_i)
            ds = p * (_dot(d_ob, v, NT_DIMS) - delta)
            dsb = ds.astype(BF16)
            dk_ref[pl.ds(off, t), :] += _dot(dsb, q, TN_DIMS) * scale
            dv_ref[pl.ds(off, t), :] += _dot(p.astype(BF16), d_ob, TN_DIMS)
            dcrow_ref[j] -= jnp.sum(ds, axis=0, keepdims=True)
            return dq + _dot(dsb, k), dci + jnp.sum(ds, axis=-1, keepdims=True)

        carry = step(0, (jnp.zeros((t, HEAD_DIM), F32), jnp.zeros((t, 1), F32)), True)
        n_mid = jnp.maximum(i - 1, 0)
        carry = lax.fori_loop(0, n_mid // 2, lambda jp, c: step(2 * jp + 2, step(2 * jp + 1, c, False), False), carry)
        carry = lax.fori_loop(0, n_mid % 2, lambda _, c: step(i - 1, c, False), carry)
        dq, dci = lax.fori_loop(0, jnp.minimum(i, 1), lambda _, c: step(i, c, True), carry)
        dq_ref[...] = (dq * scale).astype(BF16)
        dccol_ref[...] = dci

    q_spec, k_spec, v_spec = _qkv_specs(t, m, n_heads, 0)
    col = pl.BlockSpec((None, t, 1), lambda hd, i: (hd, i, 0))
    rowc = pl.BlockSpec((None, nq, 1, t), lambda hd, i: (hd, 0, 0, 0))
    head = pl.BlockSpec((t, HEAD_DIM), lambda hd, i: (i, hd))
    whole = pl.BlockSpec((m, HEAD_DIM), lambda hd, i: (0, hd))
    gvec = pl.BlockSpec((1, HEAD_DIM), lambda hd, i: (0, hd))
    return _call(body, name=name, grid=(n_heads, nq),
                 in_specs=[q_spec, k_spec, v_spec, head, head, gvec, col, col, rowc],
                 out_specs=[head, whole, whole, gvec, col, rowc],
                 out_shape=[jax.ShapeDtypeStruct((m, hw), BF16), jax.ShapeDtypeStruct((m, hw), F32),
                            jax.ShapeDtypeStruct((m, hw), F32), jax.ShapeDtypeStruct((1, hw), F32),
                            jax.ShapeDtypeStruct((n_heads, m, 1), F32), jax.ShapeDtypeStruct((n_heads, nq, 1, t), F32)],
                 operands=[qkv, qkv, qkv, o, d_on, gain, lse, ccol, crow], comm=comm)


def _sb_scores(z):
    lp = jnp.log(1.0 + jnp.exp(-jnp.abs(z)))
    return jnp.minimum(z, 0.0) - lp, jnp.minimum(-z, 0.0) - lp


def _sb_fwd(qkv, gain, n_heads, pad, name, comm=None):
    m = qkv.shape[0]
    t = _attn_block(m)
    nq = m // t
    assert nq <= LANES
    scale = HEAD_DIM ** -0.5
    hw = n_heads * HEAD_DIM

    def body(q_ref, k_ref, v_ref, g_ref, after_ref, o_ref, on_ref, run_ref):
        i = pl.program_id(1)
        q = q_ref[...]
        qpos = i * t + lax.broadcasted_iota(jnp.int32, (t, 1), 0)
        after = after_ref[...]
        lane = lax.broadcasted_iota(jnp.int32, (t, LANES), 1)

        def step(j, j_prev, j_next, state, masked):
            run, acc, a_prev, s_cur = state
            acc = acc + _dot(a_prev, _block(v_ref, j_prev, t))
            s_next = _dot(q, _block(k_ref, j_next, t), NT_DIMS)
            ls_pos, log_1m = _sb_scores(s_cur * scale)
            if masked:
                kpos = j * t + lax.broadcasted_iota(jnp.int32, (1, t), 1)
                ok = (kpos < qpos) & (kpos >= pad)
                log_1m = jnp.where(ok, log_1m, 0.0)
            a = jnp.exp(ls_pos + _dot_split(log_1m, after) + run)
            if masked:
                a = jnp.where(ok, a, 0.0)
            run_ref[...] = jnp.where(lane == j, run, run_ref[...])
            return run + jnp.sum(log_1m, axis=-1, keepdims=True), acc, a.astype(BF16), s_next

        run_ref[...] = jnp.zeros_like(run_ref)
        state = (jnp.zeros((t, 1), F32), jnp.zeros((t, HEAD_DIM), F32), jnp.zeros((t, t), BF16),
                 _dot(q, _block(k_ref, i, t), NT_DIMS))
        state = step(i, i, jnp.maximum(i - 1, 0), state, True)
        state = lax.fori_loop(1, i, lambda jj, st: step(i - jj, i - jj + 1, i - jj - 1, st, False), state)
        _, acc, a_last, _ = lax.fori_loop(0, jnp.minimum(i, 1), lambda _, st: step(0, 1, 0, st, True), state)
        o = acc + _dot(a_last, _block(v_ref, 0, t))
        o_ref[...] = o
        on_ref[...] = _head_norm(o, g_ref[...]).astype(BF16)

    q_spec, k_spec, v_spec = _qkv_specs(t, m, n_heads, 3 * n_heads)
    head = pl.BlockSpec((t, HEAD_DIM), lambda hd, i: (i, hd))
    return _call(body, name=name, grid=(n_heads, nq),
                 in_specs=[q_spec, k_spec, v_spec, pl.BlockSpec((1, HEAD_DIM), lambda hd, i: (0, hd)),
                           pl.BlockSpec((t, t), lambda hd, i: (0, 0))],
                 out_specs=[head, head, pl.BlockSpec((None, t, LANES), lambda hd, i: (hd, i, 0))],
                 out_shape=[jax.ShapeDtypeStruct((m, hw), F32), jax.ShapeDtypeStruct((m, hw), BF16),
                            jax.ShapeDtypeStruct((n_heads, m, LANES), F32)],
                 operands=[qkv, qkv, qkv, gain, _triangle(t, lambda r, c: r > c)], comm=comm)


def _sb_bwd(qkv, o, d_on, gain, runs, n_heads, pad, name, comm=None):
    m = qkv.shape[0]
    t = _attn_block(m)
    nq = m // t
    scale = HEAD_DIM ** -0.5
    hw = n_heads * HEAD_DIM

    def body(q_ref, k_ref, v_ref, o_ref, don_ref, g_ref, run_ref, after_ref, before_ref, dq_ref, dk_ref, dv_ref, dg_ref):
        i = pl.program_id(1)

        @pl.when(i == 0)
        def _():
            dk_ref[...] = jnp.zeros_like(dk_ref)
            dv_ref[...] = jnp.zeros_like(dv_ref)
            dg_ref[...] = jnp.zeros_like(dg_ref)

        q = q_ref[...]
        d_o, dgain = _head_norm_bwd(o_ref[...], don_ref[...], g_ref[...])
        dg_ref[...] += dgain
        d_ob = d_o.astype(BF16)
        runs_i = run_ref[...]
        qpos = i * t + lax.broadcasted_iota(jnp.int32, (t, 1), 0)
        after = after_ref[...]
        before = before_ref[...]
        lane = lax.broadcasted_iota(jnp.int32, (t, LANES), 1)

        def step(j, carry, masked):
            g_run, dq = carry
            k = _block(k_ref, j, t)
            v = _block(v_ref, j, t)
            off = pl.multiple_of(j * t, t)
            ls_pos, ls_neg = _sb_scores(_dot(q, k, NT_DIMS) * scale)
            log_1m = ls_neg
            if masked:
                kpos = off + lax.broadcasted_iota(jnp.int32, (1, t), 1)
                ok = (kpos < qpos) & (kpos >= pad)
                log_1m = jnp.where(ok, ls_neg, 0.0)
            run = jnp.sum(jnp.where(lane == j, runs_i, 0.0), axis=-1, keepdims=True)
            a = jnp.exp(ls_pos + _dot_split(log_1m, after) + run)
            if masked:
                a = jnp.where(ok, a, 0.0)
            g = a * _dot(d_ob, v, NT_DIMS)
            prefix = _dot(g.astype(BF16), before) + g_run
            dz = g * jnp.exp(ls_neg) - jnp.exp(ls_pos) * prefix
            if masked:
                dz = jnp.where(ok, dz, 0.0)
            dzb = dz.astype(BF16)
            dk_ref[pl.ds(off, t), :] += _dot(dzb, q, TN_DIMS) * scale
            dv_ref[pl.ds(off, t), :] += _dot(a.astype(BF16), d_ob, TN_DIMS)
            return g_run + jnp.sum(g, axis=-1, keepdims=True), dq + _dot(dzb, k)

        carry = step(0, (jnp.zeros((t, 1), F32), jnp.zeros((t, HEAD_DIM), F32)), True)
        n_mid = jnp.maximum(i - 1, 0)
        carry = lax.fori_loop(0, n_mid // 2, lambda jp, c: step(2 * jp + 2, step(2 * jp + 1, c, False), False), carry)
        carry = lax.fori_loop(0, n_mid % 2, lambda _, c: step(i - 1, c, False), carry)
        _, dq = lax.fori_loop(0, jnp.minimum(i, 1), lambda _, c: step(i, c, True), carry)
        dq_ref[...] = (dq * scale).astype(BF16)

    q_spec, k_spec, v_spec = _qkv_specs(t, m, n_heads, 3 * n_heads)
    head = pl.BlockSpec((t, HEAD_DIM), lambda hd, i: (i, hd))
    whole = pl.BlockSpec((m, HEAD_DIM), lambda hd, i: (0, hd))
    gvec = pl.BlockSpec((1, HEAD_DIM), lambda hd, i: (0, hd))
    tri = pl.BlockSpec((t, t), lambda hd, i: (0, 0))
    return _call(body, name=name, grid=(n_heads, nq),
                 in_specs=[q_spec, k_spec, v_spec, head, head, gvec, pl.BlockSpec((None, t, LANES), lambda hd, i: (hd, i, 0)),
                           tri, tri],
                 out_specs=[head, whole, whole, gvec],
                 out_shape=[jax.ShapeDtypeStruct((m, hw), BF16), jax.ShapeDtypeStruct((m, hw), F32),
                            jax.ShapeDtypeStruct((m, hw), F32), jax.ShapeDtypeStruct((1, hw), F32)],
                 operands=[qkv, qkv, qkv, o, d_on, gain, runs, _triangle(t, lambda r, c: r > c), _triangle(t, lambda r, c: r < c)],
                 comm=comm)


def _adamw(parts, w, m1, v2, name, comm=None):
    nl, r, c = w.shape
    assert len(parts) == nl
    n_parts = parts[0].shape[0]
    block_elems = 256 * 1024
    if r % 8 == 0 or c % LANES != 0:
        tr, tc = _tile(r, max(8, block_elems // (-(-c // LANES) * LANES)), 8), c
    else:
        tr, tc = r, _tile(c, max(LANES, block_elems // r // LANES * LANES), LANES)
    nr, nc = r // tr, c // tc
    bias1 = 1.0 / (1.0 - ADAM_B1 ** ADAM_STEP)
    bias2 = 1.0 / (1.0 - ADAM_B2 ** ADAM_STEP)

    def body(*refs):
        p_refs = refs[:nl]
        w_ref, m_ref, v_ref, g_ref, d_ref, nm_ref, nv_ref = refs[nl:]

        def update(p_ref):
            g = p_ref[0].astype(F32)
            for s in range(1, n_parts):
                g = g + p_ref[s].astype(F32)
            m_new = ADAM_B1 * m_ref[...] + (1.0 - ADAM_B1) * g
            v_new = ADAM_B2 * v_ref[...] + (1.0 - ADAM_B2) * (g * g)
            g_ref[...] = g
            nm_ref[...] = m_new
            nv_ref[...] = v_new
            d_ref[...] = -ADAM_LR * ((m_new * bias1) / (jnp.sqrt(v_new * bias2) + ADAM_EPS) + ADAM_WD * w_ref[...])

        for ll in range(nl):
            @pl.when(pl.program_id(0) == ll)
            def _(ll=ll):
                update(p_refs[ll])

    def part_spec(ll):
        def index(l, i, j):
            pin = jnp.where(l < ll, 0, 1)
            return 0, jnp.where(l == ll, i, pin * (nr - 1)), jnp.where(l == ll, j, pin * (nc - 1))
        return pl.BlockSpec((n_parts, tr, tc), index)

    blk = pl.BlockSpec((None, tr, tc), lambda l, i, j: (l, i, j))
    out, got = _call(body, name=name, grid=(nl, nr, nc), in_specs=[part_spec(ll) for ll in range(nl)] + [blk, blk, blk],
                     out_specs=[blk] * 4, out_shape=[jax.ShapeDtypeStruct((nl, r, c), F32)] * 4,
                     operands=list(parts) + [w, m1, v2], comm=comm)
    return out if comm is None else (out, got)


SMALL_WEIGHTS = ("ffn1_norm", "mix_norm", "b_forget", "g_fox", "g_sb", "ffn2_norm", "final_norm")
WEIGHT_ORDER = ("meta_tokens", "ffn1_norm", "ffn1_w_gate", "ffn1_w_up", "ffn1_w_down", "mix_norm", "w_in", "b_forget",
                "g_fox", "g_sb", "w_out", "ffn2_norm", "ffn2_w_gate", "ffn2_w_up", "ffn2_w_down", "final_norm")
GROUPS = {"ffn1": ("ffn1_w_gate", "ffn1_w_up", "ffn1_w_down"), "mix": ("w_in", "w_out"),
          "ffn2": ("ffn2_w_gate", "ffn2_w_up", "ffn2_w_down")}
TRANSPOSED = ("ffn1_w_gate", "ffn1_w_up", "ffn2_w_gate", "ffn2_w_up")


def _pad_lanes(a):
    extra = (-a.shape[-1]) % LANES
    return a if extra == 0 else jnp.pad(a, [(0, 0)] * (a.ndim - 1) + [(0, extra)])


def _ffn_backward(dh_b, dh, saved, gain, wg, wu, wd, tag, carried=None):
    h, xn, g, u, act = saved
    (dg, du), _ = _ffn_bwd_act(dh_b, wd, g, u, f"{tag}_bwd_act")
    d_wd = _ffn_dw(act, dh_b, f"{tag}_dwd", alpha=0.5)
    d_wg = _ffn_dw(dg, xn, f"{tag}_dwg")
    d_wu = _ffn_dw(du, xn, f"{tag}_dwu")
    if carried:
        dxn, got = _ffn_contract([(dg, wg), (du, wu)], F32, f"{tag}_dxn", group=2, comm=_Exchange(carried))
    else:
        dxn, got = _ffn_contract([(dg, wg), (du, wu)], F32, f"{tag}_dxn", group=2), []
    dh_in, dh_in_b, d_gain = _rms_bwd(dxn, h, gain, dh, f"{tag}_norm_bwd")
    return dh_in, dh_in_b, d_gain, d_wg, d_wu, d_wd, got


def _ffn_backward_last(dh_b, dh, saved, gain, wg, wu, wd, tag, first, second):
    h, xn, g, u, act = saved
    (dg, du), got_first = _ffn_bwd_act(dh_b, wd, g, u, f"{tag}_bwd_act", first)
    d_wd, got_second = _ffn_dw(act, dh_b, f"{tag}_dwd", alpha=0.5, comm=second)
    c = d_wd.shape[1]
    half = c // 2 // 16 * 16
    d_wg, got_wd = _ffn_dw(dg, xn, f"{tag}_dwg", comm=_Exchange([d_wd], rows=(0, half)))
    d_wu, got_wd = _ffn_dw(du, xn, f"{tag}_dwu", comm=_Exchange([d_wd], rows=(half, c - half), into=got_wd))
    dxn, got_wg = _ffn_contract([(dg, wg), (du, wu)], F32, f"{tag}_dxn", group=2, comm=_Exchange([d_wg]))
    dh_in, dh_in_b, d_gain = _rms_bwd(dxn, h, gain, dh, f"{tag}_norm_bwd")
    return dh_in, dh_in_b, d_gain, got_first, got_second, got_wg[0], got_wd[0], d_wu


def _mixer_forward(h, gain, w_in_t, bias, g_fox, g_sb, w_out, n_heads, pad, tag, comm_fox=None, comm_sb=None):
    m = h.shape[0]
    t = _attn_block(m)
    hw = n_heads * HEAD_DIM
    xn = _rms_fwd(h, gain, f"{tag}_norm")
    qkv = _mm([(xn, w_in_t[:6 * hw])], "nt", BF16, name=f"{tag}_qkv")
    fl = _mm([(xn, w_in_t[6 * hw:])], "nt", F32, name=f"{tag}_forget")
    c = _forget_cumsum(fl, bias, n_heads, pad, f"{tag}_cumsum")
    c_heads = c[:, :n_heads].T
    ccol = c_heads[:, :, None]
    crow = c_heads.reshape(n_heads, m // t, 1, t)
    (o_f, on_f, lse), got_fox = _fox_fwd(qkv, ccol, crow, g_fox, n_heads, pad, f"{tag}_fox", comm_fox)
    (o_s, on_s, runs), got_sb = _sb_fwd(qkv, g_sb, n_heads, pad, f"{tag}_sb", comm_sb)
    if w_out is None:
        w_out = got_sb[-1].reshape(-1, h.shape[1])
    h_out = _mm([(on_f, w_out[:hw]), (on_s, w_out[hw:])], "nn", F32, name=f"{tag}_out", res=h)
    return h_out, (h, xn, qkv, fl, ccol, crow, o_f, on_f, lse, o_s, on_s, runs), got_fox, got_sb, w_out


def _mixer_backward(dh_b, dh_out, saved, gain, w_in_t, bias, g_fox, g_sb, w_out, n_heads, pad, tag, comm_fox=None, comm_sb=None,
                    rows_in_dxn=0):
    h, xn, qkv, fl, ccol, crow, o_f, on_f, lse, o_s, on_s, runs = saved
    m = h.shape[0]
    hw = n_heads * HEAD_DIM
    d_on_f = _mm([(dh_b, w_out[:hw])], "nt", F32, name=f"{tag}_don_f")
    d_on_s = _mm([(dh_b, w_out[hw:])], "nt", F32, name=f"{tag}_don_s")
    d_wout = jnp.concatenate([_mm([(on_f, dh_b)], "tn", BF16, name=f"{tag}_dwout_f"),
                              _mm([(on_s, dh_b)], "tn", BF16, name=f"{tag}_dwout_s")], axis=0)
    (dq_f, dk_f, dv_f, dg_fox, dccol, dcrow), got_fox = _fox_bwd(
        qkv, o_f, d_on_f, g_fox, lse, ccol, crow, n_heads, pad, f"{tag}_fox_bwd", comm_fox)
    (dq_s, dk_s, dv_s, dg_sb), got_sb = _sb_bwd(qkv, o_s, d_on_s, g_sb, runs, n_heads, pad, f"{tag}_sb_bwd", comm_sb)
    dc_a = _pad_lanes(dccol[:, :, 0].T)
    dc_b = _pad_lanes(dcrow.reshape(n_heads, m).T)
    dfl, dbias = _forget_cumsum_bwd(dc_a, dc_b, fl, bias, n_heads, pad, f"{tag}_cumsum_bwd")
    dproj = jnp.concatenate([dq_f, dk_f.astype(BF16), dv_f.astype(BF16), dq_s, dk_s.astype(BF16), dv_s.astype(BF16),
                             dfl.astype(BF16)], axis=1)
    d_win = _mm([(dproj, xn)], "tn", BF16, name=f"{tag}_dwin")[:6 * hw + n_heads].reshape(N_DEV, -1, h.shape[1])
    got_win = None
    if rows_in_dxn:
        dxn, got_win = _mm([(dproj, w_in_t)], "nn", F32, name=f"{tag}_dxn", whole_k=True,
                           comm=_Exchange([d_win], rows=(0, rows_in_dxn)))
    else:
        dxn = _mm([(dproj, w_in_t)], "nn", F32, name=f"{tag}_dxn", whole_k=True)
    dh, dh_in_b, d_gain = _rms_bwd(dxn, h, gain, dh_out, f"{tag}_norm_bwd")
    return dh, dh_in_b, d_gain, d_win, dbias, dg_fox, dg_sb, d_wout, got_fox, got_sb, got_win


def kernel(x, meta_tokens, ffn1_norm, ffn1_w_gate, ffn1_w_up, ffn1_w_down, mix_norm, w_in, b_forget, g_fox, g_sb, w_out, ffn2_norm, ffn2_w_gate, ffn2_w_up, ffn2_w_down, final_norm, loss_target, m_meta_tokens, m_ffn1_norm, m_ffn1_w_gate, m_ffn1_w_up, m_ffn1_w_down, m_mix_norm, m_w_in, m_b_forget, m_g_fox, m_g_sb, m_w_out, m_ffn2_norm, m_ffn2_w_gate, m_ffn2_w_up, m_ffn2_w_down, m_final_norm, v_meta_tokens, v_ffn1_norm, v_ffn1_w_gate, v_ffn1_w_up, v_ffn1_w_down, v_mix_norm, v_w_in, v_b_forget, v_g_fox, v_g_sb, v_w_out, v_ffn2_norm, v_ffn2_w_gate, v_ffn2_w_up, v_ffn2_w_down, v_final_norm):
    given = dict(locals())
    seq, d = x.shape[1], x.shape[2]
    depth = ffn1_norm.shape[0]
    d_in = N_DEV * w_in.shape[2]
    n_heads = g_fox.shape[1] // HEAD_DIM
    hw = n_heads * HEAD_DIM
    assert seq % ROW_BLOCK == 0 and d_in == 6 * hw + n_heads and n_heads <= LANES
    pad = (-(seq + N_META)) % ROW_BLOCK
    x_off = pad + N_META

    def view(n, a):
        if n in TRANSPOSED:
            return jnp.swapaxes(a, 1, 2)
        return a.transpose(2, 0, 1) if n == "w_in" else a

    def unview(n, a):
        if n in TRANSPOSED:
            return jnp.swapaxes(a, 1, 2)
        return a.transpose(1, 2, 0) if n == "w_in" else a

    def shard(n, l):
        v = view(n, given[n])
        return (v[:, l] if n == "w_in" else v[l]).astype(BF16)

    def shards(group, l):
        return [shard(n, l) for n in GROUPS[group]]

    sh = shards("ffn1", 0)
    first = _run_alone(_Gather([sh[0], meta_tokens]), "gather_first")
    full = {}
    meta_full = first[1].transpose(1, 0, 2).reshape(N_META, d)
    h = jnp.concatenate([jnp.zeros((pad, d), F32), meta_full, x[0]], axis=0)
    weights, saved = [], []
    for l in range(depth):
        xn = _rms_fwd(h, ffn1_norm[l:l + 1], "ffn1_norm")
        if l == 0:
            mix0 = shards("mix", 0)
            g, (wu1,) = _ffn_gate(xn, first[0], "ffn1_gate", _Gather([sh[1]]))
            (u, act), (wd1,) = _ffn_up_given_gate(xn, wu1, g, "ffn1_up", _Gather([sh[2]]))
            full[("ffn1", 0)] = (first[0], wu1, wd1)
            h_out, (win3,) = _ffn_contract([(act, wd1)], F32, "ffn1_down", group=4, alpha=0.5, res=h, comm=_Gather(mix0[:1]))
        else:
            wg1, wu1, wd1 = full[("ffn1", l)]
            (g, u, act), full[("mix", l)] = _ffn_up(xn, wg1, wu1, "ffn1_up", _Gather(shards("mix", l)))
            h_out = _ffn_contract([(act, wd1)], F32, "ffn1_down", group=4, alpha=0.5, res=h)
        s1 = (h, xn, g, u, act)
        h = h_out
        if l > 0:
            win3, wout3 = full[("mix", l)]
        w_in_t = jnp.pad(win3.reshape(d_in, d), ((0, 6 * hw + LANES - d_in), (0, 0)))
        bias = _pad_lanes(b_forget[l:l + 1])
        in_sb = (shards("ffn1", l + 1) if l + 1 < depth else []) + (mix0[1:] if l == 0 else [])
        h, sm, full[("ffn2", l)], got, w_out_full = _mixer_forward(
            h, mix_norm[l:l + 1], w_in_t, bias, g_fox[l:l + 1], g_sb[l:l + 1],
            None if l == 0 else wout3.reshape(N_DEV * wout3.shape[1], d), n_heads, pad, "mix",
            _Gather(shards("ffn2", l)), _Gather(in_sb) if in_sb else None)
        if l + 1 < depth:
            full[("ffn1", l + 1)] = got[:3]
        wg2, wu2, wd2 = full[("ffn2", l)]
        xn = _rms_fwd(h, ffn2_norm[l:l + 1], "ffn2_norm")
        (g, u, act), _ = _ffn_up(xn, wg2, wu2, "ffn2_up")
        s2 = (h, xn, g, u, act)
        h = _ffn_contract([(act, wd2)], F32, "ffn2_down", group=4, alpha=0.5, res=h)
        weights.append((w_in_t, w_out_full, bias))
        saved.append((s1, sm, s2))

    dh, dh_b, d_final, loss_arr = _loss_head(h, final_norm[None, :], loss_target[0], x_off, "loss_head")
    small = {n: [None] * depth for n in SMALL_WEIGHTS[:-1]}
    partial, received = {}, {}

    def names(group, l):
        return [(n, l) for n in GROUPS[group]]

    def send(keys):
        return [partial[k] for k in keys]

    for l in reversed(range(depth)):
        w_in_t, w_out_full, bias = weights[l]
        s1, sm, s2 = saved[l]
        wg2, wu2, wd2 = full[("ffn2", l)]
        up = l + 1 < depth
        in_dxn = [("ffn1_w_gate", l + 1)] if up else []
        in_fox = names("mix", l + 1) + [("ffn1_w_up", l + 1)] if up else []
        in_sb = names("ffn2", l) + ([("ffn1_w_down", l + 1)] if up else [])
        dh, dh_b, small["ffn2_norm"][l], partial[("ffn2_w_gate", l)], partial[("ffn2_w_up", l)], partial[("ffn2_w_down", l)], got = (
            _ffn_backward(dh_b, dh, s2, ffn2_norm[l:l + 1], wg2, wu2, wd2, "ffn2", send(in_dxn)))
        received.update(zip(in_dxn, got))
        win_rows = (d_in // N_DEV // 2 // 16 * 16) if l == 0 else 0
        (dh, dh_b, small["mix_norm"][l], partial[("w_in", l)], dbias, small["g_fox"][l], small["g_sb"][l], d_wout,
         got_fox, got_sb, got_win) = _mixer_backward(
            dh_b, dh, sm, mix_norm[l:l + 1], w_in_t, bias, g_fox[l:l + 1], g_sb[l:l + 1], w_out_full, n_heads, pad, "mix",
            _Exchange(send(in_fox)) if in_fox else None, _Exchange(send(in_sb)), win_rows)
        received.update(zip(in_fox, got_fox or []))
        received.update(zip(in_sb, got_sb))
        partial[("w_out", l)] = d_wout.reshape(N_DEV, -1, d)
        small["b_forget"][l] = dbias[:, :n_heads]
        wg1, wu1, wd1 = full[("ffn1", l)]
        if l > 0:
            dh, dh_b, small["ffn1_norm"][l], partial[("ffn1_w_gate", l)], partial[("ffn1_w_up", l)], partial[("ffn1_w_down", l)], _ = (
                _ffn_backward(dh_b, dh, s1, ffn1_norm[l:l + 1], wg1, wu1, wd1, "ffn1"))
        else:
            rest = (win_rows, d_in // N_DEV - win_rows)
            (dh, dh_b, small["ffn1_norm"][0], got_a, got_b, received[("ffn1_w_gate", 0)], received[("ffn1_w_down", 0)],
             last_grad) = _ffn_backward_last(
                dh_b, dh, s1, ffn1_norm[0:1], wg1, wu1, wd1, "ffn1", _Exchange([partial[("w_out", 0)]]),
                _Exchange([partial[("w_in", 0)]], rows=rest, into=got_win))
            received[("w_out", 0)], received[("w_in", 0)] = got_a[0], got_b[0]
    grad_x = dh[x_off:][None]
    d_meta = dh[pad:x_off].reshape(N_META, N_DEV, -1).transpose(1, 0, 2)
    received[("meta_tokens", 0)] = _run_alone(_Exchange([d_meta]), "exchange_meta")[0]

    vec = [loss_arr[0:1, :]] + [_pad_lanes(jnp.concatenate(small[n], axis=0).reshape(1, -1)) for n in SMALL_WEIGHTS[:-1]]
    vec.append(d_final)
    sizes = [a.shape[1] for a in vec]
    summed = _all_reduce_small(jnp.concatenate(vec, axis=1), "reduce_small")
    loss = summed[0, 0]

    def packed(prefix):
        cols = [jnp.zeros((1, LANES), F32)]
        cols += [_pad_lanes(given[prefix + n].reshape(1, -1)) for n in SMALL_WEIGHTS]
        return jnp.concatenate(cols, axis=1)[None]

    small_out = _adamw([summed[None]], packed(""), packed("m_"), packed("v_"), "adamw_small")

    c_last = last_grad.shape[1]
    quarter = c_last // 4 // 16 * 16
    pieces = [(k * quarter, quarter if k < 3 else c_last - 3 * quarter) for k in range(4)]
    order = ["ffn2_w_gate", "ffn2_w_up", "ffn2_w_down", "ffn1_w_gate", "ffn1_w_down", "w_in", "w_out", "ffn1_w_up"]
    out, arrived = {}, []
    for k, n in enumerate(order):
        wv, mv, vv = (view(n, given[p + n]) for p in ("", "m_", "v_"))
        if n == "ffn1_w_up":
            received[(n, 0)] = arrived[0]
        if n == "w_in":
            per_layer = [_adamw([received[(n, l)]], wv[:, l][None], mv[:, l][None], vv[:, l][None], "adamw_" + n)
                         for l in range(depth)]
            res = [jnp.stack([per_layer[l][k][0] for l in range(depth)], axis=1) for k in range(4)]
        elif k < len(pieces):
            res, arrived = _adamw([received[(n, l)] for l in range(depth)], wv, mv, vv, "adamw_" + n,
                                  _Exchange([last_grad], rows=pieces[k], into=arrived))
        else:
            res = _adamw([received[(n, l)] for l in range(depth)], wv, mv, vv, "adamw_" + n)
        out[n] = [unview(n, r) for r in res]
    out["meta_tokens"] = [r[0] for r in _adamw([received[("meta_tokens", 0)]], meta_tokens[None], m_meta_tokens[None],
                                               v_meta_tokens[None], "adamw_meta_tokens")]
    offset = sizes[0]
    for n, size in zip(SMALL_WEIGHTS, sizes[1:]):
        shape, count = given[n].shape, given[n].size
        out[n] = [r[0, 0, offset:offset + count].reshape(shape) for r in small_out]
        offset += size

    result = [loss, grad_x]
    for k in range(4):
        result += [out[n][k] for n in WEIGHT_ORDER]
    return tuple(result)
```

```python
import math

import jax
import jax.numpy as jnp
from jax import lax
from jax.experimental import pallas as pl
from jax.experimental.pallas import tpu as pltpu

F32 = jnp.float32
BF16 = jnp.bfloat16

N_DEV = 8
N_META = 16
HEAD_DIM = 128
ROW_BLOCK = 128
LANES = 128
EPS = 1e-6
NEG = -1e30
ADAM_LR = 0.001
ADAM_B1 = 0.9
ADAM_B2 = 0.999
ADAM_EPS = 1e-08
ADAM_WD = 0.01
ADAM_STEP = 10
VMEM_LIMIT_BYTES = 56 * 1024 * 1024
MESH = pl.DeviceIdType.MESH

NT_DIMS = (((1,), (1,)), ((), ()))
TN_DIMS = (((0,), (0,)), ((), ()))
NN_DIMS = (((1,), (0,)), ((), ()))
ANY = pl.BlockSpec(memory_space=pl.ANY)


def _tile(n, cap, align):
    best = None
    for d in range(align, min(n, cap) + 1, align):
        if n % d == 0:
            best = d
    return best if best is not None else n


def _dot(a, b, dims=NN_DIMS):
    return lax.dot_general(a, b, dims, preferred_element_type=F32)


def _dot_split(x, u):
    hi = x.astype(BF16)
    lo = (x - hi.astype(F32)).astype(BF16)
    return _dot(hi, u) + _dot(lo, u)


def _my_position():
    return lax.axis_index("x"), lax.axis_index("y"), lax.axis_index("c")


class _Gather:
    n_phases = 3

    def __init__(self, arrs):
        self.arrs = list(arrs)
        n = len(self.arrs)
        self.out_shapes = [jax.ShapeDtypeStruct((N_DEV,) + a.shape, a.dtype) for a in self.arrs]
        self.scratch = [pltpu.SemaphoreType.DMA((n, 7)), pltpu.SemaphoreType.DMA((n, 7)), pltpu.SemaphoreType.DMA((n,))]

    def phase(self, p, ins, outs, sems):
        send_sems, recv_sems, local_sems = sems
        n = len(self.arrs)
        x, y, c = _my_position()
        me, sibling = (x, y, c), (x, y, 1 - c)
        chips = [(1 - x, y), (x, 1 - y), (1 - x, 1 - y)]

        def copy(a, k, block, to, src=None):
            slot = outs[a].at[4 * block[0] + 2 * block[1] + block[2]]
            return pltpu.make_async_remote_copy(
                src_ref=slot if src is None else src, dst_ref=slot,
                send_sem=send_sems.at[a, k], recv_sem=recv_sems.at[a, k], device_id=to, device_id_type=MESH)

        def local(a):
            return pltpu.make_async_copy(ins[a], outs[a].at[4 * x + 2 * y + c], local_sems.at[a])

        def first(a):
            return [copy(a, 0, me, sibling, src=ins[a])] + [copy(a, 1 + j, me, (*chip, c), src=ins[a]) for j, chip in enumerate(chips)]

        def passed(a, j):
            return copy(a, 4 + j, (*chips[j], c), sibling)

        if p == 0:
            for a in range(n):
                local(a).start()
            for a in range(n):
                for cp in first(a):
                    cp.start()
        elif p == 1:
            for a in range(n):
                for j, chip in enumerate(chips):
                    copy(a, 1 + j, (*chip, c), me).wait_recv()
                    passed(a, j).start()
        else:
            for a in range(n):
                copy(a, 0, sibling, me).wait_recv()
                for j, chip in enumerate(chips):
                    copy(a, 4 + j, (*chip, 1 - c), me).wait_recv()
            for a in range(n):
                for cp in first(a) + [passed(a, j) for j in range(3)]:
                    cp.wait_send()
                local(a).wait()


class _Exchange:
    n_phases = 2

    def __init__(self, arrs, rows=None, into=()):
        self.arrs = list(arrs)
        self.rows = rows
        self.into = list(into)
        n = len(self.arrs)
        self.out_shapes = [jax.ShapeDtypeStruct(a.shape, a.dtype) for a in self.arrs]
        self.scratch = [pltpu.SemaphoreType.DMA((n, 7)), pltpu.SemaphoreType.DMA((n, 7)), pltpu.SemaphoreType.DMA((n,))]

    def phase(self, p, ins, outs, sems):
        send_sems, recv_sems, local_sems = sems
        n = len(self.arrs)
        x, y, c = _my_position()
        me = 4 * x + 2 * y + c

        def peer_of(r):
            return (x ^ ((r >> 2) & 1), y ^ ((r >> 1) & 1), c ^ (r & 1))

        def part(ref, d):
            return ref.at[d] if self.rows is None else ref.at[d, pl.ds(self.rows[0], self.rows[1])]

        def copy(a, r):
            px, py, pc = peer_of(r)
            return pltpu.make_async_remote_copy(
                src_ref=part(ins[a], 4 * px + 2 * py + pc), dst_ref=part(outs[a], me),
                send_sem=send_sems.at[a, r - 1], recv_sem=recv_sems.at[a, r - 1],
                device_id=(px, py, pc), device_id_type=MESH)

        def arrival(a, r):
            px, py, pc = peer_of(r)
            slot = part(outs[a], 4 * px + 2 * py + pc)
            return pltpu.make_async_remote_copy(
                src_ref=slot, dst_ref=slot, send_sem=send_sems.at[a, r - 1], recv_sem=recv_sems.at[a, r - 1],
                device_id=(px, py, pc), device_id_type=MESH)

        def local(a):
            return pltpu.make_async_copy(part(ins[a], me), part(outs[a], me), local_sems.at[a])

        if p == 0:
            for a in range(n):
                local(a).start()
            for a in range(n):
                for r in range(1, N_DEV):
                    copy(a, r).start()
        else:
            for a in range(n):
                for r in range(1, N_DEV):
                    arrival(a, r).wait_recv()
            for a in range(n):
                for r in range(1, N_DEV):
                    copy(a, r).wait_send()
                local(a).wait()


def _run_alone(comm, name):
    n = len(comm.arrs)

    def body(*refs):
        for p in range(comm.n_phases):
            comm.phase(p, refs[:n], refs[n:2 * n], refs[2 * n:])

    return pl.pallas_call(body, name=name, out_shape=comm.out_shapes, in_specs=[ANY] * n, out_specs=[ANY] * n,
                          scratch_shapes=comm.scratch)(*comm.arrs)


def _call(body, *, name, grid, in_specs, out_specs, out_shape, operands, scratch_shapes=(), comm=None):
    scratch_shapes = list(scratch_shapes)
    params = pltpu.CompilerParams(dimension_semantics=("arbitrary",) * len(grid), vmem_limit_bytes=VMEM_LIMIT_BYTES)
    if comm is None:
        res = pl.pallas_call(body, name=name, grid=grid, in_specs=in_specs, out_specs=out_specs, out_shape=out_shape,
                             scratch_shapes=scratch_shapes, compiler_params=params)(*operands)
        return res, None
    n_in, n_out, n_sc = len(in_specs), len(out_specs), len(scratch_shapes)
    nc = len(comm.arrs)
    into = getattr(comm, "into", [])
    total = math.prod(grid)
    at = {0: 0, comm.n_phases - 1: total - 1}
    for p in range(1, comm.n_phases - 1):
        at[p] = (total * 7) // 8

    def wrapped(*refs):
        ins, cins = refs[:n_in], refs[n_in:n_in + nc]
        refs = refs[n_in + nc + len(into):]
        outs, couts = refs[:n_out], refs[n_out:n_out + nc]
        rest = refs[n_out + nc:]
        scratch, sems = rest[:n_sc], rest[n_sc:]
        step = 0
        for axis, size in enumerate(grid):
            step = step * size + pl.program_id(axis)
        for p in range(comm.n_phases - 1):
            @pl.when(step == at[p])
            def _(p=p):
                comm.phase(p, cins, couts, sems)
        body(*ins, *outs, *scratch)

        @pl.when(step == total - 1)
        def _():
            comm.phase(comm.n_phases - 1, cins, couts, sems)

    res = pl.pallas_call(
        wrapped, name=name, grid=grid, in_specs=list(in_specs) + [ANY] * (nc + len(into)),
        out_specs=list(out_specs) + [ANY] * nc, out_shape=list(out_shape) + comm.out_shapes,
        scratch_shapes=scratch_shapes + comm.scratch,
        input_output_aliases={n_in + nc + k: n_out + k for k in range(len(into))},
        compiler_params=params)(*operands, *comm.arrs, *into)
    return res[:n_out], res[n_out:]


def _all_reduce_small(vec, name):
    n = vec.shape[1]

    def body(v_ref, o_ref, buf, send_sems, recv_sems):
        x, y, c = _my_position()
        me = 4 * x + 2 * y + c

        def peer_of(r):
            return (x ^ ((r >> 2) & 1), y ^ ((r >> 1) & 1), c ^ (r & 1))

        def copy(r):
            px, py, pc = peer_of(r)
            return pltpu.make_async_remote_copy(
                src_ref=v_ref, dst_ref=buf.at[me], send_sem=send_sems.at[r - 1], recv_sem=recv_sems.at[r - 1],
                device_id=(px, py, pc), device_id_type=MESH)

        def arrival(r):
            px, py, pc = peer_of(r)
            slot = buf.at[4 * px + 2 * py + pc]
            return pltpu.make_async_remote_copy(
                src_ref=slot, dst_ref=slot, send_sem=send_sems.at[r - 1], recv_sem=recv_sems.at[r - 1],
                device_id=(px, py, pc), device_id_type=MESH)

        sends = [copy(r) for r in range(1, N_DEV)]
        for cp in sends:
            cp.start()
        buf[me] = v_ref[...]
        for r in range(1, N_DEV):
            arrival(r).wait_recv()
        for cp in sends:
            cp.wait_send()
        total = buf[0]
        for d in range(1, N_DEV):
            total = total + buf[d]
        o_ref[...] = total

    vmem = pl.BlockSpec(memory_space=pltpu.VMEM)
    return pl.pallas_call(
        body, name=name, out_shape=jax.ShapeDtypeStruct((1, n), F32), in_specs=[vmem], out_specs=vmem,
        scratch_shapes=[pltpu.VMEM((N_DEV, 1, n), F32), pltpu.SemaphoreType.DMA((7,)), pltpu.SemaphoreType.DMA((7,))],
    )(vec)


def _mm_core(pairs, dims, out_dtype, *, name, grid, out_shape, out_spec, acc_shape, alpha=1.0, res=None, comm=None):
    nk = grid[2]
    npairs = len(pairs)

    def body(*refs):
        ab = refs[:2 * npairs]
        rest = refs[2 * npairs:]
        res_ref = rest[0] if res is not None else None
        o_ref = rest[1] if res is not None else rest[0]
        acc_ref = rest[-1] if nk > 1 else None
        part = None
        for p in range(npairs):
            a_ref, b_ref = ab[2 * p], ab[2 * p + 1]
            shards = [(a_ref[s], b_ref[s]) for s in range(a_ref.shape[0])] if len(a_ref.shape) == 3 else [(a_ref[...], b_ref[...])]
            for av, bv in shards:
                d = _dot(av.astype(BF16), bv.astype(BF16), dims)
                part = d if part is None else part + d

        def finish(total):
            val = total * alpha if alpha != 1.0 else total
            if res_ref is not None:
                val = res_ref[...] + val
            o_ref[...] = val.astype(out_dtype)

        if nk == 1:
            finish(part)
        else:
            kk = pl.program_id(2)

            @pl.when(kk == 0)
            def _():
                acc_ref[...] = part

            @pl.when(kk > 0)
            def _():
                acc_ref[...] += part

            @pl.when(kk == nk - 1)
            def _():
                finish(acc_ref[...])

    operands, in_specs = [], []
    for (a, a_spec), (b, b_spec) in pairs:
        operands += [a, b]
        in_specs += [a_spec, b_spec]
    if res is not None:
        operands.append(res[0])
        in_specs.append(res[1])
    out, got = _call(body, name=name, grid=grid, in_specs=in_specs, out_specs=[out_spec],
                     out_shape=[jax.ShapeDtypeStruct(out_shape, out_dtype)], operands=operands,
                     scratch_shapes=[pltpu.VMEM(acc_shape, F32)] if nk > 1 else [], comm=comm)
    return out[0] if comm is None else (out[0], got)


def _mm(pairs, mode, out_dtype, *, name, alpha=1.0, res=None, comm=None, whole_k=False):
    a0, b0 = pairs[0]
    if mode == "nn":
        (m, k), n = a0.shape, b0.shape[1]
    elif mode == "nt":
        (m, k), n = a0.shape, b0.shape[0]
    else:
        (k, m), n = a0.shape, b0.shape[1]
    dims = {"nn": NN_DIMS, "nt": NT_DIMS, "tn": TN_DIMS}[mode]
    tm = _tile(m, 528 if whole_k else 1056, LANES if mode == "tn" else 16)
    tn = _tile(n, 512 if whole_k else 1024, LANES)
    tk = k if whole_k else _tile(k, 2048 if mode != "tn" else 2112, LANES if mode != "tn" else 16)
    a_spec = pl.BlockSpec((tk, tm), lambda i, j, kk: (kk, i)) if mode == "tn" else pl.BlockSpec((tm, tk), lambda i, j, kk: (i, kk))
    b_spec = pl.BlockSpec((tn, tk), lambda i, j, kk: (j, kk)) if mode == "nt" else pl.BlockSpec((tk, tn), lambda i, j, kk: (kk, j))
    o_spec = pl.BlockSpec((tm, tn), lambda i, j, kk: (i, j))
    return _mm_core([((a, a_spec), (b, b_spec)) for a, b in pairs], dims, out_dtype, name=name,
                    grid=(m // tm, n // tn, k // tk), out_shape=(m, n), out_spec=o_spec, acc_shape=(tm, tn),
                    alpha=alpha, res=None if res is None else (res, o_spec), comm=comm)


def _rms_fwd(h, gain, name):
    m, d = h.shape
    tm = _tile(m, 528, 16)

    def body(h_ref, g_ref, o_ref):
        hv = h_ref[...]
        r = lax.rsqrt(jnp.mean(hv * hv, axis=-1, keepdims=True) + EPS)
        o_ref[...] = (hv * r * g_ref[...]).astype(BF16)

    row = pl.BlockSpec((tm, d), lambda i: (i, 0))
    out, _ = _call(body, name=name, grid=(m // tm,), in_specs=[row, pl.BlockSpec((1, d), lambda i: (0, 0))],
                   out_specs=[row], out_shape=[jax.ShapeDtypeStruct((m, d), BF16)], operands=[h, gain])
    return out[0]


def _rms_bwd(dxn, h, gain, dres, name):
    m, d = h.shape
    tm = _tile(m, 264, 16)

    def body(dxn_ref, h_ref, g_ref, dres_ref, dh_ref, dhb_ref, dg_ref):
        hv = h_ref[...]
        r = lax.rsqrt(jnp.mean(hv * hv, axis=-1, keepdims=True) + EPS)
        xhat = hv * r
        dxn_v = dxn_ref[...]
        t = dxn_v * g_ref[...]
        dh = dres_ref[...] + r * (t - xhat * jnp.mean(t * xhat, axis=-1, keepdims=True))
        dh_ref[...] = dh
        dhb_ref[...] = dh.astype(BF16)
        part = jnp.sum(dxn_v * xhat, axis=0, keepdims=True)

        @pl.when(pl.program_id(0) == 0)
        def _():
            dg_ref[...] = part

        @pl.when(pl.program_id(0) > 0)
        def _():
            dg_ref[...] += part

    row = pl.BlockSpec((tm, d), lambda i: (i, 0))
    vec = pl.BlockSpec((1, d), lambda i: (0, 0))
    out, _ = _call(body, name=name, grid=(m // tm,), in_specs=[row, row, vec, row], out_specs=[row, row, vec],
                   out_shape=[jax.ShapeDtypeStruct((m, d), F32), jax.ShapeDtypeStruct((m, d), BF16),
                              jax.ShapeDtypeStruct((1, d), F32)],
                   operands=[dxn, h, gain, dres])
    return out


def _loss_head(h, gain, target, x_off, name):
    m, d = h.shape
    tm = ROW_BLOCK
    first = x_off // tm

    def body(h_ref, g_ref, t_ref, dh_ref, dhb_ref, dg_ref, loss_ref):
        i = pl.program_id(0)

        @pl.when(i == 0)
        def _():
            dg_ref[...] = jnp.zeros_like(dg_ref)
            loss_ref[...] = jnp.zeros_like(loss_ref)

        @pl.when(i < first)
        def _():
            dh_ref[...] = jnp.zeros_like(dh_ref)
            dhb_ref[...] = jnp.zeros_like(dhb_ref)

        @pl.when(i >= first)
        def _():
            hv = h_ref[...]
            g = g_ref[...]
            r = lax.rsqrt(jnp.mean(hv * hv, axis=-1, keepdims=True) + EPS)
            xhat = hv * r
            err = xhat * g - t_ref[...]
            loss_ref[...] += 0.5 * jnp.sum(jnp.mean(err * err, axis=-1, keepdims=True))
            dy = err * (1.0 / d)
            t = dy * g
            dh = r * (t - xhat * jnp.mean(t * xhat, axis=-1, keepdims=True))
            dh_ref[...] = dh
            dhb_ref[...] = dh.astype(BF16)
            dg_ref[...] += jnp.sum(dy * xhat, axis=0, keepdims=True)

    row = pl.BlockSpec((tm, d), lambda i: (i, 0))
    vec = pl.BlockSpec((1, d), lambda i: (0, 0))
    out, _ = _call(body, name=name, grid=(m // tm,),
                   in_specs=[row, vec, pl.BlockSpec((tm, d), lambda i: (jnp.maximum(i - first, 0), 0))],
                   out_specs=[row, row, vec, pl.BlockSpec((8, LANES), lambda i: (0, 0))],
                   out_shape=[jax.ShapeDtypeStruct((m, d), F32), jax.ShapeDtypeStruct((m, d), BF16),
                              jax.ShapeDtypeStruct((1, d), F32), jax.ShapeDtypeStruct((8, LANES), F32)],
                   operands=[h, gain, target])
    return out


def _sigmoid(z):
    return 1.0 / (1.0 + jnp.exp(-z))


def _ffn_up(xn, wg, wu, name, comm=None):
    m, d = xn.shape
    nsh, c, _ = wg.shape
    tm = _tile(m, 1056, 16)

    def body(x_ref, wg_ref, wu_ref, g_ref, u_ref, a_ref):
        xv = x_ref[...]
        g = _dot(xv, wg_ref[...], NT_DIMS)
        u = _dot(xv, wu_ref[...], NT_DIMS)
        g_ref[...] = g.astype(BF16)
        u_ref[...] = u.astype(BF16)
        a_ref[...] = (g * _sigmoid(g) * u).astype(BF16)

    out = pl.BlockSpec((None, tm, c), lambda i, j: (j, i, 0))
    w = pl.BlockSpec((None, c, d), lambda i, j: (j, 0, 0))
    return _call(body, name=name, grid=(m // tm, nsh), in_specs=[pl.BlockSpec((tm, d), lambda i, j: (i, 0)), w, w],
                 out_specs=[out, out, out], out_shape=[jax.ShapeDtypeStruct((nsh, m, c), BF16)] * 3,
                 operands=[xn, wg, wu], comm=comm)


def _ffn_gate(xn, wg, name, comm=None):
    m, d = xn.shape
    nsh, c, _ = wg.shape
    tm = _tile(m, 1056, 16)

    def body(x_ref, wg_ref, g_ref):
        g_ref[...] = _dot(x_ref[...], wg_ref[...], NT_DIMS).astype(BF16)

    out, got = _call(body, name=name, grid=(m // tm, nsh),
                     in_specs=[pl.BlockSpec((tm, d), lambda i, j: (i, 0)), pl.BlockSpec((None, c, d), lambda i, j: (j, 0, 0))],
                     out_specs=[pl.BlockSpec((None, tm, c), lambda i, j: (j, i, 0))],
                     out_shape=[jax.ShapeDtypeStruct((nsh, m, c), BF16)], operands=[xn, wg], comm=comm)
    return out[0], got


def _ffn_up_given_gate(xn, wu, g, name, comm=None):
    m, d = xn.shape
    nsh, c, _ = wu.shape
    tm = _tile(m, 1056, 16)

    def body(x_ref, wu_ref, g_ref, u_ref, a_ref):
        u = _dot(x_ref[...], wu_ref[...], NT_DIMS)
        g = g_ref[...].astype(F32)
        u_ref[...] = u.astype(BF16)
        a_ref[...] = (g * _sigmoid(g) * u).astype(BF16)

    blk = pl.BlockSpec((None, tm, c), lambda i, j: (j, i, 0))
    return _call(body, name=name, grid=(m // tm, nsh),
                 in_specs=[pl.BlockSpec((tm, d), lambda i, j: (i, 0)), pl.BlockSpec((None, c, d), lambda i, j: (j, 0, 0)), blk],
                 out_specs=[blk, blk], out_shape=[jax.ShapeDtypeStruct((nsh, m, c), BF16)] * 2,
                 operands=[xn, wu, g], comm=comm)


def _ffn_contract(pairs, out_dtype, name, *, group, alpha=1.0, res=None, comm=None):
    nsh, m, c = pairs[0][0].shape
    d = pairs[0][1].shape[2]
    tm, tn = _tile(m, 1056, 16), _tile(d, 1024 if group * len(pairs) <= 4 and res is None else 512, LANES)
    a_spec = pl.BlockSpec((group, tm, c), lambda i, j, kk: (kk, i, 0))
    b_spec = pl.BlockSpec((group, c, tn), lambda i, j, kk: (kk, 0, j))
    o_spec = pl.BlockSpec((tm, tn), lambda i, j, kk: (i, j))
    return _mm_core([((a, a_spec), (b, b_spec)) for a, b in pairs], NN_DIMS, out_dtype, name=name,
                    grid=(m // tm, d // tn, nsh // group), out_shape=(m, d), out_spec=o_spec, acc_shape=(tm, tn),
                    alpha=alpha, res=None if res is None else (res, o_spec), comm=comm)


def _ffn_bwd_act(dh, wd, g, u, name, comm=None):
    m, d = dh.shape
    nsh, c, _ = wd.shape
    tm = _tile(m, 1056, 16)

    def body(dh_ref, wd_ref, g_ref, u_ref, dg_ref, du_ref):
        dact = 0.5 * _dot(dh_ref[...], wd_ref[...], NT_DIMS)
        gv = g_ref[...].astype(F32)
        uv = u_ref[...].astype(F32)
        sig = _sigmoid(gv)
        du_ref[...] = (dact * gv * sig).astype(BF16)
        dg_ref[...] = (dact * uv * sig * (1.0 + gv * (1.0 - sig))).astype(BF16)

    blk = pl.BlockSpec((None, tm, c), lambda i, j: (j, i, 0))
    return _call(body, name=name, grid=(m // tm, nsh),
                 in_specs=[pl.BlockSpec((tm, d), lambda i, j: (i, 0)), pl.BlockSpec((None, c, d), lambda i, j: (j, 0, 0)), blk, blk],
                 out_specs=[blk, blk], out_shape=[jax.ShapeDtypeStruct((nsh, m, c), BF16)] * 2, operands=[dh, wd, g, u],
                 comm=comm)


def _ffn_dw(z, x, name, alpha=1.0, comm=None):
    nsh, m, c = z.shape
    d = x.shape[1]
    tn, tk = _tile(d, 1024, LANES), _tile(m, 2112, 16)
    return _mm_core([((z, pl.BlockSpec((None, tk, c), lambda i, j, kk: (i, kk, 0))),
                      (x, pl.BlockSpec((tk, tn), lambda i, j, kk: (kk, j))))],
                    TN_DIMS, BF16, name=name, grid=(nsh, d // tn, m // tk), out_shape=(nsh, c, d),
                    out_spec=pl.BlockSpec((None, c, tn), lambda i, j, kk: (i, 0, j)), acc_shape=(c, tn), alpha=alpha,
                    comm=comm)


def _dot3(tri, x):
    h1 = x.astype(BF16)
    r1 = x - h1.astype(F32)
    h2 = r1.astype(BF16)
    h3 = (r1 - h2.astype(F32)).astype(BF16)
    return _dot(tri, h1) + _dot(tri, h2) + _dot(tri, h3)


def _log_sigmoid(z):
    return jnp.minimum(z, 0.0) - jnp.log(1.0 + jnp.exp(-jnp.abs(z)))


def _triangle(t, cmp):
    return cmp(lax.broadcasted_iota(jnp.int32, (t, t), 0), lax.broadcasted_iota(jnp.int32, (t, t), 1)).astype(BF16)


def _forget_cumsum(fl, bias, n_heads, pad, name):
    m = fl.shape[0]
    nb = m // ROW_BLOCK

    def body(fl_ref, b_ref, c_ref):
        tri = _triangle(ROW_BLOCK, lambda r, c: r >= c)
        lane_ok = lax.broadcasted_iota(jnp.int32, (ROW_BLOCK, LANES), 1) < n_heads
        rows = lax.broadcasted_iota(jnp.int32, (ROW_BLOCK, LANES), 0)

        def step(b, carry):
            off = pl.multiple_of(b * ROW_BLOCK, ROW_BLOCK)
            lf = _log_sigmoid(fl_ref[pl.ds(off, ROW_BLOCK), :] + b_ref[...])
            lf = jnp.where(lane_ok & (rows + off >= pad), lf, 0.0)
            cs = _dot3(tri, lf) + carry
            c_ref[pl.ds(off, ROW_BLOCK), :] = cs
            return cs[ROW_BLOCK - 1:ROW_BLOCK, :]

        lax.fori_loop(0, nb, step, jnp.zeros((1, LANES), F32))

    vmem = pl.BlockSpec(memory_space=pltpu.VMEM)
    return pl.pallas_call(
        body, name=name, out_shape=jax.ShapeDtypeStruct((m, LANES), F32), in_specs=[vmem, vmem], out_specs=vmem,
        compiler_params=pltpu.CompilerParams(vmem_limit_bytes=VMEM_LIMIT_BYTES),
    )(fl, bias)


def _forget_cumsum_bwd(dc_a, dc_b, fl, bias, n_heads, pad, name):
    m = fl.shape[0]
    nb = m // ROW_BLOCK

    def body(da_ref, db_ref, fl_ref, b_ref, dfl_ref, dbias_ref):
        tri = _triangle(ROW_BLOCK, lambda r, c: r <= c)
        lane_ok = lax.broadcasted_iota(jnp.int32, (ROW_BLOCK, LANES), 1) < n_heads
        rows = lax.broadcasted_iota(jnp.int32, (ROW_BLOCK, LANES), 0)

        def step(bb, carry):
            tail, dbias = carry
            off = pl.multiple_of((nb - 1 - bb) * ROW_BLOCK, ROW_BLOCK)
            dc = da_ref[pl.ds(off, ROW_BLOCK), :] + db_ref[pl.ds(off, ROW_BLOCK), :]
            dlf = _dot3(tri, dc) + tail
            z = fl_ref[pl.ds(off, ROW_BLOCK), :] + b_ref[...]
            dfl = jnp.where(lane_ok & (rows + off >= pad), dlf * _sigmoid(-z), 0.0)
            dfl_ref[pl.ds(off, ROW_BLOCK), :] = dfl
            return dlf[0:1, :], dbias + jnp.sum(dfl, axis=0, keepdims=True)

        zero = jnp.zeros((1, LANES), F32)
        _, dbias = lax.fori_loop(0, nb, step, (zero, zero))
        dbias_ref[...] = dbias

    vmem = pl.BlockSpec(memory_space=pltpu.VMEM)
    return pl.pallas_call(
        body, name=name,
        out_shape=[jax.ShapeDtypeStruct((m, LANES), F32), jax.ShapeDtypeStruct((1, LANES), F32)],
        in_specs=[vmem] * 4, out_specs=[vmem, vmem],
        compiler_params=pltpu.CompilerParams(vmem_limit_bytes=VMEM_LIMIT_BYTES),
    )(dc_a, dc_b, fl, bias)


def _attn_block(m):
    return 3 * ROW_BLOCK if m % (3 * ROW_BLOCK) == 0 else ROW_BLOCK


def _block(ref, j, t):
    return ref[pl.ds(pl.multiple_of(j * t, t), t), :]


def _head_norm(o, gain):
    r = lax.rsqrt(jnp.mean(o * o, axis=-1, keepdims=True) + EPS)
    return o * r * gain


def _head_norm_bwd(o, d_on, gain):
    r = lax.rsqrt(jnp.mean(o * o, axis=-1, keepdims=True) + EPS)
    ohat = o * r
    t = d_on * gain
    d_o = r * (t - ohat * jnp.mean(t * ohat, axis=-1, keepdims=True))
    return d_o, jnp.sum(d_on * ohat, axis=0, keepdims=True)


def _qkv_specs(t, m, h, first_col_block):
    q = pl.BlockSpec((t, HEAD_DIM), lambda hd, i: (i, first_col_block + hd))
    k = pl.BlockSpec((m, HEAD_DIM), lambda hd, i: (0, first_col_block + h + hd))
    v = pl.BlockSpec((m, HEAD_DIM), lambda hd, i: (0, first_col_block + 2 * h + hd))
    return q, k, v


def _fox_fwd(qkv, ccol, crow, gain, n_heads, pad, name, comm=None):
    m = qkv.shape[0]
    t = _attn_block(m)
    nq = m // t
    scale = HEAD_DIM ** -0.5
    hw = n_heads * HEAD_DIM

    def body(q_ref, k_ref, v_ref, ccol_ref, crow_ref, g_ref, o_ref, on_ref, lse_ref):
        i = pl.program_id(1)
        q = q_ref[...]
        ci = ccol_ref[...]
        qpos = i * t + lax.broadcasted_iota(jnp.int32, (t, 1), 0)

        def step(j, j_prev, j_next, state, masked):
            mx, l, acc_scaled, p_prev, s_cur = state
            acc = acc_scaled + _dot(p_prev, _block(v_ref, j_prev, t))
            s_next = _dot(q, _block(k_ref, j_next, t), NT_DIMS)
            s = s_cur * scale + ci - crow_ref[j]
            if masked:
                kpos = j * t + lax.broadcasted_iota(jnp.int32, (1, t), 1)
                s = jnp.where((kpos <= qpos) & (kpos >= pad), s, NEG)
            mx_new = jnp.maximum(mx, jnp.max(s, axis=-1, keepdims=True))
            p = jnp.exp(s - mx_new)
            a = jnp.exp(mx - mx_new)
            return mx_new, a * l + jnp.sum(p, axis=-1, keepdims=True), a * acc, p.astype(BF16), s_next

        state = (jnp.full((t, 1), NEG, F32), jnp.zeros((t, 1), F32), jnp.zeros((t, HEAD_DIM), F32),
                 jnp.zeros((t, t), BF16), _dot(q, _block(k_ref, 0, t), NT_DIMS))
        state = step(0, 0, jnp.minimum(i, 1), state, True)
        state = lax.fori_loop(1, i, lambda j, st: step(j, j - 1, j + 1, st, False), state)
        mx, l, acc_scaled, p_last, _ = lax.fori_loop(0, jnp.minimum(i, 1), lambda _, st: step(i, i - 1, i, st, True), state)
        acc = acc_scaled + _dot(p_last, _block(v_ref, i, t))
        valid = qpos >= pad
        o = jnp.where(valid, acc / l, 0.0)
        o_ref[...] = o
        on_ref[...] = _head_norm(o, g_ref[...]).astype(BF16)
        lse_ref[...] = jnp.where(valid, mx + jnp.log(l), 0.0)

    q_spec, k_spec, v_spec = _qkv_specs(t, m, n_heads, 0)
    col = pl.BlockSpec((None, t, 1), lambda hd, i: (hd, i, 0))
    head = pl.BlockSpec((t, HEAD_DIM), lambda hd, i: (i, hd))
    return _call(body, name=name, grid=(n_heads, nq),
                 in_specs=[q_spec, k_spec, v_spec, col, pl.BlockSpec((None, nq, 1, t), lambda hd, i: (hd, 0, 0, 0)),
                           pl.BlockSpec((1, HEAD_DIM), lambda hd, i: (0, hd))],
                 out_specs=[head, head, col],
                 out_shape=[jax.ShapeDtypeStruct((m, hw), F32), jax.ShapeDtypeStruct((m, hw), BF16),
                            jax.ShapeDtypeStruct((n_heads, m, 1), F32)],
                 operands=[qkv, qkv, qkv, ccol, crow, gain], comm=comm)


def _fox_bwd(qkv, o, d_on, gain, lse, ccol, crow, n_heads, pad, name, comm=None):
    m = qkv.shape[0]
    t = _attn_block(m)
    nq = m // t
    scale = HEAD_DIM ** -0.5
    hw = n_heads * HEAD_DIM

    def body(q_ref, k_ref, v_ref, o_ref, don_ref, g_ref, lse_ref, ccol_ref, crow_ref,
             dq_ref, dk_ref, dv_ref, dg_ref, dccol_ref, dcrow_ref):
        i = pl.program_id(1)

        @pl.when(i == 0)
        def _():
            dk_ref[...] = jnp.zeros_like(dk_ref)
            dv_ref[...] = jnp.zeros_like(dv_ref)
            dg_ref[...] = jnp.zeros_like(dg_ref)
            dcrow_ref[...] = jnp.zeros_like(dcrow_ref)

        q = q_ref[...]
        o = o_ref[...]
        d_o, dgain = _head_norm_bwd(o, don_ref[...], g_ref[...])
        dg_ref[...] += dgain
        delta = jnp.sum(d_o * o, axis=-1, keepdims=True)
        d_ob = d_o.astype(BF16)
        ci = ccol_ref[...]
        lse_i = lse_ref[...]
        qpos = i * t + lax.broadcasted_iota(jnp.int32, (t, 1), 0)

        def step(j, carry, masked):
            dq, dci = carry
            k = _block(k_ref, j, t)
            v = _block(v_ref, j, t)
            off = pl.multiple_of(j * t, t)
            s = _dot(q, k, NT_DIMS) * scale + ci - crow_ref[j]
            if masked:
                kpos = off + lax.broadcasted_iota(jnp.int32, (1, t), 1)
                ok = (kpos <= qpos) & (kpos >= pad)
                p = jnp.where(ok, jnp.exp(jnp.where(ok, s - lse_i, 0.0)), 0.0)
            else:
                p = jnp.exp(s - lse_i)
            ds = p * (_dot(d_ob, v, NT_DIMS) - delta)
            dsb = ds.astype(BF16)
            dk_ref[pl.ds(off, t), :] += _dot(dsb, q, TN_DIMS) * scale
            dv_ref[pl.ds(off, t), :] += _dot(p.astype(BF16), d_ob, TN_DIMS)
            dcrow_ref[j] -= jnp.sum(ds, axis=0, keepdims=True)
            return dq + _dot(dsb, k), dci + jnp.sum(ds, axis=-1, keepdims=True)

        carry = step(0, (jnp.zeros((t, HEAD_DIM), F32), jnp.zeros((t, 1), F32)), True)
        n_mid = jnp.maximum(i - 1, 0)
        carry = lax.fori_loop(0, n_mid // 2, lambda jp, c: step(2 * jp + 2, step(2 * jp + 1, c, False), False), carry)
        carry = lax.fori_loop(0, n_mid % 2, lambda _, c: step(i - 1, c, False), carry)
        dq, dci = lax.fori_loop(0, jnp.minimum(i, 1), lambda _, c: step(i, c, True), carry)
        dq_ref[...] = (dq * scale).astype(BF16)
        dccol_ref[...] = dci

    q_spec, k_spec, v_spec = _qkv_specs(t, m, n_heads, 0)
    col = pl.BlockSpec((None, t, 1), lambda hd, i: (hd, i, 0))
    rowc = pl.BlockSpec((None, nq, 1, t), lambda hd, i: (hd, 0, 0, 0))
    head = pl.BlockSpec((t, HEAD_DIM), lambda hd, i: (i, hd))
    whole = pl.BlockSpec((m, HEAD_DIM), lambda hd, i: (0, hd))
    gvec = pl.BlockSpec((1, HEAD_DIM), lambda hd, i: (0, hd))
    return _call(body, name=name, grid=(n_heads, nq),
                 in_specs=[q_spec, k_spec, v_spec, head, head, gvec, col, col, rowc],
                 out_specs=[head, whole, whole, gvec, col, rowc],
                 out_shape=[jax.ShapeDtypeStruct((m, hw), BF16), jax.ShapeDtypeStruct((m, hw), F32),
                            jax.ShapeDtypeStruct((m, hw), F32), jax.ShapeDtypeStruct((1, hw), F32),
                            jax.ShapeDtypeStruct((n_heads, m, 1), F32), jax.ShapeDtypeStruct((n_heads, nq, 1, t), F32)],
                 operands=[qkv, qkv, qkv, o, d_on, gain, lse, ccol, crow], comm=comm)


def _sb_scores(z):
    lp = jnp.log(1.0 + jnp.exp(-jnp.abs(z)))
    return jnp.minimum(z, 0.0) - lp, jnp.minimum(-z, 0.0) - lp


def _sb_fwd(qkv, gain, n_heads, pad, name, comm=None):
    m = qkv.shape[0]
    t = _attn_block(m)
    nq = m // t
    assert nq <= LANES
    scale = HEAD_DIM ** -0.5
    hw = n_heads * HEAD_DIM

    def body(q_ref, k_ref, v_ref, g_ref, after_ref, o_ref, on_ref, run_ref):
        i = pl.program_id(1)
        q = q_ref[...]
        qpos = i * t + lax.broadcasted_iota(jnp.int32, (t, 1), 0)
        after = after_ref[...]
        lane = lax.broadcasted_iota(jnp.int32, (t, LANES), 1)

        def step(j, j_prev, j_next, state, masked):
            run, acc, a_prev, s_cur = state
            acc = acc + _dot(a_prev, _block(v_ref, j_prev, t))
            s_next = _dot(q, _block(k_ref, j_next, t), NT_DIMS)
            ls_pos, log_1m = _sb_scores(s_cur * scale)
            if masked:
                kpos = j * t + lax.broadcasted_iota(jnp.int32, (1, t), 1)
                ok = (kpos < qpos) & (kpos >= pad)
                log_1m = jnp.where(ok, log_1m, 0.0)
            a = jnp.exp(ls_pos + _dot_split(log_1m, after) + run)
            if masked:
                a = jnp.where(ok, a, 0.0)
            run_ref[...] = jnp.where(lane == j, run, run_ref[...])
            return run + jnp.sum(log_1m, axis=-1, keepdims=True), acc, a.astype(BF16), s_next

        run_ref[...] = jnp.zeros_like(run_ref)
        state = (jnp.zeros((t, 1), F32), jnp.zeros((t, HEAD_DIM), F32), jnp.zeros((t, t), BF16),
                 _dot(q, _block(k_ref, i, t), NT_DIMS))
        state = step(i, i, jnp.maximum(i - 1, 0), state, True)
        state = lax.fori_loop(1, i, lambda jj, st: step(i - jj, i - jj + 1, i - jj - 1, st, False), state)
        _, acc, a_last, _ = lax.fori_loop(0, jnp.minimum(i, 1), lambda _, st: step(0, 1, 0, st, True), state)
        o = acc + _dot(a_last, _block(v_ref, 0, t))
        o_ref[...] = o
        on_ref[...] = _head_norm(o, g_ref[...]).astype(BF16)

    q_spec, k_spec, v_spec = _qkv_specs(t, m, n_heads, 3 * n_heads)
    head = pl.BlockSpec((t, HEAD_DIM), lambda hd, i: (i, hd))
    return _call(body, name=name, grid=(n_heads, nq),
                 in_specs=[q_spec, k_spec, v_spec, pl.BlockSpec((1, HEAD_DIM), lambda hd, i: (0, hd)),
                           pl.BlockSpec((t, t), lambda hd, i: (0, 0))],
                 out_specs=[head, head, pl.BlockSpec((None, t, LANES), lambda hd, i: (hd, i, 0))],
                 out_shape=[jax.ShapeDtypeStruct((m, hw), F32), jax.ShapeDtypeStruct((m, hw), BF16),
                            jax.ShapeDtypeStruct((n_heads, m, LANES), F32)],
                 operands=[qkv, qkv, qkv, gain, _triangle(t, lambda r, c: r > c)], comm=comm)


def _sb_bwd(qkv, o, d_on, gain, runs, n_heads, pad, name, comm=None):
    m = qkv.shape[0]
    t = _attn_block(m)
    nq = m // t
    scale = HEAD_DIM ** -0.5
    hw = n_heads * HEAD_DIM

    def body(q_ref, k_ref, v_ref, o_ref, don_ref, g_ref, run_ref, after_ref, before_ref, dq_ref, dk_ref, dv_ref, dg_ref):
        i = pl.program_id(1)

        @pl.when(i == 0)
        def _():
            dk_ref[...] = jnp.zeros_like(dk_ref)
            dv_ref[...] = jnp.zeros_like(dv_ref)
            dg_ref[...] = jnp.zeros_like(dg_ref)

        q = q_ref[...]
        d_o, dgain = _head_norm_bwd(o_ref[...], don_ref[...], g_ref[...])
        dg_ref[...] += dgain
        d_ob = d_o.astype(BF16)
        runs_i = run_ref[...]
        qpos = i * t + lax.broadcasted_iota(jnp.int32, (t, 1), 0)
        after = after_ref[...]
        before = before_ref[...]
        lane = lax.broadcasted_iota(jnp.int32, (t, LANES), 1)

        def step(j, carry, masked):
            g_run, dq = carry
            k = _block(k_ref, j, t)
            v = _block(v_ref, j, t)
            off = pl.multiple_of(j * t, t)
            ls_pos, ls_neg = _sb_scores(_dot(q, k, NT_DIMS) * scale)
            log_1m = ls_neg
            if masked:
                kpos = off + lax.broadcasted_iota(jnp.int32, (1, t), 1)
                ok = (kpos < qpos) & (kpos >= pad)
                log_1m = jnp.where(ok, ls_neg, 0.0)
            run = jnp.sum(jnp.where(lane == j, runs_i, 0.0), axis=-1, keepdims=True)
            a = jnp.exp(ls_pos + _dot_split(log_1m, after) + run)
            if masked:
                a = jnp.where(ok, a, 0.0)
            g = a * _dot(d_ob, v, NT_DIMS)
            prefix = _dot(g.astype(BF16), before) + g_run
            dz = g * jnp.exp(ls_neg) - jnp.exp(ls_pos) * prefix
            if masked:
                dz = jnp.where(ok, dz, 0.0)
            dzb = dz.astype(BF16)
            dk_ref[pl.ds(off, t), :] += _dot(dzb, q, TN_DIMS) * scale
            dv_ref[pl.ds(off, t), :] += _dot(a.astype(BF16), d_ob, TN_DIMS)
            return g_run + jnp.sum(g, axis=-1, keepdims=True), dq + _dot(dzb, k)

        carry = step(0, (jnp.zeros((t, 1), F32), jnp.zeros((t, HEAD_DIM), F32)), True)
        n_mid = jnp.maximum(i - 1, 0)
        carry = lax.fori_loop(0, n_mid // 2, lambda jp, c: step(2 * jp + 2, step(2 * jp + 1, c, False), False), carry)
        carry = lax.fori_loop(0, n_mid % 2, lambda _, c: step(i - 1, c, False), carry)
        _, dq = lax.fori_loop(0, jnp.minimum(i, 1), lambda _, c: step(i, c, True), carry)
        dq_ref[...] = (dq * scale).astype(BF16)

    q_spec, k_spec, v_spec = _qkv_specs(t, m, n_heads, 3 * n_heads)
    head = pl.BlockSpec((t, HEAD_DIM), lambda hd, i: (i, hd))
    whole = pl.BlockSpec((m, HEAD_DIM), lambda hd, i: (0, hd))
    gvec = pl.BlockSpec((1, HEAD_DIM), lambda hd, i: (0, hd))
    tri = pl.BlockSpec((t, t), lambda hd, i: (0, 0))
    return _call(body, name=name, grid=(n_heads, nq),
                 in_specs=[q_spec, k_spec, v_spec, head, head, gvec, pl.BlockSpec((None, t, LANES), lambda hd, i: (hd, i, 0)),
                           tri, tri],
                 out_specs=[head, whole, whole, gvec],
                 out_shape=[jax.ShapeDtypeStruct((m, hw), BF16), jax.ShapeDtypeStruct((m, hw), F32),
                            jax.ShapeDtypeStruct((m, hw), F32), jax.ShapeDtypeStruct((1, hw), F32)],
                 operands=[qkv, qkv, qkv, o, d_on, gain, runs, _triangle(t, lambda r, c: r > c), _triangle(t, lambda r, c: r < c)],
                 comm=comm)


def _adamw(parts, w, m1, v2, name, comm=None):
    nl, r, c = w.shape
    assert len(parts) == nl
    n_parts = parts[0].shape[0]
    block_elems = 256 * 1024
    if r % 8 == 0 or c % LANES != 0:
        tr, tc = _tile(r, max(8, block_elems // (-(-c // LANES) * LANES)), 8), c
    else:
        tr, tc = r, _tile(c, max(LANES, block_elems // r // LANES * LANES), LANES)
    nr, nc = r // tr, c // tc
    bias1 = 1.0 / (1.0 - ADAM_B1 ** ADAM_STEP)
    bias2 = 1.0 / (1.0 - ADAM_B2 ** ADAM_STEP)

    def body(*refs):
        p_refs = refs[:nl]
        w_ref, m_ref, v_ref, g_ref, d_ref, nm_ref, nv_ref = refs[nl:]

        def update(p_ref):
            g = p_ref[0].astype(F32)
            for s in range(1, n_parts):
                g = g + p_ref[s].astype(F32)
            m_new = ADAM_B1 * m_ref[...] + (1.0 - ADAM_B1) * g
            v_new = ADAM_B2 * v_ref[...] + (1.0 - ADAM_B2) * (g * g)
            g_ref[...] = g
            nm_ref[...] = m_new
            nv_ref[...] = v_new
            d_ref[...] = -ADAM_LR * ((m_new * bias1) / (jnp.sqrt(v_new * bias2) + ADAM_EPS) + ADAM_WD * w_ref[...])

        for ll in range(nl):
            @pl.when(pl.program_id(0) == ll)
            def _(ll=ll):
                update(p_refs[ll])

    def part_spec(ll):
        def index(l, i, j):
            pin = jnp.where(l < ll, 0, 1)
            return 0, jnp.where(l == ll, i, pin * (nr - 1)), jnp.where(l == ll, j, pin * (nc - 1))
        return pl.BlockSpec((n_parts, tr, tc), index)

    blk = pl.BlockSpec((None, tr, tc), lambda l, i, j: (l, i, j))
    out, got = _call(body, name=name, grid=(nl, nr, nc), in_specs=[part_spec(ll) for ll in range(nl)] + [blk, blk, blk],
                     out_specs=[blk] * 4, out_shape=[jax.ShapeDtypeStruct((nl, r, c), F32)] * 4,
                     operands=list(parts) + [w, m1, v2], comm=comm)
    return out if comm is None else (out, got)


SMALL_WEIGHTS = ("ffn1_norm", "mix_norm", "b_forget", "g_fox", "g_sb", "ffn2_norm", "final_norm")
WEIGHT_ORDER = ("meta_tokens", "ffn1_norm", "ffn1_w_gate", "ffn1_w_up", "ffn1_w_down", "mix_norm", "w_in", "b_forget",
                "g_fox", "g_sb", "w_out", "ffn2_norm", "ffn2_w_gate", "ffn2_w_up", "ffn2_w_down", "final_norm")
GROUPS = {"ffn1": ("ffn1_w_gate", "ffn1_w_up", "ffn1_w_down"), "mix": ("w_in", "w_out"),
          "ffn2": ("ffn2_w_gate", "ffn2_w_up", "ffn2_w_down")}
TRANSPOSED = ("ffn1_w_gate", "ffn1_w_up", "ffn2_w_gate", "ffn2_w_up")


def _pad_lanes(a):
    extra = (-a.shape[-1]) % LANES
    return a if extra == 0 else jnp.pad(a, [(0, 0)] * (a.ndim - 1) + [(0, extra)])


def _ffn_backward(dh_b, dh, saved, gain, wg, wu, wd, tag, carried=()):
    h, xn, g, u, act = saved

    def half(k, which, into=()):
        if len(carried) <= k:
            return None
        rows = carried[k].shape[1]
        cut = rows // 2 // 16 * 16
        return _Exchange([carried[k]], rows=(0, cut) if which == 0 else (cut, rows - cut), into=into)

    (dg, du), got_b = _ffn_bwd_act(dh_b, wd, g, u, f"{tag}_bwd_act", half(1, 0))
    d_wd = _ffn_dw(act, dh_b, f"{tag}_dwd", alpha=0.5, comm=half(1, 1, got_b))
    d_wg = _ffn_dw(dg, xn, f"{tag}_dwg", comm=half(2, 0))
    got_c = d_wg[1] if len(carried) > 2 else ()
    d_wu = _ffn_dw(du, xn, f"{tag}_dwu", comm=half(2, 1, got_c))
    dxn = _ffn_contract([(dg, wg), (du, wu)], F32, f"{tag}_dxn", group=2, comm=_Exchange(carried[:1]) if carried else None)
    got = []
    if carried:
        dxn, got = dxn[0], list(dxn[1])
    if len(carried) > 1:
        d_wd, got_b = d_wd
        got += got_b
    if len(carried) > 2:
        d_wg, (d_wu, got_c) = d_wg[0], d_wu
        got += got_c
    dh_in, dh_in_b, d_gain = _rms_bwd(dxn, h, gain, dh, f"{tag}_norm_bwd")
    return dh_in, dh_in_b, d_gain, d_wg, d_wu, d_wd, got


def _ffn_backward_last(dh_b, dh, saved, gain, wg, wu, wd, tag, first, second):
    h, xn, g, u, act = saved
    (dg, du), got_first = _ffn_bwd_act(dh_b, wd, g, u, f"{tag}_bwd_act", first)
    d_wd, got_second = _ffn_dw(act, dh_b, f"{tag}_dwd", alpha=0.5, comm=second)
    c = d_wd.shape[1]
    half = c // 2 // 16 * 16
    d_wg, got_wd = _ffn_dw(dg, xn, f"{tag}_dwg", comm=_Exchange([d_wd], rows=(0, half)))
    d_wu, got_wd = _ffn_dw(du, xn, f"{tag}_dwu", comm=_Exchange([d_wd], rows=(half, c - half), into=got_wd))
    dxn, got_wg = _ffn_contract([(dg, wg), (du, wu)], F32, f"{tag}_dxn", group=2, comm=_Exchange([d_wg]))
    dh_in, dh_in_b, d_gain = _rms_bwd(dxn, h, gain, dh, f"{tag}_norm_bwd")
    return dh_in, dh_in_b, d_gain, got_first, got_second, got_wg[0], got_wd[0], d_wu


def _mixer_forward(h, gain, w_in_t, bias, g_fox, g_sb, w_out, n_heads, pad, tag, comm_fox=None, comm_sb=None):
    m = h.shape[0]
    t = _attn_block(m)
    hw = n_heads * HEAD_DIM
    xn = _rms_fwd(h, gain, f"{tag}_norm")
    qkv = _mm([(xn, w_in_t[:6 * hw])], "nt", BF16, name=f"{tag}_qkv")
    fl = _mm([(xn, w_in_t[6 * hw:])], "nt", F32, name=f"{tag}_forget")
    c = _forget_cumsum(fl, bias, n_heads, pad, f"{tag}_cumsum")
    c_heads = c[:, :n_heads].T
    ccol = c_heads[:, :, None]
    crow = c_heads.reshape(n_heads, m // t, 1, t)
    (o_f, on_f, lse), got_fox = _fox_fwd(qkv, ccol, crow, g_fox, n_heads, pad, f"{tag}_fox", comm_fox)
    (o_s, on_s, runs), got_sb = _sb_fwd(qkv, g_sb, n_heads, pad, f"{tag}_sb", comm_sb)
    if w_out is None:
        w_out = got_sb[-1].reshape(-1, h.shape[1])
    h_out = _mm([(on_f, w_out[:hw]), (on_s, w_out[hw:])], "nn", F32, name=f"{tag}_out", res=h)
    return h_out, (h, xn, qkv, fl, ccol, crow, o_f, on_f, lse, o_s, on_s, runs), got_fox, got_sb, w_out


def _mixer_backward(dh_b, dh_out, saved, gain, w_in_t, bias, g_fox, g_sb, w_out, n_heads, pad, tag, comm_fox=None, comm_sb=None,
                    rows_in_dxn=0):
    h, xn, qkv, fl, ccol, crow, o_f, on_f, lse, o_s, on_s, runs = saved
    m = h.shape[0]
    hw = n_heads * HEAD_DIM
    d_on_f = _mm([(dh_b, w_out[:hw])], "nt", F32, name=f"{tag}_don_f")
    d_on_s = _mm([(dh_b, w_out[hw:])], "nt", F32, name=f"{tag}_don_s")
    d_wout = jnp.concatenate([_mm([(on_f, dh_b)], "tn", BF16, name=f"{tag}_dwout_f"),
                              _mm([(on_s, dh_b)], "tn", BF16, name=f"{tag}_dwout_s")], axis=0)
    (dq_f, dk_f, dv_f, dg_fox, dccol, dcrow), got_fox = _fox_bwd(
        qkv, o_f, d_on_f, g_fox, lse, ccol, crow, n_heads, pad, f"{tag}_fox_bwd", comm_fox)
    (dq_s, dk_s, dv_s, dg_sb), got_sb = _sb_bwd(qkv, o_s, d_on_s, g_sb, runs, n_heads, pad, f"{tag}_sb_bwd", comm_sb)
    dc_a = _pad_lanes(dccol[:, :, 0].T)
    dc_b = _pad_lanes(dcrow.reshape(n_heads, m).T)
    dfl, dbias = _forget_cumsum_bwd(dc_a, dc_b, fl, bias, n_heads, pad, f"{tag}_cumsum_bwd")
    dproj = jnp.concatenate([dq_f, dk_f.astype(BF16), dv_f.astype(BF16), dq_s, dk_s.astype(BF16), dv_s.astype(BF16),
                             dfl.astype(BF16)], axis=1)
    d_win = _mm([(dproj, xn)], "tn", BF16, name=f"{tag}_dwin")[:6 * hw + n_heads].reshape(N_DEV, -1, h.shape[1])
    got_win = None
    if rows_in_dxn:
        dxn, got_win = _mm([(dproj, w_in_t)], "nn", F32, name=f"{tag}_dxn", whole_k=True,
                           comm=_Exchange([d_win], rows=(0, rows_in_dxn)))
    else:
        dxn = _mm([(dproj, w_in_t)], "nn", F32, name=f"{tag}_dxn", whole_k=True)
    dh, dh_in_b, d_gain = _rms_bwd(dxn, h, gain, dh_out, f"{tag}_norm_bwd")
    return dh, dh_in_b, d_gain, d_win, dbias, dg_fox, dg_sb, d_wout, got_fox, got_sb, got_win


def kernel(x, meta_tokens, ffn1_norm, ffn1_w_gate, ffn1_w_up, ffn1_w_down, mix_norm, w_in, b_forget, g_fox, g_sb, w_out, ffn2_norm, ffn2_w_gate, ffn2_w_up, ffn2_w_down, final_norm, loss_target, m_meta_tokens, m_ffn1_norm, m_ffn1_w_gate, m_ffn1_w_up, m_ffn1_w_down, m_mix_norm, m_w_in, m_b_forget, m_g_fox, m_g_sb, m_w_out, m_ffn2_norm, m_ffn2_w_gate, m_ffn2_w_up, m_ffn2_w_down, m_final_norm, v_meta_tokens, v_ffn1_norm, v_ffn1_w_gate, v_ffn1_w_up, v_ffn1_w_down, v_mix_norm, v_w_in, v_b_forget, v_g_fox, v_g_sb, v_w_out, v_ffn2_norm, v_ffn2_w_gate, v_ffn2_w_up, v_ffn2_w_down, v_final_norm):
    given = dict(locals())
    seq, d = x.shape[1], x.shape[2]
    depth = ffn1_norm.shape[0]
    d_in = N_DEV * w_in.shape[2]
    n_heads = g_fox.shape[1] // HEAD_DIM
    hw = n_heads * HEAD_DIM
    assert seq % ROW_BLOCK == 0 and d_in == 6 * hw + n_heads and n_heads <= LANES
    pad = (-(seq + N_META)) % ROW_BLOCK
    x_off = pad + N_META

    def view(n, a):
        if n in TRANSPOSED:
            return jnp.swapaxes(a, 1, 2)
        return a.transpose(2, 0, 1) if n == "w_in" else a

    def unview(n, a):
        if n in TRANSPOSED:
            return jnp.swapaxes(a, 1, 2)
        return a.transpose(1, 2, 0) if n == "w_in" else a

    def shard(n, l):
        v = view(n, given[n])
        return (v[:, l] if n == "w_in" else v[l]).astype(BF16)

    def shards(group, l):
        return [shard(n, l) for n in GROUPS[group]]

    sh = shards("ffn1", 0)
    first = _run_alone(_Gather([sh[0], meta_tokens]), "gather_first")
    full = {}
    meta_full = first[1].transpose(1, 0, 2).reshape(N_META, d)
    h = jnp.concatenate([jnp.zeros((pad, d), F32), meta_full, x[0]], axis=0)
    weights, saved = [], []
    for l in range(depth):
        xn = _rms_fwd(h, ffn1_norm[l:l + 1], "ffn1_norm")
        if l == 0:
            mix0 = shards("mix", 0)
            g, (wu1,) = _ffn_gate(xn, first[0], "ffn1_gate", _Gather([sh[1]]))
            (u, act), (wd1,) = _ffn_up_given_gate(xn, wu1, g, "ffn1_up", _Gather([sh[2]]))
            full[("ffn1", 0)] = (first[0], wu1, wd1)
            h_out, (win3,) = _ffn_contract([(act, wd1)], F32, "ffn1_down", group=4, alpha=0.5, res=h, comm=_Gather(mix0[:1]))
        else:
            wg1, wu1, wd1 = full[("ffn1", l)]
            (g, u, act), full[("mix", l)] = _ffn_up(xn, wg1, wu1, "ffn1_up", _Gather(shards("mix", l)))
            h_out = _ffn_contract([(act, wd1)], F32, "ffn1_down", group=4, alpha=0.5, res=h)
        s1 = (h, xn, g, u, act)
        h = h_out
        if l > 0:
            win3, wout3 = full[("mix", l)]
        w_in_t = jnp.pad(win3.reshape(d_in, d), ((0, 6 * hw + LANES - d_in), (0, 0)))
        bias = _pad_lanes(b_forget[l:l + 1])
        in_sb = (shards("ffn1", l + 1) if l + 1 < depth else []) + (mix0[1:] if l == 0 else [])
        h, sm, full[("ffn2", l)], got, w_out_full = _mixer_forward(
            h, mix_norm[l:l + 1], w_in_t, bias, g_fox[l:l + 1], g_sb[l:l + 1],
            None if l == 0 else wout3.reshape(N_DEV * wout3.shape[1], d), n_heads, pad, "mix",
            _Gather(shards("ffn2", l)), _Gather(in_sb) if in_sb else None)
        if l + 1 < depth:
            full[("ffn1", l + 1)] = got[:3]
        wg2, wu2, wd2 = full[("ffn2", l)]
        xn = _rms_fwd(h, ffn2_norm[l:l + 1], "ffn2_norm")
        (g, u, act), _ = _ffn_up(xn, wg2, wu2, "ffn2_up")
        s2 = (h, xn, g, u, act)
        h = _ffn_contract([(act, wd2)], F32, "ffn2_down", group=4, alpha=0.5, res=h)
        weights.append((w_in_t, w_out_full, bias))
        saved.append((s1, sm, s2))

    dh, dh_b, d_final, loss_arr = _loss_head(h, final_norm[None, :], loss_target[0], x_off, "loss_head")
    small = {n: [None] * depth for n in SMALL_WEIGHTS[:-1]}
    partial, received = {}, {}

    def names(group, l):
        return [(n, l) for n in GROUPS[group]]

    def send(keys):
        return [partial[k] for k in keys]

    for l in reversed(range(depth)):
        w_in_t, w_out_full, bias = weights[l]
        s1, sm, s2 = saved[l]
        wg2, wu2, wd2 = full[("ffn2", l)]
        up = l + 1 < depth
        in_dxn = names("ffn1", l + 1) if up else []
        in_fox = names("mix", l + 1) if up else []
        in_sb = names("ffn2", l)
        dh, dh_b, small["ffn2_norm"][l], partial[("ffn2_w_gate", l)], partial[("ffn2_w_up", l)], partial[("ffn2_w_down", l)], got = (
            _ffn_backward(dh_b, dh, s2, ffn2_norm[l:l + 1], wg2, wu2, wd2, "ffn2", send(in_dxn)))
        received.update(zip(in_dxn, got))
        win_rows = (d_in // N_DEV // 2 // 16 * 16) if l == 0 else 0
        (dh, dh_b, small["mix_norm"][l], partial[("w_in", l)], dbias, small["g_fox"][l], small["g_sb"][l], d_wout,
         got_fox, got_sb, got_win) = _mixer_backward(
            dh_b, dh, sm, mix_norm[l:l + 1], w_in_t, bias, g_fox[l:l + 1], g_sb[l:l + 1], w_out_full, n_heads, pad, "mix",
            _Exchange(send(in_fox)) if in_fox else None, _Exchange(send(in_sb)), win_rows)
        received.update(zip(in_fox, got_fox or []))
        received.update(zip(in_sb, got_sb))
        partial[("w_out", l)] = d_wout.reshape(N_DEV, -1, d)
        small["b_forget"][l] = dbias[:, :n_heads]
        wg1, wu1, wd1 = full[("ffn1", l)]
        if l > 0:
            dh, dh_b, small["ffn1_norm"][l], partial[("ffn1_w_gate", l)], partial[("ffn1_w_up", l)], partial[("ffn1_w_down", l)], _ = (
                _ffn_backward(dh_b, dh, s1, ffn1_norm[l:l + 1], wg1, wu1, wd1, "ffn1"))
        else:
            rest = (win_rows, d_in // N_DEV - win_rows)
            (dh, dh_b, small["ffn1_norm"][0], got_a, got_b, received[("ffn1_w_gate", 0)], received[("ffn1_w_down", 0)],
             last_grad) = _ffn_backward_last(
                dh_b, dh, s1, ffn1_norm[0:1], wg1, wu1, wd1, "ffn1", _Exchange([partial[("w_out", 0)]]),
                _Exchange([partial[("w_in", 0)]], rows=rest, into=got_win))
            received[("w_out", 0)], received[("w_in", 0)] = got_a[0], got_b[0]
    grad_x = dh[x_off:][None]
    d_meta = dh[pad:x_off].reshape(N_META, N_DEV, -1).transpose(1, 0, 2)
    received[("meta_tokens", 0)] = _run_alone(_Exchange([d_meta]), "exchange_meta")[0]

    vec = [loss_arr[0:1, :]] + [_pad_lanes(jnp.concatenate(small[n], axis=0).reshape(1, -1)) for n in SMALL_WEIGHTS[:-1]]
    vec.append(d_final)
    sizes = [a.shape[1] for a in vec]
    summed = _all_reduce_small(jnp.concatenate(vec, axis=1), "reduce_small")
    loss = summed[0, 0]

    def packed(prefix):
        cols = [jnp.zeros((1, LANES), F32)]
        cols += [_pad_lanes(given[prefix + n].reshape(1, -1)) for n in SMALL_WEIGHTS]
        return jnp.concatenate(cols, axis=1)[None]

    small_out = _adamw([summed[None]], packed(""), packed("m_"), packed("v_"), "adamw_small")

    c_last = last_grad.shape[1]
    quarter = c_last // 4 // 16 * 16
    pieces = [(k * quarter, quarter if k < 3 else c_last - 3 * quarter) for k in range(4)]
    order = ["ffn2_w_gate", "ffn2_w_up", "ffn2_w_down", "ffn1_w_gate", "ffn1_w_down", "w_in", "w_out", "ffn1_w_up"]
    out, arrived = {}, []
    for k, n in enumerate(order):
        wv, mv, vv = (view(n, given[p + n]) for p in ("", "m_", "v_"))
        if n == "ffn1_w_up":
            received[(n, 0)] = arrived[0]
        if n == "w_in":
            per_layer = [_adamw([received[(n, l)]], wv[:, l][None], mv[:, l][None], vv[:, l][None], "adamw_" + n)
                         for l in range(depth)]
            res = [jnp.stack([per_layer[l][k][0] for l in range(depth)], axis=1) for k in range(4)]
        elif k < len(pieces):
            res, arrived = _adamw([received[(n, l)] for l in range(depth)], wv, mv, vv, "adamw_" + n,
                                  _Exchange([last_grad], rows=pieces[k], into=arrived))
        else:
            res = _adamw([received[(n, l)] for l in range(depth)], wv, mv, vv, "adamw_" + n)
        out[n] = [unview(n, r) for r in res]
    out["meta_tokens"] = [r[0] for r in _adamw([received[("meta_tokens", 0)]], meta_tokens[None], m_meta_tokens[None],
                                               v_meta_tokens[None], "adamw_meta_tokens")]
    offset = sizes[0]
    for n, size in zip(SMALL_WEIGHTS, sizes[1:]):
        shape, count = given[n].shape, given[n].size
        out[n] = [r[0, 0, offset:offset + count].reshape(shape) for r in small_out]
        offset += size

    result = [loss, grad_x]
    for k in range(4):
        result += [out[n][k] for n in WEIGHT_ORDER]
    return tuple(result)
```

```python
import math

import jax
import jax.numpy as jnp
from jax import lax
from jax.experimental import pallas as pl
from jax.experimental.pallas import tpu as pltpu

F32 = jnp.float32
BF16 = jnp.bfloat16

N_DEV = 8
N_META = 16
HEAD_DIM = 128
ROW_BLOCK = 128
LANES = 128
EPS = 1e-6
NEG = -1e30
ADAM_LR = 0.001
ADAM_B1 = 0.9
ADAM_B2 = 0.999
ADAM_EPS = 1e-08
ADAM_WD = 0.01
ADAM_STEP = 10
VMEM_LIMIT_BYTES = 56 * 1024 * 1024
MESH = pl.DeviceIdType.MESH

NT_DIMS = (((1,), (1,)), ((), ()))
TN_DIMS = (((0,), (0,)), ((), ()))
NN_DIMS = (((1,), (0,)), ((), ()))
ANY = pl.BlockSpec(memory_space=pl.ANY)


def _tile(n, cap, align):
    best = None
    for d in range(align, min(n, cap) + 1, align):
        if n % d == 0:
            best = d
    return best if best is not None else n


def _dot(a, b, dims=NN_DIMS):
    return lax.dot_general(a, b, dims, preferred_element_type=F32)


def _dot_split(x, u):
    hi = x.astype(BF16)
    lo = (x - hi.astype(F32)).astype(BF16)
    return _dot(hi, u) + _dot(lo, u)


def _my_position():
    return lax.axis_index("x"), lax.axis_index("y"), lax.axis_index("c")


class _Gather:
    n_phases = 3

    def __init__(self, arrs):
        self.arrs = list(arrs)
        n = len(self.arrs)
        self.out_shapes = [jax.ShapeDtypeStruct((N_DEV,) + a.shape, a.dtype) for a in self.arrs]
        self.scratch = [pltpu.SemaphoreType.DMA((n, 7)), pltpu.SemaphoreType.DMA((n, 7)), pltpu.SemaphoreType.DMA((n,))]

    def phase(self, p, ins, outs, sems):
        send_sems, recv_sems, local_sems = sems
        n = len(self.arrs)
        x, y, c = _my_position()
        me, sibling = (x, y, c), (x, y, 1 - c)
        chips = [(1 - x, y), (x, 1 - y), (1 - x, 1 - y)]

        def copy(a, k, block, to, src=None):
            slot = outs[a].at[4 * block[0] + 2 * block[1] + block[2]]
            return pltpu.make_async_remote_copy(
                src_ref=slot if src is None else src, dst_ref=slot,
                send_sem=send_sems.at[a, k], recv_sem=recv_sems.at[a, k], device_id=to, device_id_type=MESH)

        def local(a):
            return pltpu.make_async_copy(ins[a], outs[a].at[4 * x + 2 * y + c], local_sems.at[a])

        def first(a):
            return [copy(a, 0, me, sibling, src=ins[a])] + [copy(a, 1 + j, me, (*chip, c), src=ins[a]) for j, chip in enumerate(chips)]

        def passed(a, j):
            return copy(a, 4 + j, (*chips[j], c), sibling)

        if p == 0:
            for a in range(n):
                local(a).start()
            for a in range(n):
                for cp in first(a):
                    cp.start()
        elif p == 1:
            for a in range(n):
                for j, chip in enumerate(chips):
                    copy(a, 1 + j, (*chip, c), me).wait_recv()
                    passed(a, j).start()
        else:
            for a in range(n):
                copy(a, 0, sibling, me).wait_recv()
                for j, chip in enumerate(chips):
                    copy(a, 4 + j, (*chip, 1 - c), me).wait_recv()
            for a in range(n):
                for cp in first(a) + [passed(a, j) for j in range(3)]:
                    cp.wait_send()
                local(a).wait()


class _Exchange:
    n_phases = 2

    def __init__(self, arrs, rows=None, into=()):
        self.arrs = list(arrs)
        self.rows = rows
        self.into = list(into)
        n = len(self.arrs)
        self.out_shapes = [jax.ShapeDtypeStruct(a.shape, a.dtype) for a in self.arrs]
        self.scratch = [pltpu.SemaphoreType.DMA((n, 7)), pltpu.SemaphoreType.DMA((n, 7)), pltpu.SemaphoreType.DMA((n,))]

    def phase(self, p, ins, outs, sems):
        send_sems, recv_sems, local_sems = sems
        n = len(self.arrs)
        x, y, c = _my_position()
        me = 4 * x + 2 * y + c

        def peer_of(r):
            return (x ^ ((r >> 2) & 1), y ^ ((r >> 1) & 1), c ^ (r & 1))

        def part(ref, d):
            return ref.at[d] if self.rows is None else ref.at[d, pl.ds(self.rows[0], self.rows[1])]

        def copy(a, r):
            px, py, pc = peer_of(r)
            return pltpu.make_async_remote_copy(
                src_ref=part(ins[a], 4 * px + 2 * py + pc), dst_ref=part(outs[a], me),
                send_sem=send_sems.at[a, r - 1], recv_sem=recv_sems.at[a, r - 1],
                device_id=(px, py, pc), device_id_type=MESH)

        def arrival(a, r):
            px, py, pc = peer_of(r)
            slot = part(outs[a], 4 * px + 2 * py + pc)
            return pltpu.make_async_remote_copy(
                src_ref=slot, dst_ref=slot, send_sem=send_sems.at[a, r - 1], recv_sem=recv_sems.at[a, r - 1],
                device_id=(px, py, pc), device_id_type=MESH)

        def local(a):
            return pltpu.make_async_copy(part(ins[a], me), part(outs[a], me), local_sems.at[a])

        if p == 0:
            for a in range(n):
                local(a).start()
            for a in range(n):
                for r in range(1, N_DEV):
                    copy(a, r).start()
        else:
            for a in range(n):
                for r in range(1, N_DEV):
                    arrival(a, r).wait_recv()
            for a in range(n):
                for r in range(1, N_DEV):
                    copy(a, r).wait_send()
                local(a).wait()


def _run_alone(comm, name):
    n = len(comm.arrs)

    def body(*refs):
        for p in range(comm.n_phases):
            comm.phase(p, refs[:n], refs[n:2 * n], refs[2 * n:])

    return pl.pallas_call(body, name=name, out_shape=comm.out_shapes, in_specs=[ANY] * n, out_specs=[ANY] * n,
                          scratch_shapes=comm.scratch)(*comm.arrs)


def _call(body, *, name, grid, in_specs, out_specs, out_shape, operands, scratch_shapes=(), comm=None):
    scratch_shapes = list(scratch_shapes)
    params = pltpu.CompilerParams(dimension_semantics=("arbitrary",) * len(grid), vmem_limit_bytes=VMEM_LIMIT_BYTES)
    if comm is None:
        res = pl.pallas_call(body, name=name, grid=grid, in_specs=in_specs, out_specs=out_specs, out_shape=out_shape,
                             scratch_shapes=scratch_shapes, compiler_params=params)(*operands)
        return res, None
    n_in, n_out, n_sc = len(in_specs), len(out_specs), len(scratch_shapes)
    nc = len(comm.arrs)
    into = getattr(comm, "into", [])
    total = math.prod(grid)
    at = {0: 0, comm.n_phases - 1: total - 1}
    for p in range(1, comm.n_phases - 1):
        at[p] = (total * 7) // 8

    def wrapped(*refs):
        ins, cins = refs[:n_in], refs[n_in:n_in + nc]
        refs = refs[n_in + nc + len(into):]
        outs, couts = refs[:n_out], refs[n_out:n_out + nc]
        rest = refs[n_out + nc:]
        scratch, sems = rest[:n_sc], rest[n_sc:]
        step = 0
        for axis, size in enumerate(grid):
            step = step * size + pl.program_id(axis)
        for p in range(comm.n_phases - 1):
            @pl.when(step == at[p])
            def _(p=p):
                comm.phase(p, cins, couts, sems)
        body(*ins, *outs, *scratch)

        @pl.when(step == total - 1)
        def _():
            comm.phase(comm.n_phases - 1, cins, couts, sems)

    res = pl.pallas_call(
        wrapped, name=name, grid=grid, in_specs=list(in_specs) + [ANY] * (nc + len(into)),
        out_specs=list(out_specs) + [ANY] * nc, out_shape=list(out_shape) + comm.out_shapes,
        scratch_shapes=scratch_shapes + comm.scratch,
        input_output_aliases={n_in + nc + k: n_out + k for k in range(len(into))},
        compiler_params=params)(*operands, *comm.arrs, *into)
    return res[:n_out], res[n_out:]


def _all_reduce_small(vec, name):
    n = vec.shape[1]

    def body(v_ref, o_ref, buf, send_sems, recv_sems):
        x, y, c = _my_position()
        me = 4 * x + 2 * y + c

        def peer_of(r):
            return (x ^ ((r >> 2) & 1), y ^ ((r >> 1) & 1), c ^ (r & 1))

        def copy(r):
            px, py, pc = peer_of(r)
            return pltpu.make_async_remote_copy(
                src_ref=v_ref, dst_ref=buf.at[me], send_sem=send_sems.at[r - 1], recv_sem=recv_sems.at[r - 1],
                device_id=(px, py, pc), device_id_type=MESH)

        def arrival(r):
            px, py, pc = peer_of(r)
            slot = buf.at[4 * px + 2 * py + pc]
            return pltpu.make_async_remote_copy(
                src_ref=slot, dst_ref=slot, send_sem=send_sems.at[r - 1], recv_sem=recv_sems.at[r - 1],
                device_id=(px, py, pc), device_id_type=MESH)

        sends = [copy(r) for r in range(1, N_DEV)]
        for cp in sends:
            cp.start()
        buf[me] = v_ref[...]
        for r in range(1, N_DEV):
            arrival(r).wait_recv()
        for cp in sends:
            cp.wait_send()
        total = buf[0]
        for d in range(1, N_DEV):
            total = total + buf[d]
        o_ref[...] = total

    vmem = pl.BlockSpec(memory_space=pltpu.VMEM)
    return pl.pallas_call(
        body, name=name, out_shape=jax.ShapeDtypeStruct((1, n), F32), in_specs=[vmem], out_specs=vmem,
        scratch_shapes=[pltpu.VMEM((N_DEV, 1, n), F32), pltpu.SemaphoreType.DMA((7,)), pltpu.SemaphoreType.DMA((7,))],
    )(vec)


def _mm_core(pairs, dims, out_dtype, *, name, grid, out_shape, out_spec, acc_shape, alpha=1.0, res=None, comm=None):
    nk = grid[2]
    npairs = len(pairs)

    def body(*refs):
        ab = refs[:2 * npairs]
        rest = refs[2 * npairs:]
        res_ref = rest[0] if res is not None else None
        o_ref = rest[1] if res is not None else rest[0]
        acc_ref = rest[-1] if nk > 1 else None
        part = None
        for p in range(npairs):
            a_ref, b_ref = ab[2 * p], ab[2 * p + 1]
            shards = [(a_ref[s], b_ref[s]) for s in range(a_ref.shape[0])] if len(a_ref.shape) == 3 else [(a_ref[...], b_ref[...])]
            for av, bv in shards:
                d = _dot(av.astype(BF16), bv.astype(BF16), dims)
                part = d if part is None else part + d

        def finish(total):
            val = total * alpha if alpha != 1.0 else total
            if res_ref is not None:
                val = res_ref[...] + val
            o_ref[...] = val.astype(out_dtype)

        if nk == 1:
            finish(part)
        else:
            kk = pl.program_id(2)

            @pl.when(kk == 0)
            def _():
                acc_ref[...] = part

            @pl.when(kk > 0)
            def _():
                acc_ref[...] += part

            @pl.when(kk == nk - 1)
            def _():
                finish(acc_ref[...])

    operands, in_specs = [], []
    for (a, a_spec), (b, b_spec) in pairs:
        operands += [a, b]
        in_specs += [a_spec, b_spec]
    if res is not None:
        operands.append(res[0])
        in_specs.append(res[1])
    out, got = _call(body, name=name, grid=grid, in_specs=in_specs, out_specs=[out_spec],
                     out_shape=[jax.ShapeDtypeStruct(out_shape, out_dtype)], operands=operands,
                     scratch_shapes=[pltpu.VMEM(acc_shape, F32)] if nk > 1 else [], comm=comm)
    return out[0] if comm is None else (out[0], got)


def _mm(pairs, mode, out_dtype, *, name, alpha=1.0, res=None, comm=None, whole_k=False):
    a0, b0 = pairs[0]
    if mode == "nn":
        (m, k), n = a0.shape, b0.shape[1]
    elif mode == "nt":
        (m, k), n = a0.shape, b0.shape[0]
    else:
        (k, m), n = a0.shape, b0.shape[1]
    dims = {"nn": NN_DIMS, "nt": NT_DIMS, "tn": TN_DIMS}[mode]
    tm = _tile(m, 528 if whole_k else 1056, LANES if mode == "tn" else 16)
    tn = _tile(n, 512 if whole_k else 1024, LANES)
    tk = k if whole_k else _tile(k, 2048 if mode != "tn" else 2112, LANES if mode != "tn" else 16)
    a_spec = pl.BlockSpec((tk, tm), lambda i, j, kk: (kk, i)) if mode == "tn" else pl.BlockSpec((tm, tk), lambda i, j, kk: (i, kk))
    b_spec = pl.BlockSpec((tn, tk), lambda i, j, kk: (j, kk)) if mode == "nt" else pl.BlockSpec((tk, tn), lambda i, j, kk: (kk, j))
    o_spec = pl.BlockSpec((tm, tn), lambda i, j, kk: (i, j))
    return _mm_core([((a, a_spec), (b, b_spec)) for a, b in pairs], dims, out_dtype, name=name,
                    grid=(m // tm, n // tn, k // tk), out_shape=(m, n), out_spec=o_spec, acc_shape=(tm, tn),
                    alpha=alpha, res=None if res is None else (res, o_spec), comm=comm)


def _rms_fwd(h, gain, name):
    m, d = h.shape
    tm = _tile(m, 528, 16)

    def body(h_ref, g_ref, o_ref):
        hv = h_ref[...]
        r = lax.rsqrt(jnp.mean(hv * hv, axis=-1, keepdims=True) + EPS)
        o_ref[...] = (hv * r * g_ref[...]).astype(BF16)

    row = pl.BlockSpec((tm, d), lambda i: (i, 0))
    out, _ = _call(body, name=name, grid=(m // tm,), in_specs=[row, pl.BlockSpec((1, d), lambda i: (0, 0))],
                   out_specs=[row], out_shape=[jax.ShapeDtypeStruct((m, d), BF16)], operands=[h, gain])
    return out[0]


def _rms_bwd(dxn, h, gain, dres, name):
    m, d = h.shape
    tm = _tile(m, 264, 16)

    def body(dxn_ref, h_ref, g_ref, dres_ref, dh_ref, dhb_ref, dg_ref):
        hv = h_ref[...]
        r = lax.rsqrt(jnp.mean(hv * hv, axis=-1, keepdims=True) + EPS)
        xhat = hv * r
        dxn_v = dxn_ref[...]
        t = dxn_v * g_ref[...]
        dh = dres_ref[...] + r * (t - xhat * jnp.mean(t * xhat, axis=-1, keepdims=True))
        dh_ref[...] = dh
        dhb_ref[...] = dh.astype(BF16)
        part = jnp.sum(dxn_v * xhat, axis=0, keepdims=True)

        @pl.when(pl.program_id(0) == 0)
        def _():
            dg_ref[...] = part

        @pl.when(pl.program_id(0) > 0)
        def _():
            dg_ref[...] += part

    row = pl.BlockSpec((tm, d), lambda i: (i, 0))
    vec = pl.BlockSpec((1, d), lambda i: (0, 0))
    out, _ = _call(body, name=name, grid=(m // tm,), in_specs=[row, row, vec, row], out_specs=[row, row, vec],
                   out_shape=[jax.ShapeDtypeStruct((m, d), F32), jax.ShapeDtypeStruct((m, d), BF16),
                              jax.ShapeDtypeStruct((1, d), F32)],
                   operands=[dxn, h, gain, dres])
    return out


def _loss_head(h, gain, target, x_off, name):
    m, d = h.shape
    tm = ROW_BLOCK
    first = x_off // tm

    def body(h_ref, g_ref, t_ref, dh_ref, dhb_ref, dg_ref, loss_ref):
        i = pl.program_id(0)

        @pl.when(i == 0)
        def _():
            dg_ref[...] = jnp.zeros_like(dg_ref)
            loss_ref[...] = jnp.zeros_like(loss_ref)

        @pl.when(i < first)
        def _():
            dh_ref[...] = jnp.zeros_like(dh_ref)
            dhb_ref[...] = jnp.zeros_like(dhb_ref)

        @pl.when(i >= first)
        def _():
            hv = h_ref[...]
            g = g_ref[...]
            r = lax.rsqrt(jnp.mean(hv * hv, axis=-1, keepdims=True) + EPS)
            xhat = hv * r
            err = xhat * g - t_ref[...]
            loss_ref[...] += 0.5 * jnp.sum(jnp.mean(err * err, axis=-1, keepdims=True))
            dy = err * (1.0 / d)
            t = dy * g
            dh = r * (t - xhat * jnp.mean(t * xhat, axis=-1, keepdims=True))
            dh_ref[...] = dh
            dhb_ref[...] = dh.astype(BF16)
            dg_ref[...] += jnp.sum(dy * xhat, axis=0, keepdims=True)

    row = pl.BlockSpec((tm, d), lambda i: (i, 0))
    vec = pl.BlockSpec((1, d), lambda i: (0, 0))
    out, _ = _call(body, name=name, grid=(m // tm,),
                   in_specs=[row, vec, pl.BlockSpec((tm, d), lambda i: (jnp.maximum(i - first, 0), 0))],
                   out_specs=[row, row, vec, pl.BlockSpec((8, LANES), lambda i: (0, 0))],
                   out_shape=[jax.ShapeDtypeStruct((m, d), F32), jax.ShapeDtypeStruct((m, d), BF16),
                              jax.ShapeDtypeStruct((1, d), F32), jax.ShapeDtypeStruct((8, LANES), F32)],
                   operands=[h, gain, target])
    return out


def _sigmoid(z):
    return 1.0 / (1.0 + jnp.exp(-z))


def _ffn_up(xn, wg, wu, name, comm=None):
    m, d = xn.shape
    nsh, c, _ = wg.shape
    tm = _tile(m, 1056, 16)

    def body(x_ref, wg_ref, wu_ref, g_ref, u_ref, a_ref):
        xv = x_ref[...]
        g = _dot(xv, wg_ref[...], NT_DIMS)
        u = _dot(xv, wu_ref[...], NT_DIMS)
        g_ref[...] = g.astype(BF16)
        u_ref[...] = u.astype(BF16)
        a_ref[...] = (g * _sigmoid(g) * u).astype(BF16)

    out = pl.BlockSpec((None, tm, c), lambda i, j: (j, i, 0))
    w = pl.BlockSpec((None, c, d), lambda i, j: (j, 0, 0))
    return _call(body, name=name, grid=(m // tm, nsh), in_specs=[pl.BlockSpec((tm, d), lambda i, j: (i, 0)), w, w],
                 out_specs=[out, out, out], out_shape=[jax.ShapeDtypeStruct((nsh, m, c), BF16)] * 3,
                 operands=[xn, wg, wu], comm=comm)


def _ffn_gate(xn, wg, name, comm=None):
    m, d = xn.shape
    nsh, c, _ = wg.shape
    tm = _tile(m, 1056, 16)

    def body(x_ref, wg_ref, g_ref):
        g_ref[...] = _dot(x_ref[...], wg_ref[...], NT_DIMS).astype(BF16)

    out, got = _call(body, name=name, grid=(m // tm, nsh),
                     in_specs=[pl.BlockSpec((tm, d), lambda i, j: (i, 0)), pl.BlockSpec((None, c, d), lambda i, j: (j, 0, 0))],
                     out_specs=[pl.BlockSpec((None, tm, c), lambda i, j: (j, i, 0))],
                     out_shape=[jax.ShapeDtypeStruct((nsh, m, c), BF16)], operands=[xn, wg], comm=comm)
    return out[0], got


def _ffn_up_given_gate(xn, wu, g, name, comm=None):
    m, d = xn.shape
    nsh, c, _ = wu.shape
    tm = _tile(m, 1056, 16)

    def body(x_ref, wu_ref, g_ref, u_ref, a_ref):
        u = _dot(x_ref[...], wu_ref[...], NT_DIMS)
        g = g_ref[...].astype(F32)
        u_ref[...] = u.astype(BF16)
        a_ref[...] = (g * _sigmoid(g) * u).astype(BF16)

    blk = pl.BlockSpec((None, tm, c), lambda i, j: (j, i, 0))
    return _call(body, name=name, grid=(m // tm, nsh),
                 in_specs=[pl.BlockSpec((tm, d), lambda i, j: (i, 0)), pl.BlockSpec((None, c, d), lambda i, j: (j, 0, 0)), blk],
                 out_specs=[blk, blk], out_shape=[jax.ShapeDtypeStruct((nsh, m, c), BF16)] * 2,
                 operands=[xn, wu, g], comm=comm)


def _ffn_contract(pairs, out_dtype, name, *, group, alpha=1.0, res=None, comm=None):
    nsh, m, c = pairs[0][0].shape
    d = pairs[0][1].shape[2]
    tm, tn = _tile(m, 1056, 16), _tile(d, 1024 if group * len(pairs) <= 4 and res is None else 512, LANES)
    a_spec = pl.BlockSpec((group, tm, c), lambda i, j, kk: (kk, i, 0))
    b_spec = pl.BlockSpec((group, c, tn), lambda i, j, kk: (kk, 0, j))
    o_spec = pl.BlockSpec((tm, tn), lambda i, j, kk: (i, j))
    return _mm_core([((a, a_spec), (b, b_spec)) for a, b in pairs], NN_DIMS, out_dtype, name=name,
                    grid=(m // tm, d // tn, nsh // group), out_shape=(m, d), out_spec=o_spec, acc_shape=(tm, tn),
                    alpha=alpha, res=None if res is None else (res, o_spec), comm=comm)


def _ffn_bwd_act(dh, wd, g, u, name, comm=None):
    m, d = dh.shape
    nsh, c, _ = wd.shape
    tm = _tile(m, 1056, 16)

    def body(dh_ref, wd_ref, g_ref, u_ref, dg_ref, du_ref):
        dact = 0.5 * _dot(dh_ref[...], wd_ref[...], NT_DIMS)
        gv = g_ref[...].astype(F32)
        uv = u_ref[...].astype(F32)
        sig = _sigmoid(gv)
        du_ref[...] = (dact * gv * sig).astype(BF16)
        dg_ref[...] = (dact * uv * sig * (1.0 + gv * (1.0 - sig))).astype(BF16)

    blk = pl.BlockSpec((None, tm, c), lambda i, j: (j, i, 0))
    return _call(body, name=name, grid=(m // tm, nsh),
                 in_specs=[pl.BlockSpec((tm, d), lambda i, j: (i, 0)), pl.BlockSpec((None, c, d), lambda i, j: (j, 0, 0)), blk, blk],
                 out_specs=[blk, blk], out_shape=[jax.ShapeDtypeStruct((nsh, m, c), BF16)] * 2, operands=[dh, wd, g, u],
                 comm=comm)


def _ffn_dw(z, x, name, alpha=1.0, comm=None):
    nsh, m, c = z.shape
    d = x.shape[1]
    tn, tk = _tile(d, 2048, LANES), _tile(m, 2112, 16)
    return _mm_core([((z, pl.BlockSpec((None, tk, c), lambda i, j, kk: (i, kk, 0))),
                      (x, pl.BlockSpec((tk, tn), lambda i, j, kk: (kk, j))))],
                    TN_DIMS, BF16, name=name, grid=(nsh, d // tn, m // tk), out_shape=(nsh, c, d),
                    out_spec=pl.BlockSpec((None, c, tn), lambda i, j, kk: (i, 0, j)), acc_shape=(c, tn), alpha=alpha,
                    comm=comm)


def _dot3(tri, x):
    h1 = x.astype(BF16)
    r1 = x - h1.astype(F32)
    h2 = r1.astype(BF16)
    h3 = (r1 - h2.astype(F32)).astype(BF16)
    return _dot(tri, h1) + _dot(tri, h2) + _dot(tri, h3)


def _log_sigmoid(z):
    return jnp.minimum(z, 0.0) - jnp.log(1.0 + jnp.exp(-jnp.abs(z)))


def _triangle(t, cmp):
    return cmp(lax.broadcasted_iota(jnp.int32, (t, t), 0), lax.broadcasted_iota(jnp.int32, (t, t), 1)).astype(BF16)


def _forget_cumsum(fl, bias, n_heads, pad, name):
    m = fl.shape[0]
    nb = m // ROW_BLOCK

    def body(fl_ref, b_ref, c_ref):
        tri = _triangle(ROW_BLOCK, lambda r, c: r >= c)
        lane_ok = lax.broadcasted_iota(jnp.int32, (ROW_BLOCK, LANES), 1) < n_heads
        rows = lax.broadcasted_iota(jnp.int32, (ROW_BLOCK, LANES), 0)

        def step(b, carry):
            off = pl.multiple_of(b * ROW_BLOCK, ROW_BLOCK)
            lf = _log_sigmoid(fl_ref[pl.ds(off, ROW_BLOCK), :] + b_ref[...])
            lf = jnp.where(lane_ok & (rows + off >= pad), lf, 0.0)
            cs = _dot3(tri, lf) + carry
            c_ref[pl.ds(off, ROW_BLOCK), :] = cs
            return cs[ROW_BLOCK - 1:ROW_BLOCK, :]

        lax.fori_loop(0, nb, step, jnp.zeros((1, LANES), F32))

    vmem = pl.BlockSpec(memory_space=pltpu.VMEM)
    return pl.pallas_call(
        body, name=name, out_shape=jax.ShapeDtypeStruct((m, LANES), F32), in_specs=[vmem, vmem], out_specs=vmem,
        compiler_params=pltpu.CompilerParams(vmem_limit_bytes=VMEM_LIMIT_BYTES),
    )(fl, bias)


def _forget_cumsum_bwd(dc_a, dc_b, fl, bias, n_heads, pad, name):
    m = fl.shape[0]
    nb = m // ROW_BLOCK

    def body(da_ref, db_ref, fl_ref, b_ref, dfl_ref, dbias_ref):
        tri = _triangle(ROW_BLOCK, lambda r, c: r <= c)
        lane_ok = lax.broadcasted_iota(jnp.int32, (ROW_BLOCK, LANES), 1) < n_heads
        rows = lax.broadcasted_iota(jnp.int32, (ROW_BLOCK, LANES), 0)

        def step(bb, carry):
            tail, dbias = carry
            off = pl.multiple_of((nb - 1 - bb) * ROW_BLOCK, ROW_BLOCK)
            dc = da_ref[pl.ds(off, ROW_BLOCK), :] + db_ref[pl.ds(off, ROW_BLOCK), :]
            dlf = _dot3(tri, dc) + tail
            z = fl_ref[pl.ds(off, ROW_BLOCK), :] + b_ref[...]
            dfl = jnp.where(lane_ok & (rows + off >= pad), dlf * _sigmoid(-z), 0.0)
            dfl_ref[pl.ds(off, ROW_BLOCK), :] = dfl
            return dlf[0:1, :], dbias + jnp.sum(dfl, axis=0, keepdims=True)

        zero = jnp.zeros((1, LANES), F32)
        _, dbias = lax.fori_loop(0, nb, step, (zero, zero))
        dbias_ref[...] = dbias

    vmem = pl.BlockSpec(memory_space=pltpu.VMEM)
    return pl.pallas_call(
        body, name=name,
        out_shape=[jax.ShapeDtypeStruct((m, LANES), F32), jax.ShapeDtypeStruct((1, LANES), F32)],
        in_specs=[vmem] * 4, out_specs=[vmem, vmem],
        compiler_params=pltpu.CompilerParams(vmem_limit_bytes=VMEM_LIMIT_BYTES),
    )(dc_a, dc_b, fl, bias)


def _attn_block(m):
    return 3 * ROW_BLOCK if m % (3 * ROW_BLOCK) == 0 else ROW_BLOCK


def _block(ref, j, t):
    return ref[pl.ds(pl.multiple_of(j * t, t), t), :]


def _head_norm(o, gain):
    r = lax.rsqrt(jnp.mean(o * o, axis=-1, keepdims=True) + EPS)
    return o * r * gain


def _head_norm_bwd(o, d_on, gain):
    r = lax.rsqrt(jnp.mean(o * o, axis=-1, keepdims=True) + EPS)
    ohat = o * r
    t = d_on * gain
    d_o = r * (t - ohat * jnp.mean(t * ohat, axis=-1, keepdims=True))
    return d_o, jnp.sum(d_on * ohat, axis=0, keepdims=True)


def _qkv_specs(t, m, h, first_col_block):
    q = pl.BlockSpec((t, HEAD_DIM), lambda hd, i: (i, first_col_block + hd))
    k = pl.BlockSpec((m, HEAD_DIM), lambda hd, i: (0, first_col_block + h + hd))
    v = pl.BlockSpec((m, HEAD_DIM), lambda hd, i: (0, first_col_block + 2 * h + hd))
    return q, k, v


def _fox_fwd(qkv, ccol, crow, gain, n_heads, pad, name, comm=None):
    m = qkv.shape[0]
    t = _attn_block(m)
    nq = m // t
    scale = HEAD_DIM ** -0.5
    hw = n_heads * HEAD_DIM

    def body(q_ref, k_ref, v_ref, ccol_ref, crow_ref, g_ref, o_ref, on_ref, lse_ref):
        i = pl.program_id(1)
        q = q_ref[...]
        ci = ccol_ref[...]
        qpos = i * t + lax.broadcasted_iota(jnp.int32, (t, 1), 0)

        def step(j, carry, masked):
            mx, l, acc = carry
            s = _dot(q, _block(k_ref, j, t), NT_DIMS) * scale + ci - crow_ref[j]
            if masked:
                kpos = j * t + lax.broadcasted_iota(jnp.int32, (1, t), 1)
                s = jnp.where((kpos <= qpos) & (kpos >= pad), s, NEG)
            mx_new = jnp.maximum(mx, jnp.max(s, axis=-1, keepdims=True))
            p = jnp.exp(s - mx_new)
            a = jnp.exp(mx - mx_new)
            return (mx_new, a * l + jnp.sum(p, axis=-1, keepdims=True),
                    a * acc + _dot(p.astype(BF16), _block(v_ref, j, t)))

        carry = step(0, (jnp.full((t, 1), NEG, F32), jnp.zeros((t, 1), F32), jnp.zeros((t, HEAD_DIM), F32)), True)
        n_mid = jnp.maximum(i - 1, 0)
        carry = lax.fori_loop(0, n_mid // 2, lambda jp, c: step(2 * jp + 2, step(2 * jp + 1, c, False), False), carry)
        carry = lax.fori_loop(0, n_mid % 2, lambda _, c: step(i - 1, c, False), carry)
        mx, l, acc = lax.fori_loop(0, jnp.minimum(i, 1), lambda _, c: step(i, c, True), carry)
        valid = qpos >= pad
        o = jnp.where(valid, acc / l, 0.0)
        o_ref[...] = o
        on_ref[...] = _head_norm(o, g_ref[...]).astype(BF16)
        lse_ref[...] = jnp.where(valid, mx + jnp.log(l), 0.0)

    q_spec, k_spec, v_spec = _qkv_specs(t, m, n_heads, 0)
    col = pl.BlockSpec((None, t, 1), lambda hd, i: (hd, i, 0))
    head = pl.BlockSpec((t, HEAD_DIM), lambda hd, i: (i, hd))
    return _call(body, name=name, grid=(n_heads, nq),
                 in_specs=[q_spec, k_spec, v_spec, col, pl.BlockSpec((None, nq, 1, t), lambda hd, i: (hd, 0, 0, 0)),
                           pl.BlockSpec((1, HEAD_DIM), lambda hd, i: (0, hd))],
                 out_specs=[head, head, col],
                 out_shape=[jax.ShapeDtypeStruct((m, hw), F32), jax.ShapeDtypeStruct((m, hw), BF16),
                            jax.ShapeDtypeStruct((n_heads, m, 1), F32)],
                 operands=[qkv, qkv, qkv, ccol, crow, gain], comm=comm)


def _fox_bwd(qkv, o, d_on, gain, lse, ccol, crow, n_heads, pad, name, comm=None):
    m = qkv.shape[0]
    t = _attn_block(m)
    nq = m // t
    scale = HEAD_DIM ** -0.5
    hw = n_heads * HEAD_DIM

    def body(q_ref, k_ref, v_ref, o_ref, don_ref, g_ref, lse_ref, ccol_ref, crow_ref,
             dq_ref, dk_ref, dv_ref, dg_ref, dccol_ref, dcrow_ref):
        i = pl.program_id(1)

        @pl.when(i == 0)
        def _():
            dk_ref[...] = jnp.zeros_like(dk_ref)
            dv_ref[...] = jnp.zeros_like(dv_ref)
            dg_ref[...] = jnp.zeros_like(dg_ref)
            dcrow_ref[...] = jnp.zeros_like(dcrow_ref)

        q = q_ref[...]
        o = o_ref[...]
        d_o, dgain = _head_norm_bwd(o, don_ref[...], g_ref[...])
        dg_ref[...] += dgain
        delta = jnp.sum(d_o * o, axis=-1, keepdims=True)
        d_ob = d_o.astype(BF16)
        ci = ccol_ref[...]
        lse_i = lse_ref[...]
        qpos = i * t + lax.broadcasted_iota(jnp.int32, (t, 1), 0)

        def step(j, carry, masked):
            dq, dci = carry
            k = _block(k_ref, j, t)
            v = _block(v_ref, j, t)
            off = pl.multiple_of(j * t, t)
            s = _dot(q, k, NT_DIMS) * scale + ci - crow_ref[j]
            if masked:
                kpos = off + lax.broadcasted_iota(jnp.int32, (1, t), 1)
                ok = (kpos <= qpos) & (kpos >= pad)
                p = jnp.where(ok, jnp.exp(jnp.where(ok, s - lse_i, 0.0)), 0.0)
            else:
                p = jnp.exp(s - lse_i)
            ds = p * (_dot(d_ob, v, NT_DIMS) - delta)
            dsb = ds.astype(BF16)
            dk_ref[pl.ds(off, t), :] += _dot(dsb, q, TN_DIMS) * scale
            dv_ref[pl.ds(off, t), :] += _dot(p.astype(BF16), d_ob, TN_DIMS)
            dcrow_ref[j] -= jnp.sum(ds, axis=0, keepdims=True)
            return dq + _dot(dsb, k), dci + jnp.sum(ds, axis=-1, keepdims=True)

        carry = step(0, (jnp.zeros((t, HEAD_DIM), F32), jnp.zeros((t, 1), F32)), True)
        n_mid = jnp.maximum(i - 1, 0)
        carry = lax.fori_loop(0, n_mid // 2, lambda jp, c: step(2 * jp + 2, step(2 * jp + 1, c, False), False), carry)
        carry = lax.fori_loop(0, n_mid % 2, lambda _, c: step(i - 1, c, False), carry)
        dq, dci = lax.fori_loop(0, jnp.minimum(i, 1), lambda _, c: step(i, c, True), carry)
        dq_ref[...] = (dq * scale).astype(BF16)
        dccol_ref[...] = dci

    q_spec, k_spec, v_spec = _qkv_specs(t, m, n_heads, 0)
    col = pl.BlockSpec((None, t, 1), lambda hd, i: (hd, i, 0))
    rowc = pl.BlockSpec((None, nq, 1, t), lambda hd, i: (hd, 0, 0, 0))
    head = pl.BlockSpec((t, HEAD_DIM), lambda hd, i: (i, hd))
    whole = pl.BlockSpec((m, HEAD_DIM), lambda hd, i: (0, hd))
    gvec = pl.BlockSpec((1, HEAD_DIM), lambda hd, i: (0, hd))
    return _call(body, name=name, grid=(n_heads, nq),
                 in_specs=[q_spec, k_spec, v_spec, head, head, gvec, col, col, rowc],
                 out_specs=[head, whole, whole, gvec, col, rowc],
                 out_shape=[jax.ShapeDtypeStruct((m, hw), BF16), jax.ShapeDtypeStruct((m, hw), F32),
                            jax.ShapeDtypeStruct((m, hw), F32), jax.ShapeDtypeStruct((1, hw), F32),
                            jax.ShapeDtypeStruct((n_heads, m, 1), F32), jax.ShapeDtypeStruct((n_heads, nq, 1, t), F32)],
                 operands=[qkv, qkv, qkv, o, d_on, gain, lse, ccol, crow], comm=comm)


def _sb_scores(z):
    lp = jnp.log(1.0 + jnp.exp(-jnp.abs(z)))
    return jnp.minimum(z, 0.0) - lp, jnp.minimum(-z, 0.0) - lp


def _sb_fwd(qkv, gain, n_heads, pad, name, comm=None):
    m = qkv.shape[0]
    t = _attn_block(m)
    nq = m // t
    assert nq <= LANES
    scale = HEAD_DIM ** -0.5
    hw = n_heads * HEAD_DIM

    def body(q_ref, k_ref, v_ref, g_ref, after_ref, o_ref, on_ref, run_ref):
        i = pl.program_id(1)
        q = q_ref[...]
        qpos = i * t + lax.broadcasted_iota(jnp.int32, (t, 1), 0)
        after = after_ref[...]
        lane = lax.broadcasted_iota(jnp.int32, (t, LANES), 1)

        def step(j, carry, masked):
            run, acc = carry
            ls_pos, log_1m = _sb_scores(_dot(q, _block(k_ref, j, t), NT_DIMS) * scale)
            if masked:
                kpos = j * t + lax.broadcasted_iota(jnp.int32, (1, t), 1)
                ok = (kpos < qpos) & (kpos >= pad)
                log_1m = jnp.where(ok, log_1m, 0.0)
            a = jnp.exp(ls_pos + _dot_split(log_1m, after) + run)
            if masked:
                a = jnp.where(ok, a, 0.0)
            run_ref[...] = jnp.where(lane == j, run, run_ref[...])
            return run + jnp.sum(log_1m, axis=-1, keepdims=True), acc + _dot(a.astype(BF16), _block(v_ref, j, t))

        run_ref[...] = jnp.zeros_like(run_ref)
        carry = step(i, (jnp.zeros((t, 1), F32), jnp.zeros((t, HEAD_DIM), F32)), True)
        n_mid = jnp.maximum(i - 1, 0)
        carry = lax.fori_loop(0, n_mid // 2, lambda jp, c: step(i - 2 * jp - 2, step(i - 2 * jp - 1, c, False), False), carry)
        carry = lax.fori_loop(0, n_mid % 2, lambda _, c: step(1, c, False), carry)
        _, o = lax.fori_loop(0, jnp.minimum(i, 1), lambda _, c: step(0, c, True), carry)
        o_ref[...] = o
        on_ref[...] = _head_norm(o, g_ref[...]).astype(BF16)

    q_spec, k_spec, v_spec = _qkv_specs(t, m, n_heads, 3 * n_heads)
    head = pl.BlockSpec((t, HEAD_DIM), lambda hd, i: (i, hd))
    return _call(body, name=name, grid=(n_heads, nq),
                 in_specs=[q_spec, k_spec, v_spec, pl.BlockSpec((1, HEAD_DIM), lambda hd, i: (0, hd)),
                           pl.BlockSpec((t, t), lambda hd, i: (0, 0))],
                 out_specs=[head, head, pl.BlockSpec((None, t, LANES), lambda hd, i: (hd, i, 0))],
                 out_shape=[jax.ShapeDtypeStruct((m, hw), F32), jax.ShapeDtypeStruct((m, hw), BF16),
                            jax.ShapeDtypeStruct((n_heads, m, LANES), F32)],
                 operands=[qkv, qkv, qkv, gain, _triangle(t, lambda r, c: r > c)], comm=comm)


def _sb_bwd(qkv, o, d_on, gain, runs, n_heads, pad, name, comm=None):
    m = qkv.shape[0]
    t = _attn_block(m)
    nq = m // t
    scale = HEAD_DIM ** -0.5
    hw = n_heads * HEAD_DIM

    def body(q_ref, k_ref, v_ref, o_ref, don_ref, g_ref, run_ref, after_ref, before_ref, dq_ref, dk_ref, dv_ref, dg_ref):
        i = pl.program_id(1)

        @pl.when(i == 0)
        def _():
            dk_ref[...] = jnp.zeros_like(dk_ref)
            dv_ref[...] = jnp.zeros_like(dv_ref)
            dg_ref[...] = jnp.zeros_like(dg_ref)

        q = q_ref[...]
        d_o, dgain = _head_norm_bwd(o_ref[...], don_ref[...], g_ref[...])
        dg_ref[...] += dgain
        d_ob = d_o.astype(BF16)
        runs_i = run_ref[...]
        qpos = i * t + lax.broadcasted_iota(jnp.int32, (t, 1), 0)
        after = after_ref[...]
        before = before_ref[...]
        lane = lax.broadcasted_iota(jnp.int32, (t, LANES), 1)

        def step(j, carry, masked):
            g_run, dq = carry
            k = _block(k_ref, j, t)
            v = _block(v_ref, j, t)
            off = pl.multiple_of(j * t, t)
            ls_pos, ls_neg = _sb_scores(_dot(q, k, NT_DIMS) * scale)
            log_1m = ls_neg
            if masked:
                kpos = off + lax.broadcasted_iota(jnp.int32, (1, t), 1)
                ok = (kpos < qpos) & (kpos >= pad)
                log_1m = jnp.where(ok, ls_neg, 0.0)
            run = jnp.sum(jnp.where(lane == j, runs_i, 0.0), axis=-1, keepdims=True)
            a = jnp.exp(ls_pos + _dot_split(log_1m, after) + run)
            if masked:
                a = jnp.where(ok, a, 0.0)
            g = a * _dot(d_ob, v, NT_DIMS)
            prefix = _dot(g.astype(BF16), before) + g_run
            dz = g * jnp.exp(ls_neg) - jnp.exp(ls_pos) * prefix
            if masked:
                dz = jnp.where(ok, dz, 0.0)
            dzb = dz.astype(BF16)
            dk_ref[pl.ds(off, t), :] += _dot(dzb, q, TN_DIMS) * scale
            dv_ref[pl.ds(off, t), :] += _dot(a.astype(BF16), d_ob, TN_DIMS)
            return g_run + jnp.sum(g, axis=-1, keepdims=True), dq + _dot(dzb, k)

        carry = step(0, (jnp.zeros((t, 1), F32), jnp.zeros((t, HEAD_DIM), F32)), True)
        n_mid = jnp.maximum(i - 1, 0)
        carry = lax.fori_loop(0, n_mid // 2, lambda jp, c: step(2 * jp + 2, step(2 * jp + 1, c, False), False), carry)
        carry = lax.fori_loop(0, n_mid % 2, lambda _, c: step(i - 1, c, False), carry)
        _, dq = lax.fori_loop(0, jnp.minimum(i, 1), lambda _, c: step(i, c, True), carry)
        dq_ref[...] = (dq * scale).astype(BF16)

    q_spec, k_spec, v_spec = _qkv_specs(t, m, n_heads, 3 * n_heads)
    head = pl.BlockSpec((t, HEAD_DIM), lambda hd, i: (i, hd))
    whole = pl.BlockSpec((m, HEAD_DIM), lambda hd, i: (0, hd))
    gvec = pl.BlockSpec((1, HEAD_DIM), lambda hd, i: (0, hd))
    tri = pl.BlockSpec((t, t), lambda hd, i: (0, 0))
    return _call(body, name=name, grid=(n_heads, nq),
                 in_specs=[q_spec, k_spec, v_spec, head, head, gvec, pl.BlockSpec((None, t, LANES), lambda hd, i: (hd, i, 0)),
                           tri, tri],
                 out_specs=[head, whole, whole, gvec],
                 out_shape=[jax.ShapeDtypeStruct((m, hw), BF16), jax.ShapeDtypeStruct((m, hw), F32),
                            jax.ShapeDtypeStruct((m, hw), F32), jax.ShapeDtypeStruct((1, hw), F32)],
                 operands=[qkv, qkv, qkv, o, d_on, gain, runs, _triangle(t, lambda r, c: r > c), _triangle(t, lambda r, c: r < c)],
                 comm=comm)


def _adamw(parts, w, m1, v2, name, comm=None):
    nl, r, c = w.shape
    assert len(parts) == nl
    n_parts = parts[0].shape[0]
    block_elems = 256 * 1024
    if r % 8 == 0 or c % LANES != 0:
        tr, tc = _tile(r, max(8, block_elems // (-(-c // LANES) * LANES)), 8), c
    else:
        tr, tc = r, _tile(c, max(LANES, block_elems // r // LANES * LANES), LANES)
    nr, nc = r // tr, c // tc
    bias1 = 1.0 / (1.0 - ADAM_B1 ** ADAM_STEP)
    bias2 = 1.0 / (1.0 - ADAM_B2 ** ADAM_STEP)

    def body(*refs):
        p_refs = refs[:nl]
        w_ref, m_ref, v_ref, g_ref, d_ref, nm_ref, nv_ref = refs[nl:]

        def update(p_ref):
            g = p_ref[0].astype(F32)
            for s in range(1, n_parts):
                g = g + p_ref[s].astype(F32)
            m_new = ADAM_B1 * m_ref[...] + (1.0 - ADAM_B1) * g
            v_new = ADAM_B2 * v_ref[...] + (1.0 - ADAM_B2) * (g * g)
            g_ref[...] = g
            nm_ref[...] = m_new
            nv_ref[...] = v_new
            d_ref[...] = -ADAM_LR * ((m_new * bias1) / (jnp.sqrt(v_new * bias2) + ADAM_EPS) + ADAM_WD * w_ref[...])

        for ll in range(nl):
            @pl.when(pl.program_id(0) == ll)
            def _(ll=ll):
                update(p_refs[ll])

    def part_spec(ll):
        def index(l, i, j):
            pin = jnp.where(l < ll, 0, 1)
            return 0, jnp.where(l == ll, i, pin * (nr - 1)), jnp.where(l == ll, j, pin * (nc - 1))
        return pl.BlockSpec((n_parts, tr, tc), index)

    blk = pl.BlockSpec((None, tr, tc), lambda l, i, j: (l, i, j))
    out, got = _call(body, name=name, grid=(nl, nr, nc), in_specs=[part_spec(ll) for ll in range(nl)] + [blk, blk, blk],
                     out_specs=[blk] * 4, out_shape=[jax.ShapeDtypeStruct((nl, r, c), F32)] * 4,
                     operands=list(parts) + [w, m1, v2], comm=comm)
    return out if comm is None else (out, got)


SMALL_WEIGHTS = ("ffn1_norm", "mix_norm", "b_forget", "g_fox", "g_sb", "ffn2_norm", "final_norm")
WEIGHT_ORDER = ("meta_tokens", "ffn1_norm", "ffn1_w_gate", "ffn1_w_up", "ffn1_w_down", "mix_norm", "w_in", "b_forget",
                "g_fox", "g_sb", "w_out", "ffn2_norm", "ffn2_w_gate", "ffn2_w_up", "ffn2_w_down", "final_norm")
GROUPS = {"ffn1": ("ffn1_w_gate", "ffn1_w_up", "ffn1_w_down"), "mix": ("w_in", "w_out"),
          "ffn2": ("ffn2_w_gate", "ffn2_w_up", "ffn2_w_down")}
TRANSPOSED = ("ffn1_w_gate", "ffn1_w_up", "ffn2_w_gate", "ffn2_w_up")


def _pad_lanes(a):
    extra = (-a.shape[-1]) % LANES
    return a if extra == 0 else jnp.pad(a, [(0, 0)] * (a.ndim - 1) + [(0, extra)])


def _ffn_backward(dh_b, dh, saved, gain, wg, wu, wd, tag, carried=()):
    h, xn, g, u, act = saved

    def half(k, which, into=()):
        if len(carried) <= k:
            return None
        rows = carried[k].shape[1]
        cut = rows // 2 // 16 * 16
        return _Exchange([carried[k]], rows=(0, cut) if which == 0 else (cut, rows - cut), into=into)

    (dg, du), got_b = _ffn_bwd_act(dh_b, wd, g, u, f"{tag}_bwd_act", half(1, 0))
    d_wd = _ffn_dw(act, dh_b, f"{tag}_dwd", alpha=0.5, comm=half(1, 1, got_b))
    d_wg = _ffn_dw(dg, xn, f"{tag}_dwg", comm=half(2, 0))
    got_c = d_wg[1] if len(carried) > 2 else ()
    d_wu = _ffn_dw(du, xn, f"{tag}_dwu", comm=half(2, 1, got_c))
    dxn = _ffn_contract([(dg, wg), (du, wu)], F32, f"{tag}_dxn", group=2, comm=_Exchange(carried[:1]) if carried else None)
    got = []
    if carried:
        dxn, got = dxn[0], list(dxn[1])
    if len(carried) > 1:
        d_wd, got_b = d_wd
        got += got_b
    if len(carried) > 2:
        d_wg, (d_wu, got_c) = d_wg[0], d_wu
        got += got_c
    dh_in, dh_in_b, d_gain = _rms_bwd(dxn, h, gain, dh, f"{tag}_norm_bwd")
    return dh_in, dh_in_b, d_gain, d_wg, d_wu, d_wd, got


def _ffn_backward_last(dh_b, dh, saved, gain, wg, wu, wd, tag, first, second):
    h, xn, g, u, act = saved
    (dg, du), got_first = _ffn_bwd_act(dh_b, wd, g, u, f"{tag}_bwd_act", first)
    d_wd, got_second = _ffn_dw(act, dh_b, f"{tag}_dwd", alpha=0.5, comm=second)
    c = d_wd.shape[1]
    half = c // 2 // 16 * 16
    d_wg, got_wd = _ffn_dw(dg, xn, f"{tag}_dwg", comm=_Exchange([d_wd], rows=(0, half)))
    d_wu, got_wd = _ffn_dw(du, xn, f"{tag}_dwu", comm=_Exchange([d_wd], rows=(half, c - half), into=got_wd))
    dxn, got_wg = _ffn_contract([(dg, wg), (du, wu)], F32, f"{tag}_dxn", group=2, comm=_Exchange([d_wg]))
    dh_in, dh_in_b, d_gain = _rms_bwd(dxn, h, gain, dh, f"{tag}_norm_bwd")
    return dh_in, dh_in_b, d_gain, got_first, got_second, got_wg[0], got_wd[0], d_wu


def _mixer_forward(h, gain, w_in_t, bias, g_fox, g_sb, w_out, n_heads, pad, tag, comm_fox=None, comm_sb=None):
    m = h.shape[0]
    t = _attn_block(m)
    hw = n_heads * HEAD_DIM
    xn = _rms_fwd(h, gain, f"{tag}_norm")
    qkv = _mm([(xn, w_in_t[:6 * hw])], "nt", BF16, name=f"{tag}_qkv")
    fl = _mm([(xn, w_in_t[6 * hw:])], "nt", F32, name=f"{tag}_forget")
    c = _forget_cumsum(fl, bias, n_heads, pad, f"{tag}_cumsum")
    c_heads = c[:, :n_heads].T
    ccol = c_heads[:, :, None]
    crow = c_heads.reshape(n_heads, m // t, 1, t)
    (o_f, on_f, lse), got_fox = _fox_fwd(qkv, ccol, crow, g_fox, n_heads, pad, f"{tag}_fox", comm_fox)
    (o_s, on_s, runs), got_sb = _sb_fwd(qkv, g_sb, n_heads, pad, f"{tag}_sb", comm_sb)
    if w_out is None:
        w_out = got_sb[-1].reshape(-1, h.shape[1])
    h_out = _mm([(on_f, w_out[:hw]), (on_s, w_out[hw:])], "nn", F32, name=f"{tag}_out", res=h)
    return h_out, (h, xn, qkv, fl, ccol, crow, o_f, on_f, lse, o_s, on_s, runs), got_fox, got_sb, w_out


def _mixer_backward(dh_b, dh_out, saved, gain, w_in_t, bias, g_fox, g_sb, w_out, n_heads, pad, tag, comm_fox=None, comm_sb=None,
                    rows_in_dxn=0):
    h, xn, qkv, fl, ccol, crow, o_f, on_f, lse, o_s, on_s, runs = saved
    m = h.shape[0]
    hw = n_heads * HEAD_DIM
    d_on_f = _mm([(dh_b, w_out[:hw])], "nt", F32, name=f"{tag}_don_f")
    d_on_s = _mm([(dh_b, w_out[hw:])], "nt", F32, name=f"{tag}_don_s")
    d_wout = jnp.concatenate([_mm([(on_f, dh_b)], "tn", BF16, name=f"{tag}_dwout_f"),
                              _mm([(on_s, dh_b)], "tn", BF16, name=f"{tag}_dwout_s")], axis=0)
    (dq_f, dk_f, dv_f, dg_fox, dccol, dcrow), got_fox = _fox_bwd(
        qkv, o_f, d_on_f, g_fox, lse, ccol, crow, n_heads, pad, f"{tag}_fox_bwd", comm_fox)
    (dq_s, dk_s, dv_s, dg_sb), got_sb = _sb_bwd(qkv, o_s, d_on_s, g_sb, runs, n_heads, pad, f"{tag}_sb_bwd", comm_sb)
    dc_a = _pad_lanes(dccol[:, :, 0].T)
    dc_b = _pad_lanes(dcrow.reshape(n_heads, m).T)
    dfl, dbias = _forget_cumsum_bwd(dc_a, dc_b, fl, bias, n_heads, pad, f"{tag}_cumsum_bwd")
    dproj = jnp.concatenate([dq_f, dk_f.astype(BF16), dv_f.astype(BF16), dq_s, dk_s.astype(BF16), dv_s.astype(BF16),
                             dfl.astype(BF16)], axis=1)
    d_win = _mm([(dproj, xn)], "tn", BF16, name=f"{tag}_dwin")[:6 * hw + n_heads].reshape(N_DEV, -1, h.shape[1])
    got_win = None
    if rows_in_dxn:
        dxn, got_win = _mm([(dproj, w_in_t)], "nn", F32, name=f"{tag}_dxn", whole_k=True,
                           comm=_Exchange([d_win], rows=(0, rows_in_dxn)))
    else:
        dxn = _mm([(dproj, w_in_t)], "nn", F32, name=f"{tag}_dxn", whole_k=True)
    dh, dh_in_b, d_gain = _rms_bwd(dxn, h, gain, dh_out, f"{tag}_norm_bwd")
    return dh, dh_in_b, d_gain, d_win, dbias, dg_fox, dg_sb, d_wout, got_fox, got_sb, got_win


def kernel(x, meta_tokens, ffn1_norm, ffn1_w_gate, ffn1_w_up, ffn1_w_down, mix_norm, w_in, b_forget, g_fox, g_sb, w_out, ffn2_norm, ffn2_w_gate, ffn2_w_up, ffn2_w_down, final_norm, loss_target, m_meta_tokens, m_ffn1_norm, m_ffn1_w_gate, m_ffn1_w_up, m_ffn1_w_down, m_mix_norm, m_w_in, m_b_forget, m_g_fox, m_g_sb, m_w_out, m_ffn2_norm, m_ffn2_w_gate, m_ffn2_w_up, m_ffn2_w_down, m_final_norm, v_meta_tokens, v_ffn1_norm, v_ffn1_w_gate, v_ffn1_w_up, v_ffn1_w_down, v_mix_norm, v_w_in, v_b_forget, v_g_fox, v_g_sb, v_w_out, v_ffn2_norm, v_ffn2_w_gate, v_ffn2_w_up, v_ffn2_w_down, v_final_norm):
    given = dict(locals())
    seq, d = x.shape[1], x.shape[2]
    depth = ffn1_norm.shape[0]
    d_in = N_DEV * w_in.shape[2]
    n_heads = g_fox.shape[1] // HEAD_DIM
    hw = n_heads * HEAD_DIM
    assert seq % ROW_BLOCK == 0 and d_in == 6 * hw + n_heads and n_heads <= LANES
    pad = (-(seq + N_META)) % ROW_BLOCK
    x_off = pad + N_META

    def view(n, a):
        if n in TRANSPOSED:
            return jnp.swapaxes(a, 1, 2)
        return a.transpose(2, 0, 1) if n == "w_in" else a

    def unview(n, a):
        if n in TRANSPOSED:
            return jnp.swapaxes(a, 1, 2)
        return a.transpose(1, 2, 0) if n == "w_in" else a

    def shard(n, l):
        v = view(n, given[n])
        return (v[:, l] if n == "w_in" else v[l]).astype(BF16)

    def shards(group, l):
        return [shard(n, l) for n in GROUPS[group]]

    sh = shards("ffn1", 0)
    first = _run_alone(_Gather([sh[0], meta_tokens]), "gather_first")
    full = {}
    meta_full = first[1].transpose(1, 0, 2).reshape(N_META, d)
    h = jnp.concatenate([jnp.zeros((pad, d), F32), meta_full, x[0]], axis=0)
    weights, saved = [], []
    for l in range(depth):
        xn = _rms_fwd(h, ffn1_norm[l:l + 1], "ffn1_norm")
        if l == 0:
            mix0 = shards("mix", 0)
            g, (wu1,) = _ffn_gate(xn, first[0], "ffn1_gate", _Gather([sh[1]]))
            (u, act), (wd1,) = _ffn_up_given_gate(xn, wu1, g, "ffn1_up", _Gather([sh[2]]))
            full[("ffn1", 0)] = (first[0], wu1, wd1)
            h_out, (win3,) = _ffn_contract([(act, wd1)], F32, "ffn1_down", group=4, alpha=0.5, res=h, comm=_Gather(mix0[:1]))
        else:
            wg1, wu1, wd1 = full[("ffn1", l)]
            (g, u, act), full[("mix", l)] = _ffn_up(xn, wg1, wu1, "ffn1_up", _Gather(shards("mix", l)))
            h_out = _ffn_contract([(act, wd1)], F32, "ffn1_down", group=4, alpha=0.5, res=h)
        s1 = (h, xn, g, u, act)
        h = h_out
        if l > 0:
            win3, wout3 = full[("mix", l)]
        w_in_t = jnp.pad(win3.reshape(d_in, d), ((0, 6 * hw + LANES - d_in), (0, 0)))
        bias = _pad_lanes(b_forget[l:l + 1])
        in_sb = (shards("ffn1", l + 1) if l + 1 < depth else []) + (mix0[1:] if l == 0 else [])
        h, sm, full[("ffn2", l)], got, w_out_full = _mixer_forward(
            h, mix_norm[l:l + 1], w_in_t, bias, g_fox[l:l + 1], g_sb[l:l + 1],
            None if l == 0 else wout3.reshape(N_DEV * wout3.shape[1], d), n_heads, pad, "mix",
            _Gather(shards("ffn2", l)), _Gather(in_sb) if in_sb else None)
        if l + 1 < depth:
            full[("ffn1", l + 1)] = got[:3]
        wg2, wu2, wd2 = full[("ffn2", l)]
        xn = _rms_fwd(h, ffn2_norm[l:l + 1], "ffn2_norm")
        (g, u, act), _ = _ffn_up(xn, wg2, wu2, "ffn2_up")
        s2 = (h, xn, g, u, act)
        h = _ffn_contract([(act, wd2)], F32, "ffn2_down", group=4, alpha=0.5, res=h)
        weights.append((w_in_t, w_out_full, bias))
        saved.append((s1, sm, s2))

    dh, dh_b, d_final, loss_arr = _loss_head(h, final_norm[None, :], loss_target[0], x_off, "loss_head")
    small = {n: [None] * depth for n in SMALL_WEIGHTS[:-1]}
    partial, received = {}, {}

    def names(group, l):
        return [(n, l) for n in GROUPS[group]]

    def send(keys):
        return [partial[k] for k in keys]

    for l in reversed(range(depth)):
        w_in_t, w_out_full, bias = weights[l]
        s1, sm, s2 = saved[l]
        wg2, wu2, wd2 = full[("ffn2", l)]
        up = l + 1 < depth
        in_dxn = names("ffn1", l + 1) if up else []
        in_fox = names("mix", l + 1) if up else []
        in_sb = names("ffn2", l)
        dh, dh_b, small["ffn2_norm"][l], partial[("ffn2_w_gate", l)], partial[("ffn2_w_up", l)], partial[("ffn2_w_down", l)], got = (
            _ffn_backward(dh_b, dh, s2, ffn2_norm[l:l + 1], wg2, wu2, wd2, "ffn2", send(in_dxn)))
        received.update(zip(in_dxn, got))
        win_rows = (d_in // N_DEV // 2 // 16 * 16) if l == 0 else 0
        (dh, dh_b, small["mix_norm"][l], partial[("w_in", l)], dbias, small["g_fox"][l], small["g_sb"][l], d_wout,
         got_fox, got_sb, got_win) = _mixer_backward(
            dh_b, dh, sm, mix_norm[l:l + 1], w_in_t, bias, g_fox[l:l + 1], g_sb[l:l + 1], w_out_full, n_heads, pad, "mix",
            _Exchange(send(in_fox)) if in_fox else None, _Exchange(send(in_sb)), win_rows)
        received.update(zip(in_fox, got_fox or []))
        received.update(zip(in_sb, got_sb))
        partial[("w_out", l)] = d_wout.reshape(N_DEV, -1, d)
        small["b_forget"][l] = dbias[:, :n_heads]
        wg1, wu1, wd1 = full[("ffn1", l)]
        if l > 0:
            dh, dh_b, small["ffn1_norm"][l], partial[("ffn1_w_gate", l)], partial[("ffn1_w_up", l)], partial[("ffn1_w_down", l)], _ = (
                _ffn_backward(dh_b, dh, s1, ffn1_norm[l:l + 1], wg1, wu1, wd1, "ffn1"))
        else:
            rest = (win_rows, d_in // N_DEV - win_rows)
            (dh, dh_b, small["ffn1_norm"][0], got_a, got_b, received[("ffn1_w_gate", 0)], received[("ffn1_w_down", 0)],
             last_grad) = _ffn_backward_last(
                dh_b, dh, s1, ffn1_norm[0:1], wg1, wu1, wd1, "ffn1", _Exchange([partial[("w_out", 0)]]),
                _Exchange([partial[("w_in", 0)]], rows=rest, into=got_win))
            received[("w_out", 0)], received[("w_in", 0)] = got_a[0], got_b[0]
    grad_x = dh[x_off:][None]
    d_meta = dh[pad:x_off].reshape(N_META, N_DEV, -1).transpose(1, 0, 2)
    received[("meta_tokens", 0)] = _run_alone(_Exchange([d_meta]), "exchange_meta")[0]

    vec = [loss_arr[0:1, :]] + [_pad_lanes(jnp.concatenate(small[n], axis=0).reshape(1, -1)) for n in SMALL_WEIGHTS[:-1]]
    vec.append(d_final)
    sizes = [a.shape[1] for a in vec]
    summed = _all_reduce_small(jnp.concatenate(vec, axis=1), "reduce_small")
    loss = summed[0, 0]

    def packed(prefix):
        cols = [jnp.zeros((1, LANES), F32)]
        cols += [_pad_lanes(given[prefix + n].reshape(1, -1)) for n in SMALL_WEIGHTS]
        return jnp.concatenate(cols, axis=1)[None]

    small_out = _adamw([summed[None]], packed(""), packed("m_"), packed("v_"), "adamw_small")

    c_last = last_grad.shape[1]
    quarter = c_last // 4 // 16 * 16
    pieces = [(k * quarter, quarter if k < 3 else c_last - 3 * quarter) for k in range(4)]
    order = ["ffn2_w_gate", "ffn2_w_up", "ffn2_w_down", "ffn1_w_gate", "ffn1_w_down", "w_in", "w_out", "ffn1_w_up"]
    out, arrived = {}, []
    for k, n in enumerate(order):
        wv, mv, vv = (view(n, given[p + n]) for p in ("", "m_", "v_"))
        if n == "ffn1_w_up":
            received[(n, 0)] = arrived[0]
        if n == "w_in":
            per_layer = [_adamw([received[(n, l)]], wv[:, l][None], mv[:, l][None], vv[:, l][None], "adamw_" + n)
                         for l in range(depth)]
            res = [jnp.stack([per_layer[l][k][0] for l in range(depth)], axis=1) for k in range(4)]
        elif k < len(pieces):
            res, arrived = _adamw([received[(n, l)] for l in range(depth)], wv, mv, vv, "adamw_" + n,
                                  _Exchange([last_grad], rows=pieces[k], into=arrived))
        else:
            res = _adamw([received[(n, l)] for l in range(depth)], wv, mv, vv, "adamw_" + n)
        out[n] = [unview(n, r) for r in res]
    out["meta_tokens"] = [r[0] for r in _adamw([received[("meta_tokens", 0)]], meta_tokens[None], m_meta_tokens[None],
                                               v_meta_tokens[None], "adamw_meta_tokens")]
    offset = sizes[0]
    for n, size in zip(SMALL_WEIGHTS, sizes[1:]):
        shape, count = given[n].shape, given[n].size
        out[n] = [r[0, 0, offset:offset + count].reshape(shape) for r in small_out]
        offset += size

    result = [loss, grad_x]
    for k in range(4):
        result += [out[n][k] for n in WEIGHT_ORDER]
    return tuple(result)
```

```python
import math

import jax
import jax.numpy as jnp
from jax import lax
from jax.experimental import pallas as pl
from jax.experimental.pallas import tpu as pltpu

F32 = jnp.float32
BF16 = jnp.bfloat16

N_DEV = 8
N_META = 16
HEAD_DIM = 128
ROW_BLOCK = 128
LANES = 128
EPS = 1e-6
NEG = -1e30
ADAM_LR = 0.001
ADAM_B1 = 0.9
ADAM_B2 = 0.999
ADAM_EPS = 1e-08
ADAM_WD = 0.01
ADAM_STEP = 10
VMEM_LIMIT_BYTES = 56 * 1024 * 1024
MESH = pl.DeviceIdType.MESH

NT_DIMS = (((1,), (1,)), ((), ()))
TN_DIMS = (((0,), (0,)), ((), ()))
NN_DIMS = (((1,), (0,)), ((), ()))
ANY = pl.BlockSpec(memory_space=pl.ANY)


def _tile(n, cap, align):
    best = None
    for d in range(align, min(n, cap) + 1, align):
        if n % d == 0:
            best = d
    return best if best is not None else n


def _dot(a, b, dims=NN_DIMS):
    return lax.dot_general(a, b, dims, preferred_element_type=F32)


def _dot_split(x, u):
    hi = x.astype(BF16)
    lo = (x - hi.astype(F32)).astype(BF16)
    return _dot(hi, u) + _dot(lo, u)


def _my_position():
    return lax.axis_index("x"), lax.axis_index("y"), lax.axis_index("c")


class _Gather:
    n_phases = 3

    def __init__(self, arrs):
        self.arrs = list(arrs)
        n = len(self.arrs)
        self.out_shapes = [jax.ShapeDtypeStruct((N_DEV,) + a.shape, a.dtype) for a in self.arrs]
        self.scratch = [pltpu.SemaphoreType.DMA((n, 7)), pltpu.SemaphoreType.DMA((n, 7)), pltpu.SemaphoreType.DMA((n,))]

    def phase(self, p, ins, outs, sems):
        send_sems, recv_sems, local_sems = sems
        n = len(self.arrs)
        x, y, c = _my_position()
        me, sibling = (x, y, c), (x, y, 1 - c)
        chips = [(1 - x, y), (x, 1 - y), (1 - x, 1 - y)]

        def copy(a, k, block, to, src=None):
            slot = outs[a].at[4 * block[0] + 2 * block[1] + block[2]]
            return pltpu.make_async_remote_copy(
                src_ref=slot if src is None else src, dst_ref=slot,
                send_sem=send_sems.at[a, k], recv_sem=recv_sems.at[a, k], device_id=to, device_id_type=MESH)

        def local(a):
            return pltpu.make_async_copy(ins[a], outs[a].at[4 * x + 2 * y + c], local_sems.at[a])

        def first(a):
            return [copy(a, 0, me, sibling, src=ins[a])] + [copy(a, 1 + j, me, (*chip, c), src=ins[a]) for j, chip in enumerate(chips)]

        def passed(a, j):
            return copy(a, 4 + j, (*chips[j], c), sibling)

        if p == 0:
            for a in range(n):
                local(a).start()
            for a in range(n):
                for cp in first(a):
                    cp.start()
        elif p == 1:
            for a in range(n):
                for j, chip in enumerate(chips):
                    copy(a, 1 + j, (*chip, c), me).wait_recv()
                    passed(a, j).start()
        else:
            for a in range(n):
                copy(a, 0, sibling, me).wait_recv()
                for j, chip in enumerate(chips):
                    copy(a, 4 + j, (*chip, 1 - c), me).wait_recv()
            for a in range(n):
                for cp in first(a) + [passed(a, j) for j in range(3)]:
                    cp.wait_send()
                local(a).wait()


class _Exchange:
    n_phases = 2

    def __init__(self, arrs, rows=None, into=()):
        self.arrs = list(arrs)
        self.rows = rows
        self.into = list(into)
        n = len(self.arrs)
        self.out_shapes = [jax.ShapeDtypeStruct(a.shape, a.dtype) for a in self.arrs]
        self.scratch = [pltpu.SemaphoreType.DMA((n, 7)), pltpu.SemaphoreType.DMA((n, 7)), pltpu.SemaphoreType.DMA((n,))]

    def phase(self, p, ins, outs, sems):
        send_sems, recv_sems, local_sems = sems
        n = len(self.arrs)
        x, y, c = _my_position()
        me = 4 * x + 2 * y + c

        def peer_of(r):
            return (x ^ ((r >> 2) & 1), y ^ ((r >> 1) & 1), c ^ (r & 1))

        def part(ref, d):
            return ref.at[d] if self.rows is None else ref.at[d, pl.ds(self.rows[0], self.rows[1])]

        def copy(a, r):
            px, py, pc = peer_of(r)
            return pltpu.make_async_remote_copy(
                src_ref=part(ins[a], 4 * px + 2 * py + pc), dst_ref=part(outs[a], me),
                send_sem=send_sems.at[a, r - 1], recv_sem=recv_sems.at[a, r - 1],
                device_id=(px, py, pc), device_id_type=MESH)

        def arrival(a, r):
            px, py, pc = peer_of(r)
            slot = part(outs[a], 4 * px + 2 * py + pc)
            return pltpu.make_async_remote_copy(
                src_ref=slot, dst_ref=slot, send_sem=send_sems.at[a, r - 1], recv_sem=recv_sems.at[a, r - 1],
                device_id=(px, py, pc), device_id_type=MESH)

        def local(a):
            return pltpu.make_async_copy(part(ins[a], me), part(outs[a], me), local_sems.at[a])

        if p == 0:
            for a in range(n):
                local(a).start()
            for a in range(n):
                for r in range(1, N_DEV):
                    copy(a, r).start()
        else:
            for a in range(n):
                for r in range(1, N_DEV):
                    arrival(a, r).wait_recv()
            for a in range(n):
                for r in range(1, N_DEV):
                    copy(a, r).wait_send()
                local(a).wait()


def _run_alone(comm, name):
    n = len(comm.arrs)

    def body(*refs):
        for p in range(comm.n_phases):
            comm.phase(p, refs[:n], refs[n:2 * n], refs[2 * n:])

    return pl.pallas_call(body, name=name, out_shape=comm.out_shapes, in_specs=[ANY] * n, out_specs=[ANY] * n,
                          scratch_shapes=comm.scratch)(*comm.arrs)


def _call(body, *, name, grid, in_specs, out_specs, out_shape, operands, scratch_shapes=(), comm=None):
    scratch_shapes = list(scratch_shapes)
    params = pltpu.CompilerParams(dimension_semantics=("arbitrary",) * len(grid), vmem_limit_bytes=VMEM_LIMIT_BYTES)
    if comm is None:
        res = pl.pallas_call(body, name=name, grid=grid, in_specs=in_specs, out_specs=out_specs, out_shape=out_shape,
                             scratch_shapes=scratch_shapes, compiler_params=params)(*operands)
        return res, None
    n_in, n_out, n_sc = len(in_specs), len(out_specs), len(scratch_shapes)
    nc = len(comm.arrs)
    into = getattr(comm, "into", [])
    total = math.prod(grid)
    at = {0: 0, comm.n_phases - 1: total - 1}
    for p in range(1, comm.n_phases - 1):
        at[p] = (total * 7) // 8

    def wrapped(*refs):
        ins, cins = refs[:n_in], refs[n_in:n_in + nc]
        refs = refs[n_in + nc + len(into):]
        outs, couts = refs[:n_out], refs[n_out:n_out + nc]
        rest = refs[n_out + nc:]
        scratch, sems = rest[:n_sc], rest[n_sc:]
        step = 0
        for axis, size in enumerate(grid):
            step = step * size + pl.program_id(axis)
        for p in range(comm.n_phases - 1):
            @pl.when(step == at[p])
            def _(p=p):
                comm.phase(p, cins, couts, sems)
        body(*ins, *outs, *scratch)

        @pl.when(step == total - 1)
        def _():
            comm.phase(comm.n_phases - 1, cins, couts, sems)

    res = pl.pallas_call(
        wrapped, name=name, grid=grid, in_specs=list(in_specs) + [ANY] * (nc + len(into)),
        out_specs=list(out_specs) + [ANY] * nc, out_shape=list(out_shape) + comm.out_shapes,
        scratch_shapes=scratch_shapes + comm.scratch,
        input_output_aliases={n_in + nc + k: n_out + k for k in range(len(into))},
        compiler_params=params)(*operands, *comm.arrs, *into)
    return res[:n_out], res[n_out:]


def _all_reduce_small(vec, name):
    n = vec.shape[1]

    def body(v_ref, o_ref, buf, send_sems, recv_sems):
        x, y, c = _my_position()
        me = 4 * x + 2 * y + c

        def peer_of(r):
            return (x ^ ((r >> 2) & 1), y ^ ((r >> 1) & 1), c ^ (r & 1))

        def copy(r):
            px, py, pc = peer_of(r)
            return pltpu.make_async_remote_copy(
                src_ref=v_ref, dst_ref=buf.at[me], send_sem=send_sems.at[r - 1], recv_sem=recv_sems.at[r - 1],
                device_id=(px, py, pc), device_id_type=MESH)

        def arrival(r):
            px, py, pc = peer_of(r)
            slot = buf.at[4 * px + 2 * py + pc]
            return pltpu.make_async_remote_copy(
                src_ref=slot, dst_ref=slot, send_sem=send_sems.at[r - 1], recv_sem=recv_sems.at[r - 1],
                device_id=(px, py, pc), device_id_type=MESH)

        sends = [copy(r) for r in range(1, N_DEV)]
        for cp in sends:
            cp.start()
        buf[me] = v_ref[...]
        for r in range(1, N_DEV):
            arrival(r).wait_recv()
        for cp in sends:
            cp.wait_send()
        total = buf[0]
        for d in range(1, N_DEV):
            total = total + buf[d]
        o_ref[...] = total

    vmem = pl.BlockSpec(memory_space=pltpu.VMEM)
    return pl.pallas_call(
        body, name=name, out_shape=jax.ShapeDtypeStruct((1, n), F32), in_specs=[vmem], out_specs=vmem,
        scratch_shapes=[pltpu.VMEM((N_DEV, 1, n), F32), pltpu.SemaphoreType.DMA((7,)), pltpu.SemaphoreType.DMA((7,))],
    )(vec)


def _mm_core(pairs, dims, out_dtype, *, name, grid, out_shape, out_spec, acc_shape, alpha=1.0, res=None, comm=None):
    nk = grid[2]
    npairs = len(pairs)

    def body(*refs):
        ab = refs[:2 * npairs]
        rest = refs[2 * npairs:]
        res_ref = rest[0] if res is not None else None
        o_ref = rest[1] if res is not None else rest[0]
        acc_ref = rest[-1] if nk > 1 else None
        part = None
        for p in range(npairs):
            a_ref, b_ref = ab[2 * p], ab[2 * p + 1]
            shards = [(a_ref[s], b_ref[s]) for s in range(a_ref.shape[0])] if len(a_ref.shape) == 3 else [(a_ref[...], b_ref[...])]
            for av, bv in shards:
                d = _dot(av.astype(BF16), bv.astype(BF16), dims)
                part = d if part is None else part + d

        def finish(total):
            val = total * alpha if alpha != 1.0 else total
            if res_ref is not None:
                val = res_ref[...] + val
            o_ref[...] = val.astype(out_dtype)

        if nk == 1:
            finish(part)
        else:
            kk = pl.program_id(2)

            @pl.when(kk == 0)
            def _():
                acc_ref[...] = part

            @pl.when(kk > 0)
            def _():
                acc_ref[...] += part

            @pl.when(kk == nk - 1)
            def _():
                finish(acc_ref[...])

    operands, in_specs = [], []
    for (a, a_spec), (b, b_spec) in pairs:
        operands += [a, b]
        in_specs += [a_spec, b_spec]
    if res is not None:
        operands.append(res[0])
        in_specs.append(res[1])
    out, got = _call(body, name=name, grid=grid, in_specs=in_specs, out_specs=[out_spec],
                     out_shape=[jax.ShapeDtypeStruct(out_shape, out_dtype)], operands=operands,
                     scratch_shapes=[pltpu.VMEM(acc_shape, F32)] if nk > 1 else [], comm=comm)
    return out[0] if comm is None else (out[0], got)


def _mm(pairs, mode, out_dtype, *, name, alpha=1.0, res=None, comm=None, whole_k=False):
    a0, b0 = pairs[0]
    if mode == "nn":
        (m, k), n = a0.shape, b0.shape[1]
    elif mode == "nt":
        (m, k), n = a0.shape, b0.shape[0]
    else:
        (k, m), n = a0.shape, b0.shape[1]
    dims = {"nn": NN_DIMS, "nt": NT_DIMS, "tn": TN_DIMS}[mode]
    tm = _tile(m, 528 if whole_k else 1056, LANES if mode == "tn" else 16)
    tn = _tile(n, 512 if whole_k else 1024, LANES)
    tk = k if whole_k else _tile(k, 2048 if mode != "tn" else 2112, LANES if mode != "tn" else 16)
    a_spec = pl.BlockSpec((tk, tm), lambda i, j, kk: (kk, i)) if mode == "tn" else pl.BlockSpec((tm, tk), lambda i, j, kk: (i, kk))
    b_spec = pl.BlockSpec((tn, tk), lambda i, j, kk: (j, kk)) if mode == "nt" else pl.BlockSpec((tk, tn), lambda i, j, kk: (kk, j))
    o_spec = pl.BlockSpec((tm, tn), lambda i, j, kk: (i, j))
    return _mm_core([((a, a_spec), (b, b_spec)) for a, b in pairs], dims, out_dtype, name=name,
                    grid=(m // tm, n // tn, k // tk), out_shape=(m, n), out_spec=o_spec, acc_shape=(tm, tn),
                    alpha=alpha, res=None if res is None else (res, o_spec), comm=comm)


def _rms_fwd(h, gain, name):
    m, d = h.shape
    tm = _tile(m, 528, 16)

    def body(h_ref, g_ref, o_ref):
        hv = h_ref[...]
        r = lax.rsqrt(jnp.mean(hv * hv, axis=-1, keepdims=True) + EPS)
        o_ref[...] = (hv * r * g_ref[...]).astype(BF16)

    row = pl.BlockSpec((tm, d), lambda i: (i, 0))
    out, _ = _call(body, name=name, grid=(m // tm,), in_specs=[row, pl.BlockSpec((1, d), lambda i: (0, 0))],
                   out_specs=[row], out_shape=[jax.ShapeDtypeStruct((m, d), BF16)], operands=[h, gain])
    return out[0]


def _rms_bwd(dxn, h, gain, dres, name):
    m, d = h.shape
    tm = _tile(m, 264, 16)

    def body(dxn_ref, h_ref, g_ref, dres_ref, dh_ref, dhb_ref, dg_ref):
        hv = h_ref[...]
        r = lax.rsqrt(jnp.mean(hv * hv, axis=-1, keepdims=True) + EPS)
        xhat = hv * r
        dxn_v = dxn_ref[...]
        t = dxn_v * g_ref[...]
        dh = dres_ref[...] + r * (t - xhat * jnp.mean(t * xhat, axis=-1, keepdims=True))
        dh_ref[...] = dh
        dhb_ref[...] = dh.astype(BF16)
        part = jnp.sum(dxn_v * xhat, axis=0, keepdims=True)

        @pl.when(pl.program_id(0) == 0)
        def _():
            dg_ref[...] = part

        @pl.when(pl.program_id(0) > 0)
        def _():
            dg_ref[...] += part

    row = pl.BlockSpec((tm, d), lambda i: (i, 0))
    vec = pl.BlockSpec((1, d), lambda i: (0, 0))
    out, _ = _call(body, name=name, grid=(m // tm,), in_specs=[row, row, vec, row], out_specs=[row, row, vec],
                   out_shape=[jax.ShapeDtypeStruct((m, d), F32), jax.ShapeDtypeStruct((m, d), BF16),
                              jax.ShapeDtypeStruct((1, d), F32)],
                   operands=[dxn, h, gain, dres])
    return out


def _loss_head(h, gain, target, x_off, name):
    m, d = h.shape
    tm = ROW_BLOCK
    first = x_off // tm

    def body(h_ref, g_ref, t_ref, dh_ref, dhb_ref, dg_ref, loss_ref):
        i = pl.program_id(0)

        @pl.when(i == 0)
        def _():
            dg_ref[...] = jnp.zeros_like(dg_ref)
            loss_ref[...] = jnp.zeros_like(loss_ref)

        @pl.when(i < first)
        def _():
            dh_ref[...] = jnp.zeros_like(dh_ref)
            dhb_ref[...] = jnp.zeros_like(dhb_ref)

        @pl.when(i >= first)
        def _():
            hv = h_ref[...]
            g = g_ref[...]
            r = lax.rsqrt(jnp.mean(hv * hv, axis=-1, keepdims=True) + EPS)
            xhat = hv * r
            err = xhat * g - t_ref[...]
            loss_ref[...] += 0.5 * jnp.sum(jnp.mean(err * err, axis=-1, keepdims=True))
            dy = err * (1.0 / d)
            t = dy * g
            dh = r * (t - xhat * jnp.mean(t * xhat, axis=-1, keepdims=True))
            dh_ref[...] = dh
            dhb_ref[...] = dh.astype(BF16)
            dg_ref[...] += jnp.sum(dy * xhat, axis=0, keepdims=True)

    row = pl.BlockSpec((tm, d), lambda i: (i, 0))
    vec = pl.BlockSpec((1, d), lambda i: (0, 0))
    out, _ = _call(body, name=name, grid=(m // tm,),
                   in_specs=[row, vec, pl.BlockSpec((tm, d), lambda i: (jnp.maximum(i - first, 0), 0))],
                   out_specs=[row, row, vec, pl.BlockSpec((8, LANES), lambda i: (0, 0))],
                   out_shape=[jax.ShapeDtypeStruct((m, d), F32), jax.ShapeDtypeStruct((m, d), BF16),
                              jax.ShapeDtypeStruct((1, d), F32), jax.ShapeDtypeStruct((8, LANES), F32)],
                   operands=[h, gain, target])
    return out


def _sigmoid(z):
    return 1.0 / (1.0 + jnp.exp(-z))


def _ffn_up(xn, wg, wu, name, comm=None):
    m, d = xn.shape
    nsh, c, _ = wg.shape
    tm = _tile(m, 1056, 16)

    def body(x_ref, wg_ref, wu_ref, g_ref, u_ref, a_ref):
        xv = x_ref[...]
        g = _dot(xv, wg_ref[...], NT_DIMS)
        u = _dot(xv, wu_ref[...], NT_DIMS)
        g_ref[...] = g.astype(BF16)
        u_ref[...] = u.astype(BF16)
        a_ref[...] = (g * _sigmoid(g) * u).astype(BF16)

    out = pl.BlockSpec((None, tm, c), lambda i, j: (j, i, 0))
    w = pl.BlockSpec((None, c, d), lambda i, j: (j, 0, 0))
    return _call(body, name=name, grid=(m // tm, nsh), in_specs=[pl.BlockSpec((tm, d), lambda i, j: (i, 0)), w, w],
                 out_specs=[out, out, out], out_shape=[jax.ShapeDtypeStruct((nsh, m, c), BF16)] * 3,
                 operands=[xn, wg, wu], comm=comm)


def _ffn_gate(xn, wg, name, comm=None):
    m, d = xn.shape
    nsh, c, _ = wg.shape
    tm = _tile(m, 1056, 16)

    def body(x_ref, wg_ref, g_ref):
        g_ref[...] = _dot(x_ref[...], wg_ref[...], NT_DIMS).astype(BF16)

    out, got = _call(body, name=name, grid=(m // tm, nsh),
                     in_specs=[pl.BlockSpec((tm, d), lambda i, j: (i, 0)), pl.BlockSpec((None, c, d), lambda i, j: (j, 0, 0))],
                     out_specs=[pl.BlockSpec((None, tm, c), lambda i, j: (j, i, 0))],
                     out_shape=[jax.ShapeDtypeStruct((nsh, m, c), BF16)], operands=[xn, wg], comm=comm)
    return out[0], got


def _ffn_up_given_gate(xn, wu, g, name, comm=None):
    m, d = xn.shape
    nsh, c, _ = wu.shape
    tm = _tile(m, 1056, 16)

    def body(x_ref, wu_ref, g_ref, u_ref, a_ref):
        u = _dot(x_ref[...], wu_ref[...], NT_DIMS)
        g = g_ref[...].astype(F32)
        u_ref[...] = u.astype(BF16)
        a_ref[...] = (g * _sigmoid(g) * u).astype(BF16)

    blk = pl.BlockSpec((None, tm, c), lambda i, j: (j, i, 0))
    return _call(body, name=name, grid=(m // tm, nsh),
                 in_specs=[pl.BlockSpec((tm, d), lambda i, j: (i, 0)), pl.BlockSpec((None, c, d), lambda i, j: (j, 0, 0)), blk],
                 out_specs=[blk, blk], out_shape=[jax.ShapeDtypeStruct((nsh, m, c), BF16)] * 2,
                 operands=[xn, wu, g], comm=comm)


def _ffn_contract(pairs, out_dtype, name, *, group, alpha=1.0, res=None, comm=None):
    nsh, m, c = pairs[0][0].shape
    d = pairs[0][1].shape[2]
    tm, tn = _tile(m, 1056, 16), _tile(d, 1024 if group * len(pairs) <= 4 and res is None else 512, LANES)
    a_spec = pl.BlockSpec((group, tm, c), lambda i, j, kk: (kk, i, 0))
    b_spec = pl.BlockSpec((group, c, tn), lambda i, j, kk: (kk, 0, j))
    o_spec = pl.BlockSpec((tm, tn), lambda i, j, kk: (i, j))
    return _mm_core([((a, a_spec), (b, b_spec)) for a, b in pairs], NN_DIMS, out_dtype, name=name,
                    grid=(m // tm, d // tn, nsh // group), out_shape=(m, d), out_spec=o_spec, acc_shape=(tm, tn),
                    alpha=alpha, res=None if res is None else (res, o_spec), comm=comm)


def _ffn_bwd_act(dh, wd, g, u, name, comm=None):
    m, d = dh.shape
    nsh, c, _ = wd.shape
    tm = _tile(m, 1056, 16)

    def body(dh_ref, wd_ref, g_ref, u_ref, dg_ref, du_ref):
        dact = 0.5 * _dot(dh_ref[...], wd_ref[...], NT_DIMS)
        gv = g_ref[...].astype(F32)
        uv = u_ref[...].astype(F32)
        sig = _sigmoid(gv)
        du_ref[...] = (dact * gv * sig).astype(BF16)
        dg_ref[...] = (dact * uv * sig * (1.0 + gv * (1.0 - sig))).astype(BF16)

    blk = pl.BlockSpec((None, tm, c), lambda i, j: (j, i, 0))
    return _call(body, name=name, grid=(m // tm, nsh),
                 in_specs=[pl.BlockSpec((tm, d), lambda i, j: (i, 0)), pl.BlockSpec((None, c, d), lambda i, j: (j, 0, 0)), blk, blk],
                 out_specs=[blk, blk], out_shape=[jax.ShapeDtypeStruct((nsh, m, c), BF16)] * 2, operands=[dh, wd, g, u],
                 comm=comm)


def _ffn_dw(z, x, name, alpha=1.0, comm=None):
    nsh, m, c = z.shape
    d = x.shape[1]
    tn, tk = _tile(d, 2048, LANES), _tile(m, 2112, 16)
    return _mm_core([((z, pl.BlockSpec((None, tk, c), lambda i, j, kk: (i, kk, 0))),
                      (x, pl.BlockSpec((tk, tn), lambda i, j, kk: (kk, j))))],
                    TN_DIMS, BF16, name=name, grid=(nsh, d // tn, m // tk), out_shape=(nsh, c, d),
                    out_spec=pl.BlockSpec((None, c, tn), lambda i, j, kk: (i, 0, j)), acc_shape=(c, tn), alpha=alpha,
                    comm=comm)


def _dot3(tri, x):
    h1 = x.astype(BF16)
    r1 = x - h1.astype(F32)
    h2 = r1.astype(BF16)
    h3 = (r1 - h2.astype(F32)).astype(BF16)
    return _dot(tri, h1) + _dot(tri, h2) + _dot(tri, h3)


def _log_sigmoid(z):
    return jnp.minimum(z, 0.0) - jnp.log(1.0 + jnp.exp(-jnp.abs(z)))


def _triangle(t, cmp):
    return cmp(lax.broadcasted_iota(jnp.int32, (t, t), 0), lax.broadcasted_iota(jnp.int32, (t, t), 1)).astype(BF16)


def _forget_cumsum(fl, bias, n_heads, pad, name):
    m = fl.shape[0]
    nb = m // ROW_BLOCK

    def body(fl_ref, b_ref, c_ref):
        tri = _triangle(ROW_BLOCK, lambda r, c: r >= c)
        lane_ok = lax.broadcasted_iota(jnp.int32, (ROW_BLOCK, LANES), 1) < n_heads
        rows = lax.broadcasted_iota(jnp.int32, (ROW_BLOCK, LANES), 0)

        def step(b, carry):
            off = pl.multiple_of(b * ROW_BLOCK, ROW_BLOCK)
            lf = _log_sigmoid(fl_ref[pl.ds(off, ROW_BLOCK), :] + b_ref[...])
            lf = jnp.where(lane_ok & (rows + off >= pad), lf, 0.0)
            cs = _dot3(tri, lf) + carry
            c_ref[pl.ds(off, ROW_BLOCK), :] = cs
            return cs[ROW_BLOCK - 1:ROW_BLOCK, :]

        lax.fori_loop(0, nb, step, jnp.zeros((1, LANES), F32))

    vmem = pl.BlockSpec(memory_space=pltpu.VMEM)
    return pl.pallas_call(
        body, name=name, out_shape=jax.ShapeDtypeStruct((m, LANES), F32), in_specs=[vmem, vmem], out_specs=vmem,
        compiler_params=pltpu.CompilerParams(vmem_limit_bytes=VMEM_LIMIT_BYTES),
    )(fl, bias)


def _forget_cumsum_bwd(dc_a, dc_b, fl, bias, n_heads, pad, name):
    m = fl.shape[0]
    nb = m // ROW_BLOCK

    def body(da_ref, db_ref, fl_ref, b_ref, dfl_ref, dbias_ref):
        tri = _triangle(ROW_BLOCK, lambda r, c: r <= c)
        lane_ok = lax.broadcasted_iota(jnp.int32, (ROW_BLOCK, LANES), 1) < n_heads
        rows = lax.broadcasted_iota(jnp.int32, (ROW_BLOCK, LANES), 0)

        def step(bb, carry):
            tail, dbias = carry
            off = pl.multiple_of((nb - 1 - bb) * ROW_BLOCK, ROW_BLOCK)
            dc = da_ref[pl.ds(off, ROW_BLOCK), :] + db_ref[pl.ds(off, ROW_BLOCK), :]
            dlf = _dot3(tri, dc) + tail
            z = fl_ref[pl.ds(off, ROW_BLOCK), :] + b_ref[...]
            dfl = jnp.where(lane_ok & (rows + off >= pad), dlf * _sigmoid(-z), 0.0)
            dfl_ref[pl.ds(off, ROW_BLOCK), :] = dfl
            return dlf[0:1, :], dbias + jnp.sum(dfl, axis=0, keepdims=True)

        zero = jnp.zeros((1, LANES), F32)
        _, dbias = lax.fori_loop(0, nb, step, (zero, zero))
        dbias_ref[...] = dbias

    vmem = pl.BlockSpec(memory_space=pltpu.VMEM)
    return pl.pallas_call(
        body, name=name,
        out_shape=[jax.ShapeDtypeStruct((m, LANES), F32), jax.ShapeDtypeStruct((1, LANES), F32)],
        in_specs=[vmem] * 4, out_specs=[vmem, vmem],
        compiler_params=pltpu.CompilerParams(vmem_limit_bytes=VMEM_LIMIT_BYTES),
    )(dc_a, dc_b, fl, bias)


def _attn_block(m):
    return 3 * ROW_BLOCK if m % (3 * ROW_BLOCK) == 0 else ROW_BLOCK


def _block(ref, j, t):
    return ref[pl.ds(pl.multiple_of(j * t, t), t), :]


def _head_norm(o, gain):
    r = lax.rsqrt(jnp.mean(o * o, axis=-1, keepdims=True) + EPS)
    return o * r * gain


def _head_norm_bwd(o, d_on, gain):
    r = lax.rsqrt(jnp.mean(o * o, axis=-1, keepdims=True) + EPS)
    ohat = o * r
    t = d_on * gain
    d_o = r * (t - ohat * jnp.mean(t * ohat, axis=-1, keepdims=True))
    return d_o, jnp.sum(d_on * ohat, axis=0, keepdims=True)


def _qkv_specs(t, m, h, first_col_block):
    q = pl.BlockSpec((t, HEAD_DIM), lambda hd, i: (i, first_col_block + hd))
    k = pl.BlockSpec((m, HEAD_DIM), lambda hd, i: (0, first_col_block + h + hd))
    v = pl.BlockSpec((m, HEAD_DIM), lambda hd, i: (0, first_col_block + 2 * h + hd))
    return q, k, v


def _fox_fwd(qkv, ccol, crow, gain, n_heads, pad, name, comm=None):
    m = qkv.shape[0]
    t = _attn_block(m)
    nq = m // t
    scale = HEAD_DIM ** -0.5
    hw = n_heads * HEAD_DIM

    def body(q_ref, k_ref, v_ref, ccol_ref, crow_ref, g_ref, o_ref, on_ref, lse_ref):
        i = pl.program_id(1)
        q = q_ref[...]
        ci = ccol_ref[...]
        qpos = i * t + lax.broadcasted_iota(jnp.int32, (t, 1), 0)

        def step(j, carry, masked):
            mx, l, acc = carry
            s = _dot(q, _block(k_ref, j, t), NT_DIMS) * scale + ci - crow_ref[j]
            if masked:
                kpos = j * t + lax.broadcasted_iota(jnp.int32, (1, t), 1)
                s = jnp.where((kpos <= qpos) & (kpos >= pad), s, NEG)
            mx_new = jnp.maximum(mx, jnp.max(s, axis=-1, keepdims=True))
            p = jnp.exp(s - mx_new)
            a = jnp.exp(mx - mx_new)
            return (mx_new, a * l + jnp.sum(p, axis=-1, keepdims=True),
                    a * acc + _dot(p.astype(BF16), _block(v_ref, j, t)))

        carry = step(0, (jnp.full((t, 1), NEG, F32), jnp.zeros((t, 1), F32), jnp.zeros((t, HEAD_DIM), F32)), True)
        n_mid = jnp.maximum(i - 1, 0)
        carry = lax.fori_loop(0, n_mid // 2, lambda jp, c: step(2 * jp + 2, step(2 * jp + 1, c, False), False), carry)
        carry = lax.fori_loop(0, n_mid % 2, lambda _, c: step(i - 1, c, False), carry)
        mx, l, acc = lax.fori_loop(0, jnp.minimum(i, 1), lambda _, c: step(i, c, True), carry)
        valid = qpos >= pad
        o = jnp.where(valid, acc / l, 0.0)
        o_ref[...] = o
        on_ref[...] = _head_norm(o, g_ref[...]).astype(BF16)
        lse_ref[...] = jnp.where(valid, mx + jnp.log(l), 0.0)

    q_spec, k_spec, v_spec = _qkv_specs(t, m, n_heads, 0)
    col = pl.BlockSpec((None, t, 1), lambda hd, i: (hd, i, 0))
    head = pl.BlockSpec((t, HEAD_DIM), lambda hd, i: (i, hd))
    return _call(body, name=name, grid=(n_heads, nq),
                 in_specs=[q_spec, k_spec, v_spec, col, pl.BlockSpec((None, nq, 1, t), lambda hd, i: (hd, 0, 0, 0)),
                           pl.BlockSpec((1, HEAD_DIM), lambda hd, i: (0, hd))],
                 out_specs=[head, head, col],
                 out_shape=[jax.ShapeDtypeStruct((m, hw), F32), jax.ShapeDtypeStruct((m, hw), BF16),
                            jax.ShapeDtypeStruct((n_heads, m, 1), F32)],
                 operands=[qkv, qkv, qkv, ccol, crow, gain], comm=comm)


def _fox_bwd(qkv, o, d_on, gain, lse, ccol, crow, n_heads, pad, name, comm=None):
    m = qkv.shape[0]
    t = _attn_block(m)
    nq = m // t
    scale = HEAD_DIM ** -0.5
    hw = n_heads * HEAD_DIM

    def body(q_ref, k_ref, v_ref, o_ref, don_ref, g_ref, lse_ref, ccol_ref, crow_ref,
             dq_ref, dk_ref, dv_ref, dg_ref, dccol_ref, dcrow_ref):
        i = pl.program_id(1)

        @pl.when(i == 0)
        def _():
            dk_ref[...] = jnp.zeros_like(dk_ref)
            dv_ref[...] = jnp.zeros_like(dv_ref)
            dg_ref[...] = jnp.zeros_like(dg_ref)
            dcrow_ref[...] = jnp.zeros_like(dcrow_ref)

        q = q_ref[...]
        o = o_ref[...]
        d_o, dgain = _head_norm_bwd(o, don_ref[...], g_ref[...])
        dg_ref[...] += dgain
        delta = jnp.sum(d_o * o, axis=-1, keepdims=True)
        d_ob = d_o.astype(BF16)
        ci = ccol_ref[...]
        lse_i = lse_ref[...]
        qpos = i * t + lax.broadcasted_iota(jnp.int32, (t, 1), 0)

        def step(j, carry, masked):
            dq, dci = carry
            k = _block(k_ref, j, t)
            v = _block(v_ref, j, t)
            off = pl.multiple_of(j * t, t)
            s = _dot(q, k, NT_DIMS) * scale + ci - crow_ref[j]
            if masked:
                kpos = off + lax.broadcasted_iota(jnp.int32, (1, t), 1)
                ok = (kpos <= qpos) & (kpos >= pad)
                p = jnp.where(ok, jnp.exp(jnp.where(ok, s - lse_i, 0.0)), 0.0)
            else:
                p = jnp.exp(s - lse_i)
            ds = p * (_dot(d_ob, v, NT_DIMS) - delta)
            dsb = ds.astype(BF16)
            dk_ref[pl.ds(off, t), :] += _dot(dsb, q, TN_DIMS) * scale
            dv_ref[pl.ds(off, t), :] += _dot(p.astype(BF16), d_ob, TN_DIMS)
            dcrow_ref[j] -= jnp.sum(ds, axis=0, keepdims=True)
            return dq + _dot(dsb, k), dci + jnp.sum(ds, axis=-1, keepdims=True)

        carry = step(0, (jnp.zeros((t, HEAD_DIM), F32), jnp.zeros((t, 1), F32)), True)
        n_mid = jnp.maximum(i - 1, 0)
        carry = lax.fori_loop(0, n_mid // 2, lambda jp, c: step(2 * jp + 2, step(2 * jp + 1, c, False), False), carry)
        carry = lax.fori_loop(0, n_mid % 2, lambda _, c: step(i - 1, c, False), carry)
        dq, dci = lax.fori_loop(0, jnp.minimum(i, 1), lambda _, c: step(i, c, True), carry)
        dq_ref[...] = (dq * scale).astype(BF16)
        dccol_ref[...] = dci

    q_spec, k_spec, v_spec = _qkv_specs(t, m, n_heads, 0)
    col = pl.BlockSpec((None, t, 1), lambda hd, i: (hd, i, 0))
    rowc = pl.BlockSpec((None, nq, 1, t), lambda hd, i: (hd, 0, 0, 0))
    head = pl.BlockSpec((t, HEAD_DIM), lambda hd, i: (i, hd))
    whole = pl.BlockSpec((m, HEAD_DIM), lambda hd, i: (0, hd))
    gvec = pl.BlockSpec((1, HEAD_DIM), lambda hd, i: (0, hd))
    return _call(body, name=name, grid=(n_heads, nq),
                 in_specs=[q_spec, k_spec, v_spec, head, head, gvec, col, col, rowc],
                 out_specs=[head, whole, whole, gvec, col, rowc],
                 out_shape=[jax.ShapeDtypeStruct((m, hw), BF16), jax.ShapeDtypeStruct((m, hw), F32),
                            jax.ShapeDtypeStruct((m, hw), F32), jax.ShapeDtypeStruct((1, hw), F32),
                            jax.ShapeDtypeStruct((n_heads, m, 1), F32), jax.ShapeDtypeStruct((n_heads, nq, 1, t), F32)],
                 operands=[qkv, qkv, qkv, o, d_on, gain, lse, ccol, crow], comm=comm)


def _sb_scores(z):
    lp = jnp.log(1.0 + jnp.exp(-jnp.abs(z)))
    return jnp.minimum(z, 0.0) - lp, jnp.minimum(-z, 0.0) - lp


def _sb_fwd(qkv, gain, n_heads, pad, name, comm=None):
    m = qkv.shape[0]
    t = _attn_block(m)
    nq = m // t
    assert nq <= LANES
    scale = HEAD_DIM ** -0.5
    hw = n_heads * HEAD_DIM

    def body(q_ref, k_ref, v_ref, g_ref, after_ref, o_ref, on_ref, run_ref):
        i = pl.program_id(1)
        q = q_ref[...]
        qpos = i * t + lax.broadcasted_iota(jnp.int32, (t, 1), 0)
        after = after_ref[...]
        lane = lax.broadcasted_iota(jnp.int32, (t, LANES), 1)

        def step(j, carry, masked):
            run, acc = carry
            ls_pos, log_1m = _sb_scores(_dot(q, _block(k_ref, j, t), NT_DIMS) * scale)
            if masked:
                kpos = j * t + lax.broadcasted_iota(jnp.int32, (1, t), 1)
                ok = (kpos < qpos) & (kpos >= pad)
                log_1m = jnp.where(ok, log_1m, 0.0)
            a = jnp.exp(ls_pos + _dot_split(log_1m, after) + run)
            if masked:
                a = jnp.where(ok, a, 0.0)
            run_ref[...] = jnp.where(lane == j, run, run_ref[...])
            return run + jnp.sum(log_1m, axis=-1, keepdims=True), acc + _dot(a.astype(BF16), _block(v_ref, j, t))

        run_ref[...] = jnp.zeros_like(run_ref)
        carry = step(i, (jnp.zeros((t, 1), F32), jnp.zeros((t, HEAD_DIM), F32)), True)
        n_mid = jnp.maximum(i - 1, 0)
        carry = lax.fori_loop(0, n_mid // 2, lambda jp, c: step(i - 2 * jp - 2, step(i - 2 * jp - 1, c, False), False), carry)
        carry = lax.fori_loop(0, n_mid % 2, lambda _, c: step(1, c, False), carry)
        _, o = lax.fori_loop(0, jnp.minimum(i, 1), lambda _, c: step(0, c, True), carry)
        o_ref[...] = o
        on_ref[...] = _head_norm(o, g_ref[...]).astype(BF16)

    q_spec, k_spec, v_spec = _qkv_specs(t, m, n_heads, 3 * n_heads)
    head = pl.BlockSpec((t, HEAD_DIM), lambda hd, i: (i, hd))
    return _call(body, name=name, grid=(n_heads, nq),
                 in_specs=[q_spec, k_spec, v_spec, pl.BlockSpec((1, HEAD_DIM), lambda hd, i: (0, hd)),
                           pl.BlockSpec((t, t), lambda hd, i: (0, 0))],
                 out_specs=[head, head, pl.BlockSpec((None, t, LANES), lambda hd, i: (hd, i, 0))],
                 out_shape=[jax.ShapeDtypeStruct((m, hw), F32), jax.ShapeDtypeStruct((m, hw), BF16),
                            jax.ShapeDtypeStruct((n_heads, m, LANES), F32)],
                 operands=[qkv, qkv, qkv, gain, _triangle(t, lambda r, c: r > c)], comm=comm)


def _sb_bwd(qkv, o, d_on, gain, runs, n_heads, pad, name, comm=None):
    m = qkv.shape[0]
    t = _attn_block(m)
    nq = m // t
    scale = HEAD_DIM ** -0.5
    hw = n_heads * HEAD_DIM

    def body(q_ref, k_ref, v_ref, o_ref, don_ref, g_ref, run_ref, after_ref, before_ref, dq_ref, dk_ref, dv_ref, dg_ref):
        i = pl.program_id(1)

        @pl.when(i == 0)
        def _():
            dk_ref[...] = jnp.zeros_like(dk_ref)
            dv_ref[...] = jnp.zeros_like(dv_ref)
            dg_ref[...] = jnp.zeros_like(dg_ref)

        q = q_ref[...]
        d_o, dgain = _head_norm_bwd(o_ref[...], don_ref[...], g_ref[...])
        dg_ref[...] += dgain
        d_ob = d_o.astype(BF16)
        runs_i = run_ref[...]
        qpos = i * t + lax.broadcasted_iota(jnp.int32, (t, 1), 0)
        after = after_ref[...]
        before = before_ref[...]
        lane = lax.broadcasted_iota(jnp.int32, (t, LANES), 1)

        def step(j, carry, masked):
            g_run, dq = carry
            k = _block(k_ref, j, t)
            v = _block(v_ref, j, t)
            off = pl.multiple_of(j * t, t)
            ls_pos, ls_neg = _sb_scores(_dot(q, k, NT_DIMS) * scale)
            log_1m = ls_neg
            if masked:
                kpos = off + lax.broadcasted_iota(jnp.int32, (1, t), 1)
                ok = (kpos < qpos) & (kpos >= pad)
                log_1m = jnp.where(ok, ls_neg, 0.0)
            run = jnp.sum(jnp.where(lane == j, runs_i, 0.0), axis=-1, keepdims=True)
            a = jnp.exp(ls_pos + _dot_split(log_1m, after) + run)
            if masked:
                a = jnp.where(ok, a, 0.0)
            g = a * _dot(d_ob, v, NT_DIMS)
            prefix = _dot(g.astype(BF16), before) + g_run
            dz = g * jnp.exp(ls_neg) - jnp.exp(ls_pos) * prefix
            if masked:
                dz = jnp.where(ok, dz, 0.0)
            dzb = dz.astype(BF16)
            dk_ref[pl.ds(off, t), :] += _dot(dzb, q, TN_DIMS) * scale
            dv_ref[pl.ds(off, t), :] += _dot(a.astype(BF16), d_ob, TN_DIMS)
            return g_run + jnp.sum(g, axis=-1, keepdims=True), dq + _dot(dzb, k)

        carry = step(0, (jnp.zeros((t, 1), F32), jnp.zeros((t, HEAD_DIM), F32)), True)
        n_mid = jnp.maximum(i - 1, 0)
        carry = lax.fori_loop(0, n_mid // 2, lambda jp, c: step(2 * jp + 2, step(2 * jp + 1, c, False), False), carry)
        carry = lax.fori_loop(0, n_mid % 2, lambda _, c: step(i - 1, c, False), carry)
        _, dq = lax.fori_loop(0, jnp.minimum(i, 1), lambda _, c: step(i, c, True), carry)
        dq_ref[...] = (dq * scale).astype(BF16)

    q_spec, k_spec, v_spec = _qkv_specs(t, m, n_heads, 3 * n_heads)
    head = pl.BlockSpec((t, HEAD_DIM), lambda hd, i: (i, hd))
    whole = pl.BlockSpec((m, HEAD_DIM), lambda hd, i: (0, hd))
    gvec = pl.BlockSpec((1, HEAD_DIM), lambda hd, i: (0, hd))
    tri = pl.BlockSpec((t, t), lambda hd, i: (0, 0))
    return _call(body, name=name, grid=(n_heads, nq),
                 in_specs=[q_spec, k_spec, v_spec, head, head, gvec, pl.BlockSpec((None, t, LANES), lambda hd, i: (hd, i, 0)),
                           tri, tri],
                 out_specs=[head, whole, whole, gvec],
                 out_shape=[jax.ShapeDtypeStruct((m, hw), BF16), jax.ShapeDtypeStruct((m, hw), F32),
                            jax.ShapeDtypeStruct((m, hw), F32), jax.ShapeDtypeStruct((1, hw), F32)],
                 operands=[qkv, qkv, qkv, o, d_on, gain, runs, _triangle(t, lambda r, c: r > c), _triangle(t, lambda r, c: r < c)],
                 comm=comm)


def _adamw(parts, w, m1, v2, name, comm=None):
    nl, r, c = w.shape
    assert len(parts) == nl
    n_parts = parts[0].shape[0]
    block_elems = 256 * 1024
    if r % 8 == 0 or c % LANES != 0:
        tr, tc = _tile(r, max(8, block_elems // (-(-c // LANES) * LANES)), 8), c
    else:
        tr, tc = r, _tile(c, max(LANES, block_elems // r // LANES * LANES), LANES)
    nr, nc = r // tr, c // tc
    bias1 = 1.0 / (1.0 - ADAM_B1 ** ADAM_STEP)
    bias2 = 1.0 / (1.0 - ADAM_B2 ** ADAM_STEP)

    def body(*refs):
        p_refs = refs[:nl]
        w_ref, m_ref, v_ref, g_ref, d_ref, nm_ref, nv_ref = refs[nl:]

        def update(p_ref):
            g = p_ref[0].astype(F32)
            for s in range(1, n_parts):
                g = g + p_ref[s].astype(F32)
            m_new = ADAM_B1 * m_ref[...] + (1.0 - ADAM_B1) * g
            v_new = ADAM_B2 * v_ref[...] + (1.0 - ADAM_B2) * (g * g)
            g_ref[...] = g
            nm_ref[...] = m_new
            nv_ref[...] = v_new
            d_ref[...] = -ADAM_LR * ((m_new * bias1) / (jnp.sqrt(v_new * bias2) + ADAM_EPS) + ADAM_WD * w_ref[...])

        for ll in range(nl):
            @pl.when(pl.program_id(0) == ll)
            def _(ll=ll):
                update(p_refs[ll])

    def part_spec(ll):
        def index(l, i, j):
            pin = jnp.where(l < ll, 0, 1)
            return 0, jnp.where(l == ll, i, pin * (nr - 1)), jnp.where(l == ll, j, pin * (nc - 1))
        return pl.BlockSpec((n_parts, tr, tc), index)

    blk = pl.BlockSpec((None, tr, tc), lambda l, i, j: (l, i, j))
    out, got = _call(body, name=name, grid=(nl, nr, nc), in_specs=[part_spec(ll) for ll in range(nl)] + [blk, blk, blk],
                     out_specs=[blk] * 4, out_shape=[jax.ShapeDtypeStruct((nl, r, c), F32)] * 4,
                     operands=list(parts) + [w, m1, v2], comm=comm)
    return out if comm is None else (out, got)


SMALL_WEIGHTS = ("ffn1_norm", "mix_norm", "b_forget", "g_fox", "g_sb", "ffn2_norm", "final_norm")
WEIGHT_ORDER = ("meta_tokens", "ffn1_norm", "ffn1_w_gate", "ffn1_w_up", "ffn1_w_down", "mix_norm", "w_in", "b_forget",
                "g_fox", "g_sb", "w_out", "ffn2_norm", "ffn2_w_gate", "ffn2_w_up", "ffn2_w_down", "final_norm")
GROUPS = {"ffn1": ("ffn1_w_gate", "ffn1_w_up", "ffn1_w_down"), "mix": ("w_in", "w_out"),
          "ffn2": ("ffn2_w_gate", "ffn2_w_up", "ffn2_w_down")}
TRANSPOSED = ("ffn1_w_gate", "ffn1_w_up", "ffn2_w_gate", "ffn2_w_up")


def _pad_lanes(a):
    extra = (-a.shape[-1]) % LANES
    return a if extra == 0 else jnp.pad(a, [(0, 0)] * (a.ndim - 1) + [(0, extra)])


def _ffn_backward(dh_b, dh, saved, gain, wg, wu, wd, tag, carried=()):
    h, xn, g, u, act = saved

    def half(k, which, into=()):
        if len(carried) <= k:
            return None
        rows = carried[k].shape[1]
        cut = rows // 2 // 16 * 16
        return _Exchange([carried[k]], rows=(0, cut) if which == 0 else (cut, rows - cut), into=into)

    (dg, du), got_b = _ffn_bwd_act(dh_b, wd, g, u, f"{tag}_bwd_act", half(1, 0))
    d_wd = _ffn_dw(act, dh_b, f"{tag}_dwd", alpha=0.5, comm=half(1, 1, got_b))
    d_wg = _ffn_dw(dg, xn, f"{tag}_dwg", comm=half(2, 0))
    got_c = d_wg[1] if len(carried) > 2 else ()
    d_wu = _ffn_dw(du, xn, f"{tag}_dwu", comm=half(2, 1, got_c))
    dxn = _ffn_contract([(dg, wg), (du, wu)], F32, f"{tag}_dxn", group=2, comm=_Exchange(carried[:1]) if carried else None)
    got = []
    if carried:
        dxn, got = dxn[0], list(dxn[1])
    if len(carried) > 1:
        d_wd, got_b = d_wd
        got += got_b
    if len(carried) > 2:
        d_wg, (d_wu, got_c) = d_wg[0], d_wu
        got += got_c
    dh_in, dh_in_b, d_gain = _rms_bwd(dxn, h, gain, dh, f"{tag}_norm_bwd")
    return dh_in, dh_in_b, d_gain, d_wg, d_wu, d_wd, got


def _ffn_backward_last(dh_b, dh, saved, gain, wg, wu, wd, tag, first, second):
    h, xn, g, u, act = saved
    (dg, du), got_first = _ffn_bwd_act(dh_b, wd, g, u, f"{tag}_bwd_act", first)
    d_wd, got_second = _ffn_dw(act, dh_b, f"{tag}_dwd", alpha=0.5, comm=second)
    c = d_wd.shape[1]
    half = c // 2 // 16 * 16
    d_wg, got_wd = _ffn_dw(dg, xn, f"{tag}_dwg", comm=_Exchange([d_wd], rows=(0, half)))
    d_wu, got_wd = _ffn_dw(du, xn, f"{tag}_dwu", comm=_Exchange([d_wd], rows=(half, c - half), into=got_wd))
    dxn, got_wg = _ffn_contract([(dg, wg), (du, wu)], F32, f"{tag}_dxn", group=2, comm=_Exchange([d_wg]))
    dh_in, dh_in_b, d_gain = _rms_bwd(dxn, h, gain, dh, f"{tag}_norm_bwd")
    return dh_in, dh_in_b, d_gain, got_first, got_second, got_wg[0], got_wd[0], d_wu


def _mixer_forward(h, gain, w_in_t, bias, g_fox, g_sb, w_out, n_heads, pad, tag, comm_fox=None, comm_sb=None):
    m = h.shape[0]
    t = _attn_block(m)
    hw = n_heads * HEAD_DIM
    xn = _rms_fwd(h, gain, f"{tag}_norm")
    qkv = _mm([(xn, w_in_t[:6 * hw])], "nt", BF16, name=f"{tag}_qkv")
    fl = _mm([(xn, w_in_t[6 * hw:])], "nt", F32, name=f"{tag}_forget")
    c = _forget_cumsum(fl, bias, n_heads, pad, f"{tag}_cumsum")
    c_heads = c[:, :n_heads].T
    ccol = c_heads[:, :, None]
    crow = c_heads.reshape(n_heads, m // t, 1, t)
    (o_f, on_f, lse), got_fox = _fox_fwd(qkv, ccol, crow, g_fox, n_heads, pad, f"{tag}_fox", comm_fox)
    (o_s, on_s, runs), got_sb = _sb_fwd(qkv, g_sb, n_heads, pad, f"{tag}_sb", comm_sb)
    if w_out is None:
        w_out = got_sb[-1].reshape(-1, h.shape[1])
    h_out = _mm([(on_f, w_out[:hw]), (on_s, w_out[hw:])], "nn", F32, name=f"{tag}_out", res=h)
    return h_out, (h, xn, qkv, fl, ccol, crow, o_f, on_f, lse, o_s, on_s, runs), got_fox, got_sb, w_out


def _mixer_backward(dh_b, dh_out, saved, gain, w_in_t, bias, g_fox, g_sb, w_out, n_heads, pad, tag, comm_fox=None, comm_sb=None,
                    rows_in_dxn=0):
    h, xn, qkv, fl, ccol, crow, o_f, on_f, lse, o_s, on_s, runs = saved
    m = h.shape[0]
    hw = n_heads * HEAD_DIM
    d_on_f = _mm([(dh_b, w_out[:hw])], "nt", F32, name=f"{tag}_don_f")
    d_on_s = _mm([(dh_b, w_out[hw:])], "nt", F32, name=f"{tag}_don_s")
    d_wout = jnp.concatenate([_mm([(on_f, dh_b)], "tn", BF16, name=f"{tag}_dwout_f"),
                              _mm([(on_s, dh_b)], "tn", BF16, name=f"{tag}_dwout_s")], axis=0)
    (dq_f, dk_f, dv_f, dg_fox, dccol, dcrow), got_fox = _fox_bwd(
        qkv, o_f, d_on_f, g_fox, lse, ccol, crow, n_heads, pad, f"{tag}_fox_bwd", comm_fox)
    (dq_s, dk_s, dv_s, dg_sb), got_sb = _sb_bwd(qkv, o_s, d_on_s, g_sb, runs, n_heads, pad, f"{tag}_sb_bwd", comm_sb)
    dc_a = _pad_lanes(dccol[:, :, 0].T)
    dc_b = _pad_lanes(dcrow.reshape(n_heads, m).T)
    dfl, dbias = _forget_cumsum_bwd(dc_a, dc_b, fl, bias, n_heads, pad, f"{tag}_cumsum_bwd")
    dproj = jnp.concatenate([dq_f, dk_f.astype(BF16), dv_f.astype(BF16), dq_s, dk_s.astype(BF16), dv_s.astype(BF16),
                             dfl.astype(BF16)], axis=1)
    d_win = _mm([(dproj, xn)], "tn", BF16, name=f"{tag}_dwin")[:6 * hw + n_heads].reshape(N_DEV, -1, h.shape[1])
    got_win = None
    if rows_in_dxn:
        dxn, got_win = _mm([(dproj, w_in_t)], "nn", F32, name=f"{tag}_dxn", whole_k=True,
                           comm=_Exchange([d_win], rows=(0, rows_in_dxn)))
    else:
        dxn = _mm([(dproj, w_in_t)], "nn", F32, name=f"{tag}_dxn", whole_k=True)
    dh, dh_in_b, d_gain = _rms_bwd(dxn, h, gain, dh_out, f"{tag}_norm_bwd")
    return dh, dh_in_b, d_gain, d_win, dbias, dg_fox, dg_sb, d_wout, got_fox, got_sb, got_win


def kernel(x, meta_tokens, ffn1_norm, ffn1_w_gate, ffn1_w_up, ffn1_w_down, mix_norm, w_in, b_forget, g_fox, g_sb, w_out, ffn2_norm, ffn2_w_gate, ffn2_w_up, ffn2_w_down, final_norm, loss_target, m_meta_tokens, m_ffn1_norm, m_ffn1_w_gate, m_ffn1_w_up, m_ffn1_w_down, m_mix_norm, m_w_in, m_b_forget, m_g_fox, m_g_sb, m_w_out, m_ffn2_norm, m_ffn2_w_gate, m_ffn2_w_up, m_ffn2_w_down, m_final_norm, v_meta_tokens, v_ffn1_norm, v_ffn1_w_gate, v_ffn1_w_up, v_ffn1_w_down, v_mix_norm, v_w_in, v_b_forget, v_g_fox, v_g_sb, v_w_out, v_ffn2_norm, v_ffn2_w_gate, v_ffn2_w_up, v_ffn2_w_down, v_final_norm):
    given = dict(locals())
    seq, d = x.shape[1], x.shape[2]
    depth = ffn1_norm.shape[0]
    d_in = N_DEV * w_in.shape[2]
    n_heads = g_fox.shape[1] // HEAD_DIM
    hw = n_heads * HEAD_DIM
    assert seq % ROW_BLOCK == 0 and d_in == 6 * hw + n_heads and n_heads <= LANES
    pad = (-(seq + N_META)) % ROW_BLOCK
    x_off = pad + N_META

    def view(n, a):
        if n in TRANSPOSED:
            return jnp.swapaxes(a, 1, 2)
        return a.transpose(2, 0, 1) if n == "w_in" else a

    def unview(n, a):
        if n in TRANSPOSED:
            return jnp.swapaxes(a, 1, 2)
        return a.transpose(1, 2, 0) if n == "w_in" else a

    def shard(n, l):
        v = view(n, given[n])
        return (v[:, l] if n == "w_in" else v[l]).astype(BF16)

    def shards(group, l):
        return [shard(n, l) for n in GROUPS[group]]

    sh = shards("ffn1", 0)
    first = _run_alone(_Gather([sh[0], meta_tokens]), "gather_first")
    full = {}
    meta_full = first[1].transpose(1, 0, 2).reshape(N_META, d)
    h = jnp.concatenate([jnp.zeros((pad, d), F32), meta_full, x[0]], axis=0)
    weights, saved = [], []
    for l in range(depth):
        xn = _rms_fwd(h, ffn1_norm[l:l + 1], "ffn1_norm")
        if l == 0:
            mix0 = shards("mix", 0)
            g, (wu1,) = _ffn_gate(xn, first[0], "ffn1_gate", _Gather([sh[1]]))
            (u, act), (wd1,) = _ffn_up_given_gate(xn, wu1, g, "ffn1_up", _Gather([sh[2]]))
            full[("ffn1", 0)] = (first[0], wu1, wd1)
            h_out, (win3,) = _ffn_contract([(act, wd1)], F32, "ffn1_down", group=4, alpha=0.5, res=h, comm=_Gather(mix0[:1]))
        else:
            wg1, wu1, wd1 = full[("ffn1", l)]
            (g, u, act), full[("mix", l)] = _ffn_up(xn, wg1, wu1, "ffn1_up", _Gather(shards("mix", l)))
            h_out = _ffn_contract([(act, wd1)], F32, "ffn1_down", group=4, alpha=0.5, res=h)
        s1 = (h, xn, g, u, act)
        h = h_out
        if l > 0:
            win3, wout3 = full[("mix", l)]
        w_in_t = jnp.pad(win3.reshape(d_in, d), ((0, 6 * hw + LANES - d_in), (0, 0)))
        bias = _pad_lanes(b_forget[l:l + 1])
        nxt = shards("ffn1", l + 1) if l + 1 < depth else []
        ffn2 = shards("ffn2", l)
        h, sm, (wg2, wu2), got, w_out_full = _mixer_forward(
            h, mix_norm[l:l + 1], w_in_t, bias, g_fox[l:l + 1], g_sb[l:l + 1],
            None if l == 0 else wout3.reshape(N_DEV * wout3.shape[1], d), n_heads, pad, "mix",
            _Gather(ffn2[:2]), _Gather(nxt + ffn2[2:] + (mix0[1:] if l == 0 else [])))
        if nxt:
            full[("ffn1", l + 1)] = got[:3]
        wd2 = got[len(nxt)]
        full[("ffn2", l)] = (wg2, wu2, wd2)
        xn = _rms_fwd(h, ffn2_norm[l:l + 1], "ffn2_norm")
        (g, u, act), _ = _ffn_up(xn, wg2, wu2, "ffn2_up")
        s2 = (h, xn, g, u, act)
        h = _ffn_contract([(act, wd2)], F32, "ffn2_down", group=4, alpha=0.5, res=h)
        weights.append((w_in_t, w_out_full, bias))
        saved.append((s1, sm, s2))

    dh, dh_b, d_final, loss_arr = _loss_head(h, final_norm[None, :], loss_target[0], x_off, "loss_head")
    small = {n: [None] * depth for n in SMALL_WEIGHTS[:-1]}
    partial, received = {}, {}

    def names(group, l):
        return [(n, l) for n in GROUPS[group]]

    def send(keys):
        return [partial[k] for k in keys]

    for l in reversed(range(depth)):
        w_in_t, w_out_full, bias = weights[l]
        s1, sm, s2 = saved[l]
        wg2, wu2, wd2 = full[("ffn2", l)]
        up = l + 1 < depth
        in_dxn = names("ffn1", l + 1) if up else []
        in_fox = names("mix", l + 1) if up else []
        in_sb = names("ffn2", l)
        dh, dh_b, small["ffn2_norm"][l], partial[("ffn2_w_gate", l)], partial[("ffn2_w_up", l)], partial[("ffn2_w_down", l)], got = (
            _ffn_backward(dh_b, dh, s2, ffn2_norm[l:l + 1], wg2, wu2, wd2, "ffn2", send(in_dxn)))
        received.update(zip(in_dxn, got))
        win_rows = (d_in // N_DEV // 2 // 16 * 16) if l == 0 else 0
        (dh, dh_b, small["mix_norm"][l], partial[("w_in", l)], dbias, small["g_fox"][l], small["g_sb"][l], d_wout,
         got_fox, got_sb, got_win) = _mixer_backward(
            dh_b, dh, sm, mix_norm[l:l + 1], w_in_t, bias, g_fox[l:l + 1], g_sb[l:l + 1], w_out_full, n_heads, pad, "mix",
            _Exchange(send(in_fox)) if in_fox else None, _Exchange(send(in_sb)), win_rows)
        received.update(zip(in_fox, got_fox or []))
        received.update(zip(in_sb, got_sb))
        partial[("w_out", l)] = d_wout.reshape(N_DEV, -1, d)
        small["b_forget"][l] = dbias[:, :n_heads]
        wg1, wu1, wd1 = full[("ffn1", l)]
        if l > 0:
            dh, dh_b, small["ffn1_norm"][l], partial[("ffn1_w_gate", l)], partial[("ffn1_w_up", l)], partial[("ffn1_w_down", l)], _ = (
                _ffn_backward(dh_b, dh, s1, ffn1_norm[l:l + 1], wg1, wu1, wd1, "ffn1"))
        else:
            rest = (win_rows, d_in // N_DEV - win_rows)
            (dh, dh_b, small["ffn1_norm"][0], got_a, got_b, received[("ffn1_w_gate", 0)], received[("ffn1_w_down", 0)],
             last_grad) = _ffn_backward_last(
                dh_b, dh, s1, ffn1_norm[0:1], wg1, wu1, wd1, "ffn1", _Exchange([partial[("w_out", 0)]]),
                _Exchange([partial[("w_in", 0)]], rows=rest, into=got_win))
            received[("w_out", 0)], received[("w_in", 0)] = got_a[0], got_b[0]
    grad_x = dh[x_off:][None]
    d_meta = dh[pad:x_off].reshape(N_META, N_DEV, -1).transpose(1, 0, 2)
    received[("meta_tokens", 0)] = _run_alone(_Exchange([d_meta]), "exchange_meta")[0]

    vec = [loss_arr[0:1, :]] + [_pad_lanes(jnp.concatenate(small[n], axis=0).reshape(1, -1)) for n in SMALL_WEIGHTS[:-1]]
    vec.append(d_final)
    sizes = [a.shape[1] for a in vec]
    summed = _all_reduce_small(jnp.concatenate(vec, axis=1), "reduce_small")
    loss = summed[0, 0]

    def packed(prefix):
        cols = [jnp.zeros((1, LANES), F32)]
        cols += [_pad_lanes(given[prefix + n].reshape(1, -1)) for n in SMALL_WEIGHTS]
        return jnp.concatenate(cols, axis=1)[None]

    small_out = _adamw([summed[None]], packed(""), packed("m_"), packed("v_"), "adamw_small")

    c_last = last_grad.shape[1]
    quarter = c_last // 4 // 16 * 16
    pieces = [(k * quarter, quarter if k < 3 else c_last - 3 * quarter) for k in range(4)]
    order = ["ffn2_w_gate", "ffn2_w_up", "ffn2_w_down", "ffn1_w_gate", "ffn1_w_down", "w_in", "w_out", "ffn1_w_up"]
    out, arrived = {}, []
    for k, n in enumerate(order):
        wv, mv, vv = (view(n, given[p + n]) for p in ("", "m_", "v_"))
        if n == "ffn1_w_up":
            received[(n, 0)] = arrived[0]
        if n == "w_in":
            per_layer = [_adamw([received[(n, l)]], wv[:, l][None], mv[:, l][None], vv[:, l][None], "adamw_" + n)
                         for l in range(depth)]
            res = [jnp.stack([per_layer[l][k][0] for l in range(depth)], axis=1) for k in range(4)]
        elif k < len(pieces):
            res, arrived = _adamw([received[(n, l)] for l in range(depth)], wv, mv, vv, "adamw_" + n,
                                  _Exchange([last_grad], rows=pieces[k], into=arrived))
        else:
            res = _adamw([received[(n, l)] for l in range(depth)], wv, mv, vv, "adamw_" + n)
        out[n] = [unview(n, r) for r in res]
    out["meta_tokens"] = [r[0] for r in _adamw([received[("meta_tokens", 0)]], meta_tokens[None], m_meta_tokens[None],
                                               v_meta_tokens[None], "adamw_meta_tokens")]
    offset = sizes[0]
    for n, size in zip(SMALL_WEIGHTS, sizes[1:]):
        shape, count = given[n].shape, given[n].size
        out[n] = [r[0, 0, offset:offset + count].reshape(shape) for r in small_out]
        offset += size

    result = [loss, grad_x]
    for k in range(4):
        result += [out[n][k] for n in WEIGHT_ORDER]
    return tuple(result)
```

```python
import math

import jax
import jax.numpy as jnp
from jax import lax
from jax.experimental import pallas as pl
from jax.experimental.pallas import tpu as pltpu

F32 = jnp.float32
BF16 = jnp.bfloat16

N_DEV = 8
N_META = 16
HEAD_DIM = 128
ROW_BLOCK = 128
LANES = 128
EPS = 1e-6
NEG = -1e30
ADAM_LR = 0.001
ADAM_B1 = 0.9
ADAM_B2 = 0.999
ADAM_EPS = 1e-08
ADAM_WD = 0.01
ADAM_STEP = 10
VMEM_LIMIT_BYTES = 56 * 1024 * 1024
MESH = pl.DeviceIdType.MESH

NT_DIMS = (((1,), (1,)), ((), ()))
TN_DIMS = (((0,), (0,)), ((), ()))
NN_DIMS = (((1,), (0,)), ((), ()))
ANY = pl.BlockSpec(memory_space=pl.ANY)


def _tile(n, cap, align):
    best = None
    for d in range(align, min(n, cap) + 1, align):
        if n % d == 0:
            best = d
    return best if best is not None else n


def _dot(a, b, dims=NN_DIMS):
    return lax.dot_general(a, b, dims, preferred_element_type=F32)


def _dot_split(x, u):
    hi = x.astype(BF16)
    lo = (x - hi.astype(F32)).astype(BF16)
    return _dot(hi, u) + _dot(lo, u)


def _my_position():
    return lax.axis_index("x"), lax.axis_index("y"), lax.axis_index("c")


class _Gather:
    n_phases = 3

    def __init__(self, arrs):
        self.arrs = list(arrs)
        n = len(self.arrs)
        self.out_shapes = [jax.ShapeDtypeStruct((N_DEV,) + a.shape, a.dtype) for a in self.arrs]
        self.scratch = [pltpu.SemaphoreType.DMA((n, 7)), pltpu.SemaphoreType.DMA((n, 7)), pltpu.SemaphoreType.DMA((n,))]

    def phase(self, p, ins, outs, sems):
        send_sems, recv_sems, local_sems = sems
        n = len(self.arrs)
        x, y, c = _my_position()
        me, sibling = (x, y, c), (x, y, 1 - c)
        chips = [(1 - x, y), (x, 1 - y), (1 - x, 1 - y)]

        def copy(a, k, block, to, src=None):
            slot = outs[a].at[4 * block[0] + 2 * block[1] + block[2]]
            return pltpu.make_async_remote_copy(
                src_ref=slot if src is None else src, dst_ref=slot,
                send_sem=send_sems.at[a, k], recv_sem=recv_sems.at[a, k], device_id=to, device_id_type=MESH)

        def local(a):
            return pltpu.make_async_copy(ins[a], outs[a].at[4 * x + 2 * y + c], local_sems.at[a])

        def first(a):
            return [copy(a, 0, me, sibling, src=ins[a])] + [copy(a, 1 + j, me, (*chip, c), src=ins[a]) for j, chip in enumerate(chips)]

        def passed(a, j):
            return copy(a, 4 + j, (*chips[j], c), sibling)

        if p == 0:
            for a in range(n):
                local(a).start()
            for a in range(n):
                for cp in first(a):
                    cp.start()
        elif p == 1:
            for a in range(n):
                for j, chip in enumerate(chips):
                    copy(a, 1 + j, (*chip, c), me).wait_recv()
                    passed(a, j).start()
        else:
            for a in range(n):
                copy(a, 0, sibling, me).wait_recv()
                for j, chip in enumerate(chips):
                    copy(a, 4 + j, (*chip, 1 - c), me).wait_recv()
            for a in range(n):
                for cp in first(a) + [passed(a, j) for j in range(3)]:
                    cp.wait_send()
                local(a).wait()


class _Exchange:
    n_phases = 2

    def __init__(self, arrs, rows=None, into=()):
        self.arrs = list(arrs)
        self.rows = rows
        self.into = list(into)
        n = len(self.arrs)
        self.out_shapes = [jax.ShapeDtypeStruct(a.shape, a.dtype) for a in self.arrs]
        self.scratch = [pltpu.SemaphoreType.DMA((n, 7)), pltpu.SemaphoreType.DMA((n, 7)), pltpu.SemaphoreType.DMA((n,))]

    def phase(self, p, ins, outs, sems):
        send_sems, recv_sems, local_sems = sems
        n = len(self.arrs)
        x, y, c = _my_position()
        me = 4 * x + 2 * y + c

        def peer_of(r):
            return (x ^ ((r >> 2) & 1), y ^ ((r >> 1) & 1), c ^ (r & 1))

        def part(ref, d):
            return ref.at[d] if self.rows is None else ref.at[d, pl.ds(self.rows[0], self.rows[1])]

        def copy(a, r):
            px, py, pc = peer_of(r)
            return pltpu.make_async_remote_copy(
                src_ref=part(ins[a], 4 * px + 2 * py + pc), dst_ref=part(outs[a], me),
                send_sem=send_sems.at[a, r - 1], recv_sem=recv_sems.at[a, r - 1],
                device_id=(px, py, pc), device_id_type=MESH)

        def arrival(a, r):
            px, py, pc = peer_of(r)
            slot = part(outs[a], 4 * px + 2 * py + pc)
            return pltpu.make_async_remote_copy(
                src_ref=slot, dst_ref=slot, send_sem=send_sems.at[a, r - 1], recv_sem=recv_sems.at[a, r - 1],
                device_id=(px, py, pc), device_id_type=MESH)

        def local(a):
            return pltpu.make_async_copy(part(ins[a], me), part(outs[a], me), local_sems.at[a])

        if p == 0:
            for a in range(n):
                local(a).start()
            for a in range(n):
                for r in range(1, N_DEV):
                    copy(a, r).start()
        else:
            for a in range(n):
                for r in range(1, N_DEV):
                    arrival(a, r).wait_recv()
            for a in range(n):
                for r in range(1, N_DEV):
                    copy(a, r).wait_send()
                local(a).wait()


def _run_alone(comm, name):
    n = len(comm.arrs)

    def body(*refs):
        for p in range(comm.n_phases):
            comm.phase(p, refs[:n], refs[n:2 * n], refs[2 * n:])

    return pl.pallas_call(body, name=name, out_shape=comm.out_shapes, in_specs=[ANY] * n, out_specs=[ANY] * n,
                          scratch_shapes=comm.scratch)(*comm.arrs)


def _call(body, *, name, grid, in_specs, out_specs, out_shape, operands, scratch_shapes=(), comm=None):
    scratch_shapes = list(scratch_shapes)
    params = pltpu.CompilerParams(dimension_semantics=("arbitrary",) * len(grid), vmem_limit_bytes=VMEM_LIMIT_BYTES)
    if comm is None:
        res = pl.pallas_call(body, name=name, grid=grid, in_specs=in_specs, out_specs=out_specs, out_shape=out_shape,
                             scratch_shapes=scratch_shapes, compiler_params=params)(*operands)
        return res, None
    n_in, n_out, n_sc = len(in_specs), len(out_specs), len(scratch_shapes)
    nc = len(comm.arrs)
    into = getattr(comm, "into", [])
    total = math.prod(grid)
    at = {0: 0, comm.n_phases - 1: total - 1}
    for p in range(1, comm.n_phases - 1):
        at[p] = (total * 7) // 8

    def wrapped(*refs):
        ins, cins = refs[:n_in], refs[n_in:n_in + nc]
        refs = refs[n_in + nc + len(into):]
        outs, couts = refs[:n_out], refs[n_out:n_out + nc]
        rest = refs[n_out + nc:]
        scratch, sems = rest[:n_sc], rest[n_sc:]
        step = 0
        for axis, size in enumerate(grid):
            step = step * size + pl.program_id(axis)
        for p in range(comm.n_phases - 1):
            @pl.when(step == at[p])
            def _(p=p):
                comm.phase(p, cins, couts, sems)
        body(*ins, *outs, *scratch)

        @pl.when(step == total - 1)
        def _():
            comm.phase(comm.n_phases - 1, cins, couts, sems)

    res = pl.pallas_call(
        wrapped, name=name, grid=grid, in_specs=list(in_specs) + [ANY] * (nc + len(into)),
        out_specs=list(out_specs) + [ANY] * nc, out_shape=list(out_shape) + comm.out_shapes,
        scratch_shapes=scratch_shapes + comm.scratch,
        input_output_aliases={n_in + nc + k: n_out + k for k in range(len(into))},
        compiler_params=params)(*operands, *comm.arrs, *into)
    return res[:n_out], res[n_out:]


def _all_reduce_small(vec, name):
    n = vec.shape[1]

    def body(v_ref, o_ref, buf, send_sems, recv_sems):
        x, y, c = _my_position()
        me = 4 * x + 2 * y + c

        def peer_of(r):
            return (x ^ ((r >> 2) & 1), y ^ ((r >> 1) & 1), c ^ (r & 1))

        def copy(r):
            px, py, pc = peer_of(r)
            return pltpu.make_async_remote_copy(
                src_ref=v_ref, dst_ref=buf.at[me], send_sem=send_sems.at[r - 1], recv_sem=recv_sems.at[r - 1],
                device_id=(px, py, pc), device_id_type=MESH)

        def arrival(r):
            px, py, pc = peer_of(r)
            slot = buf.at[4 * px + 2 * py + pc]
            return pltpu.make_async_remote_copy(
                src_ref=slot, dst_ref=slot, send_sem=send_sems.at[r - 1], recv_sem=recv_sems.at[r - 1],
                device_id=(px, py, pc), device_id_type=MESH)

        sends = [copy(r) for r in range(1, N_DEV)]
        for cp in sends:
            cp.start()
        buf[me] = v_ref[...]
        for r in range(1, N_DEV):
            arrival(r).wait_recv()
        for cp in sends:
            cp.wait_send()
        total = buf[0]
        for d in range(1, N_DEV):
            total = total + buf[d]
        o_ref[...] = total

    vmem = pl.BlockSpec(memory_space=pltpu.VMEM)
    return pl.pallas_call(
        body, name=name, out_shape=jax.ShapeDtypeStruct((1, n), F32), in_specs=[vmem], out_specs=vmem,
        scratch_shapes=[pltpu.VMEM((N_DEV, 1, n), F32), pltpu.SemaphoreType.DMA((7,)), pltpu.SemaphoreType.DMA((7,))],
    )(vec)


def _mm_core(pairs, dims, out_dtype, *, name, grid, out_shape, out_spec, acc_shape, alpha=1.0, res=None, comm=None):
    nk = grid[2]
    npairs = len(pairs)

    def body(*refs):
        ab = refs[:2 * npairs]
        rest = refs[2 * npairs:]
        res_ref = rest[0] if res is not None else None
        o_ref = rest[1] if res is not None else rest[0]
        acc_ref = rest[-1] if nk > 1 else None
        part = None
        for p in range(npairs):
            a_ref, b_ref = ab[2 * p], ab[2 * p + 1]
            shards = [(a_ref[s], b_ref[s]) for s in range(a_ref.shape[0])] if len(a_ref.shape) == 3 else [(a_ref[...], b_ref[...])]
            for av, bv in shards:
                d = _dot(av.astype(BF16), bv.astype(BF16), dims)
                part = d if part is None else part + d

        def finish(total):
            val = total * alpha if alpha != 1.0 else total
            if res_ref is not None:
                val = res_ref[...] + val
            o_ref[...] = val.astype(out_dtype)

        if nk == 1:
            finish(part)
        else:
            kk = pl.program_id(2)

            @pl.when(kk == 0)
            def _():
                acc_ref[...] = part

            @pl.when(kk > 0)
            def _():
                acc_ref[...] += part

            @pl.when(kk == nk - 1)
            def _():
                finish(acc_ref[...])

    operands, in_specs = [], []
    for (a, a_spec), (b, b_spec) in pairs:
        operands += [a, b]
        in_specs += [a_spec, b_spec]
    if res is not None:
        operands.append(res[0])
        in_specs.append(res[1])
    out, got = _call(body, name=name, grid=grid, in_specs=in_specs, out_specs=[out_spec],
                     out_shape=[jax.ShapeDtypeStruct(out_shape, out_dtype)], operands=operands,
                     scratch_shapes=[pltpu.VMEM(acc_shape, F32)] if nk > 1 else [], comm=comm)
    return out[0] if comm is None else (out[0], got)


def _mm(pairs, mode, out_dtype, *, name, alpha=1.0, res=None, comm=None, whole_k=False):
    a0, b0 = pairs[0]
    if mode == "nn":
        (m, k), n = a0.shape, b0.shape[1]
    elif mode == "nt":
        (m, k), n = a0.shape, b0.shape[0]
    else:
        (k, m), n = a0.shape, b0.shape[1]
    dims = {"nn": NN_DIMS, "nt": NT_DIMS, "tn": TN_DIMS}[mode]
    tm = _tile(m, 528 if whole_k else 1056, LANES if mode == "tn" else 16)
    tn = _tile(n, 512 if whole_k else 1024, LANES)
    tk = k if whole_k else _tile(k, 2048 if mode != "tn" else 2112, LANES if mode != "tn" else 16)
    a_spec = pl.BlockSpec((tk, tm), lambda i, j, kk: (kk, i)) if mode == "tn" else pl.BlockSpec((tm, tk), lambda i, j, kk: (i, kk))
    b_spec = pl.BlockSpec((tn, tk), lambda i, j, kk: (j, kk)) if mode == "nt" else pl.BlockSpec((tk, tn), lambda i, j, kk: (kk, j))
    o_spec = pl.BlockSpec((tm, tn), lambda i, j, kk: (i, j))
    return _mm_core([((a, a_spec), (b, b_spec)) for a, b in pairs], dims, out_dtype, name=name,
                    grid=(m // tm, n // tn, k // tk), out_shape=(m, n), out_spec=o_spec, acc_shape=(tm, tn),
                    alpha=alpha, res=None if res is None else (res, o_spec), comm=comm)


def _rms_fwd(h, gain, name):
    m, d = h.shape
    tm = _tile(m, 528, 16)

    def body(h_ref, g_ref, o_ref):
        hv = h_ref[...]
        r = lax.rsqrt(jnp.mean(hv * hv, axis=-1, keepdims=True) + EPS)
        o_ref[...] = (hv * r * g_ref[...]).astype(BF16)

    row = pl.BlockSpec((tm, d), lambda i: (i, 0))
    out, _ = _call(body, name=name, grid=(m // tm,), in_specs=[row, pl.BlockSpec((1, d), lambda i: (0, 0))],
                   out_specs=[row], out_shape=[jax.ShapeDtypeStruct((m, d), BF16)], operands=[h, gain])
    return out[0]


def _rms_bwd(dxn, h, gain, dres, name):
    m, d = h.shape
    tm = _tile(m, 264, 16)

    def body(dxn_ref, h_ref, g_ref, dres_ref, dh_ref, dhb_ref, dg_ref):
        hv = h_ref[...]
        r = lax.rsqrt(jnp.mean(hv * hv, axis=-1, keepdims=True) + EPS)
        xhat = hv * r
        dxn_v = dxn_ref[...]
        t = dxn_v * g_ref[...]
        dh = dres_ref[...] + r * (t - xhat * jnp.mean(t * xhat, axis=-1, keepdims=True))
        dh_ref[...] = dh
        dhb_ref[...] = dh.astype(BF16)
        part = jnp.sum(dxn_v * xhat, axis=0, keepdims=True)

        @pl.when(pl.program_id(0) == 0)
        def _():
            dg_ref[...] = part

        @pl.when(pl.program_id(0) > 0)
        def _():
            dg_ref[...] += part

    row = pl.BlockSpec((tm, d), lambda i: (i, 0))
    vec = pl.BlockSpec((1, d), lambda i: (0, 0))
    out, _ = _call(body, name=name, grid=(m // tm,), in_specs=[row, row, vec, row], out_specs=[row, row, vec],
                   out_shape=[jax.ShapeDtypeStruct((m, d), F32), jax.ShapeDtypeStruct((m, d), BF16),
                              jax.ShapeDtypeStruct((1, d), F32)],
                   operands=[dxn, h, gain, dres])
    return out


def _loss_head(h, gain, target, x_off, name):
    m, d = h.shape
    tm = ROW_BLOCK
    first = x_off // tm

    def body(h_ref, g_ref, t_ref, dh_ref, dhb_ref, dg_ref, loss_ref):
        i = pl.program_id(0)

        @pl.when(i == 0)
        def _():
            dg_ref[...] = jnp.zeros_like(dg_ref)
            loss_ref[...] = jnp.zeros_like(loss_ref)

        @pl.when(i < first)
        def _():
            dh_ref[...] = jnp.zeros_like(dh_ref)
            dhb_ref[...] = jnp.zeros_like(dhb_ref)

        @pl.when(i >= first)
        def _():
            hv = h_ref[...]
            g = g_ref[...]
            r = lax.rsqrt(jnp.mean(hv * hv, axis=-1, keepdims=True) + EPS)
            xhat = hv * r
            err = xhat * g - t_ref[...]
            loss_ref[...] += 0.5 * jnp.sum(jnp.mean(err * err, axis=-1, keepdims=True))
            dy = err * (1.0 / d)
            t = dy * g
            dh = r * (t - xhat * jnp.mean(t * xhat, axis=-1, keepdims=True))
            dh_ref[...] = dh
            dhb_ref[...] = dh.astype(BF16)
            dg_ref[...] += jnp.sum(dy * xhat, axis=0, keepdims=True)

    row = pl.BlockSpec((tm, d), lambda i: (i, 0))
    vec = pl.BlockSpec((1, d), lambda i: (0, 0))
    out, _ = _call(body, name=name, grid=(m // tm,),
                   in_specs=[row, vec, pl.BlockSpec((tm, d), lambda i: (jnp.maximum(i - first, 0), 0))],
                   out_specs=[row, row, vec, pl.BlockSpec((8, LANES), lambda i: (0, 0))],
                   out_shape=[jax.ShapeDtypeStruct((m, d), F32), jax.ShapeDtypeStruct((m, d), BF16),
                              jax.ShapeDtypeStruct((1, d), F32), jax.ShapeDtypeStruct((8, LANES), F32)],
                   operands=[h, gain, target])
    return out


def _sigmoid(z):
    return 1.0 / (1.0 + jnp.exp(-z))


def _ffn_up(xn, wg, wu, name, comm=None):
    m, d = xn.shape
    nsh, c, _ = wg.shape
    tm = _tile(m, 1056, 16)

    def body(x_ref, wg_ref, wu_ref, g_ref, u_ref, a_ref):
        xv = x_ref[...]
        g = _dot(xv, wg_ref[...], NT_DIMS)
        u = _dot(xv, wu_ref[...], NT_DIMS)
        g_ref[...] = g.astype(BF16)
        u_ref[...] = u.astype(BF16)
        a_ref[...] = (g * _sigmoid(g) * u).astype(BF16)

    out = pl.BlockSpec((None, tm, c), lambda i, j: (j, i, 0))
    w = pl.BlockSpec((None, c, d), lambda i, j: (j, 0, 0))
    return _call(body, name=name, grid=(m // tm, nsh), in_specs=[pl.BlockSpec((tm, d), lambda i, j: (i, 0)), w, w],
                 out_specs=[out, out, out], out_shape=[jax.ShapeDtypeStruct((nsh, m, c), BF16)] * 3,
                 operands=[xn, wg, wu], comm=comm)


def _ffn_gate(xn, wg, name, comm=None):
    m, d = xn.shape
    nsh, c, _ = wg.shape
    tm = _tile(m, 1056, 16)

    def body(x_ref, wg_ref, g_ref):
        g_ref[...] = _dot(x_ref[...], wg_ref[...], NT_DIMS).astype(BF16)

    out, got = _call(body, name=name, grid=(m // tm, nsh),
                     in_specs=[pl.BlockSpec((tm, d), lambda i, j: (i, 0)), pl.BlockSpec((None, c, d), lambda i, j: (j, 0, 0))],
                     out_specs=[pl.BlockSpec((None, tm, c), lambda i, j: (j, i, 0))],
                     out_shape=[jax.ShapeDtypeStruct((nsh, m, c), BF16)], operands=[xn, wg], comm=comm)
    return out[0], got


def _ffn_up_given_gate(xn, wu, g, name, comm=None):
    m, d = xn.shape
    nsh, c, _ = wu.shape
    tm = _tile(m, 1056, 16)

    def body(x_ref, wu_ref, g_ref, u_ref, a_ref):
        u = _dot(x_ref[...], wu_ref[...], NT_DIMS)
        g = g_ref[...].astype(F32)
        u_ref[...] = u.astype(BF16)
        a_ref[...] = (g * _sigmoid(g) * u).astype(BF16)

    blk = pl.BlockSpec((None, tm, c), lambda i, j: (j, i, 0))
    return _call(body, name=name, grid=(m // tm, nsh),
                 in_specs=[pl.BlockSpec((tm, d), lambda i, j: (i, 0)), pl.BlockSpec((None, c, d), lambda i, j: (j, 0, 0)), blk],
                 out_specs=[blk, blk], out_shape=[jax.ShapeDtypeStruct((nsh, m, c), BF16)] * 2,
                 operands=[xn, wu, g], comm=comm)


def _ffn_contract(pairs, out_dtype, name, *, group, alpha=1.0, res=None, comm=None):
    nsh, m, c = pairs[0][0].shape
    d = pairs[0][1].shape[2]
    tm, tn = _tile(m, 1056, 16), _tile(d, 1024 if group * len(pairs) <= (4 if res is None else 2) else 512, LANES)
    a_spec = pl.BlockSpec((group, tm, c), lambda i, j, kk: (kk, i, 0))
    b_spec = pl.BlockSpec((group, c, tn), lambda i, j, kk: (kk, 0, j))
    o_spec = pl.BlockSpec((tm, tn), lambda i, j, kk: (i, j))
    return _mm_core([((a, a_spec), (b, b_spec)) for a, b in pairs], NN_DIMS, out_dtype, name=name,
                    grid=(m // tm, d // tn, nsh // group), out_shape=(m, d), out_spec=o_spec, acc_shape=(tm, tn),
                    alpha=alpha, res=None if res is None else (res, o_spec), comm=comm)


def _ffn_bwd_act(dh, wd, g, u, name, comm=None):
    m, d = dh.shape
    nsh, c, _ = wd.shape
    tm = _tile(m, 1056, 16)

    def body(dh_ref, wd_ref, g_ref, u_ref, dg_ref, du_ref):
        dact = 0.5 * _dot(dh_ref[...], wd_ref[...], NT_DIMS)
        gv = g_ref[...].astype(F32)
        uv = u_ref[...].astype(F32)
        sig = _sigmoid(gv)
        du_ref[...] = (dact * gv * sig).astype(BF16)
        dg_ref[...] = (dact * uv * sig * (1.0 + gv * (1.0 - sig))).astype(BF16)

    blk = pl.BlockSpec((None, tm, c), lambda i, j: (j, i, 0))
    return _call(body, name=name, grid=(m // tm, nsh),
                 in_specs=[pl.BlockSpec((tm, d), lambda i, j: (i, 0)), pl.BlockSpec((None, c, d), lambda i, j: (j, 0, 0)), blk, blk],
                 out_specs=[blk, blk], out_shape=[jax.ShapeDtypeStruct((nsh, m, c), BF16)] * 2, operands=[dh, wd, g, u],
                 comm=comm)


def _ffn_dw(z, x, name, alpha=1.0, comm=None):
    nsh, m, c = z.shape
    d = x.shape[1]
    tn, tk = _tile(d, 2048, LANES), _tile(m, 2112, 16)
    return _mm_core([((z, pl.BlockSpec((None, tk, c), lambda i, j, kk: (i, kk, 0))),
                      (x, pl.BlockSpec((tk, tn), lambda i, j, kk: (kk, j))))],
                    TN_DIMS, BF16, name=name, grid=(nsh, d // tn, m // tk), out_shape=(nsh, c, d),
                    out_spec=pl.BlockSpec((None, c, tn), lambda i, j, kk: (i, 0, j)), acc_shape=(c, tn), alpha=alpha,
                    comm=comm)


def _dot3(tri, x):
    h1 = x.astype(BF16)
    r1 = x - h1.astype(F32)
    h2 = r1.astype(BF16)
    h3 = (r1 - h2.astype(F32)).astype(BF16)
    return _dot(tri, h1) + _dot(tri, h2) + _dot(tri, h3)


def _log_sigmoid(z):
    return jnp.minimum(z, 0.0) - jnp.log(1.0 + jnp.exp(-jnp.abs(z)))


def _triangle(t, cmp):
    return cmp(lax.broadcasted_iota(jnp.int32, (t, t), 0), lax.broadcasted_iota(jnp.int32, (t, t), 1)).astype(BF16)


def _forget_cumsum(fl, bias, n_heads, pad, name):
    m = fl.shape[0]
    nb = m // ROW_BLOCK

    def body(fl_ref, b_ref, c_ref):
        tri = _triangle(ROW_BLOCK, lambda r, c: r >= c)
        lane_ok = lax.broadcasted_iota(jnp.int32, (ROW_BLOCK, LANES), 1) < n_heads
        rows = lax.broadcasted_iota(jnp.int32, (ROW_BLOCK, LANES), 0)

        def step(b, carry):
            off = pl.multiple_of(b * ROW_BLOCK, ROW_BLOCK)
            lf = _log_sigmoid(fl_ref[pl.ds(off, ROW_BLOCK), :] + b_ref[...])
            lf = jnp.where(lane_ok & (rows + off >= pad), lf, 0.0)
            cs = _dot3(tri, lf) + carry
            c_ref[pl.ds(off, ROW_BLOCK), :] = cs
            return cs[ROW_BLOCK - 1:ROW_BLOCK, :]

        lax.fori_loop(0, nb, step, jnp.zeros((1, LANES), F32))

    vmem = pl.BlockSpec(memory_space=pltpu.VMEM)
    return pl.pallas_call(
        body, name=name, out_shape=jax.ShapeDtypeStruct((m, LANES), F32), in_specs=[vmem, vmem], out_specs=vmem,
        compiler_params=pltpu.CompilerParams(vmem_limit_bytes=VMEM_LIMIT_BYTES),
    )(fl, bias)


def _forget_cumsum_bwd(dc_a, dc_b, fl, bias, n_heads, pad, name):
    m = fl.shape[0]
    nb = m // ROW_BLOCK

    def body(da_ref, db_ref, fl_ref, b_ref, dfl_ref, dbias_ref):
        tri = _triangle(ROW_BLOCK, lambda r, c: r <= c)
        lane_ok = lax.broadcasted_iota(jnp.int32, (ROW_BLOCK, LANES), 1) < n_heads
        rows = lax.broadcasted_iota(jnp.int32, (ROW_BLOCK, LANES), 0)

        def step(bb, carry):
            tail, dbias = carry
            off = pl.multiple_of((nb - 1 - bb) * ROW_BLOCK, ROW_BLOCK)
            dc = da_ref[pl.ds(off, ROW_BLOCK), :] + db_ref[pl.ds(off, ROW_BLOCK), :]
            dlf = _dot3(tri, dc) + tail
            z = fl_ref[pl.ds(off, ROW_BLOCK), :] + b_ref[...]
            dfl = jnp.where(lane_ok & (rows + off >= pad), dlf * _sigmoid(-z), 0.0)
            dfl_ref[pl.ds(off, ROW_BLOCK), :] = dfl
            return dlf[0:1, :], dbias + jnp.sum(dfl, axis=0, keepdims=True)

        zero = jnp.zeros((1, LANES), F32)
        _, dbias = lax.fori_loop(0, nb, step, (zero, zero))
        dbias_ref[...] = dbias

    vmem = pl.BlockSpec(memory_space=pltpu.VMEM)
    return pl.pallas_call(
        body, name=name,
        out_shape=[jax.ShapeDtypeStruct((m, LANES), F32), jax.ShapeDtypeStruct((1, LANES), F32)],
        in_specs=[vmem] * 4, out_specs=[vmem, vmem],
        compiler_params=pltpu.CompilerParams(vmem_limit_bytes=VMEM_LIMIT_BYTES),
    )(dc_a, dc_b, fl, bias)


def _attn_block(m):
    return 3 * ROW_BLOCK if m % (3 * ROW_BLOCK) == 0 else ROW_BLOCK


def _block(ref, j, t):
    return ref[pl.ds(pl.multiple_of(j * t, t), t), :]


def _head_norm(o, gain):
    r = lax.rsqrt(jnp.mean(o * o, axis=-1, keepdims=True) + EPS)
    return o * r * gain


def _head_norm_bwd(o, d_on, gain):
    r = lax.rsqrt(jnp.mean(o * o, axis=-1, keepdims=True) + EPS)
    ohat = o * r
    t = d_on * gain
    d_o = r * (t - ohat * jnp.mean(t * ohat, axis=-1, keepdims=True))
    return d_o, jnp.sum(d_on * ohat, axis=0, keepdims=True)


def _qkv_specs(t, m, h, first_col_block):
    q = pl.BlockSpec((t, HEAD_DIM), lambda hd, i: (i, first_col_block + hd))
    k = pl.BlockSpec((m, HEAD_DIM), lambda hd, i: (0, first_col_block + h + hd))
    v = pl.BlockSpec((m, HEAD_DIM), lambda hd, i: (0, first_col_block + 2 * h + hd))
    return q, k, v


def _fox_fwd(qkv, ccol, crow, gain, n_heads, pad, name, comm=None):
    m = qkv.shape[0]
    t = _attn_block(m)
    nq = m // t
    scale = HEAD_DIM ** -0.5
    hw = n_heads * HEAD_DIM

    def body(q_ref, k_ref, v_ref, ccol_ref, crow_ref, g_ref, o_ref, on_ref, lse_ref):
        i = pl.program_id(1)
        q = q_ref[...]
        ci = ccol_ref[...]
        qpos = i * t + lax.broadcasted_iota(jnp.int32, (t, 1), 0)

        def step(j, carry, masked):
            mx, l, acc = carry
            s = _dot(q, _block(k_ref, j, t), NT_DIMS) * scale + ci - crow_ref[j]
            if masked:
                kpos = j * t + lax.broadcasted_iota(jnp.int32, (1, t), 1)
                s = jnp.where((kpos <= qpos) & (kpos >= pad), s, NEG)
            mx_new = jnp.maximum(mx, jnp.max(s, axis=-1, keepdims=True))
            p = jnp.exp(s - mx_new)
            a = jnp.exp(mx - mx_new)
            return (mx_new, a * l + jnp.sum(p, axis=-1, keepdims=True),
                    a * acc + _dot(p.astype(BF16), _block(v_ref, j, t)))

        carry = step(0, (jnp.full((t, 1), NEG, F32), jnp.zeros((t, 1), F32), jnp.zeros((t, HEAD_DIM), F32)), True)
        n_mid = jnp.maximum(i - 1, 0)
        carry = lax.fori_loop(0, n_mid // 2, lambda jp, c: step(2 * jp + 2, step(2 * jp + 1, c, False), False), carry)
        carry = lax.fori_loop(0, n_mid % 2, lambda _, c: step(i - 1, c, False), carry)
        mx, l, acc = lax.fori_loop(0, jnp.minimum(i, 1), lambda _, c: step(i, c, True), carry)
        valid = qpos >= pad
        o = jnp.where(valid, acc / l, 0.0)
        o_ref[...] = o
        on_ref[...] = _head_norm(o, g_ref[...]).astype(BF16)
        lse_ref[...] = jnp.where(valid, mx + jnp.log(l), 0.0)

    q_spec, k_spec, v_spec = _qkv_specs(t, m, n_heads, 0)
    col = pl.BlockSpec((None, t, 1), lambda hd, i: (hd, i, 0))
    head = pl.BlockSpec((t, HEAD_DIM), lambda hd, i: (i, hd))
    return _call(body, name=name, grid=(n_heads, nq),
                 in_specs=[q_spec, k_spec, v_spec, col, pl.BlockSpec((None, nq, 1, t), lambda hd, i: (hd, 0, 0, 0)),
                           pl.BlockSpec((1, HEAD_DIM), lambda hd, i: (0, hd))],
                 out_specs=[head, head, col],
                 out_shape=[jax.ShapeDtypeStruct((m, hw), F32), jax.ShapeDtypeStruct((m, hw), BF16),
                            jax.ShapeDtypeStruct((n_heads, m, 1), F32)],
                 operands=[qkv, qkv, qkv, ccol, crow, gain], comm=comm)


def _fox_bwd(qkv, o, d_on, gain, lse, ccol, crow, n_heads, pad, name, comm=None):
    m = qkv.shape[0]
    t = _attn_block(m)
    nq = m // t
    scale = HEAD_DIM ** -0.5
    hw = n_heads * HEAD_DIM

    def body(q_ref, k_ref, v_ref, o_ref, don_ref, g_ref, lse_ref, ccol_ref, crow_ref,
             dq_ref, dk_ref, dv_ref, dg_ref, dccol_ref, dcrow_ref):
        i = pl.program_id(1)

        @pl.when(i == 0)
        def _():
            dk_ref[...] = jnp.zeros_like(dk_ref)
            dv_ref[...] = jnp.zeros_like(dv_ref)
            dg_ref[...] = jnp.zeros_like(dg_ref)
            dcrow_ref[...] = jnp.zeros_like(dcrow_ref)

        q = q_ref[...]
        o = o_ref[...]
        d_o, dgain = _head_norm_bwd(o, don_ref[...], g_ref[...])
        dg_ref[...] += dgain
        delta = jnp.sum(d_o * o, axis=-1, keepdims=True)
        d_ob = d_o.astype(BF16)
        ci = ccol_ref[...]
        lse_i = lse_ref[...]
        qpos = i * t + lax.broadcasted_iota(jnp.int32, (t, 1), 0)

        def step(j, carry, masked):
            dq, dci = carry
            k = _block(k_ref, j, t)
            v = _block(v_ref, j, t)
            off = pl.multiple_of(j * t, t)
            s = _dot(q, k, NT_DIMS) * scale + ci - crow_ref[j]
            if masked:
                kpos = off + lax.broadcasted_iota(jnp.int32, (1, t), 1)
                ok = (kpos <= qpos) & (kpos >= pad)
                p = jnp.where(ok, jnp.exp(jnp.where(ok, s - lse_i, 0.0)), 0.0)
            else:
                p = jnp.exp(s - lse_i)
            ds = p * (_dot(d_ob, v, NT_DIMS) - delta)
            dsb = ds.astype(BF16)
            dk_ref[pl.ds(off, t), :] += _dot(dsb, q, TN_DIMS) * scale
            dv_ref[pl.ds(off, t), :] += _dot(p.astype(BF16), d_ob, TN_DIMS)
            dcrow_ref[j] -= jnp.sum(ds, axis=0, keepdims=True)
            return dq + _dot(dsb, k), dci + jnp.sum(ds, axis=-1, keepdims=True)

        carry = step(0, (jnp.zeros((t, HEAD_DIM), F32), jnp.zeros((t, 1), F32)), True)
        n_mid = jnp.maximum(i - 1, 0)
        carry = lax.fori_loop(0, n_mid // 2, lambda jp, c: step(2 * jp + 2, step(2 * jp + 1, c, False), False), carry)
        carry = lax.fori_loop(0, n_mid % 2, lambda _, c: step(i - 1, c, False), carry)
        dq, dci = lax.fori_loop(0, jnp.minimum(i, 1), lambda _, c: step(i, c, True), carry)
        dq_ref[...] = (dq * scale).astype(BF16)
        dccol_ref[...] = dci

    q_spec, k_spec, v_spec = _qkv_specs(t, m, n_heads, 0)
    col = pl.BlockSpec((None, t, 1), lambda hd, i: (hd, i, 0))
    rowc = pl.BlockSpec((None, nq, 1, t), lambda hd, i: (hd, 0, 0, 0))
    head = pl.BlockSpec((t, HEAD_DIM), lambda hd, i: (i, hd))
    whole = pl.BlockSpec((m, HEAD_DIM), lambda hd, i: (0, hd))
    gvec = pl.BlockSpec((1, HEAD_DIM), lambda hd, i: (0, hd))
    return _call(body, name=name, grid=(n_heads, nq),
                 in_specs=[q_spec, k_spec, v_spec, head, head, gvec, col, col, rowc],
                 out_specs=[head, whole, whole, gvec, col, rowc],
                 out_shape=[jax.ShapeDtypeStruct((m, hw), BF16), jax.ShapeDtypeStruct((m, hw), F32),
                            jax.ShapeDtypeStruct((m, hw), F32), jax.ShapeDtypeStruct((1, hw), F32),
                            jax.ShapeDtypeStruct((n_heads, m, 1), F32), jax.ShapeDtypeStruct((n_heads, nq, 1, t), F32)],
                 operands=[qkv, qkv, qkv, o, d_on, gain, lse, ccol, crow], comm=comm)


def _sb_scores(z):
    lp = jnp.log(1.0 + jnp.exp(-jnp.abs(z)))
    return jnp.minimum(z, 0.0) - lp, jnp.minimum(-z, 0.0) - lp


def _sb_fwd(qkv, gain, n_heads, pad, name, comm=None):
    m = qkv.shape[0]
    t = _attn_block(m)
    nq = m // t
    assert nq <= LANES
    scale = HEAD_DIM ** -0.5
    hw = n_heads * HEAD_DIM

    def body(q_ref, k_ref, v_ref, g_ref, after_ref, o_ref, on_ref, run_ref):
        i = pl.program_id(1)
        q = q_ref[...]
        qpos = i * t + lax.broadcasted_iota(jnp.int32, (t, 1), 0)
        after = after_ref[...]
        lane = lax.broadcasted_iota(jnp.int32, (t, LANES), 1)

        def step(j, carry, masked):
            run, acc = carry
            ls_pos, log_1m = _sb_scores(_dot(q, _block(k_ref, j, t), NT_DIMS) * scale)
            if masked:
                kpos = j * t + lax.broadcasted_iota(jnp.int32, (1, t), 1)
                ok = (kpos < qpos) & (kpos >= pad)
                log_1m = jnp.where(ok, log_1m, 0.0)
            a = jnp.exp(ls_pos + _dot_split(log_1m, after) + run)
            if masked:
                a = jnp.where(ok, a, 0.0)
            run_ref[...] = jnp.where(lane == j, run, run_ref[...])
            return run + jnp.sum(log_1m, axis=-1, keepdims=True), acc + _dot(a.astype(BF16), _block(v_ref, j, t))

        run_ref[...] = jnp.zeros_like(run_ref)
        carry = step(i, (jnp.zeros((t, 1), F32), jnp.zeros((t, HEAD_DIM), F32)), True)
        n_mid = jnp.maximum(i - 1, 0)
        carry = lax.fori_loop(0, n_mid // 2, lambda jp, c: step(i - 2 * jp - 2, step(i - 2 * jp - 1, c, False), False), carry)
        carry = lax.fori_loop(0, n_mid % 2, lambda _, c: step(1, c, False), carry)
        _, o = lax.fori_loop(0, jnp.minimum(i, 1), lambda _, c: step(0, c, True), carry)
        o_ref[...] = o
        on_ref[...] = _head_norm(o, g_ref[...]).astype(BF16)

    q_spec, k_spec, v_spec = _qkv_specs(t, m, n_heads, 3 * n_heads)
    head = pl.BlockSpec((t, HEAD_DIM), lambda hd, i: (i, hd))
    return _call(body, name=name, grid=(n_heads, nq),
                 in_specs=[q_spec, k_spec, v_spec, pl.BlockSpec((1, HEAD_DIM), lambda hd, i: (0, hd)),
                           pl.BlockSpec((t, t), lambda hd, i: (0, 0))],
                 out_specs=[head, head, pl.BlockSpec((None, t, LANES), lambda hd, i: (hd, i, 0))],
                 out_shape=[jax.ShapeDtypeStruct((m, hw), F32), jax.ShapeDtypeStruct((m, hw), BF16),
                            jax.ShapeDtypeStruct((n_heads, m, LANES), F32)],
                 operands=[qkv, qkv, qkv, gain, _triangle(t, lambda r, c: r > c)], comm=comm)


def _sb_bwd(qkv, o, d_on, gain, runs, n_heads, pad, name, comm=None):
    m = qkv.shape[0]
    t = _attn_block(m)
    nq = m // t
    scale = HEAD_DIM ** -0.5
    hw = n_heads * HEAD_DIM

    def body(q_ref, k_ref, v_ref, o_ref, don_ref, g_ref, run_ref, after_ref, before_ref, dq_ref, dk_ref, dv_ref, dg_ref):
        i = pl.program_id(1)

        @pl.when(i == 0)
        def _():
            dk_ref[...] = jnp.zeros_like(dk_ref)
            dv_ref[...] = jnp.zeros_like(dv_ref)
            dg_ref[...] = jnp.zeros_like(dg_ref)

        q = q_ref[...]
        d_o, dgain = _head_norm_bwd(o_ref[...], don_ref[...], g_ref[...])
        dg_ref[...] += dgain
        d_ob = d_o.astype(BF16)
        runs_i = run_ref[...]
        qpos = i * t + lax.broadcasted_iota(jnp.int32, (t, 1), 0)
        after = after_ref[...]
        before = before_ref[...]
        lane = lax.broadcasted_iota(jnp.int32, (t, LANES), 1)

        def step(j, carry, masked):
            g_run, dq = carry
            k = _block(k_ref, j, t)
            v = _block(v_ref, j, t)
            off = pl.multiple_of(j * t, t)
            ls_pos, ls_neg = _sb_scores(_dot(q, k, NT_DIMS) * scale)
            log_1m = ls_neg
            if masked:
                kpos = off + lax.broadcasted_iota(jnp.int32, (1, t), 1)
                ok = (kpos < qpos) & (kpos >= pad)
                log_1m = jnp.where(ok, ls_neg, 0.0)
            run = jnp.sum(jnp.where(lane == j, runs_i, 0.0), axis=-1, keepdims=True)
            a = jnp.exp(ls_pos + _dot_split(log_1m, after) + run)
            if masked:
                a = jnp.where(ok, a, 0.0)
            g = a * _dot(d_ob, v, NT_DIMS)
            prefix = _dot(g.astype(BF16), before) + g_run
            dz = g * jnp.exp(ls_neg) - jnp.exp(ls_pos) * prefix
            if masked:
                dz = jnp.where(ok, dz, 0.0)
            dzb = dz.astype(BF16)
            dk_ref[pl.ds(off, t), :] += _dot(dzb, q, TN_DIMS) * scale
            dv_ref[pl.ds(off, t), :] += _dot(a.astype(BF16), d_ob, TN_DIMS)
            return g_run + jnp.sum(g, axis=-1, keepdims=True), dq + _dot(dzb, k)

        carry = step(0, (jnp.zeros((t, 1), F32), jnp.zeros((t, HEAD_DIM), F32)), True)
        n_mid = jnp.maximum(i - 1, 0)
        carry = lax.fori_loop(0, n_mid // 2, lambda jp, c: step(2 * jp + 2, step(2 * jp + 1, c, False), False), carry)
        carry = lax.fori_loop(0, n_mid % 2, lambda _, c: step(i - 1, c, False), carry)
        _, dq = lax.fori_loop(0, jnp.minimum(i, 1), lambda _, c: step(i, c, True), carry)
        dq_ref[...] = (dq * scale).astype(BF16)

    q_spec, k_spec, v_spec = _qkv_specs(t, m, n_heads, 3 * n_heads)
    head = pl.BlockSpec((t, HEAD_DIM), lambda hd, i: (i, hd))
    whole = pl.BlockSpec((m, HEAD_DIM), lambda hd, i: (0, hd))
    gvec = pl.BlockSpec((1, HEAD_DIM), lambda hd, i: (0, hd))
    tri = pl.BlockSpec((t, t), lambda hd, i: (0, 0))
    return _call(body, name=name, grid=(n_heads, nq),
                 in_specs=[q_spec, k_spec, v_spec, head, head, gvec, pl.BlockSpec((None, t, LANES), lambda hd, i: (hd, i, 0)),
                           tri, tri],
                 out_specs=[head, whole, whole, gvec],
                 out_shape=[jax.ShapeDtypeStruct((m, hw), BF16), jax.ShapeDtypeStruct((m, hw), F32),
                            jax.ShapeDtypeStruct((m, hw), F32), jax.ShapeDtypeStruct((1, hw), F32)],
                 operands=[qkv, qkv, qkv, o, d_on, gain, runs, _triangle(t, lambda r, c: r > c), _triangle(t, lambda r, c: r < c)],
                 comm=comm)


def _adamw(parts, w, m1, v2, name, comm=None):
    nl, r, c = w.shape
    assert len(parts) == nl
    n_parts = parts[0].shape[0]
    block_elems = 256 * 1024
    if r % 8 == 0 or c % LANES != 0:
        tr, tc = _tile(r, max(8, block_elems // (-(-c // LANES) * LANES)), 8), c
    else:
        tr, tc = r, _tile(c, max(LANES, block_elems // r // LANES * LANES), LANES)
    nr, nc = r // tr, c // tc
    bias1 = 1.0 / (1.0 - ADAM_B1 ** ADAM_STEP)
    bias2 = 1.0 / (1.0 - ADAM_B2 ** ADAM_STEP)

    def body(*refs):
        p_refs = refs[:nl]
        w_ref, m_ref, v_ref, g_ref, d_ref, nm_ref, nv_ref = refs[nl:]

        def update(p_ref):
            g = p_ref[0].astype(F32)
            for s in range(1, n_parts):
                g = g + p_ref[s].astype(F32)
            m_new = ADAM_B1 * m_ref[...] + (1.0 - ADAM_B1) * g
            v_new = ADAM_B2 * v_ref[...] + (1.0 - ADAM_B2) * (g * g)
            g_ref[...] = g
            nm_ref[...] = m_new
            nv_ref[...] = v_new
            d_ref[...] = -ADAM_LR * ((m_new * bias1) / (jnp.sqrt(v_new * bias2) + ADAM_EPS) + ADAM_WD * w_ref[...])

        for ll in range(nl):
            @pl.when(pl.program_id(0) == ll)
            def _(ll=ll):
                update(p_refs[ll])

    def part_spec(ll):
        def index(l, i, j):
            pin = jnp.where(l < ll, 0, 1)
            return 0, jnp.where(l == ll, i, pin * (nr - 1)), jnp.where(l == ll, j, pin * (nc - 1))
        return pl.BlockSpec((n_parts, tr, tc), index)

    blk = pl.BlockSpec((None, tr, tc), lambda l, i, j: (l, i, j))
    out, got = _call(body, name=name, grid=(nl, nr, nc), in_specs=[part_spec(ll) for ll in range(nl)] + [blk, blk, blk],
                     out_specs=[blk] * 4, out_shape=[jax.ShapeDtypeStruct((nl, r, c), F32)] * 4,
                     operands=list(parts) + [w, m1, v2], comm=comm)
    return out if comm is None else (out, got)


SMALL_WEIGHTS = ("ffn1_norm", "mix_norm", "b_forget", "g_fox", "g_sb", "ffn2_norm", "final_norm")
WEIGHT_ORDER = ("meta_tokens", "ffn1_norm", "ffn1_w_gate", "ffn1_w_up", "ffn1_w_down", "mix_norm", "w_in", "b_forget",
                "g_fox", "g_sb", "w_out", "ffn2_norm", "ffn2_w_gate", "ffn2_w_up", "ffn2_w_down", "final_norm")
GROUPS = {"ffn1": ("ffn1_w_gate", "ffn1_w_up", "ffn1_w_down"), "mix": ("w_in", "w_out"),
          "ffn2": ("ffn2_w_gate", "ffn2_w_up", "ffn2_w_down")}
TRANSPOSED = ("ffn1_w_gate", "ffn1_w_up", "ffn2_w_gate", "ffn2_w_up")


def _pad_lanes(a):
    extra = (-a.shape[-1]) % LANES
    return a if extra == 0 else jnp.pad(a, [(0, 0)] * (a.ndim - 1) + [(0, extra)])


def _ffn_backward(dh_b, dh, saved, gain, wg, wu, wd, tag, carried=()):
    h, xn, g, u, act = saved

    def half(k, which, into=()):
        if len(carried) <= k:
            return None
        rows = carried[k].shape[1]
        cut = rows // 2 // 16 * 16
        return _Exchange([carried[k]], rows=(0, cut) if which == 0 else (cut, rows - cut), into=into)

    (dg, du), got_b = _ffn_bwd_act(dh_b, wd, g, u, f"{tag}_bwd_act", half(1, 0))
    d_wd = _ffn_dw(act, dh_b, f"{tag}_dwd", alpha=0.5, comm=half(1, 1, got_b))
    d_wg = _ffn_dw(dg, xn, f"{tag}_dwg", comm=half(2, 0))
    got_c = d_wg[1] if len(carried) > 2 else ()
    d_wu = _ffn_dw(du, xn, f"{tag}_dwu", comm=half(2, 1, got_c))
    dxn = _ffn_contract([(dg, wg), (du, wu)], F32, f"{tag}_dxn", group=2, comm=_Exchange(carried[:1]) if carried else None)
    got = []
    if carried:
        dxn, got = dxn[0], list(dxn[1])
    if len(carried) > 1:
        d_wd, got_b = d_wd
        got += got_b
    if len(carried) > 2:
        d_wg, (d_wu, got_c) = d_wg[0], d_wu
        got += got_c
    dh_in, dh_in_b, d_gain = _rms_bwd(dxn, h, gain, dh, f"{tag}_norm_bwd")
    return dh_in, dh_in_b, d_gain, d_wg, d_wu, d_wd, got


def _ffn_backward_last(dh_b, dh, saved, gain, wg, wu, wd, tag, first, second):
    h, xn, g, u, act = saved
    (dg, du), got_first = _ffn_bwd_act(dh_b, wd, g, u, f"{tag}_bwd_act", first)
    d_wd, got_second = _ffn_dw(act, dh_b, f"{tag}_dwd", alpha=0.5, comm=second)
    c = d_wd.shape[1]
    half = c // 2 // 16 * 16
    d_wg, got_wd = _ffn_dw(dg, xn, f"{tag}_dwg", comm=_Exchange([d_wd], rows=(0, half)))
    d_wu, got_wd = _ffn_dw(du, xn, f"{tag}_dwu", comm=_Exchange([d_wd], rows=(half, c - half), into=got_wd))
    dxn, got_wg = _ffn_contract([(dg, wg), (du, wu)], F32, f"{tag}_dxn", group=2, comm=_Exchange([d_wg]))
    dh_in, dh_in_b, d_gain = _rms_bwd(dxn, h, gain, dh, f"{tag}_norm_bwd")
    return dh_in, dh_in_b, d_gain, got_first, got_second, got_wg[0], got_wd[0], d_wu


def _mixer_forward(h, gain, w_in_t, bias, g_fox, g_sb, w_out, n_heads, pad, tag, comm_fox=None, comm_sb=None):
    m = h.shape[0]
    t = _attn_block(m)
    hw = n_heads * HEAD_DIM
    xn = _rms_fwd(h, gain, f"{tag}_norm")
    qkv = _mm([(xn, w_in_t[:6 * hw])], "nt", BF16, name=f"{tag}_qkv")
    fl = _mm([(xn, w_in_t[6 * hw:])], "nt", F32, name=f"{tag}_forget")
    c = _forget_cumsum(fl, bias, n_heads, pad, f"{tag}_cumsum")
    c_heads = c[:, :n_heads].T
    ccol = c_heads[:, :, None]
    crow = c_heads.reshape(n_heads, m // t, 1, t)
    (o_f, on_f, lse), got_fox = _fox_fwd(qkv, ccol, crow, g_fox, n_heads, pad, f"{tag}_fox", comm_fox)
    (o_s, on_s, runs), got_sb = _sb_fwd(qkv, g_sb, n_heads, pad, f"{tag}_sb", comm_sb)
    if w_out is None:
        w_out = got_sb[-1].reshape(-1, h.shape[1])
    h_out = _mm([(on_f, w_out[:hw]), (on_s, w_out[hw:])], "nn", F32, name=f"{tag}_out", res=h)
    return h_out, (h, xn, qkv, fl, ccol, crow, o_f, on_f, lse, o_s, on_s, runs), got_fox, got_sb, w_out


def _mixer_backward(dh_b, dh_out, saved, gain, w_in_t, bias, g_fox, g_sb, w_out, n_heads, pad, tag, comm_fox=None, comm_sb=None,
                    rows_in_dxn=0):
    h, xn, qkv, fl, ccol, crow, o_f, on_f, lse, o_s, on_s, runs = saved
    m = h.shape[0]
    hw = n_heads * HEAD_DIM
    d_on_f = _mm([(dh_b, w_out[:hw])], "nt", F32, name=f"{tag}_don_f")
    d_on_s = _mm([(dh_b, w_out[hw:])], "nt", F32, name=f"{tag}_don_s")
    d_wout = jnp.concatenate([_mm([(on_f, dh_b)], "tn", BF16, name=f"{tag}_dwout_f"),
                              _mm([(on_s, dh_b)], "tn", BF16, name=f"{tag}_dwout_s")], axis=0)
    (dq_f, dk_f, dv_f, dg_fox, dccol, dcrow), got_fox = _fox_bwd(
        qkv, o_f, d_on_f, g_fox, lse, ccol, crow, n_heads, pad, f"{tag}_fox_bwd", comm_fox)
    (dq_s, dk_s, dv_s, dg_sb), got_sb = _sb_bwd(qkv, o_s, d_on_s, g_sb, runs, n_heads, pad, f"{tag}_sb_bwd", comm_sb)
    dc_a = _pad_lanes(dccol[:, :, 0].T)
    dc_b = _pad_lanes(dcrow.reshape(n_heads, m).T)
    dfl, dbias = _forget_cumsum_bwd(dc_a, dc_b, fl, bias, n_heads, pad, f"{tag}_cumsum_bwd")
    dproj = jnp.concatenate([dq_f, dk_f.astype(BF16), dv_f.astype(BF16), dq_s, dk_s.astype(BF16), dv_s.astype(BF16),
                             dfl.astype(BF16)], axis=1)
    d_win = _mm([(dproj, xn)], "tn", BF16, name=f"{tag}_dwin")[:6 * hw + n_heads].reshape(N_DEV, -1, h.shape[1])
    got_win = None
    if rows_in_dxn:
        dxn, got_win = _mm([(dproj, w_in_t)], "nn", F32, name=f"{tag}_dxn", whole_k=True,
                           comm=_Exchange([d_win], rows=(0, rows_in_dxn)))
    else:
        dxn = _mm([(dproj, w_in_t)], "nn", F32, name=f"{tag}_dxn", whole_k=True)
    dh, dh_in_b, d_gain = _rms_bwd(dxn, h, gain, dh_out, f"{tag}_norm_bwd")
    return dh, dh_in_b, d_gain, d_win, dbias, dg_fox, dg_sb, d_wout, got_fox, got_sb, got_win


def kernel(x, meta_tokens, ffn1_norm, ffn1_w_gate, ffn1_w_up, ffn1_w_down, mix_norm, w_in, b_forget, g_fox, g_sb, w_out, ffn2_norm, ffn2_w_gate, ffn2_w_up, ffn2_w_down, final_norm, loss_target, m_meta_tokens, m_ffn1_norm, m_ffn1_w_gate, m_ffn1_w_up, m_ffn1_w_down, m_mix_norm, m_w_in, m_b_forget, m_g_fox, m_g_sb, m_w_out, m_ffn2_norm, m_ffn2_w_gate, m_ffn2_w_up, m_ffn2_w_down, m_final_norm, v_meta_tokens, v_ffn1_norm, v_ffn1_w_gate, v_ffn1_w_up, v_ffn1_w_down, v_mix_norm, v_w_in, v_b_forget, v_g_fox, v_g_sb, v_w_out, v_ffn2_norm, v_ffn2_w_gate, v_ffn2_w_up, v_ffn2_w_down, v_final_norm):
    given = dict(locals())
    seq, d = x.shape[1], x.shape[2]
    depth = ffn1_norm.shape[0]
    d_in = N_DEV * w_in.shape[2]
    n_heads = g_fox.shape[1] // HEAD_DIM
    hw = n_heads * HEAD_DIM
    assert seq % ROW_BLOCK == 0 and d_in == 6 * hw + n_heads and n_heads <= LANES
    pad = (-(seq + N_META)) % ROW_BLOCK
    x_off = pad + N_META

    def view(n, a):
        if n in TRANSPOSED:
            return jnp.swapaxes(a, 1, 2)
        return a.transpose(2, 0, 1) if n == "w_in" else a

    def unview(n, a):
        if n in TRANSPOSED:
            return jnp.swapaxes(a, 1, 2)
        return a.transpose(1, 2, 0) if n == "w_in" else a

    def shard(n, l):
        v = view(n, given[n])
        return (v[:, l] if n == "w_in" else v[l]).astype(BF16)

    def shards(group, l):
        return [shard(n, l) for n in GROUPS[group]]

    sh = shards("ffn1", 0)
    first = _run_alone(_Gather([sh[0], meta_tokens]), "gather_first")
    full = {}
    meta_full = first[1].transpose(1, 0, 2).reshape(N_META, d)
    h = jnp.concatenate([jnp.zeros((pad, d), F32), meta_full, x[0]], axis=0)
    weights, saved = [], []
    for l in range(depth):
        xn = _rms_fwd(h, ffn1_norm[l:l + 1], "ffn1_norm")
        if l == 0:
            mix0 = shards("mix", 0)
            g, (wu1,) = _ffn_gate(xn, first[0], "ffn1_gate", _Gather([sh[1]]))
            (u, act), (wd1,) = _ffn_up_given_gate(xn, wu1, g, "ffn1_up", _Gather([sh[2]]))
            full[("ffn1", 0)] = (first[0], wu1, wd1)
            h_out, (win3,) = _ffn_contract([(act, wd1)], F32, "ffn1_down", group=2, alpha=0.5, res=h, comm=_Gather(mix0[:1]))
        else:
            wg1, wu1, wd1 = full[("ffn1", l)]
            (g, u, act), full[("mix", l)] = _ffn_up(xn, wg1, wu1, "ffn1_up", _Gather(shards("mix", l)))
            h_out = _ffn_contract([(act, wd1)], F32, "ffn1_down", group=2, alpha=0.5, res=h)
        s1 = (h, xn, g, u, act)
        h = h_out
        if l > 0:
            win3, wout3 = full[("mix", l)]
        w_in_t = jnp.pad(win3.reshape(d_in, d), ((0, 6 * hw + LANES - d_in), (0, 0)))
        bias = _pad_lanes(b_forget[l:l + 1])
        nxt = shards("ffn1", l + 1) if l + 1 < depth else []
        ffn2 = shards("ffn2", l)
        h, sm, (wg2, wu2), got, w_out_full = _mixer_forward(
            h, mix_norm[l:l + 1], w_in_t, bias, g_fox[l:l + 1], g_sb[l:l + 1],
            None if l == 0 else wout3.reshape(N_DEV * wout3.shape[1], d), n_heads, pad, "mix",
            _Gather(ffn2[:2]), _Gather(nxt + ffn2[2:] + (mix0[1:] if l == 0 else [])))
        if nxt:
            full[("ffn1", l + 1)] = got[:3]
        wd2 = got[len(nxt)]
        full[("ffn2", l)] = (wg2, wu2, wd2)
        xn = _rms_fwd(h, ffn2_norm[l:l + 1], "ffn2_norm")
        (g, u, act), _ = _ffn_up(xn, wg2, wu2, "ffn2_up")
        s2 = (h, xn, g, u, act)
        h = _ffn_contract([(act, wd2)], F32, "ffn2_down", group=2, alpha=0.5, res=h)
        weights.append((w_in_t, w_out_full, bias))
        saved.append((s1, sm, s2))

    dh, dh_b, d_final, loss_arr = _loss_head(h, final_norm[None, :], loss_target[0], x_off, "loss_head")
    small = {n: [None] * depth for n in SMALL_WEIGHTS[:-1]}
    partial, received = {}, {}

    def names(group, l):
        return [(n, l) for n in GROUPS[group]]

    def send(keys):
        return [partial[k] for k in keys]

    for l in reversed(range(depth)):
        w_in_t, w_out_full, bias = weights[l]
        s1, sm, s2 = saved[l]
        wg2, wu2, wd2 = full[("ffn2", l)]
        up = l + 1 < depth
        in_dxn = names("ffn1", l + 1) if up else []
        in_fox = names("mix", l + 1) if up else []
        in_sb = names("ffn2", l)
        dh, dh_b, small["ffn2_norm"][l], partial[("ffn2_w_gate", l)], partial[("ffn2_w_up", l)], partial[("ffn2_w_down", l)], got = (
            _ffn_backward(dh_b, dh, s2, ffn2_norm[l:l + 1], wg2, wu2, wd2, "ffn2", send(in_dxn)))
        received.update(zip(in_dxn, got))
        win_rows = (d_in // N_DEV // 2 // 16 * 16) if l == 0 else 0
        (dh, dh_b, small["mix_norm"][l], partial[("w_in", l)], dbias, small["g_fox"][l], small["g_sb"][l], d_wout,
         got_fox, got_sb, got_win) = _mixer_backward(
            dh_b, dh, sm, mix_norm[l:l + 1], w_in_t, bias, g_fox[l:l + 1], g_sb[l:l + 1], w_out_full, n_heads, pad, "mix",
            _Exchange(send(in_fox)) if in_fox else None, _Exchange(send(in_sb)), win_rows)
        received.update(zip(in_fox, got_fox or []))
        received.update(zip(in_sb, got_sb))
        partial[("w_out", l)] = d_wout.reshape(N_DEV, -1, d)
        small["b_forget"][l] = dbias[:, :n_heads]
        wg1, wu1, wd1 = full[("ffn1", l)]
        if l > 0:
            dh, dh_b, small["ffn1_norm"][l], partial[("ffn1_w_gate", l)], partial[("ffn1_w_up", l)], partial[("ffn1_w_down", l)], _ = (
                _ffn_backward(dh_b, dh, s1, ffn1_norm[l:l + 1], wg1, wu1, wd1, "ffn1"))
        else:
            rest = (win_rows, d_in // N_DEV - win_rows)
            (dh, dh_b, small["ffn1_norm"][0], got_a, got_b, received[("ffn1_w_gate", 0)], received[("ffn1_w_down", 0)],
             last_grad) = _ffn_backward_last(
                dh_b, dh, s1, ffn1_norm[0:1], wg1, wu1, wd1, "ffn1", _Exchange([partial[("w_out", 0)]]),
                _Exchange([partial[("w_in", 0)]], rows=rest, into=got_win))
            received[("w_out", 0)], received[("w_in", 0)] = got_a[0], got_b[0]
    grad_x = dh[x_off:][None]
    d_meta = dh[pad:x_off].reshape(N_META, N_DEV, -1).transpose(1, 0, 2)
    received[("meta_tokens", 0)] = _run_alone(_Exchange([d_meta]), "exchange_meta")[0]

    vec = [loss_arr[0:1, :]] + [_pad_lanes(jnp.concatenate(small[n], axis=0).reshape(1, -1)) for n in SMALL_WEIGHTS[:-1]]
    vec.append(d_final)
    sizes = [a.shape[1] for a in vec]
    summed = _all_reduce_small(jnp.concatenate(vec, axis=1), "reduce_small")
    loss = summed[0, 0]

    def packed(prefix):
        cols = [jnp.zeros((1, LANES), F32)]
        cols += [_pad_lanes(given[prefix + n].reshape(1, -1)) for n in SMALL_WEIGHTS]
        return jnp.concatenate(cols, axis=1)[None]

    small_out = _adamw([summed[None]], packed(""), packed("m_"), packed("v_"), "adamw_small")

    c_last = last_grad.shape[1]
    quarter = c_last // 4 // 16 * 16
    pieces = [(k * quarter, quarter if k < 3 else c_last - 3 * quarter) for k in range(4)]
    order = ["ffn2_w_gate", "ffn2_w_up", "ffn2_w_down", "ffn1_w_gate", "ffn1_w_down", "w_in", "w_out", "ffn1_w_up"]
    out, arrived = {}, []
    for k, n in enumerate(order):
        wv, mv, vv = (view(n, given[p + n]) for p in ("", "m_", "v_"))
        if n == "ffn1_w_up":
            received[(n, 0)] = arrived[0]
        if n == "w_in":
            per_layer = [_adamw([received[(n, l)]], wv[:, l][None], mv[:, l][None], vv[:, l][None], "adamw_" + n)
                         for l in range(depth)]
            res = [jnp.stack([per_layer[l][k][0] for l in range(depth)], axis=1) for k in range(4)]
        elif k < len(pieces):
            res, arrived = _adamw([received[(n, l)] for l in range(depth)], wv, mv, vv, "adamw_" + n,
                                  _Exchange([last_grad], rows=pieces[k], into=arrived))
        else:
            res = _adamw([received[(n, l)] for l in range(depth)], wv, mv, vv, "adamw_" + n)
        out[n] = [unview(n, r) for r in res]
    out["meta_tokens"] = [r[0] for r in _adamw([received[("meta_tokens", 0)]], meta_tokens[None], m_meta_tokens[None],
                                               v_meta_tokens[None], "adamw_meta_tokens")]
    offset = sizes[0]
    for n, size in zip(SMALL_WEIGHTS, sizes[1:]):
        shape, count = given[n].shape, given[n].size
        out[n] = [r[0, 0, offset:offset + count].reshape(shape) for r in small_out]
        offset += size

    result = [loss, grad_x]
    for k in range(4):
        result += [out[n][k] for n in WEIGHT_ORDER]
    return tuple(result)
```

```python
import math

import jax
import jax.numpy as jnp
from jax import lax
from jax.experimental import pallas as pl
from jax.experimental.pallas import tpu as pltpu

F32 = jnp.float32
BF16 = jnp.bfloat16

N_DEV = 8
N_META = 16
HEAD_DIM = 128
ROW_BLOCK = 128
LANES = 128
EPS = 1e-6
NEG = -1e30
ADAM_LR = 0.001
ADAM_B1 = 0.9
ADAM_B2 = 0.999
ADAM_EPS = 1e-08
ADAM_WD = 0.01
ADAM_STEP = 10
VMEM_LIMIT_BYTES = 56 * 1024 * 1024
MESH = pl.DeviceIdType.MESH

NT_DIMS = (((1,), (1,)), ((), ()))
TN_DIMS = (((0,), (0,)), ((), ()))
NN_DIMS = (((1,), (0,)), ((), ()))
ANY = pl.BlockSpec(memory_space=pl.ANY)


def _tile(n, cap, align):
    best = None
    for d in range(align, min(n, cap) + 1, align):
        if n % d == 0:
            best = d
    return best if best is not None else n


def _dot(a, b, dims=NN_DIMS):
    return lax.dot_general(a, b, dims, preferred_element_type=F32)


def _dot_split(x, u):
    hi = x.astype(BF16)
    lo = (x - hi.astype(F32)).astype(BF16)
    return _dot(hi, u) + _dot(lo, u)


def _my_position():
    return lax.axis_index("x"), lax.axis_index("y"), lax.axis_index("c")


class _Gather:
    n_phases = 3

    def __init__(self, arrs):
        self.arrs = list(arrs)
        n = len(self.arrs)
        self.out_shapes = [jax.ShapeDtypeStruct((N_DEV,) + a.shape, a.dtype) for a in self.arrs]
        self.scratch = [pltpu.SemaphoreType.DMA((n, 7)), pltpu.SemaphoreType.DMA((n, 7)), pltpu.SemaphoreType.DMA((n,))]

    def phase(self, p, ins, outs, sems):
        send_sems, recv_sems, local_sems = sems
        n = len(self.arrs)
        x, y, c = _my_position()
        me, sibling = (x, y, c), (x, y, 1 - c)
        chips = [(1 - x, y), (x, 1 - y), (1 - x, 1 - y)]

        def copy(a, k, block, to, src=None):
            slot = outs[a].at[4 * block[0] + 2 * block[1] + block[2]]
            return pltpu.make_async_remote_copy(
                src_ref=slot if src is None else src, dst_ref=slot,
                send_sem=send_sems.at[a, k], recv_sem=recv_sems.at[a, k], device_id=to, device_id_type=MESH)

        def local(a):
            return pltpu.make_async_copy(ins[a], outs[a].at[4 * x + 2 * y + c], local_sems.at[a])

        def first(a):
            return [copy(a, 0, me, sibling, src=ins[a])] + [copy(a, 1 + j, me, (*chip, c), src=ins[a]) for j, chip in enumerate(chips)]

        def passed(a, j):
            return copy(a, 4 + j, (*chips[j], c), sibling)

        if p == 0:
            for a in range(n):
                local(a).start()
            for a in range(n):
                for cp in first(a):
                    cp.start()
        elif p == 1:
            for a in range(n):
                for j, chip in enumerate(chips):
                    copy(a, 1 + j, (*chip, c), me).wait_recv()
                    passed(a, j).start()
        else:
            for a in range(n):
                copy(a, 0, sibling, me).wait_recv()
                for j, chip in enumerate(chips):
                    copy(a, 4 + j, (*chip, 1 - c), me).wait_recv()
            for a in range(n):
                for cp in first(a) + [passed(a, j) for j in range(3)]:
                    cp.wait_send()
                local(a).wait()


class _Exchange:
    n_phases = 2

    def __init__(self, arrs, rows=None, into=()):
        self.arrs = list(arrs)
        self.rows = rows
        self.into = list(into)
        n = len(self.arrs)
        self.out_shapes = [jax.ShapeDtypeStruct(a.shape, a.dtype) for a in self.arrs]
        self.scratch = [pltpu.SemaphoreType.DMA((n, 7)), pltpu.SemaphoreType.DMA((n, 7)), pltpu.SemaphoreType.DMA((n,))]

    def phase(self, p, ins, outs, sems):
        send_sems, recv_sems, local_sems = sems
        n = len(self.arrs)
        x, y, c = _my_position()
        me = 4 * x + 2 * y + c

        def peer_of(r):
            return (x ^ ((r >> 2) & 1), y ^ ((r >> 1) & 1), c ^ (r & 1))

        def part(ref, d):
            return ref.at[d] if self.rows is None else ref.at[d, pl.ds(self.rows[0], self.rows[1])]

        def copy(a, r):
            px, py, pc = peer_of(r)
            return pltpu.make_async_remote_copy(
                src_ref=part(ins[a], 4 * px + 2 * py + pc), dst_ref=part(outs[a], me),
                send_sem=send_sems.at[a, r - 1], recv_sem=recv_sems.at[a, r - 1],
                device_id=(px, py, pc), device_id_type=MESH)

        def arrival(a, r):
            px, py, pc = peer_of(r)
            slot = part(outs[a], 4 * px + 2 * py + pc)
            return pltpu.make_async_remote_copy(
                src_ref=slot, dst_ref=slot, send_sem=send_sems.at[a, r - 1], recv_sem=recv_sems.at[a, r - 1],
                device_id=(px, py, pc), device_id_type=MESH)

        def local(a):
            return pltpu.make_async_copy(part(ins[a], me), part(outs[a], me), local_sems.at[a])

        if p == 0:
            for a in range(n):
                local(a).start()
            for a in range(n):
                for r in range(1, N_DEV):
                    copy(a, r).start()
        else:
            for a in range(n):
                for r in range(1, N_DEV):
                    arrival(a, r).wait_recv()
            for a in range(n):
                for r in range(1, N_DEV):
                    copy(a, r).wait_send()
                local(a).wait()


def _run_alone(comm, name):
    n = len(comm.arrs)

    def body(*refs):
        for p in range(comm.n_phases):
            comm.phase(p, refs[:n], refs[n:2 * n], refs[2 * n:])

    return pl.pallas_call(body, name=name, out_shape=comm.out_shapes, in_specs=[ANY] * n, out_specs=[ANY] * n,
                          scratch_shapes=comm.scratch)(*comm.arrs)


def _call(body, *, name, grid, in_specs, out_specs, out_shape, operands, scratch_shapes=(), comm=None):
    scratch_shapes = list(scratch_shapes)
    params = pltpu.CompilerParams(dimension_semantics=("arbitrary",) * len(grid), vmem_limit_bytes=VMEM_LIMIT_BYTES)
    if comm is None:
        res = pl.pallas_call(body, name=name, grid=grid, in_specs=in_specs, out_specs=out_specs, out_shape=out_shape,
                             scratch_shapes=scratch_shapes, compiler_params=params)(*operands)
        return res, None
    n_in, n_out, n_sc = len(in_specs), len(out_specs), len(scratch_shapes)
    nc = len(comm.arrs)
    into = getattr(comm, "into", [])
    total = math.prod(grid)
    at = {0: 0, comm.n_phases - 1: total - 1}
    for p in range(1, comm.n_phases - 1):
        at[p] = (total * 7) // 8

    def wrapped(*refs):
        ins, cins = refs[:n_in], refs[n_in:n_in + nc]
        refs = refs[n_in + nc + len(into):]
        outs, couts = refs[:n_out], refs[n_out:n_out + nc]
        rest = refs[n_out + nc:]
        scratch, sems = rest[:n_sc], rest[n_sc:]
        step = 0
        for axis, size in enumerate(grid):
            step = step * size + pl.program_id(axis)
        for p in range(comm.n_phases - 1):
            @pl.when(step == at[p])
            def _(p=p):
                comm.phase(p, cins, couts, sems)
        body(*ins, *outs, *scratch)

        @pl.when(step == total - 1)
        def _():
            comm.phase(comm.n_phases - 1, cins, couts, sems)

    res = pl.pallas_call(
        wrapped, name=name, grid=grid, in_specs=list(in_specs) + [ANY] * (nc + len(into)),
        out_specs=list(out_specs) + [ANY] * nc, out_shape=list(out_shape) + comm.out_shapes,
        scratch_shapes=scratch_shapes + comm.scratch,
        input_output_aliases={n_in + nc + k: n_out + k for k in range(len(into))},
        compiler_params=params)(*operands, *comm.arrs, *into)
    return res[:n_out], res[n_out:]


def _all_reduce_small(vec, name):
    n = vec.shape[1]

    def body(v_ref, o_ref, buf, send_sems, recv_sems):
        x, y, c = _my_position()
        me = 4 * x + 2 * y + c

        def peer_of(r):
            return (x ^ ((r >> 2) & 1), y ^ ((r >> 1) & 1), c ^ (r & 1))

        def copy(r):
            px, py, pc = peer_of(r)
            return pltpu.make_async_remote_copy(
                src_ref=v_ref, dst_ref=buf.at[me], send_sem=send_sems.at[r - 1], recv_sem=recv_sems.at[r - 1],
                device_id=(px, py, pc), device_id_type=MESH)

        def arrival(r):
            px, py, pc = peer_of(r)
            slot = buf.at[4 * px + 2 * py + pc]
            return pltpu.make_async_remote_copy(
                src_ref=slot, dst_ref=slot, send_sem=send_sems.at[r - 1], recv_sem=recv_sems.at[r - 1],
                device_id=(px, py, pc), device_id_type=MESH)

        sends = [copy(r) for r in range(1, N_DEV)]
        for cp in sends:
            cp.start()
        buf[me] = v_ref[...]
        for r in range(1, N_DEV):
            arrival(r).wait_recv()
        for cp in sends:
            cp.wait_send()
        total = buf[0]
        for d in range(1, N_DEV):
            total = total + buf[d]
        o_ref[...] = total

    vmem = pl.BlockSpec(memory_space=pltpu.VMEM)
    return pl.pallas_call(
        body, name=name, out_shape=jax.ShapeDtypeStruct((1, n), F32), in_specs=[vmem], out_specs=vmem,
        scratch_shapes=[pltpu.VMEM((N_DEV, 1, n), F32), pltpu.SemaphoreType.DMA((7,)), pltpu.SemaphoreType.DMA((7,))],
    )(vec)


def _mm_core(pairs, dims, out_dtype, *, name, grid, out_shape, out_spec, acc_shape, alpha=1.0, res=None, comm=None):
    nk = grid[2]
    npairs = len(pairs)

    def body(*refs):
        ab = refs[:2 * npairs]
        rest = refs[2 * npairs:]
        res_ref = rest[0] if res is not None else None
        o_ref = rest[1] if res is not None else rest[0]
        acc_ref = rest[-1] if nk > 1 else None
        part = None
        for p in range(npairs):
            a_ref, b_ref = ab[2 * p], ab[2 * p + 1]
            shards = [(a_ref[s], b_ref[s]) for s in range(a_ref.shape[0])] if len(a_ref.shape) == 3 else [(a_ref[...], b_ref[...])]
            for av, bv in shards:
                d = _dot(av.astype(BF16), bv.astype(BF16), dims)
                part = d if part is None else part + d

        def finish(total):
            val = total * alpha if alpha != 1.0 else total
            if res_ref is not None:
                val = res_ref[...] + val
            o_ref[...] = val.astype(out_dtype)

        if nk == 1:
            finish(part)
        else:
            kk = pl.program_id(2)

            @pl.when(kk == 0)
            def _():
                acc_ref[...] = part

            @pl.when(kk > 0)
            def _():
                acc_ref[...] += part

            @pl.when(kk == nk - 1)
            def _():
                finish(acc_ref[...])

    operands, in_specs = [], []
    for (a, a_spec), (b, b_spec) in pairs:
        operands += [a, b]
        in_specs += [a_spec, b_spec]
    if res is not None:
        operands.append(res[0])
        in_specs.append(res[1])
    out, got = _call(body, name=name, grid=grid, in_specs=in_specs, out_specs=[out_spec],
                     out_shape=[jax.ShapeDtypeStruct(out_shape, out_dtype)], operands=operands,
                     scratch_shapes=[pltpu.VMEM(acc_shape, F32)] if nk > 1 else [], comm=comm)
    return out[0] if comm is None else (out[0], got)


def _mm(pairs, mode, out_dtype, *, name, alpha=1.0, res=None, comm=None, whole_k=False):
    a0, b0 = pairs[0]
    if mode == "nn":
        (m, k), n = a0.shape, b0.shape[1]
    elif mode == "nt":
        (m, k), n = a0.shape, b0.shape[0]
    else:
        (k, m), n = a0.shape, b0.shape[1]
    dims = {"nn": NN_DIMS, "nt": NT_DIMS, "tn": TN_DIMS}[mode]
    tm = _tile(m, 528 if whole_k else 1056, LANES if mode == "tn" else 16)
    tn = _tile(n, 512 if whole_k else 1024, LANES)
    tk = k if whole_k else _tile(k, 2048 if mode != "tn" else 2112, LANES if mode != "tn" else 16)
    a_spec = pl.BlockSpec((tk, tm), lambda i, j, kk: (kk, i)) if mode == "tn" else pl.BlockSpec((tm, tk), lambda i, j, kk: (i, kk))
    b_spec = pl.BlockSpec((tn, tk), lambda i, j, kk: (j, kk)) if mode == "nt" else pl.BlockSpec((tk, tn), lambda i, j, kk: (kk, j))
    o_spec = pl.BlockSpec((tm, tn), lambda i, j, kk: (i, j))
    return _mm_core([((a, a_spec), (b, b_spec)) for a, b in pairs], dims, out_dtype, name=name,
                    grid=(m // tm, n // tn, k // tk), out_shape=(m, n), out_spec=o_spec, acc_shape=(tm, tn),
                    alpha=alpha, res=None if res is None else (res, o_spec), comm=comm)


def _rms_fwd(h, gain, name):
    m, d = h.shape
    tm = _tile(m, 528, 16)

    def body(h_ref, g_ref, o_ref):
        hv = h_ref[...]
        r = lax.rsqrt(jnp.mean(hv * hv, axis=-1, keepdims=True) + EPS)
        o_ref[...] = (hv * r * g_ref[...]).astype(BF16)

    row = pl.BlockSpec((tm, d), lambda i: (i, 0))
    out, _ = _call(body, name=name, grid=(m // tm,), in_specs=[row, pl.BlockSpec((1, d), lambda i: (0, 0))],
                   out_specs=[row], out_shape=[jax.ShapeDtypeStruct((m, d), BF16)], operands=[h, gain])
    return out[0]


def _rms_bwd(dxn, h, gain, dres, name):
    m, d = h.shape
    tm = _tile(m, 384, 16)

    def body(dxn_ref, h_ref, g_ref, dres_ref, dh_ref, dhb_ref, dg_ref):
        hv = h_ref[...]
        r = lax.rsqrt(jnp.mean(hv * hv, axis=-1, keepdims=True) + EPS)
        xhat = hv * r
        dxn_v = dxn_ref[...]
        t = dxn_v * g_ref[...]
        dh = dres_ref[...] + r * (t - xhat * jnp.mean(t * xhat, axis=-1, keepdims=True))
        dh_ref[...] = dh
        dhb_ref[...] = dh.astype(BF16)
        part = jnp.sum(dxn_v * xhat, axis=0, keepdims=True)

        @pl.when(pl.program_id(0) == 0)
        def _():
            dg_ref[...] = part

        @pl.when(pl.program_id(0) > 0)
        def _():
            dg_ref[...] += part

    row = pl.BlockSpec((tm, d), lambda i: (i, 0))
    vec = pl.BlockSpec((1, d), lambda i: (0, 0))
    out, _ = _call(body, name=name, grid=(m // tm,), in_specs=[row, row, vec, row], out_specs=[row, row, vec],
                   out_shape=[jax.ShapeDtypeStruct((m, d), F32), jax.ShapeDtypeStruct((m, d), BF16),
                              jax.ShapeDtypeStruct((1, d), F32)],
                   operands=[dxn, h, gain, dres])
    return out


def _loss_head(h, gain, target, x_off, name):
    m, d = h.shape
    tm = ROW_BLOCK
    first = x_off // tm

    def body(h_ref, g_ref, t_ref, dh_ref, dhb_ref, dg_ref, loss_ref):
        i = pl.program_id(0)

        @pl.when(i == 0)
        def _():
            dg_ref[...] = jnp.zeros_like(dg_ref)
            loss_ref[...] = jnp.zeros_like(loss_ref)

        @pl.when(i < first)
        def _():
            dh_ref[...] = jnp.zeros_like(dh_ref)
            dhb_ref[...] = jnp.zeros_like(dhb_ref)

        @pl.when(i >= first)
        def _():
            hv = h_ref[...]
            g = g_ref[...]
            r = lax.rsqrt(jnp.mean(hv * hv, axis=-1, keepdims=True) + EPS)
            xhat = hv * r
            err = xhat * g - t_ref[...]
            loss_ref[...] += 0.5 * jnp.sum(jnp.mean(err * err, axis=-1, keepdims=True))
            dy = err * (1.0 / d)
            t = dy * g
            dh = r * (t - xhat * jnp.mean(t * xhat, axis=-1, keepdims=True))
            dh_ref[...] = dh
            dhb_ref[...] = dh.astype(BF16)
            dg_ref[...] += jnp.sum(dy * xhat, axis=0, keepdims=True)

    row = pl.BlockSpec((tm, d), lambda i: (i, 0))
    vec = pl.BlockSpec((1, d), lambda i: (0, 0))
    out, _ = _call(body, name=name, grid=(m // tm,),
                   in_specs=[row, vec, pl.BlockSpec((tm, d), lambda i: (jnp.maximum(i - first, 0), 0))],
                   out_specs=[row, row, vec, pl.BlockSpec((8, LANES), lambda i: (0, 0))],
                   out_shape=[jax.ShapeDtypeStruct((m, d), F32), jax.ShapeDtypeStruct((m, d), BF16),
                              jax.ShapeDtypeStruct((1, d), F32), jax.ShapeDtypeStruct((8, LANES), F32)],
                   operands=[h, gain, target])
    return out


def _sigmoid(z):
    return 1.0 / (1.0 + jnp.exp(-z))


def _ffn_up(xn, wg, wu, name, comm=None):
    m, d = xn.shape
    nsh, c, _ = wg.shape
    tm = _tile(m, 1056, 16)

    def body(x_ref, wg_ref, wu_ref, g_ref, u_ref, a_ref):
        xv = x_ref[...]
        g = _dot(xv, wg_ref[...], NT_DIMS)
        u = _dot(xv, wu_ref[...], NT_DIMS)
        g_ref[...] = g.astype(BF16)
        u_ref[...] = u.astype(BF16)
        a_ref[...] = (g * _sigmoid(g) * u).astype(BF16)

    out = pl.BlockSpec((None, tm, c), lambda i, j: (j, i, 0))
    w = pl.BlockSpec((None, c, d), lambda i, j: (j, 0, 0))
    return _call(body, name=name, grid=(m // tm, nsh), in_specs=[pl.BlockSpec((tm, d), lambda i, j: (i, 0)), w, w],
                 out_specs=[out, out, out], out_shape=[jax.ShapeDtypeStruct((nsh, m, c), BF16)] * 3,
                 operands=[xn, wg, wu], comm=comm)


def _ffn_gate(xn, wg, name, comm=None):
    m, d = xn.shape
    nsh, c, _ = wg.shape
    tm = _tile(m, 1056, 16)

    def body(x_ref, wg_ref, g_ref):
        g_ref[...] = _dot(x_ref[...], wg_ref[...], NT_DIMS).astype(BF16)

    out, got = _call(body, name=name, grid=(m // tm, nsh),
                     in_specs=[pl.BlockSpec((tm, d), lambda i, j: (i, 0)), pl.BlockSpec((None, c, d), lambda i, j: (j, 0, 0))],
                     out_specs=[pl.BlockSpec((None, tm, c), lambda i, j: (j, i, 0))],
                     out_shape=[jax.ShapeDtypeStruct((nsh, m, c), BF16)], operands=[xn, wg], comm=comm)
    return out[0], got


def _ffn_up_given_gate(xn, wu, g, name, comm=None):
    m, d = xn.shape
    nsh, c, _ = wu.shape
    tm = _tile(m, 1056, 16)

    def body(x_ref, wu_ref, g_ref, u_ref, a_ref):
        u = _dot(x_ref[...], wu_ref[...], NT_DIMS)
        g = g_ref[...].astype(F32)
        u_ref[...] = u.astype(BF16)
        a_ref[...] = (g * _sigmoid(g) * u).astype(BF16)

    blk = pl.BlockSpec((None, tm, c), lambda i, j: (j, i, 0))
    return _call(body, name=name, grid=(m // tm, nsh),
                 in_specs=[pl.BlockSpec((tm, d), lambda i, j: (i, 0)), pl.BlockSpec((None, c, d), lambda i, j: (j, 0, 0)), blk],
                 out_specs=[blk, blk], out_shape=[jax.ShapeDtypeStruct((nsh, m, c), BF16)] * 2,
                 operands=[xn, wu, g], comm=comm)


def _ffn_contract(pairs, out_dtype, name, *, group, alpha=1.0, res=None, comm=None):
    nsh, m, c = pairs[0][0].shape
    d = pairs[0][1].shape[2]
    tm, tn = _tile(m, 1056, 16), _tile(d, 1024 if group * len(pairs) <= (4 if res is None else 2) else 512, LANES)
    a_spec = pl.BlockSpec((group, tm, c), lambda i, j, kk: (kk, i, 0))
    b_spec = pl.BlockSpec((group, c, tn), lambda i, j, kk: (kk, 0, j))
    o_spec = pl.BlockSpec((tm, tn), lambda i, j, kk: (i, j))
    return _mm_core([((a, a_spec), (b, b_spec)) for a, b in pairs], NN_DIMS, out_dtype, name=name,
                    grid=(m // tm, d // tn, nsh // group), out_shape=(m, d), out_spec=o_spec, acc_shape=(tm, tn),
                    alpha=alpha, res=None if res is None else (res, o_spec), comm=comm)


def _ffn_bwd_act(dh, wd, g, u, name, comm=None):
    m, d = dh.shape
    nsh, c, _ = wd.shape
    tm = _tile(m, 1056, 16)

    def body(dh_ref, wd_ref, g_ref, u_ref, dg_ref, du_ref):
        dact = 0.5 * _dot(dh_ref[...], wd_ref[...], NT_DIMS)
        gv = g_ref[...].astype(F32)
        uv = u_ref[...].astype(F32)
        sig = _sigmoid(gv)
        du_ref[...] = (dact * gv * sig).astype(BF16)
        dg_ref[...] = (dact * uv * sig * (1.0 + gv * (1.0 - sig))).astype(BF16)

    blk = pl.BlockSpec((None, tm, c), lambda i, j: (j, i, 0))
    return _call(body, name=name, grid=(m // tm, nsh),
                 in_specs=[pl.BlockSpec((tm, d), lambda i, j: (i, 0)), pl.BlockSpec((None, c, d), lambda i, j: (j, 0, 0)), blk, blk],
                 out_specs=[blk, blk], out_shape=[jax.ShapeDtypeStruct((nsh, m, c), BF16)] * 2, operands=[dh, wd, g, u],
                 comm=comm)


def _ffn_dw(z, x, name, alpha=1.0, comm=None):
    nsh, m, c = z.shape
    d = x.shape[1]
    tn, tk = _tile(d, 2048, LANES), _tile(m, 2112, 16)
    return _mm_core([((z, pl.BlockSpec((None, tk, c), lambda i, j, kk: (i, kk, 0))),
                      (x, pl.BlockSpec((tk, tn), lambda i, j, kk: (kk, j))))],
                    TN_DIMS, BF16, name=name, grid=(nsh, d // tn, m // tk), out_shape=(nsh, c, d),
                    out_spec=pl.BlockSpec((None, c, tn), lambda i, j, kk: (i, 0, j)), acc_shape=(c, tn), alpha=alpha,
                    comm=comm)


def _dot3(tri, x):
    h1 = x.astype(BF16)
    r1 = x - h1.astype(F32)
    h2 = r1.astype(BF16)
    h3 = (r1 - h2.astype(F32)).astype(BF16)
    return _dot(tri, h1) + _dot(tri, h2) + _dot(tri, h3)


def _log_sigmoid(z):
    return jnp.minimum(z, 0.0) - jnp.log(1.0 + jnp.exp(-jnp.abs(z)))


def _triangle(t, cmp):
    return cmp(lax.broadcasted_iota(jnp.int32, (t, t), 0), lax.broadcasted_iota(jnp.int32, (t, t), 1)).astype(BF16)


def _forget_cumsum(fl, bias, n_heads, pad, name):
    m = fl.shape[0]
    nb = m // ROW_BLOCK

    def body(fl_ref, b_ref, c_ref):
        tri = _triangle(ROW_BLOCK, lambda r, c: r >= c)
        lane_ok = lax.broadcasted_iota(jnp.int32, (ROW_BLOCK, LANES), 1) < n_heads
        rows = lax.broadcasted_iota(jnp.int32, (ROW_BLOCK, LANES), 0)

        def step(b, carry):
            off = pl.multiple_of(b * ROW_BLOCK, ROW_BLOCK)
            lf = _log_sigmoid(fl_ref[pl.ds(off, ROW_BLOCK), :] + b_ref[...])
            lf = jnp.where(lane_ok & (rows + off >= pad), lf, 0.0)
            cs = _dot3(tri, lf) + carry
            c_ref[pl.ds(off, ROW_BLOCK), :] = cs
            return cs[ROW_BLOCK - 1:ROW_BLOCK, :]

        lax.fori_loop(0, nb, step, jnp.zeros((1, LANES), F32))

    vmem = pl.BlockSpec(memory_space=pltpu.VMEM)
    return pl.pallas_call(
        body, name=name, out_shape=jax.ShapeDtypeStruct((m, LANES), F32), in_specs=[vmem, vmem], out_specs=vmem,
        compiler_params=pltpu.CompilerParams(vmem_limit_bytes=VMEM_LIMIT_BYTES),
    )(fl, bias)


def _forget_cumsum_bwd(dc_a, dc_b, fl, bias, n_heads, pad, name):
    m = fl.shape[0]
    nb = m // ROW_BLOCK

    def body(da_ref, db_ref, fl_ref, b_ref, dfl_ref, dbias_ref):
        tri = _triangle(ROW_BLOCK, lambda r, c: r <= c)
        lane_ok = lax.broadcasted_iota(jnp.int32, (ROW_BLOCK, LANES), 1) < n_heads
        rows = lax.broadcasted_iota(jnp.int32, (ROW_BLOCK, LANES), 0)

        def step(bb, carry):
            tail, dbias = carry
            off = pl.multiple_of((nb - 1 - bb) * ROW_BLOCK, ROW_BLOCK)
            dc = da_ref[pl.ds(off, ROW_BLOCK), :] + db_ref[pl.ds(off, ROW_BLOCK), :]
            dlf = _dot3(tri, dc) + tail
            z = fl_ref[pl.ds(off, ROW_BLOCK), :] + b_ref[...]
            dfl = jnp.where(lane_ok & (rows + off >= pad), dlf * _sigmoid(-z), 0.0)
            dfl_ref[pl.ds(off, ROW_BLOCK), :] = dfl
            return dlf[0:1, :], dbias + jnp.sum(dfl, axis=0, keepdims=True)

        zero = jnp.zeros((1, LANES), F32)
        _, dbias = lax.fori_loop(0, nb, step, (zero, zero))
        dbias_ref[...] = dbias

    vmem = pl.BlockSpec(memory_space=pltpu.VMEM)
    return pl.pallas_call(
        body, name=name,
        out_shape=[jax.ShapeDtypeStruct((m, LANES), F32), jax.ShapeDtypeStruct((1, LANES), F32)],
        in_specs=[vmem] * 4, out_specs=[vmem, vmem],
        compiler_params=pltpu.CompilerParams(vmem_limit_bytes=VMEM_LIMIT_BYTES),
    )(dc_a, dc_b, fl, bias)


def _attn_block(m):
    return 3 * ROW_BLOCK if m % (3 * ROW_BLOCK) == 0 else ROW_BLOCK


def _block(ref, j, t):
    return ref[pl.ds(pl.multiple_of(j * t, t), t), :]


def _head_norm(o, gain):
    r = lax.rsqrt(jnp.mean(o * o, axis=-1, keepdims=True) + EPS)
    return o * r * gain


def _head_norm_bwd(o, d_on, gain):
    r = lax.rsqrt(jnp.mean(o * o, axis=-1, keepdims=True) + EPS)
    ohat = o * r
    t = d_on * gain
    d_o = r * (t - ohat * jnp.mean(t * ohat, axis=-1, keepdims=True))
    return d_o, jnp.sum(d_on * ohat, axis=0, keepdims=True)


def _qkv_specs(t, m, h, first_col_block):
    q = pl.BlockSpec((t, HEAD_DIM), lambda hd, i: (i, first_col_block + hd))
    k = pl.BlockSpec((m, HEAD_DIM), lambda hd, i: (0, first_col_block + h + hd))
    v = pl.BlockSpec((m, HEAD_DIM), lambda hd, i: (0, first_col_block + 2 * h + hd))
    return q, k, v


def _fox_fwd(qkv, ccol, crow, gain, n_heads, pad, name, comm=None):
    m = qkv.shape[0]
    t = _attn_block(m)
    nq = m // t
    scale = HEAD_DIM ** -0.5
    hw = n_heads * HEAD_DIM

    def body(q_ref, k_ref, v_ref, ccol_ref, crow_ref, g_ref, o_ref, on_ref, lse_ref):
        i = pl.program_id(1)
        q = q_ref[...]
        ci = ccol_ref[...]
        qpos = i * t + lax.broadcasted_iota(jnp.int32, (t, 1), 0)

        def step(j, carry, masked):
            mx, l, acc = carry
            s = _dot(q, _block(k_ref, j, t), NT_DIMS) * scale + ci - crow_ref[j]
            if masked:
                kpos = j * t + lax.broadcasted_iota(jnp.int32, (1, t), 1)
                s = jnp.where((kpos <= qpos) & (kpos >= pad), s, NEG)
            mx_new = jnp.maximum(mx, jnp.max(s, axis=-1, keepdims=True))
            p = jnp.exp(s - mx_new)
            a = jnp.exp(mx - mx_new)
            return (mx_new, a * l + jnp.sum(p, axis=-1, keepdims=True),
                    a * acc + _dot(p.astype(BF16), _block(v_ref, j, t)))

        carry = step(0, (jnp.full((t, 1), NEG, F32), jnp.zeros((t, 1), F32), jnp.zeros((t, HEAD_DIM), F32)), True)
        n_mid = jnp.maximum(i - 1, 0)
        carry = lax.fori_loop(0, n_mid // 2, lambda jp, c: step(2 * jp + 2, step(2 * jp + 1, c, False), False), carry)
        carry = lax.fori_loop(0, n_mid % 2, lambda _, c: step(i - 1, c, False), carry)
        mx, l, acc = lax.fori_loop(0, jnp.minimum(i, 1), lambda _, c: step(i, c, True), carry)
        valid = qpos >= pad
        o = jnp.where(valid, acc / l, 0.0)
        o_ref[...] = o
        on_ref[...] = _head_norm(o, g_ref[...]).astype(BF16)
        lse_ref[...] = jnp.where(valid, mx + jnp.log(l), 0.0)

    q_spec, k_spec, v_spec = _qkv_specs(t, m, n_heads, 0)
    col = pl.BlockSpec((None, t, 1), lambda hd, i: (hd, i, 0))
    head = pl.BlockSpec((t, HEAD_DIM), lambda hd, i: (i, hd))
    return _call(body, name=name, grid=(n_heads, nq),
                 in_specs=[q_spec, k_spec, v_spec, col, pl.BlockSpec((None, nq, 1, t), lambda hd, i: (hd, 0, 0, 0)),
                           pl.BlockSpec((1, HEAD_DIM), lambda hd, i: (0, hd))],
                 out_specs=[head, head, col],
                 out_shape=[jax.ShapeDtypeStruct((m, hw), F32), jax.ShapeDtypeStruct((m, hw), BF16),
                            jax.ShapeDtypeStruct((n_heads, m, 1), F32)],
                 operands=[qkv, qkv, qkv, ccol, crow, gain], comm=comm)


def _fox_bwd(qkv, o, d_on, gain, lse, ccol, crow, n_heads, pad, name, comm=None):
    m = qkv.shape[0]
    t = _attn_block(m)
    nq = m // t
    scale = HEAD_DIM ** -0.5
    hw = n_heads * HEAD_DIM

    def body(q_ref, k_ref, v_ref, o_ref, don_ref, g_ref, lse_ref, ccol_ref, crow_ref,
             dq_ref, dk_ref, dv_ref, dg_ref, dccol_ref, dcrow_ref):
        i = pl.program_id(1)

        @pl.when(i == 0)
        def _():
            dk_ref[...] = jnp.zeros_like(dk_ref)
            dv_ref[...] = jnp.zeros_like(dv_ref)
            dg_ref[...] = jnp.zeros_like(dg_ref)
            dcrow_ref[...] = jnp.zeros_like(dcrow_ref)

        q = q_ref[...]
        o = o_ref[...]
        d_o, dgain = _head_norm_bwd(o, don_ref[...], g_ref[...])
        dg_ref[...] += dgain
        delta = jnp.sum(d_o * o, axis=-1, keepdims=True)
        d_ob = d_o.astype(BF16)
        ci = ccol_ref[...]
        lse_i = lse_ref[...]
        qpos = i * t + lax.broadcasted_iota(jnp.int32, (t, 1), 0)

        def step(j, carry, masked):
            dq, dci = carry
            k = _block(k_ref, j, t)
            v = _block(v_ref, j, t)
            off = pl.multiple_of(j * t, t)
            s = _dot(q, k, NT_DIMS) * scale + ci - crow_ref[j]
            if masked:
                kpos = off + lax.broadcasted_iota(jnp.int32, (1, t), 1)
                ok = (kpos <= qpos) & (kpos >= pad)
                p = jnp.where(ok, jnp.exp(jnp.where(ok, s - lse_i, 0.0)), 0.0)
            else:
                p = jnp.exp(s - lse_i)
            ds = p * (_dot(d_ob, v, NT_DIMS) - delta)
            dsb = ds.astype(BF16)
            dk_ref[pl.ds(off, t), :] += _dot(dsb, q, TN_DIMS) * scale
            dv_ref[pl.ds(off, t), :] += _dot(p.astype(BF16), d_ob, TN_DIMS)
            dcrow_ref[j] -= jnp.sum(ds, axis=0, keepdims=True)
            return dq + _dot(dsb, k), dci + jnp.sum(ds, axis=-1, keepdims=True)

        carry = step(0, (jnp.zeros((t, HEAD_DIM), F32), jnp.zeros((t, 1), F32)), True)
        n_mid = jnp.maximum(i - 1, 0)
        carry = lax.fori_loop(0, n_mid // 2, lambda jp, c: step(2 * jp + 2, step(2 * jp + 1, c, False), False), carry)
        carry = lax.fori_loop(0, n_mid % 2, lambda _, c: step(i - 1, c, False), carry)
        dq, dci = lax.fori_loop(0, jnp.minimum(i, 1), lambda _, c: step(i, c, True), carry)
        dq_ref[...] = (dq * scale).astype(BF16)
        dccol_ref[...] = dci

    q_spec, k_spec, v_spec = _qkv_specs(t, m, n_heads, 0)
    col = pl.BlockSpec((None, t, 1), lambda hd, i: (hd, i, 0))
    rowc = pl.BlockSpec((None, nq, 1, t), lambda hd, i: (hd, 0, 0, 0))
    head = pl.BlockSpec((t, HEAD_DIM), lambda hd, i: (i, hd))
    whole = pl.BlockSpec((m, HEAD_DIM), lambda hd, i: (0, hd))
    gvec = pl.BlockSpec((1, HEAD_DIM), lambda hd, i: (0, hd))
    return _call(body, name=name, grid=(n_heads, nq),
                 in_specs=[q_spec, k_spec, v_spec, head, head, gvec, col, col, rowc],
                 out_specs=[head, whole, whole, gvec, col, rowc],
                 out_shape=[jax.ShapeDtypeStruct((m, hw), BF16), jax.ShapeDtypeStruct((m, hw), F32),
                            jax.ShapeDtypeStruct((m, hw), F32), jax.ShapeDtypeStruct((1, hw), F32),
                            jax.ShapeDtypeStruct((n_heads, m, 1), F32), jax.ShapeDtypeStruct((n_heads, nq, 1, t), F32)],
                 operands=[qkv, qkv, qkv, o, d_on, gain, lse, ccol, crow], comm=comm)


def _sb_scores(z):
    lp = jnp.log(1.0 + jnp.exp(-jnp.abs(z)))
    return jnp.minimum(z, 0.0) - lp, jnp.minimum(-z, 0.0) - lp


def _sb_fwd(qkv, gain, n_heads, pad, name, comm=None):
    m = qkv.shape[0]
    t = _attn_block(m)
    nq = m // t
    assert nq <= LANES
    scale = HEAD_DIM ** -0.5
    hw = n_heads * HEAD_DIM

    def body(q_ref, k_ref, v_ref, g_ref, after_ref, o_ref, on_ref, run_ref):
        i = pl.program_id(1)
        q = q_ref[...]
        qpos = i * t + lax.broadcasted_iota(jnp.int32, (t, 1), 0)
        after = after_ref[...]
        lane = lax.broadcasted_iota(jnp.int32, (t, LANES), 1)

        def step(j, carry, masked):
            run, acc = carry
            ls_pos, log_1m = _sb_scores(_dot(q, _block(k_ref, j, t), NT_DIMS) * scale)
            if masked:
                kpos = j * t + lax.broadcasted_iota(jnp.int32, (1, t), 1)
                ok = (kpos < qpos) & (kpos >= pad)
                log_1m = jnp.where(ok, log_1m, 0.0)
            a = jnp.exp(ls_pos + _dot_split(log_1m, after) + run)
            if masked:
                a = jnp.where(ok, a, 0.0)
            run_ref[...] = jnp.where(lane == j, run, run_ref[...])
            return run + jnp.sum(log_1m, axis=-1, keepdims=True), acc + _dot(a.astype(BF16), _block(v_ref, j, t))

        run_ref[...] = jnp.zeros_like(run_ref)
        carry = step(i, (jnp.zeros((t, 1), F32), jnp.zeros((t, HEAD_DIM), F32)), True)
        n_mid = jnp.maximum(i - 1, 0)
        carry = lax.fori_loop(0, n_mid // 2, lambda jp, c: step(i - 2 * jp - 2, step(i - 2 * jp - 1, c, False), False), carry)
        carry = lax.fori_loop(0, n_mid % 2, lambda _, c: step(1, c, False), carry)
        _, o = lax.fori_loop(0, jnp.minimum(i, 1), lambda _, c: step(0, c, True), carry)
        o_ref[...] = o
        on_ref[...] = _head_norm(o, g_ref[...]).astype(BF16)

    q_spec, k_spec, v_spec = _qkv_specs(t, m, n_heads, 3 * n_heads)
    head = pl.BlockSpec((t, HEAD_DIM), lambda hd, i: (i, hd))
    return _call(body, name=name, grid=(n_heads, nq),
                 in_specs=[q_spec, k_spec, v_spec, pl.BlockSpec((1, HEAD_DIM), lambda hd, i: (0, hd)),
                           pl.BlockSpec((t, t), lambda hd, i: (0, 0))],
                 out_specs=[head, head, pl.BlockSpec((None, t, LANES), lambda hd, i: (hd, i, 0))],
                 out_shape=[jax.ShapeDtypeStruct((m, hw), F32), jax.ShapeDtypeStruct((m, hw), BF16),
                            jax.ShapeDtypeStruct((n_heads, m, LANES), F32)],
                 operands=[qkv, qkv, qkv, gain, _triangle(t, lambda r, c: r > c)], comm=comm)


def _sb_bwd(qkv, o, d_on, gain, runs, n_heads, pad, name, comm=None):
    m = qkv.shape[0]
    t = _attn_block(m)
    nq = m // t
    scale = HEAD_DIM ** -0.5
    hw = n_heads * HEAD_DIM

    def body(q_ref, k_ref, v_ref, o_ref, don_ref, g_ref, run_ref, after_ref, before_ref, dq_ref, dk_ref, dv_ref, dg_ref):
        i = pl.program_id(1)

        @pl.when(i == 0)
        def _():
            dk_ref[...] = jnp.zeros_like(dk_ref)
            dv_ref[...] = jnp.zeros_like(dv_ref)
            dg_ref[...] = jnp.zeros_like(dg_ref)

        q = q_ref[...]
        d_o, dgain = _head_norm_bwd(o_ref[...], don_ref[...], g_ref[...])
        dg_ref[...] += dgain
        d_ob = d_o.astype(BF16)
        runs_i = run_ref[...]
        qpos = i * t + lax.broadcasted_iota(jnp.int32, (t, 1), 0)
        after = after_ref[...]
        before = before_ref[...]
        lane = lax.broadcasted_iota(jnp.int32, (t, LANES), 1)

        def step(j, carry, masked):
            g_run, dq = carry
            k = _block(k_ref, j, t)
            v = _block(v_ref, j, t)
            off = pl.multiple_of(j * t, t)
            ls_pos, ls_neg = _sb_scores(_dot(q, k, NT_DIMS) * scale)
            log_1m = ls_neg
            if masked:
                kpos = off + lax.broadcasted_iota(jnp.int32, (1, t), 1)
                ok = (kpos < qpos) & (kpos >= pad)
                log_1m = jnp.where(ok, ls_neg, 0.0)
            run = jnp.sum(jnp.where(lane == j, runs_i, 0.0), axis=-1, keepdims=True)
            a = jnp.exp(ls_pos + _dot_split(log_1m, after) + run)
            if masked:
                a = jnp.where(ok, a, 0.0)
            g = a * _dot(d_ob, v, NT_DIMS)
            prefix = _dot(g.astype(BF16), before) + g_run
            dz = g * jnp.exp(ls_neg) - jnp.exp(ls_pos) * prefix
            if masked:
                dz = jnp.where(ok, dz, 0.0)
            dzb = dz.astype(BF16)
            dk_ref[pl.ds(off, t), :] += _dot(dzb, q, TN_DIMS) * scale
            dv_ref[pl.ds(off, t), :] += _dot(a.astype(BF16), d_ob, TN_DIMS)
            return g_run + jnp.sum(g, axis=-1, keepdims=True), dq + _dot(dzb, k)

        carry = step(0, (jnp.zeros((t, 1), F32), jnp.zeros((t, HEAD_DIM), F32)), True)
        n_mid = jnp.maximum(i - 1, 0)
        carry = lax.fori_loop(0, n_mid // 2, lambda jp, c: step(2 * jp + 2, step(2 * jp + 1, c, False), False), carry)
        carry = lax.fori_loop(0, n_mid % 2, lambda _, c: step(i - 1, c, False), carry)
        _, dq = lax.fori_loop(0, jnp.minimum(i, 1), lambda _, c: step(i, c, True), carry)
        dq_ref[...] = (dq * scale).astype(BF16)

    q_spec, k_spec, v_spec = _qkv_specs(t, m, n_heads, 3 * n_heads)
    head = pl.BlockSpec((t, HEAD_DIM), lambda hd, i: (i, hd))
    whole = pl.BlockSpec((m, HEAD_DIM), lambda hd, i: (0, hd))
    gvec = pl.BlockSpec((1, HEAD_DIM), lambda hd, i: (0, hd))
    tri = pl.BlockSpec((t, t), lambda hd, i: (0, 0))
    return _call(body, name=name, grid=(n_heads, nq),
                 in_specs=[q_spec, k_spec, v_spec, head, head, gvec, pl.BlockSpec((None, t, LANES), lambda hd, i: (hd, i, 0)),
                           tri, tri],
                 out_specs=[head, whole, whole, gvec],
                 out_shape=[jax.ShapeDtypeStruct((m, hw), BF16), jax.ShapeDtypeStruct((m, hw), F32),
                            jax.ShapeDtypeStruct((m, hw), F32), jax.ShapeDtypeStruct((1, hw), F32)],
                 operands=[qkv, qkv, qkv, o, d_on, gain, runs, _triangle(t, lambda r, c: r > c), _triangle(t, lambda r, c: r < c)],
                 comm=comm)


def _adamw(parts, w, m1, v2, name, comm=None):
    nl, r, c = w.shape
    assert len(parts) == nl
    n_parts = parts[0].shape[0]
    block_elems = 256 * 1024
    if r % 8 == 0 or c % LANES != 0:
        tr, tc = _tile(r, max(8, block_elems // (-(-c // LANES) * LANES)), 8), c
    else:
        tr, tc = r, _tile(c, max(LANES, block_elems // r // LANES * LANES), LANES)
    nr, nc = r // tr, c // tc
    bias1 = 1.0 / (1.0 - ADAM_B1 ** ADAM_STEP)
    bias2 = 1.0 / (1.0 - ADAM_B2 ** ADAM_STEP)

    def body(*refs):
        p_refs = refs[:nl]
        w_ref, m_ref, v_ref, g_ref, d_ref, nm_ref, nv_ref = refs[nl:]

        def update(p_ref):
            g = p_ref[0].astype(F32)
            for s in range(1, n_parts):
                g = g + p_ref[s].astype(F32)
            m_new = ADAM_B1 * m_ref[...] + (1.0 - ADAM_B1) * g
            v_new = ADAM_B2 * v_ref[...] + (1.0 - ADAM_B2) * (g * g)
            g_ref[...] = g
            nm_ref[...] = m_new
            nv_ref[...] = v_new
            d_ref[...] = -ADAM_LR * ((m_new * bias1) / (jnp.sqrt(v_new * bias2) + ADAM_EPS) + ADAM_WD * w_ref[...])

        for ll in range(nl):
            @pl.when(pl.program_id(0) == ll)
            def _(ll=ll):
                update(p_refs[ll])

    def part_spec(ll):
        def index(l, i, j):
            pin = jnp.where(l < ll, 0, 1)
            return 0, jnp.where(l == ll, i, pin * (nr - 1)), jnp.where(l == ll, j, pin * (nc - 1))
        return pl.BlockSpec((n_parts, tr, tc), index)

    blk = pl.BlockSpec((None, tr, tc), lambda l, i, j: (l, i, j))
    out, got = _call(body, name=name, grid=(nl, nr, nc), in_specs=[part_spec(ll) for ll in range(nl)] + [blk, blk, blk],
                     out_specs=[blk] * 4, out_shape=[jax.ShapeDtypeStruct((nl, r, c), F32)] * 4,
                     operands=list(parts) + [w, m1, v2], comm=comm)
    return out if comm is None else (out, got)


SMALL_WEIGHTS = ("ffn1_norm", "mix_norm", "b_forget", "g_fox", "g_sb", "ffn2_norm", "final_norm")
WEIGHT_ORDER = ("meta_tokens", "ffn1_norm", "ffn1_w_gate", "ffn1_w_up", "ffn1_w_down", "mix_norm", "w_in", "b_forget",
                "g_fox", "g_sb", "w_out", "ffn2_norm", "ffn2_w_gate", "ffn2_w_up", "ffn2_w_down", "final_norm")
GROUPS = {"ffn1": ("ffn1_w_gate", "ffn1_w_up", "ffn1_w_down"), "mix": ("w_in", "w_out"),
          "ffn2": ("ffn2_w_gate", "ffn2_w_up", "ffn2_w_down")}
TRANSPOSED = ("ffn1_w_gate", "ffn1_w_up", "ffn2_w_gate", "ffn2_w_up")


def _pad_lanes(a):
    extra = (-a.shape[-1]) % LANES
    return a if extra == 0 else jnp.pad(a, [(0, 0)] * (a.ndim - 1) + [(0, extra)])


def _ffn_backward(dh_b, dh, saved, gain, wg, wu, wd, tag, carried=()):
    h, xn, g, u, act = saved

    def half(k, which, into=()):
        if len(carried) <= k:
            return None
        rows = carried[k].shape[1]
        cut = rows // 2 // 16 * 16
        return _Exchange([carried[k]], rows=(0, cut) if which == 0 else (cut, rows - cut), into=into)

    (dg, du), got_b = _ffn_bwd_act(dh_b, wd, g, u, f"{tag}_bwd_act", half(1, 0))
    d_wd = _ffn_dw(act, dh_b, f"{tag}_dwd", alpha=0.5, comm=half(1, 1, got_b))
    d_wg = _ffn_dw(dg, xn, f"{tag}_dwg", comm=half(2, 0))
    got_c = d_wg[1] if len(carried) > 2 else ()
    d_wu = _ffn_dw(du, xn, f"{tag}_dwu", comm=half(2, 1, got_c))
    dxn = _ffn_contract([(dg, wg), (du, wu)], F32, f"{tag}_dxn", group=2, comm=_Exchange(carried[:1]) if carried else None)
    got = []
    if carried:
        dxn, got = dxn[0], list(dxn[1])
    if len(carried) > 1:
        d_wd, got_b = d_wd
        got += got_b
    if len(carried) > 2:
        d_wg, (d_wu, got_c) = d_wg[0], d_wu
        got += got_c
    dh_in, dh_in_b, d_gain = _rms_bwd(dxn, h, gain, dh, f"{tag}_norm_bwd")
    return dh_in, dh_in_b, d_gain, d_wg, d_wu, d_wd, got


def _ffn_backward_last(dh_b, dh, saved, gain, wg, wu, wd, tag, first, second):
    h, xn, g, u, act = saved
    (dg, du), got_first = _ffn_bwd_act(dh_b, wd, g, u, f"{tag}_bwd_act", first)
    d_wd, got_second = _ffn_dw(act, dh_b, f"{tag}_dwd", alpha=0.5, comm=second)
    c = d_wd.shape[1]
    half = c // 2 // 16 * 16
    d_wg, got_wd = _ffn_dw(dg, xn, f"{tag}_dwg", comm=_Exchange([d_wd], rows=(0, half)))
    d_wu, got_wd = _ffn_dw(du, xn, f"{tag}_dwu", comm=_Exchange([d_wd], rows=(half, c - half), into=got_wd))
    dxn, got_wg = _ffn_contract([(dg, wg), (du, wu)], F32, f"{tag}_dxn", group=2, comm=_Exchange([d_wg]))
    dh_in, dh_in_b, d_gain = _rms_bwd(dxn, h, gain, dh, f"{tag}_norm_bwd")
    return dh_in, dh_in_b, d_gain, got_first, got_second, got_wg[0], got_wd[0], d_wu


def _mixer_forward(h, gain, w_in_t, bias, g_fox, g_sb, w_out, n_heads, pad, tag, comm_fox=None, comm_sb=None):
    m = h.shape[0]
    t = _attn_block(m)
    hw = n_heads * HEAD_DIM
    xn = _rms_fwd(h, gain, f"{tag}_norm")
    qkv = _mm([(xn, w_in_t[:6 * hw])], "nt", BF16, name=f"{tag}_qkv")
    fl = _mm([(xn, w_in_t[6 * hw:])], "nt", F32, name=f"{tag}_forget")
    c = _forget_cumsum(fl, bias, n_heads, pad, f"{tag}_cumsum")
    c_heads = c[:, :n_heads].T
    ccol = c_heads[:, :, None]
    crow = c_heads.reshape(n_heads, m // t, 1, t)
    (o_f, on_f, lse), got_fox = _fox_fwd(qkv, ccol, crow, g_fox, n_heads, pad, f"{tag}_fox", comm_fox)
    (o_s, on_s, runs), got_sb = _sb_fwd(qkv, g_sb, n_heads, pad, f"{tag}_sb", comm_sb)
    if w_out is None:
        w_out = got_sb[-1].reshape(-1, h.shape[1])
    h_out = _mm([(on_f, w_out[:hw]), (on_s, w_out[hw:])], "nn", F32, name=f"{tag}_out", res=h)
    return h_out, (h, xn, qkv, fl, ccol, crow, o_f, on_f, lse, o_s, on_s, runs), got_fox, got_sb, w_out


def _mixer_backward(dh_b, dh_out, saved, gain, w_in_t, bias, g_fox, g_sb, w_out, n_heads, pad, tag, comm_fox=None, comm_sb=None,
                    rows_in_dxn=0):
    h, xn, qkv, fl, ccol, crow, o_f, on_f, lse, o_s, on_s, runs = saved
    m = h.shape[0]
    hw = n_heads * HEAD_DIM
    d_on_f = _mm([(dh_b, w_out[:hw])], "nt", F32, name=f"{tag}_don_f")
    d_on_s = _mm([(dh_b, w_out[hw:])], "nt", F32, name=f"{tag}_don_s")
    d_wout = jnp.concatenate([_mm([(on_f, dh_b)], "tn", BF16, name=f"{tag}_dwout_f"),
                              _mm([(on_s, dh_b)], "tn", BF16, name=f"{tag}_dwout_s")], axis=0)
    (dq_f, dk_f, dv_f, dg_fox, dccol, dcrow), got_fox = _fox_bwd(
        qkv, o_f, d_on_f, g_fox, lse, ccol, crow, n_heads, pad, f"{tag}_fox_bwd", comm_fox)
    (dq_s, dk_s, dv_s, dg_sb), got_sb = _sb_bwd(qkv, o_s, d_on_s, g_sb, runs, n_heads, pad, f"{tag}_sb_bwd", comm_sb)
    dc_a = _pad_lanes(dccol[:, :, 0].T)
    dc_b = _pad_lanes(dcrow.reshape(n_heads, m).T)
    dfl, dbias = _forget_cumsum_bwd(dc_a, dc_b, fl, bias, n_heads, pad, f"{tag}_cumsum_bwd")
    dproj = jnp.concatenate([dq_f, dk_f.astype(BF16), dv_f.astype(BF16), dq_s, dk_s.astype(BF16), dv_s.astype(BF16),
                             dfl.astype(BF16)], axis=1)
    d_win = _mm([(dproj, xn)], "tn", BF16, name=f"{tag}_dwin")[:6 * hw + n_heads].reshape(N_DEV, -1, h.shape[1])
    got_win = None
    if rows_in_dxn:
        dxn, got_win = _mm([(dproj, w_in_t)], "nn", F32, name=f"{tag}_dxn", whole_k=True,
                           comm=_Exchange([d_win], rows=(0, rows_in_dxn)))
    else:
        dxn = _mm([(dproj, w_in_t)], "nn", F32, name=f"{tag}_dxn", whole_k=True)
    dh, dh_in_b, d_gain = _rms_bwd(dxn, h, gain, dh_out, f"{tag}_norm_bwd")
    return dh, dh_in_b, d_gain, d_win, dbias, dg_fox, dg_sb, d_wout, got_fox, got_sb, got_win


def kernel(x, meta_tokens, ffn1_norm, ffn1_w_gate, ffn1_w_up, ffn1_w_down, mix_norm, w_in, b_forget, g_fox, g_sb, w_out, ffn2_norm, ffn2_w_gate, ffn2_w_up, ffn2_w_down, final_norm, loss_target, m_meta_tokens, m_ffn1_norm, m_ffn1_w_gate, m_ffn1_w_up, m_ffn1_w_down, m_mix_norm, m_w_in, m_b_forget, m_g_fox, m_g_sb, m_w_out, m_ffn2_norm, m_ffn2_w_gate, m_ffn2_w_up, m_ffn2_w_down, m_final_norm, v_meta_tokens, v_ffn1_norm, v_ffn1_w_gate, v_ffn1_w_up, v_ffn1_w_down, v_mix_norm, v_w_in, v_b_forget, v_g_fox, v_g_sb, v_w_out, v_ffn2_norm, v_ffn2_w_gate, v_ffn2_w_up, v_ffn2_w_down, v_final_norm):
    given = dict(locals())
    seq, d = x.shape[1], x.shape[2]
    depth = ffn1_norm.shape[0]
    d_in = N_DEV * w_in.shape[2]
    n_heads = g_fox.shape[1] // HEAD_DIM
    hw = n_heads * HEAD_DIM
    assert seq % ROW_BLOCK == 0 and d_in == 6 * hw + n_heads and n_heads <= LANES
    pad = (-(seq + N_META)) % ROW_BLOCK
    x_off = pad + N_META

    def view(n, a):
        if n in TRANSPOSED:
            return jnp.swapaxes(a, 1, 2)
        return a.transpose(2, 0, 1) if n == "w_in" else a

    def unview(n, a):
        if n in TRANSPOSED:
            return jnp.swapaxes(a, 1, 2)
        return a.transpose(1, 2, 0) if n == "w_in" else a

    def shard(n, l):
        v = view(n, given[n])
        return (v[:, l] if n == "w_in" else v[l]).astype(BF16)

    def shards(group, l):
        return [shard(n, l) for n in GROUPS[group]]

    sh = shards("ffn1", 0)
    first = _run_alone(_Gather([sh[0], meta_tokens]), "gather_first")
    full = {}
    meta_full = first[1].transpose(1, 0, 2).reshape(N_META, d)
    h = jnp.concatenate([jnp.zeros((pad, d), F32), meta_full, x[0]], axis=0)
    weights, saved = [], []
    for l in range(depth):
        xn = _rms_fwd(h, ffn1_norm[l:l + 1], "ffn1_norm")
        if l == 0:
            mix0 = shards("mix", 0)
            g, (wu1,) = _ffn_gate(xn, first[0], "ffn1_gate", _Gather([sh[1]]))
            (u, act), (wd1,) = _ffn_up_given_gate(xn, wu1, g, "ffn1_up", _Gather([sh[2]]))
            full[("ffn1", 0)] = (first[0], wu1, wd1)
            h_out, (win3,) = _ffn_contract([(act, wd1)], F32, "ffn1_down", group=2, alpha=0.5, res=h, comm=_Gather(mix0[:1]))
        else:
            wg1, wu1, wd1 = full[("ffn1", l)]
            (g, u, act), full[("mix", l)] = _ffn_up(xn, wg1, wu1, "ffn1_up", _Gather(shards("mix", l)))
            h_out = _ffn_contract([(act, wd1)], F32, "ffn1_down", group=2, alpha=0.5, res=h)
        s1 = (h, xn, g, u, act)
        h = h_out
        if l > 0:
            win3, wout3 = full[("mix", l)]
        w_in_t = jnp.pad(win3.reshape(d_in, d), ((0, 6 * hw + LANES - d_in), (0, 0)))
        bias = _pad_lanes(b_forget[l:l + 1])
        nxt = shards("ffn1", l + 1) if l + 1 < depth else []
        ffn2 = shards("ffn2", l)
        h, sm, (wg2, wu2), got, w_out_full = _mixer_forward(
            h, mix_norm[l:l + 1], w_in_t, bias, g_fox[l:l + 1], g_sb[l:l + 1],
            None if l == 0 else wout3.reshape(N_DEV * wout3.shape[1], d), n_heads, pad, "mix",
            _Gather(ffn2[:2]), _Gather(nxt + ffn2[2:] + (mix0[1:] if l == 0 else [])))
        if nxt:
            full[("ffn1", l + 1)] = got[:3]
        wd2 = got[len(nxt)]
        full[("ffn2", l)] = (wg2, wu2, wd2)
        xn = _rms_fwd(h, ffn2_norm[l:l + 1], "ffn2_norm")
        (g, u, act), _ = _ffn_up(xn, wg2, wu2, "ffn2_up")
        s2 = (h, xn, g, u, act)
        h = _ffn_contract([(act, wd2)], F32, "ffn2_down", group=2, alpha=0.5, res=h)
        weights.append((w_in_t, w_out_full, bias))
        saved.append((s1, sm, s2))

    dh, dh_b, d_final, loss_arr = _loss_head(h, final_norm[None, :], loss_target[0], x_off, "loss_head")
    small = {n: [None] * depth for n in SMALL_WEIGHTS[:-1]}
    partial, received = {}, {}

    def names(group, l):
        return [(n, l) for n in GROUPS[group]]

    def send(keys):
        return [partial[k] for k in keys]

    for l in reversed(range(depth)):
        w_in_t, w_out_full, bias = weights[l]
        s1, sm, s2 = saved[l]
        wg2, wu2, wd2 = full[("ffn2", l)]
        up = l + 1 < depth
        in_dxn = names("ffn1", l + 1) if up else []
        in_fox = names("mix", l + 1) if up else []
        in_sb = names("ffn2", l)
        dh, dh_b, small["ffn2_norm"][l], partial[("ffn2_w_gate", l)], partial[("ffn2_w_up", l)], partial[("ffn2_w_down", l)], got = (
            _ffn_backward(dh_b, dh, s2, ffn2_norm[l:l + 1], wg2, wu2, wd2, "ffn2", send(in_dxn)))
        received.update(zip(in_dxn, got))
        win_rows = (d_in // N_DEV // 2 // 16 * 16) if l == 0 else 0
        (dh, dh_b, small["mix_norm"][l], partial[("w_in", l)], dbias, small["g_fox"][l], small["g_sb"][l], d_wout,
         got_fox, got_sb, got_win) = _mixer_backward(
            dh_b, dh, sm, mix_norm[l:l + 1], w_in_t, bias, g_fox[l:l + 1], g_sb[l:l + 1], w_out_full, n_heads, pad, "mix",
            _Exchange(send(in_fox)) if in_fox else None, _Exchange(send(in_sb)), win_rows)
        received.update(zip(in_fox, got_fox or []))
        received.update(zip(in_sb, got_sb))
        partial[("w_out", l)] = d_wout.reshape(N_DEV, -1, d)
        small["b_forget"][l] = dbias[:, :n_heads]
        wg1, wu1, wd1 = full[("ffn1", l)]
        if l > 0:
            dh, dh_b, small["ffn1_norm"][l], partial[("ffn1_w_gate", l)], partial[("ffn1_w_up", l)], partial[("ffn1_w_down", l)], _ = (
                _ffn_backward(dh_b, dh, s1, ffn1_norm[l:l + 1], wg1, wu1, wd1, "ffn1"))
        else:
            rest = (win_rows, d_in // N_DEV - win_rows)
            (dh, dh_b, small["ffn1_norm"][0], got_a, got_b, received[("ffn1_w_gate", 0)], received[("ffn1_w_down", 0)],
             last_grad) = _ffn_backward_last(
                dh_b, dh, s1, ffn1_norm[0:1], wg1, wu1, wd1, "ffn1", _Exchange([partial[("w_out", 0)]]),
                _Exchange([partial[("w_in", 0)]], rows=rest, into=got_win))
            received[("w_out", 0)], received[("w_in", 0)] = got_a[0], got_b[0]
    grad_x = dh[x_off:][None]
    d_meta = dh[pad:x_off].reshape(N_META, N_DEV, -1).transpose(1, 0, 2)
    received[("meta_tokens", 0)] = _run_alone(_Exchange([d_meta]), "exchange_meta")[0]

    vec = [loss_arr[0:1, :]] + [_pad_lanes(jnp.concatenate(small[n], axis=0).reshape(1, -1)) for n in SMALL_WEIGHTS[:-1]]
    vec.append(d_final)
    sizes = [a.shape[1] for a in vec]
    summed = _all_reduce_small(jnp.concatenate(vec, axis=1), "reduce_small")
    loss = summed[0, 0]

    def packed(prefix):
        cols = [jnp.zeros((1, LANES), F32)]
        cols += [_pad_lanes(given[prefix + n].reshape(1, -1)) for n in SMALL_WEIGHTS]
        return jnp.concatenate(cols, axis=1)[None]

    small_out = _adamw([summed[None]], packed(""), packed("m_"), packed("v_"), "adamw_small")

    c_last = last_grad.shape[1]
    quarter = c_last // 4 // 16 * 16
    pieces = [(k * quarter, quarter if k < 3 else c_last - 3 * quarter) for k in range(4)]
    order = ["ffn2_w_gate", "ffn2_w_up", "ffn2_w_down", "ffn1_w_gate", "ffn1_w_down", "w_in", "w_out", "ffn1_w_up"]
    out, arrived = {}, []
    for k, n in enumerate(order):
        wv, mv, vv = (view(n, given[p + n]) for p in ("", "m_", "v_"))
        if n == "ffn1_w_up":
            received[(n, 0)] = arrived[0]
        if n == "w_in":
            per_layer = [_adamw([received[(n, l)]], wv[:, l][None], mv[:, l][None], vv[:, l][None], "adamw_" + n)
                         for l in range(depth)]
            res = [jnp.stack([per_layer[l][k][0] for l in range(depth)], axis=1) for k in range(4)]
        elif k < len(pieces):
            res, arrived = _adamw([received[(n, l)] for l in range(depth)], wv, mv, vv, "adamw_" + n,
                                  _Exchange([last_grad], rows=pieces[k], into=arrived))
        else:
            res = _adamw([received[(n, l)] for l in range(depth)], wv, mv, vv, "adamw_" + n)
        out[n] = [unview(n, r) for r in res]
    out["meta_tokens"] = [r[0] for r in _adamw([received[("meta_tokens", 0)]], meta_tokens[None], m_meta_tokens[None],
                                               v_meta_tokens[None], "adamw_meta_tokens")]
    offset = sizes[0]
    for n, size in zip(SMALL_WEIGHTS, sizes[1:]):
        shape, count = given[n].shape, given[n].size
        out[n] = [r[0, 0, offset:offset + count].reshape(shape) for r in small_out]
        offset += size

    result = [loss, grad_x]
    for k in range(4):
        result += [out[n][k] for n in WEIGHT_ORDER]
    return tuple(result)
```
